```python
import math
import jax
import jax.numpy as jnp
from jax import lax
import numpy as np


D_MODEL = 2048
BATCH = 8
SEQ = 4096
DEPTH = 4

N_EVEN = (DEPTH + 1) // 2
N_ODD = DEPTH // 2
EPS = 1e-6

S5_WIDTH = D_MODEL // 2
S5_GROUP_SIZE = 16
S5_GROUPS = S5_WIDTH // S5_GROUP_SIZE
S5_STATE = 64
S5_MIN_DECAY = 1e-4

HGRN_WIDTH = D_MODEL - S5_WIDTH
HGRN_HEAD_DIM = 128
HGRN_HEADS = HGRN_WIDTH // HGRN_HEAD_DIM
HGRN_CHUNK = 64

IN_EVEN = S5_WIDTH + 4 * HGRN_WIDTH

ATT_HEAD_DIM = 64
ATT_HEADS = D_MODEL // ATT_HEAD_DIM
ATT_GROUP = 8
ATT_KV_HEADS = ATT_HEADS // ATT_GROUP
WINDOW = 128
ATT_BLOCK = 128
QKV_WIDTH = (ATT_HEADS + 2 * ATT_KV_HEADS) * ATT_HEAD_DIM

D_FF = 4 * D_MODEL

kernel_name = 'hybrid_s5_hgrn2_swa_block'

F32 = jnp.float32


def rms_norm(x, gain):
    xf = x.astype(F32)
    y = xf * lax.rsqrt(jnp.mean(xf * xf, axis=-1, keepdims=True) + EPS)
    return (y * gain.astype(F32)).astype(x.dtype)


def alibi_slopes(n_heads):
    return jnp.exp2(-8.0 * jnp.arange(1, n_heads + 1, dtype=F32) / n_heads)


def hgrn_lower_bounds(lb_param):
    p = jax.nn.softmax(lb_param.astype(F32), axis=0)
    return jnp.cumsum(p, axis=0) - p[0:1]


def s5_mixer(u, lam_re, lam_im, log_dt, b_re, b_im, c_re, c_im, d_skip, w_glu, b_glu):
    bsz, seqlen, _ = u.shape
    uf = u.astype(F32).reshape(bsz, seqlen, S5_GROUPS, S5_GROUP_SIZE)
    lr = jnp.minimum(lam_re.astype(F32), -S5_MIN_DECAY)
    li = lam_im.astype(F32)
    dt = jnp.exp(log_dt.astype(F32))[:, None]
    mag = jnp.exp(lr * dt)
    ar = mag * jnp.cos(li * dt)
    ai = mag * jnp.sin(li * dt)
    den = lr * lr + li * li
    zr = ((ar - 1.0) * lr + ai * li) / den
    zi = (ai * lr - (ar - 1.0) * li) / den
    br, bi = b_re.astype(F32), b_im.astype(F32)
    bbr = zr[..., None] * br - zi[..., None] * bi
    bbi = zr[..., None] * bi + zi[..., None] * br
    bu_re = jnp.einsum('blgc,gpc->blgp', uf, bbr)
    bu_im = jnp.einsum('blgc,gpc->blgp', uf, bbi)
    a_re = jnp.broadcast_to(ar, bu_re.shape)
    a_im = jnp.broadcast_to(ai, bu_im.shape)

    def combine(e1, e2):
        a1r, a1i, b1r, b1i = e1
        a2r, a2i, b2r, b2i = e2
        return (a2r * a1r - a2i * a1i,
                a2r * a1i + a2i * a1r,
                a2r * b1r - a2i * b1i + b2r,
                a2r * b1i + a2i * b1r + b2i)

    _, _, x_re, x_im = lax.associative_scan(combine, (a_re, a_im, bu_re, bu_im), axis=1)
    y = (jnp.einsum('blgp,gcp->blgc', x_re, c_re.astype(F32))
         - jnp.einsum('blgp,gcp->blgc', x_im, c_im.astype(F32))
         + d_skip.astype(F32) * uf)
    y = jax.nn.gelu(y.reshape(bsz, seqlen, S5_WIDTH))
    return y * jax.nn.sigmoid(y @ w_glu.astype(F32) + b_glu.astype(F32))


def chunkwise_gated_recurrence(q, log_f, k, v):
    bsz, seqlen, heads, dk = q.shape
    dv = v.shape[-1]
    n_chunks = seqlen // HGRN_CHUNK

    def to_chunks(t):
        return t.reshape(bsz, n_chunks, HGRN_CHUNK, heads, t.shape[-1]).transpose(1, 0, 3, 2, 4)

    qc, gc, kc, vc = to_chunks(q), to_chunks(log_f), to_chunks(k), to_chunks(v)
    causal = jnp.tril(jnp.ones((HGRN_CHUNK, HGRN_CHUNK), dtype=bool))[:, :, None]

    def step(state, inputs):
        q_, g_, k_, v_ = inputs
        b = jnp.cumsum(g_, axis=2)
        diff = b[:, :, :, None, :] - b[:, :, None, :, :]
        decay = jnp.exp(jnp.where(causal, diff, -jnp.inf))
        scores = jnp.einsum('bhtk,bhsk,bhtsk->bhts', q_, k_, decay)
        out = (jnp.einsum('bhts,bhsv->bhtv', scores, v_)
               + jnp.einsum('bhtk,bhkv->bhtv', q_ * jnp.exp(b), state))
        b_last = b[:, :, -1:, :]
        state = (jnp.exp(b_last[:, :, 0, :])[..., None] * state
                 + jnp.einsum('bhsk,bhsv->bhkv', k_ * jnp.exp(b_last - b), v_))
        return state, out

    state0 = jnp.zeros((bsz, heads, dk, dv), F32)
    _, outs = lax.scan(step, state0, (qc, gc, kc, vc))
    return outs.transpose(1, 0, 3, 2, 4).reshape(bsz, seqlen, heads, dv)


def hgrn2_mixer(q, f, i, g, lower_bound, o_gain):
    bsz, seqlen, _ = q.shape

    def heads(t):
        return t.astype(F32).reshape(bsz, seqlen, HGRN_HEADS, HGRN_HEAD_DIM)

    lb = lower_bound.reshape(HGRN_HEADS, HGRN_HEAD_DIM)
    forget = lb + (1.0 - lb) * jax.nn.sigmoid(heads(f))
    o = chunkwise_gated_recurrence(jax.nn.silu(heads(q)), jnp.log(forget), 1.0 - forget, heads(i))
    o = rms_norm(o, o_gain) * jax.nn.silu(heads(g))
    return o.reshape(bsz, seqlen, HGRN_WIDTH)


def s5_hgrn2_layer(xn, w_in, lam_re, lam_im, log_dt, b_re, b_im, c_re, c_im, d_skip,
                   w_glu, b_glu, lower_bound, o_gain, w_out):
    proj = xn @ w_in
    u, q, f, i, g = jnp.split(proj, [S5_WIDTH, S5_WIDTH + HGRN_WIDTH,
                                     S5_WIDTH + 2 * HGRN_WIDTH, S5_WIDTH + 3 * HGRN_WIDTH], axis=-1)
    y_a = s5_mixer(u, lam_re, lam_im, log_dt, b_re, b_im, c_re, c_im, d_skip, w_glu, b_glu)
    y_b = hgrn2_mixer(q, f, i, g, lower_bound, o_gain)
    return jnp.concatenate([y_a, y_b], axis=-1).astype(xn.dtype) @ w_out


def sliding_window_attention(q, k, v, sinks):
    bsz, seqlen = q.shape[:2]
    nb = seqlen // ATT_BLOCK
    qb = q.reshape(bsz, nb, ATT_BLOCK, ATT_KV_HEADS, ATT_GROUP, ATT_HEAD_DIM)

    def band(t):
        tb = t.reshape(bsz, nb, ATT_BLOCK, ATT_KV_HEADS, ATT_HEAD_DIM)
        prev = jnp.pad(tb, ((0, 0), (1, 0), (0, 0), (0, 0), (0, 0)))[:, :-1]
        return jnp.concatenate([prev, tb], axis=2)

    kw, vw = band(k), band(v)
    scores = jnp.einsum('bnqkgd,bnskd->bnkgqs', qb, kw) * (ATT_HEAD_DIM ** -0.5)
    dist = jnp.arange(ATT_BLOCK)[:, None] + ATT_BLOCK - jnp.arange(2 * ATT_BLOCK)[None, :]
    in_window = (dist >= 0) & (dist < WINDOW)
    key_pos = (jnp.arange(nb)[:, None] * ATT_BLOCK - ATT_BLOCK
               + jnp.arange(2 * ATT_BLOCK)[None, :])
    mask = in_window[None] & (key_pos >= 0)[:, None, :]
    slopes = alibi_slopes(ATT_HEADS).reshape(ATT_KV_HEADS, ATT_GROUP)
    bias = -slopes[:, :, None, None] * dist.astype(F32)
    scores = jnp.where(mask[None, :, None, None], scores + bias, -jnp.inf)
    sink = jnp.broadcast_to(sinks.astype(F32).reshape(ATT_KV_HEADS, ATT_GROUP)[None, None, :, :, None, None],
                            scores.shape[:-1] + (1,))
    probs = jax.nn.softmax(jnp.concatenate([scores, sink], axis=-1), axis=-1)[..., :-1]
    out = jnp.einsum('bnkgqs,bnskd->bnqkgd', probs, vw)
    return out.reshape(bsz, seqlen, ATT_HEADS * ATT_HEAD_DIM)


def attention_layer(xn, w_qkv, q_gain, k_gain, sinks, w_out):
    bsz, seqlen, _ = xn.shape
    proj = xn @ w_qkv
    q, k, v = jnp.split(proj, [ATT_HEADS * ATT_HEAD_DIM, (ATT_HEADS + ATT_KV_HEADS) * ATT_HEAD_DIM], axis=-1)
    q = rms_norm(q.astype(F32).reshape(bsz, seqlen, ATT_HEADS, ATT_HEAD_DIM), q_gain)
    k = rms_norm(k.astype(F32).reshape(bsz, seqlen, ATT_KV_HEADS, ATT_HEAD_DIM), k_gain)
    v = v.astype(F32).reshape(bsz, seqlen, ATT_KV_HEADS, ATT_HEAD_DIM)
    o = sliding_window_attention(q, k, v, sinks)
    return o.astype(xn.dtype) @ w_out


def squared_relu_mlp(xn, w_up, w_down):
    return jnp.square(jax.nn.relu(xn @ w_up)) @ w_down


def _fwd_setup_inputs(seed: int = 0) -> dict:
    key = jax.random.key(seed)
    ks = iter(jax.random.split(key, 32))

    def nrm(shape, scale):
        return scale * jax.random.normal(next(ks), shape, F32)

    x = nrm((BATCH, SEQ, D_MODEL), 1.0)
    even_norm = 1.0 + nrm((N_EVEN, D_MODEL), 0.02)
    even_w_in = nrm((N_EVEN, D_MODEL, IN_EVEN), D_MODEL ** -0.5)
    s5_lambda_re = -0.5 + nrm((N_EVEN, S5_GROUPS, S5_STATE), 0.01)
    s5_lambda_im = math.pi * jnp.arange(S5_STATE, dtype=F32) + nrm((N_EVEN, S5_GROUPS, S5_STATE), 0.01)
    s5_log_dt = jax.random.uniform(next(ks), (N_EVEN, S5_GROUPS), F32, math.log(1e-3), math.log(1e-1))
    s5_b_re = nrm((N_EVEN, S5_GROUPS, S5_STATE, S5_GROUP_SIZE), (2 * S5_GROUP_SIZE) ** -0.5)
    s5_b_im = nrm((N_EVEN, S5_GROUPS, S5_STATE, S5_GROUP_SIZE), (2 * S5_GROUP_SIZE) ** -0.5)
    s5_c_re = nrm((N_EVEN, S5_GROUPS, S5_GROUP_SIZE, S5_STATE), S5_STATE ** -0.5)
    s5_c_im = nrm((N_EVEN, S5_GROUPS, S5_GROUP_SIZE, S5_STATE), S5_STATE ** -0.5)
    s5_d = nrm((N_EVEN, S5_GROUPS, S5_GROUP_SIZE), 1.0)
    s5_w_glu = nrm((N_EVEN, S5_WIDTH, S5_WIDTH), S5_WIDTH ** -0.5)
    s5_b_glu = nrm((N_EVEN, S5_WIDTH), 0.01)
    hgrn_lower_bound = nrm((N_EVEN, HGRN_WIDTH), 0.1)
    hgrn_o_norm = 1.0 + nrm((N_EVEN, HGRN_HEAD_DIM), 0.02)
    even_w_out = nrm((N_EVEN, D_MODEL, D_MODEL), D_MODEL ** -0.5)
    odd_norm = 1.0 + nrm((N_ODD, D_MODEL), 0.02)
    odd_w_qkv = nrm((N_ODD, D_MODEL, QKV_WIDTH), D_MODEL ** -0.5)
    q_norm = 1.0 + nrm((N_ODD, ATT_HEAD_DIM), 0.02)
    k_norm = 1.0 + nrm((N_ODD, ATT_HEAD_DIM), 0.02)
    att_sinks = nrm((N_ODD, ATT_HEADS), 0.5)
    odd_w_out = nrm((N_ODD, D_MODEL, D_MODEL), D_MODEL ** -0.5)
    mlp_norm = 1.0 + nrm((DEPTH, D_MODEL), 0.02)
    mlp_w_up = nrm((DEPTH, D_MODEL, D_FF), D_MODEL ** -0.5)
    mlp_w_down = nrm((DEPTH, D_FF, D_MODEL), D_FF ** -0.5)
    return {'x': x, 'even_norm': even_norm, 'even_w_in': even_w_in,
            's5_lambda_re': s5_lambda_re, 's5_lambda_im': s5_lambda_im, 's5_log_dt': s5_log_dt,
            's5_b_re': s5_b_re, 's5_b_im': s5_b_im, 's5_c_re': s5_c_re, 's5_c_im': s5_c_im,
            's5_d': s5_d, 's5_w_glu': s5_w_glu, 's5_b_glu': s5_b_glu,
            'hgrn_lower_bound': hgrn_lower_bound, 'hgrn_o_norm': hgrn_o_norm, 'even_w_out': even_w_out,
            'odd_norm': odd_norm, 'odd_w_qkv': odd_w_qkv, 'q_norm': q_norm, 'k_norm': k_norm,
            'att_sinks': att_sinks, 'odd_w_out': odd_w_out,
            'mlp_norm': mlp_norm, 'mlp_w_up': mlp_w_up, 'mlp_w_down': mlp_w_down}


def _fwd_reference(x, even_norm, even_w_in, s5_lambda_re, s5_lambda_im, s5_log_dt, s5_b_re, s5_b_im,
              s5_c_re, s5_c_im, s5_d, s5_w_glu, s5_b_glu, hgrn_lower_bound, hgrn_o_norm, even_w_out,
              odd_norm, odd_w_qkv, q_norm, k_norm, att_sinks, odd_w_out, mlp_norm, mlp_w_up, mlp_w_down):
    h = x
    lower_bounds = hgrn_lower_bounds(hgrn_lower_bound)
    for layer in range(DEPTH):
        j = layer // 2
        if layer % 2 == 0:
            y = s5_hgrn2_layer(rms_norm(h, even_norm[j]), even_w_in[j], s5_lambda_re[j], s5_lambda_im[j],
                               s5_log_dt[j], s5_b_re[j], s5_b_im[j], s5_c_re[j], s5_c_im[j], s5_d[j],
                               s5_w_glu[j], s5_b_glu[j], lower_bounds[j], hgrn_o_norm[j], even_w_out[j])
        else:
            y = attention_layer(rms_norm(h, odd_norm[j]), odd_w_qkv[j], q_norm[j], k_norm[j],
                                att_sinks[j], odd_w_out[j])
        h = h + y.astype(h.dtype)
        h = h + squared_relu_mlp(rms_norm(h, mlp_norm[layer]), mlp_w_up[layer], mlp_w_down[layer]).astype(h.dtype)
    return h


import jax as _jax
import jax.numpy as _jnp

TWIN_FORMAT = 'train_step'
FWD_PARAMS = ['x', 'even_norm', 'even_w_in', 's5_lambda_re', 's5_lambda_im', 's5_log_dt', 's5_b_re', 's5_b_im', 's5_c_re', 's5_c_im', 's5_d', 's5_w_glu', 's5_b_glu', 'hgrn_lower_bound', 'hgrn_o_norm', 'even_w_out', 'odd_norm', 'odd_w_qkv', 'q_norm', 'k_norm', 'att_sinks', 'odd_w_out', 'mlp_norm', 'mlp_w_up', 'mlp_w_down']
TWIN_WEIGHTS = ['even_norm', 'even_w_in', 's5_lambda_re', 's5_lambda_im', 's5_log_dt', 's5_b_re', 's5_b_im', 's5_c_re', 's5_c_im', 's5_d', 's5_w_glu', 's5_b_glu', 'hgrn_lower_bound', 'hgrn_o_norm', 'even_w_out', 'odd_norm', 'odd_w_qkv', 'q_norm', 'k_norm', 'att_sinks', 'odd_w_out', 'mlp_norm', 'mlp_w_up', 'mlp_w_down']
TWIN_DIFF_INPUT = 'x'
TWIN_INPUTS = ['x', 'even_norm', 'even_w_in', 's5_lambda_re', 's5_lambda_im', 's5_log_dt', 's5_b_re', 's5_b_im', 's5_c_re', 's5_c_im', 's5_d', 's5_w_glu', 's5_b_glu', 'hgrn_lower_bound', 'hgrn_o_norm', 'even_w_out', 'odd_norm', 'odd_w_qkv', 'q_norm', 'k_norm', 'att_sinks', 'odd_w_out', 'mlp_norm', 'mlp_w_up', 'mlp_w_down', 'loss_target', 'm_even_norm', 'm_even_w_in', 'm_s5_lambda_re', 'm_s5_lambda_im', 'm_s5_log_dt', 'm_s5_b_re', 'm_s5_b_im', 'm_s5_c_re', 'm_s5_c_im', 'm_s5_d', 'm_s5_w_glu', 'm_s5_b_glu', 'm_hgrn_lower_bound', 'm_hgrn_o_norm', 'm_even_w_out', 'm_odd_norm', 'm_odd_w_qkv', 'm_q_norm', 'm_k_norm', 'm_att_sinks', 'm_odd_w_out', 'm_mlp_norm', 'm_mlp_w_up', 'm_mlp_w_down', 'v_even_norm', 'v_even_w_in', 'v_s5_lambda_re', 'v_s5_lambda_im', 'v_s5_log_dt', 'v_s5_b_re', 'v_s5_b_im', 'v_s5_c_re', 'v_s5_c_im', 'v_s5_d', 'v_s5_w_glu', 'v_s5_b_glu', 'v_hgrn_lower_bound', 'v_hgrn_o_norm', 'v_even_w_out', 'v_odd_norm', 'v_odd_w_qkv', 'v_q_norm', 'v_k_norm', 'v_att_sinks', 'v_odd_w_out', 'v_mlp_norm', 'v_mlp_w_up', 'v_mlp_w_down']
TWIN_OUTPUTS = ['loss', 'grad_x', 'grad_even_norm', 'grad_even_w_in', 'grad_s5_lambda_re', 'grad_s5_lambda_im', 'grad_s5_log_dt', 'grad_s5_b_re', 'grad_s5_b_im', 'grad_s5_c_re', 'grad_s5_c_im', 'grad_s5_d', 'grad_s5_w_glu', 'grad_s5_b_glu', 'grad_hgrn_lower_bound', 'grad_hgrn_o_norm', 'grad_even_w_out', 'grad_odd_norm', 'grad_odd_w_qkv', 'grad_q_norm', 'grad_k_norm', 'grad_att_sinks', 'grad_odd_w_out', 'grad_mlp_norm', 'grad_mlp_w_up', 'grad_mlp_w_down', 'delta_even_norm', 'delta_even_w_in', 'delta_s5_lambda_re', 'delta_s5_lambda_im', 'delta_s5_log_dt', 'delta_s5_b_re', 'delta_s5_b_im', 'delta_s5_c_re', 'delta_s5_c_im', 'delta_s5_d', 'delta_s5_w_glu', 'delta_s5_b_glu', 'delta_hgrn_lower_bound', 'delta_hgrn_o_norm', 'delta_even_w_out', 'delta_odd_norm', 'delta_odd_w_qkv', 'delta_q_norm', 'delta_k_norm', 'delta_att_sinks', 'delta_odd_w_out', 'delta_mlp_norm', 'delta_mlp_w_up', 'delta_mlp_w_down', 'new_m_even_norm', 'new_m_even_w_in', 'new_m_s5_lambda_re', 'new_m_s5_lambda_im', 'new_m_s5_log_dt', 'new_m_s5_b_re', 'new_m_s5_b_im', 'new_m_s5_c_re', 'new_m_s5_c_im', 'new_m_s5_d', 'new_m_s5_w_glu', 'new_m_s5_b_glu', 'new_m_hgrn_lower_bound', 'new_m_hgrn_o_norm', 'new_m_even_w_out', 'new_m_odd_norm', 'new_m_odd_w_qkv', 'new_m_q_norm', 'new_m_k_norm', 'new_m_att_sinks', 'new_m_odd_w_out', 'new_m_mlp_norm', 'new_m_mlp_w_up', 'new_m_mlp_w_down', 'new_v_even_norm', 'new_v_even_w_in', 'new_v_s5_lambda_re', 'new_v_s5_lambda_im', 'new_v_s5_log_dt', 'new_v_s5_b_re', 'new_v_s5_b_im', 'new_v_s5_c_re', 'new_v_s5_c_im', 'new_v_s5_d', 'new_v_s5_w_glu', 'new_v_s5_b_glu', 'new_v_hgrn_lower_bound', 'new_v_hgrn_o_norm', 'new_v_even_w_out', 'new_v_odd_norm', 'new_v_odd_w_qkv', 'new_v_q_norm', 'new_v_k_norm', 'new_v_att_sinks', 'new_v_odd_w_out', 'new_v_mlp_norm', 'new_v_mlp_w_up', 'new_v_mlp_w_down']
TWIN_LEAF_KINDS = {'loss': 'loss', 'grad_x': 'grad_x', 'grad_even_norm': 'grad_w', 'grad_even_w_in': 'grad_w', 'grad_s5_lambda_re': 'grad_w', 'grad_s5_lambda_im': 'grad_w', 'grad_s5_log_dt': 'grad_w', 'grad_s5_b_re': 'grad_w', 'grad_s5_b_im': 'grad_w', 'grad_s5_c_re': 'grad_w', 'grad_s5_c_im': 'grad_w', 'grad_s5_d': 'grad_w', 'grad_s5_w_glu': 'grad_w', 'grad_s5_b_glu': 'grad_w', 'grad_hgrn_lower_bound': 'grad_w', 'grad_hgrn_o_norm': 'grad_w', 'grad_even_w_out': 'grad_w', 'grad_odd_norm': 'grad_w', 'grad_odd_w_qkv': 'grad_w', 'grad_q_norm': 'grad_w', 'grad_k_norm': 'grad_w', 'grad_att_sinks': 'grad_w', 'grad_odd_w_out': 'grad_w', 'grad_mlp_norm': 'grad_w', 'grad_mlp_w_up': 'grad_w', 'grad_mlp_w_down': 'grad_w', 'delta_even_norm': 'delta_w', 'delta_even_w_in': 'delta_w', 'delta_s5_lambda_re': 'delta_w', 'delta_s5_lambda_im': 'delta_w', 'delta_s5_log_dt': 'delta_w', 'delta_s5_b_re': 'delta_w', 'delta_s5_b_im': 'delta_w', 'delta_s5_c_re': 'delta_w', 'delta_s5_c_im': 'delta_w', 'delta_s5_d': 'delta_w', 'delta_s5_w_glu': 'delta_w', 'delta_s5_b_glu': 'delta_w', 'delta_hgrn_lower_bound': 'delta_w', 'delta_hgrn_o_norm': 'delta_w', 'delta_even_w_out': 'delta_w', 'delta_odd_norm': 'delta_w', 'delta_odd_w_qkv': 'delta_w', 'delta_q_norm': 'delta_w', 'delta_k_norm': 'delta_w', 'delta_att_sinks': 'delta_w', 'delta_odd_w_out': 'delta_w', 'delta_mlp_norm': 'delta_w', 'delta_mlp_w_up': 'delta_w', 'delta_mlp_w_down': 'delta_w', 'new_m_even_norm': 'new_m', 'new_m_even_w_in': 'new_m', 'new_m_s5_lambda_re': 'new_m', 'new_m_s5_lambda_im': 'new_m', 'new_m_s5_log_dt': 'new_m', 'new_m_s5_b_re': 'new_m', 'new_m_s5_b_im': 'new_m', 'new_m_s5_c_re': 'new_m', 'new_m_s5_c_im': 'new_m', 'new_m_s5_d': 'new_m', 'new_m_s5_w_glu': 'new_m', 'new_m_s5_b_glu': 'new_m', 'new_m_hgrn_lower_bound': 'new_m', 'new_m_hgrn_o_norm': 'new_m', 'new_m_even_w_out': 'new_m', 'new_m_odd_norm': 'new_m', 'new_m_odd_w_qkv': 'new_m', 'new_m_q_norm': 'new_m', 'new_m_k_norm': 'new_m', 'new_m_att_sinks': 'new_m', 'new_m_odd_w_out': 'new_m', 'new_m_mlp_norm': 'new_m', 'new_m_mlp_w_up': 'new_m', 'new_m_mlp_w_down': 'new_m', 'new_v_even_norm': 'new_v', 'new_v_even_w_in': 'new_v', 'new_v_s5_lambda_re': 'new_v', 'new_v_s5_lambda_im': 'new_v', 'new_v_s5_log_dt': 'new_v', 'new_v_s5_b_re': 'new_v', 'new_v_s5_b_im': 'new_v', 'new_v_s5_c_re': 'new_v', 'new_v_s5_c_im': 'new_v', 'new_v_s5_d': 'new_v', 'new_v_s5_w_glu': 'new_v', 'new_v_s5_b_glu': 'new_v', 'new_v_hgrn_lower_bound': 'new_v', 'new_v_hgrn_o_norm': 'new_v', 'new_v_even_w_out': 'new_v', 'new_v_odd_norm': 'new_v', 'new_v_odd_w_qkv': 'new_v', 'new_v_q_norm': 'new_v', 'new_v_k_norm': 'new_v', 'new_v_att_sinks': 'new_v', 'new_v_odd_w_out': 'new_v', 'new_v_mlp_norm': 'new_v', 'new_v_mlp_w_up': 'new_v', 'new_v_mlp_w_down': 'new_v'}


def _forward(args):
    return _fwd_reference(*[args[k] for k in FWD_PARAMS])


def _output_shape():
    def fwd():
        inp = _fwd_setup_inputs(0)
        return _fwd_reference(*[inp[k] for k in FWD_PARAMS])
    out = _jax.eval_shape(fwd)
    return out.shape, out.dtype

N_MICROBATCH = 1
ADAM_LR = 0.001
ADAM_B1 = 0.9
ADAM_B2 = 0.999
ADAM_EPS = 1e-08
ADAM_WD = 0.01
ADAM_STEP = 10
PER_EXAMPLE_BATCH_AXIS = {'x': 0, 'loss_target': 0}
SHARED_INPUTS = []
_WEIGHT_DTYPES = {'even_norm': _jnp.float32, 'even_w_in': _jnp.float32, 's5_lambda_re': _jnp.float32, 's5_lambda_im': _jnp.float32, 's5_log_dt': _jnp.float32, 's5_b_re': _jnp.float32, 's5_b_im': _jnp.float32, 's5_c_re': _jnp.float32, 's5_c_im': _jnp.float32, 's5_d': _jnp.float32, 's5_w_glu': _jnp.float32, 's5_b_glu': _jnp.float32, 'hgrn_lower_bound': _jnp.float32, 'hgrn_o_norm': _jnp.float32, 'even_w_out': _jnp.float32, 'odd_norm': _jnp.float32, 'odd_w_qkv': _jnp.float32, 'q_norm': _jnp.float32, 'k_norm': _jnp.float32, 'att_sinks': _jnp.float32, 'odd_w_out': _jnp.float32, 'mlp_norm': _jnp.float32, 'mlp_w_up': _jnp.float32, 'mlp_w_down': _jnp.float32}
MOMENT_SCALE = {'even_norm': 6.198702e+00, 'even_w_in': 2.597215e+00, 's5_lambda_re': 2.577967e-01, 's5_lambda_im': 2.271272e-01, 's5_log_dt': 1.119165e+01, 's5_b_re': 1.889939e-01, 's5_b_im': 1.707966e-01, 's5_c_re': 2.370441e-01, 's5_c_im': 2.310935e-01, 's5_d': 7.907340e+00, 's5_w_glu': 1.192514e+00, 's5_b_glu': 3.463341e+00, 'hgrn_lower_bound': 8.888724e-02, 'hgrn_o_norm': 4.735220e+01, 'even_w_out': 5.790602e+00, 'odd_norm': 1.198235e+01, 'odd_w_qkv': 9.916010e+00, 'q_norm': 1.647503e+01, 'k_norm': 1.650912e+01, 'att_sinks': 2.657936e+01, 'odd_w_out': 8.941820e+00, 'mlp_norm': 4.975082e+01, 'mlp_w_up': 4.814584e+00, 'mlp_w_down': 1.685003e+01}


def _to_microbatches(a, axis):
    t = _jnp.moveaxis(a, axis, 0)
    t = t.reshape((N_MICROBATCH, t.shape[0] // N_MICROBATCH) + t.shape[1:])
    return _jnp.moveaxis(t, 1, axis + 1)


def setup_inputs(seed: int = 0) -> dict:
    inp = _fwd_setup_inputs(seed)
    key = _jax.random.fold_in(_jax.random.key(seed), 7919)
    shape, _ = _output_shape()
    out = dict(inp)
    out["loss_target"] = _jax.random.normal(_jax.random.fold_in(key, 0), shape, _jnp.float32)
    for i, name in enumerate(TWIN_WEIGHTS):
        w = inp[name].astype(_jnp.float32)
        if MOMENT_SCALE is None:
            s = _jnp.sqrt(_jnp.mean(_jnp.square(w)) + 1e-30)
        else:
            s = MOMENT_SCALE[name]
        km, kv = _jax.random.split(_jax.random.fold_in(key, i + 1))
        out[name] = w
        out["m_" + name] = s * _jax.random.normal(km, w.shape, _jnp.float32)
        out["v_" + name] = (s * s) * _jax.random.uniform(kv, w.shape, _jnp.float32, 0.5, 1.5)
    if N_MICROBATCH > 1:
        for name, axis in PER_EXAMPLE_BATCH_AXIS.items():
            out[name] = _to_microbatches(out[name], axis)
    return {'x': out['x'], 'even_norm': out['even_norm'], 'even_w_in': out['even_w_in'], 's5_lambda_re': out['s5_lambda_re'], 's5_lambda_im': out['s5_lambda_im'], 's5_log_dt': out['s5_log_dt'], 's5_b_re': out['s5_b_re'], 's5_b_im': out['s5_b_im'], 's5_c_re': out['s5_c_re'], 's5_c_im': out['s5_c_im'], 's5_d': out['s5_d'], 's5_w_glu': out['s5_w_glu'], 's5_b_glu': out['s5_b_glu'], 'hgrn_lower_bound': out['hgrn_lower_bound'], 'hgrn_o_norm': out['hgrn_o_norm'], 'even_w_out': out['even_w_out'], 'odd_norm': out['odd_norm'], 'odd_w_qkv': out['odd_w_qkv'], 'q_norm': out['q_norm'], 'k_norm': out['k_norm'], 'att_sinks': out['att_sinks'], 'odd_w_out': out['odd_w_out'], 'mlp_norm': out['mlp_norm'], 'mlp_w_up': out['mlp_w_up'], 'mlp_w_down': out['mlp_w_down'], 'loss_target': out['loss_target'], 'm_even_norm': out['m_even_norm'], 'm_even_w_in': out['m_even_w_in'], 'm_s5_lambda_re': out['m_s5_lambda_re'], 'm_s5_lambda_im': out['m_s5_lambda_im'], 'm_s5_log_dt': out['m_s5_log_dt'], 'm_s5_b_re': out['m_s5_b_re'], 'm_s5_b_im': out['m_s5_b_im'], 'm_s5_c_re': out['m_s5_c_re'], 'm_s5_c_im': out['m_s5_c_im'], 'm_s5_d': out['m_s5_d'], 'm_s5_w_glu': out['m_s5_w_glu'], 'm_s5_b_glu': out['m_s5_b_glu'], 'm_hgrn_lower_bound': out['m_hgrn_lower_bound'], 'm_hgrn_o_norm': out['m_hgrn_o_norm'], 'm_even_w_out': out['m_even_w_out'], 'm_odd_norm': out['m_odd_norm'], 'm_odd_w_qkv': out['m_odd_w_qkv'], 'm_q_norm': out['m_q_norm'], 'm_k_norm': out['m_k_norm'], 'm_att_sinks': out['m_att_sinks'], 'm_odd_w_out': out['m_odd_w_out'], 'm_mlp_norm': out['m_mlp_norm'], 'm_mlp_w_up': out['m_mlp_w_up'], 'm_mlp_w_down': out['m_mlp_w_down'], 'v_even_norm': out['v_even_norm'], 'v_even_w_in': out['v_even_w_in'], 'v_s5_lambda_re': out['v_s5_lambda_re'], 'v_s5_lambda_im': out['v_s5_lambda_im'], 'v_s5_log_dt': out['v_s5_log_dt'], 'v_s5_b_re': out['v_s5_b_re'], 'v_s5_b_im': out['v_s5_b_im'], 'v_s5_c_re': out['v_s5_c_re'], 'v_s5_c_im': out['v_s5_c_im'], 'v_s5_d': out['v_s5_d'], 'v_s5_w_glu': out['v_s5_w_glu'], 'v_s5_b_glu': out['v_s5_b_glu'], 'v_hgrn_lower_bound': out['v_hgrn_lower_bound'], 'v_hgrn_o_norm': out['v_hgrn_o_norm'], 'v_even_w_out': out['v_even_w_out'], 'v_odd_norm': out['v_odd_norm'], 'v_odd_w_qkv': out['v_odd_w_qkv'], 'v_q_norm': out['v_q_norm'], 'v_k_norm': out['v_k_norm'], 'v_att_sinks': out['v_att_sinks'], 'v_odd_w_out': out['v_odd_w_out'], 'v_mlp_norm': out['v_mlp_norm'], 'v_mlp_w_up': out['v_mlp_w_up'], 'v_mlp_w_down': out['v_mlp_w_down']}


def _loss(weights, diff, rest, loss_target):
    with _jax.named_scope("forward"):
        args = {**rest, TWIN_DIFF_INPUT: diff, **{k: w.astype(_WEIGHT_DTYPES[k]) for k, w in weights.items()}}
        y = _forward(args)
    with _jax.named_scope("loss_head"):
        err = _jnp.square(y.astype(_jnp.float32) - loss_target)
        return 0.5 * _jnp.sum(_jnp.mean(err, axis=-1)) if err.ndim else 0.5 * err


def _adamw(w, g, m, v):
    m = ADAM_B1 * m + (1.0 - ADAM_B1) * g
    v = ADAM_B2 * v + (1.0 - ADAM_B2) * _jnp.square(g)
    m_hat = m / (1.0 - ADAM_B1 ** ADAM_STEP)
    v_hat = v / (1.0 - ADAM_B2 ** ADAM_STEP)
    delta = -ADAM_LR * (m_hat / (_jnp.sqrt(v_hat) + ADAM_EPS) + ADAM_WD * w)
    return delta, m, v


def reference(x, even_norm, even_w_in, s5_lambda_re, s5_lambda_im, s5_log_dt, s5_b_re, s5_b_im, s5_c_re, s5_c_im, s5_d, s5_w_glu, s5_b_glu, hgrn_lower_bound, hgrn_o_norm, even_w_out, odd_norm, odd_w_qkv, q_norm, k_norm, att_sinks, odd_w_out, mlp_norm, mlp_w_up, mlp_w_down, loss_target, m_even_norm, m_even_w_in, m_s5_lambda_re, m_s5_lambda_im, m_s5_log_dt, m_s5_b_re, m_s5_b_im, m_s5_c_re, m_s5_c_im, m_s5_d, m_s5_w_glu, m_s5_b_glu, m_hgrn_lower_bound, m_hgrn_o_norm, m_even_w_out, m_odd_norm, m_odd_w_qkv, m_q_norm, m_k_norm, m_att_sinks, m_odd_w_out, m_mlp_norm, m_mlp_w_up, m_mlp_w_down, v_even_norm, v_even_w_in, v_s5_lambda_re, v_s5_lambda_im, v_s5_log_dt, v_s5_b_re, v_s5_b_im, v_s5_c_re, v_s5_c_im, v_s5_d, v_s5_w_glu, v_s5_b_glu, v_hgrn_lower_bound, v_hgrn_o_norm, v_even_w_out, v_odd_norm, v_odd_w_qkv, v_q_norm, v_k_norm, v_att_sinks, v_odd_w_out, v_mlp_norm, v_mlp_w_up, v_mlp_w_down):
    given = dict(x=x, even_norm=even_norm, even_w_in=even_w_in, s5_lambda_re=s5_lambda_re, s5_lambda_im=s5_lambda_im, s5_log_dt=s5_log_dt, s5_b_re=s5_b_re, s5_b_im=s5_b_im, s5_c_re=s5_c_re, s5_c_im=s5_c_im, s5_d=s5_d, s5_w_glu=s5_w_glu, s5_b_glu=s5_b_glu, hgrn_lower_bound=hgrn_lower_bound, hgrn_o_norm=hgrn_o_norm, even_w_out=even_w_out, odd_norm=odd_norm, odd_w_qkv=odd_w_qkv, q_norm=q_norm, k_norm=k_norm, att_sinks=att_sinks, odd_w_out=odd_w_out, mlp_norm=mlp_norm, mlp_w_up=mlp_w_up, mlp_w_down=mlp_w_down, loss_target=loss_target, m_even_norm=m_even_norm, m_even_w_in=m_even_w_in, m_s5_lambda_re=m_s5_lambda_re, m_s5_lambda_im=m_s5_lambda_im, m_s5_log_dt=m_s5_log_dt, m_s5_b_re=m_s5_b_re, m_s5_b_im=m_s5_b_im, m_s5_c_re=m_s5_c_re, m_s5_c_im=m_s5_c_im, m_s5_d=m_s5_d, m_s5_w_glu=m_s5_w_glu, m_s5_b_glu=m_s5_b_glu, m_hgrn_lower_bound=m_hgrn_lower_bound, m_hgrn_o_norm=m_hgrn_o_norm, m_even_w_out=m_even_w_out, m_odd_norm=m_odd_norm, m_odd_w_qkv=m_odd_w_qkv, m_q_norm=m_q_norm, m_k_norm=m_k_norm, m_att_sinks=m_att_sinks, m_odd_w_out=m_odd_w_out, m_mlp_norm=m_mlp_norm, m_mlp_w_up=m_mlp_w_up, m_mlp_w_down=m_mlp_w_down, v_even_norm=v_even_norm, v_even_w_in=v_even_w_in, v_s5_lambda_re=v_s5_lambda_re, v_s5_lambda_im=v_s5_lambda_im, v_s5_log_dt=v_s5_log_dt, v_s5_b_re=v_s5_b_re, v_s5_b_im=v_s5_b_im, v_s5_c_re=v_s5_c_re, v_s5_c_im=v_s5_c_im, v_s5_d=v_s5_d, v_s5_w_glu=v_s5_w_glu, v_s5_b_glu=v_s5_b_glu, v_hgrn_lower_bound=v_hgrn_lower_bound, v_hgrn_o_norm=v_hgrn_o_norm, v_even_w_out=v_even_w_out, v_odd_norm=v_odd_norm, v_odd_w_qkv=v_odd_w_qkv, v_q_norm=v_q_norm, v_k_norm=v_k_norm, v_att_sinks=v_att_sinks, v_odd_w_out=v_odd_w_out, v_mlp_norm=v_mlp_norm, v_mlp_w_up=v_mlp_w_up, v_mlp_w_down=v_mlp_w_down)
    weights = {n: given[n] for n in TWIN_WEIGHTS}
    shared = {n: given[n] for n in SHARED_INPUTS}
    per_example = {n: given[n] for n in ['x']}
    grad_fn = _jax.value_and_grad(_loss, argnums=(0, 1))

    def one_microbatch(ex, loss_target):
        ex = dict(ex)
        diff = ex.pop(TWIN_DIFF_INPUT)
        return grad_fn(weights, diff, {**shared, **ex}, loss_target)

    if N_MICROBATCH == 1:
        loss, (grad_w, grad_x) = one_microbatch(per_example, given["loss_target"])
    else:
        def body(carry, xs):
            loss_sum, grad_sum = carry
            l_k, (gw_k, gx_k) = one_microbatch(xs[0], xs[1])
            with _jax.named_scope("update"):
                return (loss_sum + l_k, _jax.tree.map(_jnp.add, grad_sum, gw_k)), gx_k

        init = (_jnp.zeros((), _jnp.float32), _jax.tree.map(_jnp.zeros_like, weights))
        (loss, grad_w), grad_x = _jax.lax.scan(body, init, (per_example, given["loss_target"]))
    with _jax.named_scope("update"):
        delta_w, new_m, new_v = {}, {}, {}
        for n in TWIN_WEIGHTS:
            delta_w[n], new_m[n], new_v[n] = _adamw(weights[n], grad_w[n], given["m_" + n], given["v_" + n])
    return (loss, grad_x, *[grad_w[n] for n in TWIN_WEIGHTS], *[delta_w[n] for n in TWIN_WEIGHTS],
            *[new_m[n] for n in TWIN_WEIGHTS], *[new_v[n] for n in TWIN_WEIGHTS])
```

```python
import math

import jax
import jax.numpy as jnp
from jax import lax
from jax.experimental import pallas as pl
from jax.experimental.pallas import tpu as pltpu

F32 = jnp.float32
BF16 = jnp.bfloat16
MESH = pl.DeviceIdType.MESH

D_MODEL = 2048
DEPTH = 4
EPS = 1e-6
S5_WIDTH = 1024
S5_GROUPS = 64
S5_STATE = 64
S5_GROUP_SIZE = 16
S5_MIN_DECAY = 1e-4
S5_CHUNK = 128
S5_LEVELS = 7
HGRN_WIDTH = 1024
HGRN_HEADS = 8
HGRN_DIM = 128
HGRN_SUB = 16
HGRN_BLOCK = 128
ATT_HEADS = 32
ATT_KV = 4
ATT_DIM = 64
ATT_BLOCK = 128
QKV_WIDTH = (ATT_HEADS + 2 * ATT_KV) * ATT_DIM
D_FF = 4 * D_MODEL
N_DEV = 8
NEG = -1e30
VMEM_LIMIT = 56 * 1024 * 1024

ADAM_LR, ADAM_B1, ADAM_B2, ADAM_EPS, ADAM_WD, ADAM_STEP = 0.001, 0.9, 0.999, 1e-08, 0.01, 10


def _params(sem=None):
    return pltpu.CompilerParams(dimension_semantics=sem, vmem_limit_bytes=VMEM_LIMIT)


def _sds(shape, dtype):
    return jax.ShapeDtypeStruct(shape, dtype)


def _mm(name, a, b, mode, out_dtypes=(F32,), epi=None, extras=(), tm=512, tn=1024, tk=2048):
    if mode == "nn":
        (m, k), n = a.shape, b.shape[1]
    elif mode == "nt":
        (m, k), n = a.shape, b.shape[0]
    else:
        (k, m), n = a.shape, b.shape[1]
    tm, tn, tk = min(tm, m), min(tn, n), min(tk, k)
    assert m % tm == 0 and n % tn == 0 and k % tk == 0, (name, m, n, k)
    nk = k // tk
    if mode == "nn":
        a_spec = pl.BlockSpec((tm, tk), lambda i, j, kk: (i, kk))
        b_spec = pl.BlockSpec((tk, tn), lambda i, j, kk: (kk, j))
        dims = (((1,), (0,)), ((), ()))
    elif mode == "nt":
        a_spec = pl.BlockSpec((tm, tk), lambda i, j, kk: (i, kk))
        b_spec = pl.BlockSpec((tn, tk), lambda i, j, kk: (j, kk))
        dims = (((1,), (1,)), ((), ()))
    else:
        a_spec = pl.BlockSpec((tk, tm), lambda i, j, kk: (kk, i))
        b_spec = pl.BlockSpec((tk, tn), lambda i, j, kk: (kk, j))
        dims = (((0,), (0,)), ((), ()))
    o_spec = pl.BlockSpec((tm, tn), lambda i, j, kk: (i, j))
    n_ex, n_out = len(extras), len(out_dtypes)

    def body(*refs):
        a_ref, b_ref = refs[0], refs[1]
        ex_refs = refs[2:2 + n_ex]
        out_refs = refs[2 + n_ex:2 + n_ex + n_out]
        acc_ref = refs[2 + n_ex + n_out] if nk > 1 else None
        av, bv = a_ref[...], b_ref[...]
        if av.dtype != BF16:
            av = av.astype(BF16)
        if bv.dtype != BF16:
            bv = bv.astype(BF16)
        part = lax.dot_general(av, bv, dims, preferred_element_type=F32)

        def finish(acc):
            outs = epi(acc, *[r[...] for r in ex_refs]) if epi is not None else (acc,)
            for r, o in zip(out_refs, outs):
                r[...] = o.astype(r.dtype)

        if nk == 1:
            finish(part)
        else:
            kk = pl.program_id(2)

            @pl.when(kk == 0)
            def _():
                acc_ref[...] = part

            @pl.when(kk > 0)
            def _():
                acc_ref[...] += part

            @pl.when(kk == nk - 1)
            def _():
                finish(acc_ref[...])

    outs = pl.pallas_call(
        body, name=name,
        grid=(m // tm, n // tn, nk),
        in_specs=[a_spec, b_spec] + [o_spec] * n_ex,
        out_specs=[o_spec] * n_out,
        out_shape=[_sds((m, n), dt) for dt in out_dtypes],
        scratch_shapes=[pltpu.VMEM((tm, tn), F32)] if nk > 1 else [],
        compiler_params=_params(("parallel", "parallel", "arbitrary")),
    )(a, b, *extras)
    return outs[0] if n_out == 1 else outs


def _rowwise(name, fn, rows, vecs, outs, accs=(), tr=256):
    n_rows = rows[0].shape[0]
    tr = min(tr, n_rows)
    assert n_rows % tr == 0
    n_r, n_v, n_o, n_a = len(rows), len(vecs), len(outs), len(accs)

    def body(*refs):
        ins = [r[...] for r in refs[:n_r + n_v]]
        o_refs = refs[n_r + n_v:n_r + n_v + n_o]
        a_refs = refs[n_r + n_v + n_o:]
        ro, ao = fn(*ins)
        for r, o in zip(o_refs, ro):
            r[...] = o.astype(r.dtype)
        if n_a:
            step = pl.program_id(0)

            @pl.when(step == 0)
            def _():
                for r, o in zip(a_refs, ao):
                    r[...] = o

            @pl.when(step > 0)
            def _():
                for r, o in zip(a_refs, ao):
                    r[...] += o

    res = pl.pallas_call(
        body, name=name,
        grid=(n_rows // tr,),
        in_specs=[pl.BlockSpec((tr, r.shape[1]), lambda i: (i, 0)) for r in rows]
        + [pl.BlockSpec(v.shape, lambda i: (0, 0)) for v in vecs],
        out_specs=[pl.BlockSpec((tr, w), lambda i: (i, 0)) for w, _ in outs]
        + [pl.BlockSpec((1, w), lambda i: (0, 0)) for w in accs],
        out_shape=[_sds((n_rows, w), dt) for w, dt in outs] + [_sds((1, w), F32) for w in accs],
        compiler_params=_params(("arbitrary",)),
    )(*rows, *vecs)
    return res


def _colsum(x):
    return jnp.sum(x, axis=0, keepdims=True)


def _sigmoid(x):
    return 1.0 / (1.0 + jnp.exp(-x))


_GELU_C = math.sqrt(2.0 / math.pi)


def _gelu(y):
    return 0.5 * y * (1.0 + jnp.tanh(_GELU_C * (y + 0.044715 * y * y * y)))


def _gelu_grad(y):
    t = jnp.tanh(_GELU_C * (y + 0.044715 * y * y * y))
    return 0.5 * (1.0 + t) + 0.5 * y * (1.0 - t * t) * _GELU_C * (1.0 + 3.0 * 0.044715 * y * y)


def _rms_fwd(name, h, gain):
    def fn(x, g):
        r = lax.rsqrt(jnp.mean(x * x, axis=1, keepdims=True) + EPS)
        return (x * r * g, r), ()
    return _rowwise(name, fn, [h], [gain.reshape(1, -1)], [(h.shape[1], BF16), (1, F32)])


def _rms_bwd(name, h, rstd, gain, dxn, dres):
    def fn(x, r, dy, dr, g):
        xh = x * r
        gdy = dy * g
        dx = r * (gdy - xh * jnp.mean(gdy * xh, axis=1, keepdims=True)) + dr
        return (dx, dx), (_colsum(dy * xh),)
    w = h.shape[1]
    return _rowwise(name, fn, [h, rstd, dxn, dres], [gain.reshape(1, -1)], [(w, F32), (w, BF16)], [w])


def _loss_head(h, target):
    w = h.shape[1]

    def fn(x, t):
        e = x - t
        return (e * (1.0 / w), e * (1.0 / w)), (jnp.zeros((1, 128), F32) + jnp.sum(e * e),)
    return _rowwise("loss_head", fn, [h, target], [], [(w, F32), (w, BF16)], [128])


def _adamw(name, w, g, m, v):
    c1 = 1.0 - ADAM_B1 ** ADAM_STEP
    c2 = 1.0 - ADAM_B2 ** ADAM_STEP

    def fn(w_, g_, m_, v_):
        mn = ADAM_B1 * m_ + (1.0 - ADAM_B1) * g_
        vn = ADAM_B2 * v_ + (1.0 - ADAM_B2) * (g_ * g_)
        delta = -ADAM_LR * ((mn / c1) / (jnp.sqrt(vn / c2) + ADAM_EPS) + ADAM_WD * w_)
        return (delta, mn, vn), ()
    c = w.shape[1]
    return _rowwise(name, fn, [w, g, m, v], [], [(c, F32)] * 3)


def _s5_discretize(lam_re, lam_im, log_dt, b_re, b_im):
    lr = jnp.minimum(lam_re, -S5_MIN_DECAY)
    li = lam_im
    dt = jnp.exp(log_dt)[:, None]
    mag = jnp.exp(lr * dt)
    ar = mag * jnp.cos(li * dt)
    ai = mag * jnp.sin(li * dt)
    den = lr * lr + li * li
    zr = ((ar - 1.0) * lr + ai * li) / den
    zi = (ai * lr - (ar - 1.0) * li) / den
    bbr = zr[..., None] * b_re - zi[..., None] * b_im
    bbi = zr[..., None] * b_im + zi[..., None] * b_re
    return ar, ai, bbr, bbi


def _s5_matrices(ar, ai, bbr, bbi, c_re, c_im, d_skip):
    eye = jnp.eye(8, dtype=F32)
    bt = jnp.concatenate([bbr, bbi], axis=1).transpose(0, 2, 1)
    bm8 = (bt.reshape(8, 8, 16, 1, 128) * eye[None, :, None, :, None]).reshape(8, 128, 1024)
    ct = jnp.concatenate([c_re, -c_im], axis=2).transpose(0, 2, 1)
    cm8 = (ct.reshape(8, 8, 128, 1, 16) * eye[None, :, None, :, None]).reshape(8, 1024, 128)
    p1, p2 = [], []
    pr, pi = ar, ai
    for _ in range(S5_LEVELS):
        p1.append(jnp.concatenate([pr, pr], axis=1).reshape(8, 1024))
        p2.append(jnp.concatenate([-pi, pi], axis=1).reshape(8, 1024))
        pr, pi = pr * pr - pi * pi, 2.0 * pr * pi
    p1.append(jnp.zeros_like(p1[0]))
    p2.append(jnp.zeros_like(p2[0]))
    return (bm8.astype(BF16), cm8.astype(BF16), jnp.stack(p1, axis=1), jnp.stack(p2, axis=1),
            d_skip.reshape(8, 1, 128))


def _s5_unpack_grads(dbm8, dcm8, da, dd):
    db = jnp.einsum("agcgp->agcp", dbm8.reshape(8, 8, 16, 8, 128)).reshape(S5_GROUPS, 16, 128).transpose(0, 2, 1)
    dc = jnp.einsum("agpgc->agpc", dcm8.reshape(8, 8, 128, 8, 16)).reshape(S5_GROUPS, 128, 16).transpose(0, 2, 1)
    a1 = da[:, 0, :].reshape(S5_GROUPS, 128)
    a2 = da[:, 1, :].reshape(S5_GROUPS, 128)
    dar = a1[:, :64] + a1[:, 64:]
    dai = a2[:, 64:] - a2[:, :64]
    return dar, dai, db[:, :64, :], db[:, 64:, :], dc[..., :64], -dc[..., 64:], dd.reshape(S5_GROUPS, 16)


def _swap(x):
    return pltpu.roll(x, 64, 1)


def _s5_scan(x, p1, p2, row, carry_term):
    t = x.shape[0]
    x = x + jnp.where(row == 0, carry_term, 0.0)
    for k in range(S5_LEVELS):
        s = 1 << k
        sh = jnp.where(row >= s, pltpu.roll(x, s, 0), 0.0)
        x = x + p1[k:k + 1] * sh + p2[k:k + 1] * _swap(sh)
    return x


def _s5_scan_rev(g, p1, p2, row, carry_term):
    t = g.shape[0]
    g = g + jnp.where(row == t - 1, carry_term, 0.0)
    for k in range(S5_LEVELS):
        s = 1 << k
        sh = jnp.where(row < t - s, pltpu.roll(g, t - s, 0), 0.0)
        g = g + p1[k:k + 1] * sh - p2[k:k + 1] * _swap(sh)
    return g


def _s5_fwd(name, proj, mats):
    bm8, cm8, p1, p2, d8 = mats
    n_rows = proj.shape[0]
    t = S5_CHUNK
    nch = n_rows // t

    def body(u_ref, bm_ref, cm_ref, p1_ref, p2_ref, d_ref, y_ref, z_ref, st_ref, carry):
        @pl.when(pl.program_id(1) == 0)
        def _():
            carry[...] = jnp.zeros_like(carry)

        st_ref[...] = carry[...]
        u = u_ref[...]
        bu = jnp.dot(u.astype(BF16), bm_ref[...], preferred_element_type=F32)
        row = lax.broadcasted_iota(jnp.int32, (t, 128), 0)
        xs = []
        for gl in range(8):
            sl = slice(128 * gl, 128 * (gl + 1))
            p1g, p2g = p1_ref[:, sl], p2_ref[:, sl]
            c8 = carry[:, sl]
            ac = p1g[0:1] * c8 + p2g[0:1] * _swap(c8)
            x = _s5_scan(bu[:, sl], p1g, p2g, row, ac[0:1])
            carry[:, sl] = jnp.broadcast_to(x[t - 1:t, :], (8, 128))
            xs.append(x)
        xall = jnp.concatenate(xs, axis=1)
        y = jnp.dot(xall.astype(BF16), cm_ref[...], preferred_element_type=F32) + d_ref[...] * u
        y_ref[...] = y
        z_ref[...] = _gelu(y).astype(BF16)

    return pl.pallas_call(
        body, name=name,
        grid=(8, nch),
        in_specs=[
            pl.BlockSpec((t, 128), lambda g, c: (c, g)),
            pl.BlockSpec((None, 128, 1024), lambda g, c: (g, 0, 0)),
            pl.BlockSpec((None, 1024, 128), lambda g, c: (g, 0, 0)),
            pl.BlockSpec((None, 8, 1024), lambda g, c: (g, 0, 0)),
            pl.BlockSpec((None, 8, 1024), lambda g, c: (g, 0, 0)),
            pl.BlockSpec((None, 1, 128), lambda g, c: (g, 0, 0)),
        ],
        out_specs=[
            pl.BlockSpec((t, 128), lambda g, c: (c, g)),
            pl.BlockSpec((t, 128), lambda g, c: (c, g)),
            pl.BlockSpec((None, None, 8, 1024), lambda g, c: (g, c, 0, 0)),
        ],
        out_shape=[_sds((n_rows, S5_WIDTH), F32), _sds((n_rows, S5_WIDTH), BF16), _sds((8, nch, 8, 1024), F32)],
        scratch_shapes=[pltpu.VMEM((8, 1024), F32)],
        compiler_params=_params(("parallel", "arbitrary")),
    )(proj, bm8, cm8, p1, p2, d8)


def _s5_bwd(name, proj, dy, states, mats):
    bm8, cm8, p1, p2, d8 = mats
    n_rows = proj.shape[0]
    t = S5_CHUNK
    nch = n_rows // t
    nt_dims = (((1,), (1,)), ((), ()))
    tn_dims = (((0,), (0,)), ((), ()))

    def body(u_ref, dy_ref, st_ref, bm_ref, cm_ref, p1_ref, p2_ref, d_ref,
             du_ref, dbm_ref, dcm_ref, da_ref, dd_ref, gcarry):
        @pl.when(pl.program_id(1) == 0)
        def _():
            gcarry[...] = jnp.zeros_like(gcarry)
            dbm_ref[...] = jnp.zeros_like(dbm_ref)
            dcm_ref[...] = jnp.zeros_like(dcm_ref)
            da_ref[...] = jnp.zeros_like(da_ref)
            dd_ref[...] = jnp.zeros_like(dd_ref)

        u = u_ref[...]
        dyv = dy_ref[...]
        ub, dyb = u.astype(BF16), dyv.astype(BF16)
        bu = jnp.dot(ub, bm_ref[...], preferred_element_type=F32)
        dxd = lax.dot_general(dyb, cm_ref[...], nt_dims, preferred_element_type=F32)
        row = lax.broadcasted_iota(jnp.int32, (t, 128), 0)
        xs, gs, da1, da2 = [], [], [], []
        for gl in range(8):
            sl = slice(128 * gl, 128 * (gl + 1))
            p1g, p2g = p1_ref[:, sl], p2_ref[:, sl]
            c8 = st_ref[:, sl]
            ac = p1g[0:1] * c8 + p2g[0:1] * _swap(c8)
            x = _s5_scan(bu[:, sl], p1g, p2g, row, ac[0:1])
            g8 = gcarry[:, sl]
            gac = p1g[0:1] * g8 - p2g[0:1] * _swap(g8)
            g = _s5_scan_rev(dxd[:, sl], p1g, p2g, row, gac[0:1])
            gcarry[:, sl] = jnp.broadcast_to(g[0:1, :], (8, 128))
            xprev = jnp.where(row >= 1, pltpu.roll(x, 1, 0), c8[0:1])
            da1.append(_colsum(g * xprev))
            da2.append(_colsum(g * _swap(xprev)))
            xs.append(x)
            gs.append(g)
        xall = jnp.concatenate(xs, axis=1).astype(BF16)
        gall = jnp.concatenate(gs, axis=1).astype(BF16)
        dcm_ref[...] += lax.dot_general(xall, dyb, tn_dims, preferred_element_type=F32)
        dbm_ref[...] += lax.dot_general(ub, gall, tn_dims, preferred_element_type=F32)
        du = lax.dot_general(gall, bm_ref[...], nt_dims, preferred_element_type=F32) + d_ref[...] * dyv
        du_ref[...] = du.astype(BF16)
        dd_ref[...] += _colsum(dyv * u)
        da_ref[0:1, :] += jnp.concatenate(da1, axis=1)
        da_ref[1:2, :] += jnp.concatenate(da2, axis=1)

    rev = lambda g, c: (nch - 1 - c, g)
    return pl.pallas_call(
        body, name=name,
        grid=(8, nch),
        in_specs=[
            pl.BlockSpec((t, 128), rev),
            pl.BlockSpec((t, 128), rev),
            pl.BlockSpec((None, None, 8, 1024), lambda g, c: (g, nch - 1 - c, 0, 0)),
            pl.BlockSpec((None, 128, 1024), lambda g, c: (g, 0, 0)),
            pl.BlockSpec((None, 1024, 128), lambda g, c: (g, 0, 0)),
            pl.BlockSpec((None, 8, 1024), lambda g, c: (g, 0, 0)),
            pl.BlockSpec((None, 8, 1024), lambda g, c: (g, 0, 0)),
            pl.BlockSpec((None, 1, 128), lambda g, c: (g, 0, 0)),
        ],
        out_specs=[
            pl.BlockSpec((t, 128), rev),
            pl.BlockSpec((None, 128, 1024), lambda g, c: (g, 0, 0)),
            pl.BlockSpec((None, 1024, 128), lambda g, c: (g, 0, 0)),
            pl.BlockSpec((None, 8, 1024), lambda g, c: (g, 0, 0)),
            pl.BlockSpec((None, 1, 128), lambda g, c: (g, 0, 0)),
        ],
        out_shape=[_sds((n_rows, S5_WIDTH), BF16), _sds((8, 128, 1024), F32), _sds((8, 1024, 128), F32),
                   _sds((8, 8, 1024), F32), _sds((8, 1, 128), F32)],
        scratch_shapes=[pltpu.VMEM((8, 1024), F32)],
        compiler_params=_params(("parallel", "arbitrary")),
    )(proj, dy, states, bm8, cm8, p1, p2, d8)


def _hgrn_lower_bounds(lb_param):
    p = jax.nn.softmax(lb_param, axis=0)
    return jnp.cumsum(p, axis=0) - p[0:1]


def _prefix16(x, r16):
    for s in (1, 2, 4, 8):
        x = x + jnp.where(r16 >= s, pltpu.roll(x, s, 0), 0.0)
    return x


def _suffix16(x, r16):
    n = x.shape[0]
    for s in (1, 2, 4, 8):
        x = x + jnp.where(r16 < HGRN_SUB - s, pltpu.roll(x, n - s, 0), 0.0)
    return x


_NT = (((1,), (1,)), ((), ()))
_TN = (((0,), (0,)), ((), ()))


def _dotf(a, b, dims=(((1,), (0,)), ((), ()))):
    return lax.dot_general(a.astype(BF16), b.astype(BF16), dims, preferred_element_type=F32)


def _hgrn_specs(n_blocks, rev):
    r = HGRN_BLOCK
    blk = (lambda b: n_blocks - 1 - b) if rev else (lambda b: b)
    proj_specs = [pl.BlockSpec((r, 128), (lambda h, b, c=c: (blk(b), 8 * c + h))) for c in (1, 2, 3, 4)]
    lb_spec = pl.BlockSpec((None, 1, 128), lambda h, b: (h, 0, 0))
    gain_spec = pl.BlockSpec((1, 128), lambda h, b: (0, 0))
    row_spec = pl.BlockSpec((r, 128), lambda h, b: (blk(b), h))
    st_spec = pl.BlockSpec((None, None, 128, 128), lambda h, b: (h, blk(b), 0, 0))
    return proj_specs, lb_spec, gain_spec, row_spec, st_spec, blk


def _hgrn_fwd(name, proj, lb, gain):
    n_rows = proj.shape[0]
    r = HGRN_BLOCK
    nb = n_rows // r
    nsub = r // HGRN_SUB
    proj_specs, lb_spec, gain_spec, row_spec, st_spec, _ = _hgrn_specs(nb, False)

    def body(q_ref, f_ref, i_ref, g_ref, lb_ref, gain_ref, o_ref, y_ref, st_ref, st_scr):
        @pl.when(pl.program_id(1) == 0)
        def _():
            st_scr[...] = jnp.zeros_like(st_scr)

        st_ref[...] = st_scr[...]
        q, f, v, g = q_ref[...], f_ref[...], i_ref[...], g_ref[...]
        lbv = lb_ref[...]
        qs = q * _sigmoid(q)
        fg = lbv + (1.0 - lbv) * _sigmoid(f)
        kk = 1.0 - fg
        r16 = lax.broadcasted_iota(jnp.int32, (r, 128), 0) & (HGRN_SUB - 1)
        b = _prefix16(jnp.log(fg), r16)
        qh = qs * jnp.exp(b)
        rs = lax.broadcasted_iota(jnp.int32, (HGRN_SUB, 128), 0)
        st = st_scr[...]
        outs = []
        for i in range(nsub):
            sl = slice(HGRN_SUB * i, HGRN_SUB * (i + 1))
            qsi, kki, vi, bi = qs[sl], kk[sl], v[sl], b[sl]
            o_i = _dotf(qh[sl], st, _NT)
            for s in range(HGRN_SUB):
                e = jnp.exp(jnp.where(rs >= s, bi - bi[s:s + 1], NEG))
                col = jnp.sum(qsi * e * kki[s:s + 1], axis=1, keepdims=True)
                o_i = o_i + col * vi[s:s + 1]
            bl = bi[HGRN_SUB - 1:HGRN_SUB]
            st = st * jnp.exp(bl) + _dotf(vi, kki * jnp.exp(bl - bi), _TN)
            outs.append(o_i)
        st_scr[...] = st
        o = jnp.concatenate(outs, axis=0)
        o_ref[...] = o
        rn = lax.rsqrt(jnp.mean(o * o, axis=1, keepdims=True) + EPS)
        y_ref[...] = (o * rn * gain_ref[...] * (g * _sigmoid(g))).astype(BF16)

    return pl.pallas_call(
        body, name=name,
        grid=(HGRN_HEADS, nb),
        in_specs=proj_specs + [lb_spec, gain_spec],
        out_specs=[row_spec, row_spec, st_spec],
        out_shape=[_sds((n_rows, HGRN_WIDTH), F32), _sds((n_rows, HGRN_WIDTH), BF16),
                   _sds((HGRN_HEADS, nb, 128, 128), F32)],
        scratch_shapes=[pltpu.VMEM((128, 128), F32)],
        compiler_params=_params(("parallel", "arbitrary")),
    )(proj, proj, proj, proj, lb, gain)


def _hgrn_bwd(name, proj, lb, gain, o_saved, states, dycat):
    n_rows = proj.shape[0]
    r = HGRN_BLOCK
    nb = n_rows // r
    nsub = r // HGRN_SUB
    proj_specs, lb_spec, gain_spec, row_spec, st_spec, blk = _hgrn_specs(nb, True)
    dy_spec = pl.BlockSpec((r, 128), lambda h, b: (blk(b), 8 + h))
    acc_spec = pl.BlockSpec((None, 1, 128), lambda h, b: (h, 0, 0))

    def body(q_ref, f_ref, i_ref, g_ref, lb_ref, gain_ref, o_ref, st_ref, dy_ref,
             dq_ref, df_ref, di_ref, dg_ref, dlb_ref, dgain_ref, dst_scr, sub_scr):
        @pl.when(pl.program_id(1) == 0)
        def _():
            dst_scr[...] = jnp.zeros_like(dst_scr)
            dlb_ref[...] = jnp.zeros_like(dlb_ref)
            dgain_ref[...] = jnp.zeros_like(dgain_ref)

        q, f, v, g = q_ref[...], f_ref[...], i_ref[...], g_ref[...]
        lbv, gain_v = lb_ref[...], gain_ref[...]
        sq = _sigmoid(q)
        qs = q * sq
        sf = _sigmoid(f)
        fg = lbv + (1.0 - lbv) * sf
        kk = 1.0 - fg
        r16 = lax.broadcasted_iota(jnp.int32, (r, 128), 0) & (HGRN_SUB - 1)
        b = _prefix16(jnp.log(fg), r16)
        eb = jnp.exp(b)
        qh = qs * eb

        o, dy = o_ref[...], dy_ref[...]
        rn = lax.rsqrt(jnp.mean(o * o, axis=1, keepdims=True) + EPS)
        on = o * rn
        sg = _sigmoid(g)
        sil = g * sg
        dgain_ref[...] += _colsum(dy * on * sil)
        dg_ref[...] = (dy * on * gain_v * (sg * (1.0 + g * (1.0 - sg)))).astype(BF16)
        don = dy * gain_v * sil
        do = rn * (don - on * jnp.mean(don * on, axis=1, keepdims=True))

        st = st_ref[...]
        for i in range(nsub):
            sl = slice(HGRN_SUB * i, HGRN_SUB * (i + 1))
            sub_scr[i] = st
            bi = b[sl]
            bl = bi[HGRN_SUB - 1:HGRN_SUB]
            st = st * jnp.exp(bl) + _dotf(v[sl], kk[sl] * jnp.exp(bl - bi), _TN)

        rs = lax.broadcasted_iota(jnp.int32, (HGRN_SUB, 128), 0)
        dst = dst_scr[...]
        parts = [None] * nsub
        for i in reversed(range(nsub)):
            sl = slice(HGRN_SUB * i, HGRN_SUB * (i + 1))
            sp = sub_scr[i]
            qsi, kki, vi, bi, doi, qhi = qs[sl], kk[sl], v[sl], b[sl], do[sl], qh[sl]
            bl = bi[HGRN_SUB - 1:HGRN_SUB]
            ebl = jnp.exp(bl)
            dec = jnp.exp(bl - bi)
            khat = kki * dec
            dqh = _dotf(doi, sp)
            dkhat = _dotf(vi, dst)
            dv = _dotf(khat, dst, _NT)
            zrow = _colsum(sp * dst) * ebl
            dq_in = jnp.zeros((HGRN_SUB, 128), F32)
            dk_in = jnp.zeros((HGRN_SUB, 128), F32)
            dv_in = jnp.zeros((HGRN_SUB, 128), F32)
            for s in range(HGRN_SUB):
                e = jnp.exp(jnp.where(rs >= s, bi - bi[s:s + 1], NEG))
                dpc = jnp.sum(doi * vi[s:s + 1], axis=1, keepdims=True)
                w = qsi * e
                pc = jnp.sum(w * kki[s:s + 1], axis=1, keepdims=True)
                dq_in = dq_in + dpc * e * kki[s:s + 1]
                dk_in = jnp.where(rs == s, _colsum(dpc * w), dk_in)
                dv_in = jnp.where(rs == s, _colsum(pc * doi), dv_in)
            kd = khat * dkhat
            parts[i] = (qsi * dq_in - kki * dk_in + qhi * dqh, kd, jnp.broadcast_to(zrow, (HGRN_SUB, 128)),
                        dq_in + dqh * eb[sl], dk_in + dkhat * dec, dv + dv_in)
            dst = dst * ebl + _dotf(doi, qhi, _TN)
        dst_scr[...] = dst

        cat = lambda j: jnp.concatenate([p[j] for p in parts], axis=0)
        d_b, kd, zr, dqs, dkk, dvv = (cat(j) for j in range(6))
        dlf = _suffix16(d_b, r16) + _prefix16(kd, r16) - kd + zr
        dfg = dlf / fg - dkk
        df_ref[...] = (dfg * (1.0 - lbv) * sf * (1.0 - sf)).astype(BF16)
        dlb_ref[...] += _colsum(dfg * (1.0 - sf))
        dq_ref[...] = (dqs * (sq * (1.0 + q * (1.0 - sq)))).astype(BF16)
        di_ref[...] = dvv.astype(BF16)

    return pl.pallas_call(
        body, name=name,
        grid=(HGRN_HEADS, nb),
        in_specs=proj_specs + [lb_spec, gain_spec, row_spec, st_spec, dy_spec],
        out_specs=[row_spec] * 4 + [acc_spec, acc_spec],
        out_shape=[_sds((n_rows, HGRN_WIDTH), BF16)] * 4 + [_sds((HGRN_HEADS, 1, 128), F32)] * 2,
        scratch_shapes=[pltpu.VMEM((128, 128), F32), pltpu.VMEM((nsub, 128, 128), F32)],
        compiler_params=_params(("parallel", "arbitrary")),
    )(proj, proj, proj, proj, lb, gain, o_saved, states, dycat)


def _alibi_slopes():
    return jnp.exp2(-8.0 * jnp.arange(1, ATT_HEADS + 1, dtype=F32) / ATT_HEADS)


def _swa_specs(n_blocks):
    blk = ATT_BLOCK
    prev = lambda i: jnp.maximum(i - 1, 0)
    smem = pl.BlockSpec(memory_space=pltpu.SMEM)
    return [
        smem, smem,
        pl.BlockSpec((blk, ATT_HEADS * ATT_DIM), lambda i: (i, 0)),
        pl.BlockSpec((blk, 256), lambda i: (i, 8)),
        pl.BlockSpec((blk, 256), lambda i: (prev(i), 8)),
        pl.BlockSpec((blk, 256), lambda i: (i, 9)),
        pl.BlockSpec((blk, 256), lambda i: (prev(i), 9)),
        pl.BlockSpec((1, ATT_DIM), lambda i: (0, 0)),
        pl.BlockSpec((1, ATT_DIM), lambda i: (0, 0)),
    ]


def _swa_mask(i):
    t_i = lax.broadcasted_iota(jnp.int32, (ATT_BLOCK, 2 * ATT_BLOCK), 0)
    s_i = lax.broadcasted_iota(jnp.int32, (ATT_BLOCK, 2 * ATT_BLOCK), 1)
    dist = t_i + ATT_BLOCK - s_i
    valid = (dist >= 0) & (dist < ATT_BLOCK) & ((s_i >= ATT_BLOCK) | (i > 0))
    return valid, dist.astype(F32)


def _swa_probs(qn, kn, slope, sink, valid, distf):
    s = lax.dot_general(qn, kn, _NT, preferred_element_type=F32) * (ATT_DIM ** -0.5) - slope * distf
    s = jnp.where(valid, s, NEG)
    m = jnp.maximum(jnp.max(s, axis=1, keepdims=True), sink)
    p = jnp.exp(s - m)
    es = jnp.exp(sink - m)
    inv = 1.0 / (jnp.sum(p, axis=1, keepdims=True) + es)
    return p * inv, es * inv


def _swa_fwd(name, qkv, q_gain, k_gain, sinks, slopes):
    n_rows = qkv.shape[0]
    nb = n_rows // ATT_BLOCK

    def body(sink_ref, slope_ref, q_ref, kc_ref, kp_ref, vc_ref, vp_ref, qg_ref, kg_ref, o_ref, kn_scr, v_scr):
        i = pl.program_id(0)
        kb = jnp.concatenate([kp_ref[...], kc_ref[...]], axis=0)
        vb = jnp.concatenate([vp_ref[...], vc_ref[...]], axis=0)
        for g in range(ATT_KV):
            kg = kb[:, 64 * g:64 * (g + 1)]
            rk = lax.rsqrt(jnp.mean(kg * kg, axis=1, keepdims=True) + EPS)
            kn_scr[g] = (kg * rk * kg_ref[...]).astype(BF16)
            v_scr[g] = vb[:, 64 * g:64 * (g + 1)].astype(BF16)
        valid, distf = _swa_mask(i)
        qgv = qg_ref[...]

        def pair(j, carry):
            g = j // 4
            kn, vv = kn_scr[g], v_scr[g]
            off = pl.multiple_of(j * 128, 128)
            qq = q_ref[:, pl.ds(off, 128)]
            outs = []
            for hh in range(2):
                h = 2 * j + hh
                qh = qq[:, 64 * hh:64 * (hh + 1)]
                rq = lax.rsqrt(jnp.mean(qh * qh, axis=1, keepdims=True) + EPS)
                pn, _ = _swa_probs((qh * rq * qgv).astype(BF16), kn, slope_ref[h], sink_ref[h], valid, distf)
                outs.append(jnp.dot(pn.astype(BF16), vv, preferred_element_type=F32))
            o_ref[:, pl.ds(off, 128)] = jnp.concatenate(outs, axis=1).astype(BF16)
            return carry

        lax.fori_loop(0, ATT_HEADS // 2, pair, 0)

    return pl.pallas_call(
        body, name=name,
        grid=(nb,),
        in_specs=_swa_specs(nb),
        out_specs=pl.BlockSpec((ATT_BLOCK, ATT_HEADS * ATT_DIM), lambda i: (i, 0)),
        out_shape=_sds((n_rows, ATT_HEADS * ATT_DIM), BF16),
        scratch_shapes=[pltpu.VMEM((ATT_KV, 2 * ATT_BLOCK, ATT_DIM), BF16)] * 2,
        compiler_params=_params(("arbitrary",)),
    )(sinks, slopes, qkv, qkv, qkv, qkv, qkv, q_gain.reshape(1, -1), k_gain.reshape(1, -1))


def _swa_bwd(name, qkv, q_gain, k_gain, sinks, slopes, d_out):
    n_rows = qkv.shape[0]
    nb = n_rows // ATT_BLOCK
    blk = ATT_BLOCK

    def body(sink_ref, slope_ref, q_ref, kc_ref, kp_ref, vc_ref, vp_ref, qg_ref, kg_ref, do_ref,
             dq_ref, dkc_ref, dkp_ref, dvc_ref, dvp_ref, dsink_ref, dqg_ref, dkg_ref,
             kn_scr, v_scr, dkn_scr, dv_scr):
        i = pl.program_id(0)

        @pl.when(i == 0)
        def _():
            dsink_ref[...] = jnp.zeros_like(dsink_ref)
            dqg_ref[...] = jnp.zeros_like(dqg_ref)
            dkg_ref[...] = jnp.zeros_like(dkg_ref)

        kb = jnp.concatenate([kp_ref[...], kc_ref[...]], axis=0)
        vb = jnp.concatenate([vp_ref[...], vc_ref[...]], axis=0)
        kgv, qgv = kg_ref[...], qg_ref[...]
        for g in range(ATT_KV):
            kg = kb[:, 64 * g:64 * (g + 1)]
            rk = lax.rsqrt(jnp.mean(kg * kg, axis=1, keepdims=True) + EPS)
            kn_scr[g] = (kg * rk * kgv).astype(BF16)
            v_scr[g] = vb[:, 64 * g:64 * (g + 1)].astype(BF16)
        dkn_scr[...] = jnp.zeros_like(dkn_scr)
        dv_scr[...] = jnp.zeros_like(dv_scr)
        valid, distf = _swa_mask(i)
        scale = ATT_DIM ** -0.5

        def pair(j, dqg):
            g = j // 4
            kn, vv = kn_scr[g], v_scr[g]
            off = pl.multiple_of(j * 128, 128)
            qq = q_ref[:, pl.ds(off, 128)]
            dd = do_ref[:, pl.ds(off, 128)]
            outs = []
            for hh in range(2):
                h = 2 * j + hh
                qh = qq[:, 64 * hh:64 * (hh + 1)]
                rq = lax.rsqrt(jnp.mean(qh * qh, axis=1, keepdims=True) + EPS)
                qhat = qh * rq
                qn = (qhat * qgv).astype(BF16)
                pn, ps = _swa_probs(qn, kn, slope_ref[h], sink_ref[h], valid, distf)
                doh = dd[:, 64 * hh:64 * (hh + 1)].astype(BF16)
                dp = lax.dot_general(doh, vv, _NT, preferred_element_type=F32)
                delta = jnp.sum(pn * dp, axis=1, keepdims=True)
                ds = (pn * (dp - delta)).astype(BF16)
                dsink_ref[pl.ds(h, 1), :] += jnp.zeros((1, 128), F32) - jnp.sum(ps * delta)
                dv_scr[g] += lax.dot_general(pn.astype(BF16), doh, _TN, preferred_element_type=F32)
                dkn_scr[g] += lax.dot_general(ds, qn, _TN, preferred_element_type=F32) * scale
                dqn = jnp.dot(ds, kn, preferred_element_type=F32) * scale
                dqg = dqg + _colsum(dqn * qhat)
                dqhat = dqn * qgv
                outs.append(rq * (dqhat - qhat * jnp.mean(dqhat * qhat, axis=1, keepdims=True)))
            dq_ref[:, pl.ds(off, 128)] = jnp.concatenate(outs, axis=1).astype(BF16)
            return dqg

        dqg_ref[...] += lax.fori_loop(0, ATT_HEADS // 2, pair, jnp.zeros((1, ATT_DIM), F32))

        dks, dkg = [], jnp.zeros((1, ATT_DIM), F32)
        for g in range(ATT_KV):
            kg = kb[:, 64 * g:64 * (g + 1)]
            rk = lax.rsqrt(jnp.mean(kg * kg, axis=1, keepdims=True) + EPS)
            khat = kg * rk
            dkn = dkn_scr[g]
            dkg = dkg + _colsum(dkn * khat)
            dkhat = dkn * kgv
            dks.append(rk * (dkhat - khat * jnp.mean(dkhat * khat, axis=1, keepdims=True)))
        dkg_ref[...] += dkg
        dk = jnp.concatenate(dks, axis=1).astype(BF16)
        dv = jnp.concatenate([dv_scr[g] for g in range(ATT_KV)], axis=1).astype(BF16)
        dkp_ref[...] = dk[:blk]
        dkc_ref[...] = dk[blk:]
        dvp_ref[...] = dv[:blk]
        dvc_ref[...] = dv[blk:]

    kv_spec = pl.BlockSpec((blk, 256), lambda i: (i, 0))
    full = pl.BlockSpec((blk, ATT_HEADS * ATT_DIM), lambda i: (i, 0))
    acc64 = pl.BlockSpec((1, ATT_DIM), lambda i: (0, 0))
    return pl.pallas_call(
        body, name=name,
        grid=(nb,),
        in_specs=_swa_specs(nb) + [full],
        out_specs=[full, kv_spec, kv_spec, kv_spec, kv_spec,
                   pl.BlockSpec((ATT_HEADS, 128), lambda i: (0, 0)), acc64, acc64],
        out_shape=[_sds((n_rows, ATT_HEADS * ATT_DIM), BF16)] + [_sds((n_rows, 256), BF16)] * 4
        + [_sds((ATT_HEADS, 128), F32), _sds((1, ATT_DIM), F32), _sds((1, ATT_DIM), F32)],
        scratch_shapes=[pltpu.VMEM((ATT_KV, 2 * blk, ATT_DIM), BF16)] * 2
        + [pltpu.VMEM((ATT_KV, 2 * blk, ATT_DIM), F32)] * 2,
        compiler_params=_params(("arbitrary",)),
    )(sinks, slopes, qkv, qkv, qkv, qkv, qkv, q_gain.reshape(1, -1), k_gain.reshape(1, -1), d_out)


def _mesh_pos():
    return lax.axis_index("x"), lax.axis_index("y"), lax.axis_index("c")


def _allgather(name, shard):
    def body(x_ref, out_ref, send_sems, recv_sems, local_sem):
        x, y, c = _mesh_pos()
        me, sibling = (x, y, c), (x, y, 1 - c)
        chips = [(1 - x, y), (x, 1 - y), (1 - x, 1 - y)]

        def slot(px, py, pc):
            return out_ref.at[4 * px + 2 * py + pc]

        def copy(k, block, to, src=None):
            return pltpu.make_async_remote_copy(
                src_ref=slot(*block) if src is None else src, dst_ref=slot(*block),
                send_sem=send_sems.at[k], recv_sem=recv_sems.at[k], device_id=to, device_id_type=MESH)

        mine = pltpu.make_async_copy(x_ref, slot(*me), local_sem)
        mine.start()
        first = [copy(0, me, sibling, src=x_ref)]
        first += [copy(1 + j, me, (*chip, c), src=x_ref) for j, chip in enumerate(chips)]
        for cp in first:
            cp.start()
        passed = [copy(4 + j, (*chip, c), sibling) for j, chip in enumerate(chips)]
        for j, chip in enumerate(chips):
            copy(1 + j, (*chip, c), me).wait_recv()
            passed[j].start()
        copy(0, sibling, me).wait_recv()
        for j, chip in enumerate(chips):
            copy(4 + j, (*chip, 1 - c), me).wait_recv()
        for cp in first + passed:
            cp.wait_send()
        mine.wait()

    return pl.pallas_call(
        body, name=name,
        out_shape=_sds((N_DEV,) + shard.shape, shard.dtype),
        in_specs=[pl.BlockSpec(memory_space=pl.ANY)],
        out_specs=pl.BlockSpec(memory_space=pl.ANY),
        scratch_shapes=[pltpu.SemaphoreType.DMA((7,)), pltpu.SemaphoreType.DMA((7,)), pltpu.SemaphoreType.DMA],
    )(shard)


def _exchange(name, blocks):
    def body(x_ref, out_ref, send_sems, recv_sems, local_sem):
        x, y, c = _mesh_pos()
        me = 4 * x + 2 * y + c
        mine = pltpu.make_async_copy(x_ref.at[me], out_ref.at[me], local_sem)
        mine.start()
        copies = []
        for k in range(1, N_DEV):
            px, py, pc = x ^ ((k >> 2) & 1), y ^ ((k >> 1) & 1), c ^ (k & 1)
            peer = 4 * px + 2 * py + pc
            send = pltpu.make_async_remote_copy(
                src_ref=x_ref.at[peer], dst_ref=out_ref.at[me],
                send_sem=send_sems.at[k - 1], recv_sem=recv_sems.at[k - 1],
                device_id=(px, py, pc), device_id_type=MESH)
            recv = pltpu.make_async_remote_copy(
                src_ref=x_ref.at[peer], dst_ref=out_ref.at[peer],
                send_sem=send_sems.at[k - 1], recv_sem=recv_sems.at[k - 1],
                device_id=(px, py, pc), device_id_type=MESH)
            send.start()
            copies.append((send, recv))
        for send, recv in copies:
            recv.wait_recv()
        for send, recv in copies:
            send.wait_send()
        mine.wait()

    return pl.pallas_call(
        body, name=name,
        out_shape=_sds(blocks.shape, blocks.dtype),
        in_specs=[pl.BlockSpec(memory_space=pl.ANY)],
        out_specs=pl.BlockSpec(memory_space=pl.ANY),
        scratch_shapes=[pltpu.SemaphoreType.DMA((7,)), pltpu.SemaphoreType.DMA((7,)), pltpu.SemaphoreType.DMA],
    )(blocks)


def _sum8(name, blocks, tr=512):
    _, n_rows, n_cols = blocks.shape
    tr = min(tr, n_rows)
    assert n_rows % tr == 0

    def body(x_ref, o_ref):
        acc = x_ref[0].astype(F32)
        for s in range(1, N_DEV):
            acc = acc + x_ref[s].astype(F32)
        o_ref[...] = acc

    return pl.pallas_call(
        body, name=name,
        grid=(n_rows // tr,),
        in_specs=[pl.BlockSpec((N_DEV, tr, n_cols), lambda i: (0, i, 0))],
        out_specs=pl.BlockSpec((tr, n_cols), lambda i: (i, 0)),
        out_shape=_sds((n_rows, n_cols), F32),
        compiler_params=_params(("parallel",)),
    )(blocks)


_BIG = (
    ("even_w_in", 2, 2, D_MODEL, 5120),
    ("s5_w_glu", 1, 2, S5_WIDTH, S5_WIDTH),
    ("even_w_out", 1, 2, D_MODEL, D_MODEL),
    ("odd_w_qkv", 2, 2, D_MODEL, QKV_WIDTH),
    ("odd_w_out", 1, 2, D_MODEL, D_MODEL),
    ("mlp_w_up", 2, DEPTH, D_MODEL, D_FF),
    ("mlp_w_down", 1, DEPTH, D_FF, D_MODEL),
)
_PACK_COLS = 1024


def _big_rows(nl, rows, cols):
    return nl * rows * cols // N_DEV // _PACK_COLS


def _unpack_gathered(gathered):
    out, off = {}, 0
    for name, axis, nl, rows, cols in _BIG:
        n = _big_rows(nl, rows, cols)
        part = gathered[:, off:off + n]
        off += n
        if axis == 2:
            w = part.reshape(N_DEV, nl, rows, cols // N_DEV).transpose(1, 2, 0, 3)
        else:
            w = part.reshape(N_DEV, nl, rows // N_DEV, cols).transpose(1, 0, 2, 3)
        out[name] = w.reshape(nl, rows, cols)
    return out


def _pack_by_owner(full):
    parts = []
    for name, axis, nl, rows, cols in _BIG:
        g = full[name].astype(BF16)
        if axis == 2:
            g = g.reshape(nl, rows, N_DEV, cols // N_DEV).transpose(2, 0, 1, 3)
        else:
            g = g.reshape(nl, N_DEV, rows // N_DEV, cols).transpose(1, 0, 2, 3)
        parts.append(g.reshape(N_DEV, -1, _PACK_COLS))
    return jnp.concatenate(parts, axis=1)


def _pack_small(arrs, row_mult=512):
    parts = []
    for a in arrs:
        f = a.astype(F32).reshape(-1)
        parts.append(jnp.pad(f, (0, (-f.shape[0]) % 128)))
    f = jnp.concatenate(parts)
    f = jnp.pad(f, (0, (-f.shape[0]) % (128 * row_mult)))
    return f.reshape(-1, 128)


def _unpack_small(flat, shapes):
    f = flat.reshape(-1)
    out, off = [], 0
    for s in shapes:
        n = math.prod(s)
        out.append(f[off:off + n].reshape(s))
        off += n + (-n) % 128
    return out


_WEIGHTS = ("even_norm", "even_w_in", "s5_lambda_re", "s5_lambda_im", "s5_log_dt", "s5_b_re", "s5_b_im",
            "s5_c_re", "s5_c_im", "s5_d", "s5_w_glu", "s5_b_glu", "hgrn_lower_bound", "hgrn_o_norm",
            "even_w_out", "odd_norm", "odd_w_qkv", "q_norm", "k_norm", "att_sinks", "odd_w_out",
            "mlp_norm", "mlp_w_up", "mlp_w_down")
_BIG_NAMES = tuple(b[0] for b in _BIG)
_SMALL_NAMES = tuple(n for n in _WEIGHTS if n not in _BIG_NAMES)


def _add_res(acc, res):
    return (acc + res,)


def _mlp_fwd(h, gain, w_up, w_down):
    xn, rstd = _rms_fwd("rms_fwd", h, gain)
    up, act = _mm("mm_up", xn, w_up, "nn", out_dtypes=(F32, BF16),
                  epi=lambda acc: (acc, jnp.square(jnp.maximum(acc, 0.0))))
    out = _mm("mm_down", act, w_down, "nn", epi=_add_res, extras=(h,))
    return out, (h, gain, xn, rstd, up, act, w_up, w_down)


def _mlp_bwd(cache, dh, dhb):
    h, gain, xn, rstd, up, act, w_up, w_down = cache
    dup = _mm("mm_dact", dhb, w_down, "nt", out_dtypes=(BF16,),
              epi=lambda acc, u: (acc * (2.0 * jnp.maximum(u, 0.0)),), extras=(up,))
    dw_down = _mm("mm_dw_down", act, dhb, "tn", tk=512)
    dxn = _mm("mm_dxn_up", dup, w_up, "nt")
    dw_up = _mm("mm_dw_up", xn, dup, "tn", tk=512)
    dh_in, dhb_in, dgain = _rms_bwd("rms_bwd", h, rstd, gain, dxn, dh)
    return dh_in, dhb_in, dgain, dw_up, dw_down


def _even_fwd(h, p):
    xn, rstd = _rms_fwd("rms_fwd", h, p["norm"])
    proj = _mm("mm_w_in", xn, p["w_in"], "nn")
    y_pre, z, s5_states = _s5_fwd("s5_fwd", proj, p["mats"])
    gate = _mm("mm_glu", z, p["w_glu"], "nn")
    (ya,) = _rowwise("glu_fwd", lambda y, gt, b: ((_gelu(y) * _sigmoid(gt + b),), ()),
                     [y_pre, gate], [p["b_glu"].reshape(1, -1)], [(S5_WIDTH, BF16)])
    o, yb, h_states = _hgrn_fwd("hgrn_fwd", proj, p["lb"].reshape(8, 1, 128), p["o_gain"].reshape(1, 128))
    ycat = jnp.concatenate([ya, yb], axis=1)
    out = _mm("mm_w_out", ycat, p["w_out"], "nn", epi=_add_res, extras=(h,))
    return out, (h, xn, rstd, proj, y_pre, z, s5_states, gate, o, h_states, ycat)


def _even_bwd(cache, p, dh, dhb):
    h, xn, rstd, proj, y_pre, z, s5_states, gate, o, h_states, ycat = cache
    g = {}
    dycat = _mm("mm_dy_out", dhb, p["w_out"], "nt")
    g["w_out"] = _mm("mm_dw_out", ycat, dhb, "tn", tk=512)
    dq, df, di, dg, dlb, dgain = _hgrn_bwd("hgrn_bwd", proj, p["lb"].reshape(8, 1, 128),
                                           p["o_gain"].reshape(1, 128), o, h_states, dycat)
    g["lb"] = dlb.reshape(-1)
    g["o_gain"] = jnp.sum(dgain, axis=0).reshape(-1)

    def glu_bwd1(dyc, y, gt, b):
        zf = _gelu(y)
        s = _sigmoid(gt + b)
        dya = dyc[:, :S5_WIDTH]
        d_gate = dya * zf * s * (1.0 - s)
        return (d_gate, dya * s), (_colsum(d_gate),)

    d_gate, dz_direct, db_glu = _rowwise("glu_bwd_gate", glu_bwd1, [dycat, y_pre, gate], [p["b_glu"].reshape(1, -1)],
                                         [(S5_WIDTH, BF16), (S5_WIDTH, F32)], [S5_WIDTH])
    g["b_glu"] = db_glu.reshape(-1)
    dz_gate = _mm("mm_dz_glu", d_gate, p["w_glu"], "nt")
    g["w_glu"] = _mm("mm_dw_glu", z, d_gate, "tn", tk=512)
    (dy_pre,) = _rowwise("glu_bwd_gelu", lambda a, b, y: (((a + b) * _gelu_grad(y),), ()),
                         [dz_direct, dz_gate, y_pre], [], [(S5_WIDTH, F32)])
    du, dbm, dcm, da, dd = _s5_bwd("s5_bwd", proj, dy_pre, s5_states, p["mats"])
    g["s5"] = (dbm, dcm, da, dd)
    dproj = jnp.concatenate([du, dq, df, di, dg], axis=1)
    dxn = _mm("mm_dxn_in", dproj, p["w_in"], "nt", tk=2560)
    g["w_in"] = _mm("mm_dw_in", xn, dproj, "tn", tk=512)
    dh_in, dhb_in, dnorm = _rms_bwd("rms_bwd", h, rstd, p["norm"], dxn, dh)
    g["norm"] = dnorm.reshape(-1)
    return dh_in, dhb_in, g


def _odd_fwd(h, p):
    xn, rstd = _rms_fwd("rms_fwd", h, p["norm"])
    qkv = _mm("mm_w_qkv", xn, p["w_qkv"], "nn", tn=1280)
    o = _swa_fwd("swa_fwd", qkv, p["q_gain"], p["k_gain"], p["sinks"], p["slopes"])
    out = _mm("mm_w_out", o, p["w_out"], "nn", epi=_add_res, extras=(h,))
    return out, (h, xn, rstd, qkv, o)


def _shift_up_block(x):
    return jnp.concatenate([x[ATT_BLOCK:], jnp.zeros((ATT_BLOCK, x.shape[1]), x.dtype)], axis=0)


def _odd_bwd(cache, p, dh, dhb):
    h, xn, rstd, qkv, o = cache
    g = {}
    d_o = _mm("mm_dy_out", dhb, p["w_out"], "nt")
    g["w_out"] = _mm("mm_dw_out", o, dhb, "tn", tk=512)
    dq, dkc, dkp, dvc, dvp, dsink, dqg, dkg = _swa_bwd("swa_bwd", qkv, p["q_gain"], p["k_gain"], p["sinks"],
                                                       p["slopes"], d_o)
    dk = (dkc.astype(F32) + _shift_up_block(dkp).astype(F32)).astype(BF16)
    dv = (dvc.astype(F32) + _shift_up_block(dvp).astype(F32)).astype(BF16)
    g["sinks"], g["q_gain"], g["k_gain"] = dsink[:, 0], dqg.reshape(-1), dkg.reshape(-1)
    dqkv = jnp.concatenate([dq, dk, dv], axis=1)
    dxn = _mm("mm_dxn_qkv", dqkv, p["w_qkv"], "nt", tk=1280)
    g["w_qkv"] = _mm("mm_dw_qkv", xn, dqkv, "tn", tn=1280, tk=512)
    dh_in, dhb_in, dnorm = _rms_bwd("rms_bwd", h, rstd, p["norm"], dxn, dh)
    g["norm"] = dnorm.reshape(-1)
    return dh_in, dhb_in, g


def kernel(x, even_norm, even_w_in, s5_lambda_re, s5_lambda_im, s5_log_dt, s5_b_re, s5_b_im, s5_c_re, s5_c_im, s5_d, s5_w_glu, s5_b_glu, hgrn_lower_bound, hgrn_o_norm, even_w_out, odd_norm, odd_w_qkv, q_norm, k_norm, att_sinks, odd_w_out, mlp_norm, mlp_w_up, mlp_w_down, loss_target, m_even_norm, m_even_w_in, m_s5_lambda_re, m_s5_lambda_im, m_s5_log_dt, m_s5_b_re, m_s5_b_im, m_s5_c_re, m_s5_c_im, m_s5_d, m_s5_w_glu, m_s5_b_glu, m_hgrn_lower_bound, m_hgrn_o_norm, m_even_w_out, m_odd_norm, m_odd_w_qkv, m_q_norm, m_k_norm, m_att_sinks, m_odd_w_out, m_mlp_norm, m_mlp_w_up, m_mlp_w_down, v_even_norm, v_even_w_in, v_s5_lambda_re, v_s5_lambda_im, v_s5_log_dt, v_s5_b_re, v_s5_b_im, v_s5_c_re, v_s5_c_im, v_s5_d, v_s5_w_glu, v_s5_b_glu, v_hgrn_lower_bound, v_hgrn_o_norm, v_even_w_out, v_odd_norm, v_odd_w_qkv, v_q_norm, v_k_norm, v_att_sinks, v_odd_w_out, v_mlp_norm, v_mlp_w_up, v_mlp_w_down):
    a = dict(locals())
    n_rows = x.shape[1]
    xi, yi, ci = _mesh_pos()
    me = 4 * xi + 2 * yi + ci

    shard = jnp.concatenate([a[n].astype(BF16).reshape(-1, _PACK_COLS) for n in _BIG_NAMES], axis=0)
    w = _unpack_gathered(_allgather("gather_weights", shard))
    odd_shard = jnp.pad(odd_norm, ((0, 6), (0, 0)))
    odd_norm_full = _allgather("gather_odd_norm", odd_shard)[:, :2].transpose(1, 0, 2).reshape(2, D_MODEL)

    lower_bounds, lb_vjp = jax.vjp(_hgrn_lower_bounds, hgrn_lower_bound)
    slopes = _alibi_slopes()
    even_p, odd_p, s5_vjps = [], [], []
    for j in range(2):
        disc, vjp = jax.vjp(_s5_discretize, s5_lambda_re[j], s5_lambda_im[j], s5_log_dt[j], s5_b_re[j], s5_b_im[j])
        s5_vjps.append(vjp)
        even_p.append(dict(norm=even_norm[j], w_in=w["even_w_in"][j], w_glu=w["s5_w_glu"][j], b_glu=s5_b_glu[j],
                           mats=_s5_matrices(*disc, s5_c_re[j], s5_c_im[j], s5_d[j]),
                           lb=lower_bounds[j], o_gain=hgrn_o_norm[j], w_out=w["even_w_out"][j]))
        odd_p.append(dict(norm=odd_norm_full[j], w_qkv=w["odd_w_qkv"][j], q_gain=q_norm[j], k_gain=k_norm[j],
                          sinks=att_sinks[j], slopes=slopes, w_out=w["odd_w_out"][j]))

    h = x.reshape(n_rows, D_MODEL)
    caches = []
    for layer in range(DEPTH):
        j = layer // 2
        h, c_mix = (_even_fwd(h, even_p[j]) if layer % 2 == 0 else _odd_fwd(h, odd_p[j]))
        h, c_mlp = _mlp_fwd(h, mlp_norm[layer], w["mlp_w_up"][layer], w["mlp_w_down"][layer])
        caches.append((c_mix, c_mlp))
    dh, dhb, sq = _loss_head(h, loss_target.reshape(n_rows, D_MODEL))
    loss = lax.psum(0.5 * sq[0, 0] / D_MODEL, ("x", "y", "c"))

    mix_g, mlp_g = [None] * DEPTH, [None] * DEPTH
    for layer in reversed(range(DEPTH)):
        j = layer // 2
        c_mix, c_mlp = caches[layer]
        dh, dhb, d_mlp_norm, dw_up, dw_down = _mlp_bwd(c_mlp, dh, dhb)
        mlp_g[layer] = (d_mlp_norm.reshape(-1), dw_up, dw_down)
        if layer % 2 == 0:
            dh, dhb, mix_g[layer] = _even_bwd(c_mix, even_p[j], dh, dhb)
        else:
            dh, dhb, mix_g[layer] = _odd_bwd(c_mix, odd_p[j], dh, dhb)
    grad_x = dh.reshape(x.shape)

    ev, od = [mix_g[0], mix_g[2]], [mix_g[1], mix_g[3]]
    full = {
        "even_w_in": jnp.stack([g["w_in"] for g in ev]), "s5_w_glu": jnp.stack([g["w_glu"] for g in ev]),
        "even_w_out": jnp.stack([g["w_out"] for g in ev]), "odd_w_qkv": jnp.stack([g["w_qkv"] for g in od]),
        "odd_w_out": jnp.stack([g["w_out"] for g in od]), "mlp_w_up": jnp.stack([g[1] for g in mlp_g]),
        "mlp_w_down": jnp.stack([g[2] for g in mlp_g]),
    }
    big_flat = _sum8("sum_big_grads", _exchange("exchange_big_grads", _pack_by_owner(full)))
    grads, off = {}, 0
    for name, axis, nl, rows, cols in _BIG:
        n = _big_rows(nl, rows, cols)
        grads[name] = big_flat[off:off + n].reshape(a[name].shape)
        off += n

    s5_g = []
    for j in range(2):
        dar, dai, dbbr, dbbi, dcr, dci, dd = _s5_unpack_grads(*ev[j]["s5"])
        s5_g.append(tuple(s5_vjps[j]((dar, dai, dbbr, dbbi))) + (dcr, dci, dd))
    (d_lb_param,) = lb_vjp(jnp.stack([g["lb"] for g in ev]))
    small = {
        "even_norm": jnp.stack([g["norm"] for g in ev]),
        "s5_lambda_re": jnp.stack([g[0] for g in s5_g]), "s5_lambda_im": jnp.stack([g[1] for g in s5_g]),
        "s5_log_dt": jnp.stack([g[2] for g in s5_g]), "s5_b_re": jnp.stack([g[3] for g in s5_g]),
        "s5_b_im": jnp.stack([g[4] for g in s5_g]), "s5_c_re": jnp.stack([g[5] for g in s5_g]),
        "s5_c_im": jnp.stack([g[6] for g in s5_g]), "s5_d": jnp.stack([g[7] for g in s5_g]),
        "s5_b_glu": jnp.stack([g["b_glu"] for g in ev]), "hgrn_lower_bound": d_lb_param,
        "hgrn_o_norm": jnp.stack([g["o_gain"] for g in ev]), "odd_norm": jnp.stack([g["norm"] for g in od]),
        "q_norm": jnp.stack([g["q_gain"] for g in od]), "k_norm": jnp.stack([g["k_gain"] for g in od]),
        "att_sinks": jnp.stack([g["sinks"] for g in od]), "mlp_norm": jnp.stack([g[0] for g in mlp_g]),
    }
    small_shapes = [small[n].shape for n in _SMALL_NAMES]
    small_sum = _sum8("sum_small_grads", _allgather("gather_small_grads", _pack_small([small[n] for n in _SMALL_NAMES])))
    for n, g in zip(_SMALL_NAMES, _unpack_small(small_sum, small_shapes)):
        grads[n] = g
    grads["odd_norm"] = lax.dynamic_slice_in_dim(grads["odd_norm"], me * (D_MODEL // N_DEV), D_MODEL // N_DEV, axis=1)

    delta, new_m, new_v = {}, {}, {}
    for name in _BIG_NAMES:
        to2d = lambda t, c=a[name].shape[-1]: t.reshape(-1, c)
        d_, m_, v_ = _adamw("adamw_" + name, to2d(a[name]), to2d(grads[name]), to2d(a["m_" + name]), to2d(a["v_" + name]))
        delta[name], new_m[name], new_v[name] = (t.reshape(a[name].shape) for t in (d_, m_, v_))
    packed = [_pack_small([src[n] for n in _SMALL_NAMES])
              for src in (a, grads, {n: a["m_" + n] for n in _SMALL_NAMES}, {n: a["v_" + n] for n in _SMALL_NAMES})]
    shapes = [a[n].shape for n in _SMALL_NAMES]
    for dst, flat in zip((delta, new_m, new_v), _adamw("adamw_small", *packed)):
        for n, t in zip(_SMALL_NAMES, _unpack_small(flat, shapes)):
            dst[n] = t

    return (loss, grad_x, *[grads[n] for n in _WEIGHTS], *[delta[n] for n in _WEIGHTS],
            *[new_m[n] for n in _WEIGHTS], *[new_v[n] for n in _WEIGHTS])
```

```python
import math

import jax
import jax.numpy as jnp
from jax import lax
from jax.experimental import pallas as pl
from jax.experimental.pallas import tpu as pltpu

F32 = jnp.float32
BF16 = jnp.bfloat16
MESH = pl.DeviceIdType.MESH

D_MODEL = 2048
DEPTH = 4
EPS = 1e-6
S5_WIDTH = 1024
S5_GROUPS = 64
S5_STATE = 64
S5_GROUP_SIZE = 16
S5_MIN_DECAY = 1e-4
S5_CHUNK = 128
S5_LEVELS = 7
HGRN_WIDTH = 1024
HGRN_HEADS = 8
HGRN_DIM = 128
HGRN_SUB = 16
HGRN_BLOCK = 128
ATT_HEADS = 32
ATT_KV = 4
ATT_DIM = 64
ATT_BLOCK = 128
QKV_WIDTH = (ATT_HEADS + 2 * ATT_KV) * ATT_DIM
D_FF = 4 * D_MODEL
N_DEV = 8
NEG = -1e30
VMEM_LIMIT = 56 * 1024 * 1024

ADAM_LR, ADAM_B1, ADAM_B2, ADAM_EPS, ADAM_WD, ADAM_STEP = 0.001, 0.9, 0.999, 1e-08, 0.01, 10


def _params(sem=None):
    return pltpu.CompilerParams(dimension_semantics=sem, vmem_limit_bytes=VMEM_LIMIT)


def _sds(shape, dtype):
    return jax.ShapeDtypeStruct(shape, dtype)


def _mm(name, a, b, mode, out_dtypes=(F32,), epi=None, extras=(), tm=512, tn=1024, tk=2048):
    if mode == "nn":
        (m, k), n = a.shape, b.shape[1]
    elif mode == "nt":
        (m, k), n = a.shape, b.shape[0]
    else:
        (k, m), n = a.shape, b.shape[1]
    tm, tn, tk = min(tm, m), min(tn, n), min(tk, k)
    assert m % tm == 0 and n % tn == 0 and k % tk == 0, (name, m, n, k)
    nk = k // tk
    if mode == "nn":
        a_spec = pl.BlockSpec((tm, tk), lambda i, j, kk: (i, kk))
        b_spec = pl.BlockSpec((tk, tn), lambda i, j, kk: (kk, j))
        dims = (((1,), (0,)), ((), ()))
    elif mode == "nt":
        a_spec = pl.BlockSpec((tm, tk), lambda i, j, kk: (i, kk))
        b_spec = pl.BlockSpec((tn, tk), lambda i, j, kk: (j, kk))
        dims = (((1,), (1,)), ((), ()))
    else:
        a_spec = pl.BlockSpec((tk, tm), lambda i, j, kk: (kk, i))
        b_spec = pl.BlockSpec((tk, tn), lambda i, j, kk: (kk, j))
        dims = (((0,), (0,)), ((), ()))
    o_spec = pl.BlockSpec((tm, tn), lambda i, j, kk: (i, j))
    n_ex, n_out = len(extras), len(out_dtypes)

    def body(*refs):
        a_ref, b_ref = refs[0], refs[1]
        ex_refs = refs[2:2 + n_ex]
        out_refs = refs[2 + n_ex:2 + n_ex + n_out]
        acc_ref = refs[2 + n_ex + n_out] if nk > 1 else None
        av, bv = a_ref[...], b_ref[...]
        if av.dtype != BF16:
            av = av.astype(BF16)
        if bv.dtype != BF16:
            bv = bv.astype(BF16)
        part = lax.dot_general(av, bv, dims, preferred_element_type=F32)

        def finish(acc):
            outs = epi(acc, *[r[...] for r in ex_refs]) if epi is not None else (acc,)
            for r, o in zip(out_refs, outs):
                r[...] = o.astype(r.dtype)

        if nk == 1:
            finish(part)
        else:
            kk = pl.program_id(2)

            @pl.when(kk == 0)
            def _():
                acc_ref[...] = part

            @pl.when(kk > 0)
            def _():
                acc_ref[...] += part

            @pl.when(kk == nk - 1)
            def _():
                finish(acc_ref[...])

    outs = pl.pallas_call(
        body, name=name,
        grid=(m // tm, n // tn, nk),
        in_specs=[a_spec, b_spec] + [o_spec] * n_ex,
        out_specs=[o_spec] * n_out,
        out_shape=[_sds((m, n), dt) for dt in out_dtypes],
        scratch_shapes=[pltpu.VMEM((tm, tn), F32)] if nk > 1 else [],
        compiler_params=_params(("parallel", "parallel", "arbitrary")),
    )(a, b, *extras)
    return outs[0] if n_out == 1 else outs


def _rowwise(name, fn, rows, vecs, outs, accs=(), tr=256):
    n_rows = rows[0].shape[0]
    tr = min(tr, n_rows)
    assert n_rows % tr == 0
    n_r, n_v, n_o, n_a = len(rows), len(vecs), len(outs), len(accs)

    def body(*refs):
        ins = [r[...] for r in refs[:n_r + n_v]]
        o_refs = refs[n_r + n_v:n_r + n_v + n_o]
        a_refs = refs[n_r + n_v + n_o:]
        ro, ao = fn(*ins)
        for r, o in zip(o_refs, ro):
            r[...] = o.astype(r.dtype)
        if n_a:
            step = pl.program_id(0)

            @pl.when(step == 0)
            def _():
                for r, o in zip(a_refs, ao):
                    r[...] = o

            @pl.when(step > 0)
            def _():
                for r, o in zip(a_refs, ao):
                    r[...] += o

    res = pl.pallas_call(
        body, name=name,
        grid=(n_rows // tr,),
        in_specs=[pl.BlockSpec((tr, r.shape[1]), lambda i: (i, 0)) for r in rows]
        + [pl.BlockSpec(v.shape, lambda i: (0, 0)) for v in vecs],
        out_specs=[pl.BlockSpec((tr, w), lambda i: (i, 0)) for w, _ in outs]
        + [pl.BlockSpec((1, w), lambda i: (0, 0)) for w in accs],
        out_shape=[_sds((n_rows, w), dt) for w, dt in outs] + [_sds((1, w), F32) for w in accs],
        compiler_params=_params(("arbitrary",)),
    )(*rows, *vecs)
    return res


def _colsum(x):
    return jnp.sum(x, axis=0, keepdims=True)


def _sigmoid(x):
    return 1.0 / (1.0 + jnp.exp(-x))


_GELU_C = math.sqrt(2.0 / math.pi)


def _gelu(y):
    return 0.5 * y * (1.0 + jnp.tanh(_GELU_C * (y + 0.044715 * y * y * y)))


def _gelu_grad(y):
    t = jnp.tanh(_GELU_C * (y + 0.044715 * y * y * y))
    return 0.5 * (1.0 + t) + 0.5 * y * (1.0 - t * t) * _GELU_C * (1.0 + 3.0 * 0.044715 * y * y)


def _rms_fwd(name, h, gain):
    def fn(x, g):
        r = lax.rsqrt(jnp.mean(x * x, axis=1, keepdims=True) + EPS)
        return (x * r * g, r), ()
    return _rowwise(name, fn, [h], [gain.reshape(1, -1)], [(h.shape[1], BF16), (1, F32)])


def _rms_bwd(name, h, rstd, gain, dxn, dres):
    def fn(x, r, dy, dr, g):
        xh = x * r
        gdy = dy * g
        dx = r * (gdy - xh * jnp.mean(gdy * xh, axis=1, keepdims=True)) + dr
        return (dx, dx), (_colsum(dy * xh),)
    w = h.shape[1]
    return _rowwise(name, fn, [h, rstd, dxn, dres], [gain.reshape(1, -1)], [(w, F32), (w, BF16)], [w])


def _loss_head(h, target):
    w = h.shape[1]

    def fn(x, t):
        e = x - t
        return (e * (1.0 / w), e * (1.0 / w)), (jnp.zeros((1, 128), F32) + jnp.sum(e * e),)
    return _rowwise("loss_head", fn, [h, target], [], [(w, F32), (w, BF16)], [128])


def _adamw(name, w, g, m, v):
    c1 = 1.0 - ADAM_B1 ** ADAM_STEP
    c2 = 1.0 - ADAM_B2 ** ADAM_STEP

    def fn(w_, g_, m_, v_):
        mn = ADAM_B1 * m_ + (1.0 - ADAM_B1) * g_
        vn = ADAM_B2 * v_ + (1.0 - ADAM_B2) * (g_ * g_)
        delta = -ADAM_LR * ((mn / c1) / (jnp.sqrt(vn / c2) + ADAM_EPS) + ADAM_WD * w_)
        return (delta, mn, vn), ()
    c = w.shape[1]
    return _rowwise(name, fn, [w, g, m, v], [], [(c, F32)] * 3)


def _s5_discretize(lam_re, lam_im, log_dt, b_re, b_im):
    lr = jnp.minimum(lam_re, -S5_MIN_DECAY)
    li = lam_im
    dt = jnp.exp(log_dt)[:, None]
    mag = jnp.exp(lr * dt)
    ar = mag * jnp.cos(li * dt)
    ai = mag * jnp.sin(li * dt)
    den = lr * lr + li * li
    zr = ((ar - 1.0) * lr + ai * li) / den
    zi = (ai * lr - (ar - 1.0) * li) / den
    bbr = zr[..., None] * b_re - zi[..., None] * b_im
    bbi = zr[..., None] * b_im + zi[..., None] * b_re
    return ar, ai, bbr, bbi


def _s5_matrices(ar, ai, bbr, bbi, c_re, c_im, d_skip):
    eye = jnp.eye(8, dtype=F32)
    bt = jnp.stack([bbr, bbi], axis=1).transpose(0, 3, 1, 2)
    bt = bt.reshape(8, 8, 16, 1, 2, 64) * eye[None, :, None, :, None, None]
    bm8 = bt.reshape(8, 8, 16, 4, 2, 2, 64).transpose(0, 1, 2, 3, 5, 4, 6).reshape(8, 128, 1024)
    ct = jnp.stack([c_re, -c_im], axis=1).transpose(0, 1, 3, 2)
    ct = ct.reshape(8, 8, 2, 64, 1, 16) * eye[None, :, None, None, :, None]
    cm8 = ct.reshape(8, 4, 2, 2, 64, 8, 16).transpose(0, 1, 3, 2, 4, 5, 6).reshape(8, 1024, 128)
    prs, pis = [], []
    pr, pi = ar, ai
    for _ in range(S5_LEVELS):
        prs.append(pr.reshape(8, 512))
        pis.append(pi.reshape(8, 512))
        pr, pi = pr * pr - pi * pi, 2.0 * pr * pi
    prs.append(jnp.zeros_like(prs[0]))
    pis.append(jnp.zeros_like(pis[0]))
    return (bm8.astype(BF16), cm8.astype(BF16), jnp.stack(prs, axis=1), jnp.stack(pis, axis=1),
            d_skip.reshape(8, 1, 128))


def _s5_unpack_grads(dbm8, dcm8, da, dd):
    db = dbm8.reshape(8, 8, 16, 4, 2, 2, 64).transpose(0, 1, 2, 3, 5, 4, 6).reshape(8, 8, 16, 8, 2, 64)
    db = jnp.einsum("agcgqp->agcqp", db).reshape(S5_GROUPS, 16, 2, 64)
    dc = dcm8.reshape(8, 4, 2, 2, 64, 8, 16).transpose(0, 1, 3, 2, 4, 5, 6).reshape(8, 8, 2, 64, 8, 16)
    dc = jnp.einsum("agqpgc->agqpc", dc).reshape(S5_GROUPS, 2, 64, 16)
    dar = da[:, 0, :].reshape(S5_GROUPS, 64)
    dai = da[:, 1, :].reshape(S5_GROUPS, 64)
    return (dar, dai, db[:, :, 0, :].transpose(0, 2, 1), db[:, :, 1, :].transpose(0, 2, 1),
            dc[:, 0].transpose(0, 2, 1), -dc[:, 1].transpose(0, 2, 1), dd.reshape(S5_GROUPS, 16))


def _shift_rows(x, s, row, down):
    t = x.shape[0]
    if s % 8 == 0:
        z = jnp.zeros((s, x.shape[1]), x.dtype)
        return jnp.concatenate([z, x[:t - s]], axis=0) if down else jnp.concatenate([x[s:], z], axis=0)
    if down:
        return jnp.where(row >= s, pltpu.roll(x, s, 0), 0.0)
    return jnp.where(row < t - s, pltpu.roll(x, t - s, 0), 0.0)


def _s5_scan(xr, xi, pr, pi, cr, ci, row, conj):
    t = xr[0].shape[0]
    sg = -1.0 if conj else 1.0
    edge = (t - 1) if conj else 0
    n = len(xr)
    for k in range(n):
        sl = slice(128 * k, 128 * (k + 1))
        p_r, p_i = pr[0:1, sl], sg * pi[0:1, sl]
        xr[k] = xr[k] + jnp.where(row == edge, p_r * cr[k] - p_i * ci[k], 0.0)
        xi[k] = xi[k] + jnp.where(row == edge, p_r * ci[k] + p_i * cr[k], 0.0)
    for lvl in range(S5_LEVELS):
        s = 1 << lvl
        for k in range(n):
            sl = slice(128 * k, 128 * (k + 1))
            p_r, p_i = pr[lvl:lvl + 1, sl], sg * pi[lvl:lvl + 1, sl]
            sr = _shift_rows(xr[k], s, row, not conj)
            si = _shift_rows(xi[k], s, row, not conj)
            xr[k] = xr[k] + p_r * sr - p_i * si
            xi[k] = xi[k] + p_r * si + p_i * sr
    return xr, xi


def _s5_fwd(name, proj, mats):
    bm8, cm8, p1, p2, d8 = mats
    n_rows = proj.shape[0]
    t = S5_CHUNK
    nch = n_rows // t

    def body(u_ref, bm_ref, cm_ref, pr_ref, pi_ref, d_ref, y_ref, z_ref, st_ref, carry):
        @pl.when(pl.program_id(1) == 0)
        def _():
            carry[...] = jnp.zeros_like(carry)

        cv = carry[...]
        st_ref[...] = cv
        u = u_ref[...]
        bu = jnp.dot(u.astype(BF16), bm_ref[...], preferred_element_type=F32)
        row = lax.broadcasted_iota(jnp.int32, (t, 128), 0)
        tile = lambda v, j: v[:, 128 * j:128 * (j + 1)]
        xr, xi = _s5_scan([tile(bu, 2 * k) for k in range(4)], [tile(bu, 2 * k + 1) for k in range(4)],
                          pr_ref[...], pi_ref[...], [tile(cv, 2 * k)[0:1] for k in range(4)],
                          [tile(cv, 2 * k + 1)[0:1] for k in range(4)], row, False)
        xall = jnp.concatenate([v for k in range(4) for v in (xr[k], xi[k])], axis=1)
        carry[...] = jnp.broadcast_to(xall[t - 1:t, :], (8, 1024))
        y = jnp.dot(xall.astype(BF16), cm_ref[...], preferred_element_type=F32) + d_ref[...] * u
        y_ref[...] = y
        z_ref[...] = _gelu(y).astype(BF16)

    return pl.pallas_call(
        body, name=name,
        grid=(8, nch),
        in_specs=[
            pl.BlockSpec((t, 128), lambda g, c: (c, g)),
            pl.BlockSpec((None, 128, 1024), lambda g, c: (g, 0, 0)),
            pl.BlockSpec((None, 1024, 128), lambda g, c: (g, 0, 0)),
            pl.BlockSpec((None, 8, 512), lambda g, c: (g, 0, 0)),
            pl.BlockSpec((None, 8, 512), lambda g, c: (g, 0, 0)),
            pl.BlockSpec((None, 1, 128), lambda g, c: (g, 0, 0)),
        ],
        out_specs=[
            pl.BlockSpec((t, 128), lambda g, c: (c, g)),
            pl.BlockSpec((t, 128), lambda g, c: (c, g)),
            pl.BlockSpec((None, None, 8, 1024), lambda g, c: (g, c, 0, 0)),
        ],
        out_shape=[_sds((n_rows, S5_WIDTH), F32), _sds((n_rows, S5_WIDTH), BF16), _sds((8, nch, 8, 1024), F32)],
        scratch_shapes=[pltpu.VMEM((8, 1024), F32)],
        compiler_params=_params(("parallel", "arbitrary")),
    )(proj, bm8, cm8, p1, p2, d8)


def _s5_bwd(name, proj, dy, states, mats):
    bm8, cm8, p1, p2, d8 = mats
    n_rows = proj.shape[0]
    t = S5_CHUNK
    nch = n_rows // t
    nt_dims = (((1,), (1,)), ((), ()))
    tn_dims = (((0,), (0,)), ((), ()))

    def body(u_ref, dy_ref, st_ref, bm_ref, cm_ref, pr_ref, pi_ref, d_ref,
             du_ref, dbm_ref, dcm_ref, da_ref, dd_ref, gcarry):
        @pl.when(pl.program_id(1) == 0)
        def _():
            gcarry[...] = jnp.zeros_like(gcarry)
            dbm_ref[...] = jnp.zeros_like(dbm_ref)
            dcm_ref[...] = jnp.zeros_like(dcm_ref)
            da_ref[...] = jnp.zeros_like(da_ref)
            dd_ref[...] = jnp.zeros_like(dd_ref)

        u = u_ref[...]
        dyv = dy_ref[...]
        ub, dyb = u.astype(BF16), dyv.astype(BF16)
        bu = jnp.dot(ub, bm_ref[...], preferred_element_type=F32)
        dxd = lax.dot_general(dyb, cm_ref[...], nt_dims, preferred_element_type=F32)
        row = lax.broadcasted_iota(jnp.int32, (t, 128), 0)
        tile = lambda v, j: v[:, 128 * j:128 * (j + 1)]
        prv, piv, cv, gv = pr_ref[...], pi_ref[...], st_ref[...], gcarry[...]
        cr = [tile(cv, 2 * k)[0:1] for k in range(4)]
        ci = [tile(cv, 2 * k + 1)[0:1] for k in range(4)]
        xr, xi = _s5_scan([tile(bu, 2 * k) for k in range(4)], [tile(bu, 2 * k + 1) for k in range(4)],
                          prv, piv, cr, ci, row, False)
        gr, gi = _s5_scan([tile(dxd, 2 * k) for k in range(4)], [tile(dxd, 2 * k + 1) for k in range(4)],
                          prv, piv, [tile(gv, 2 * k)[0:1] for k in range(4)],
                          [tile(gv, 2 * k + 1)[0:1] for k in range(4)], row, True)
        dar, dai = [], []
        for k in range(4):
            xpr = jnp.where(row >= 1, pltpu.roll(xr[k], 1, 0), cr[k])
            xpi = jnp.where(row >= 1, pltpu.roll(xi[k], 1, 0), ci[k])
            dar.append(_colsum(gr[k] * xpr + gi[k] * xpi))
            dai.append(_colsum(gi[k] * xpr - gr[k] * xpi))
        xall = jnp.concatenate([v for k in range(4) for v in (xr[k], xi[k])], axis=1).astype(BF16)
        gf = jnp.concatenate([v for k in range(4) for v in (gr[k], gi[k])], axis=1)
        gcarry[...] = jnp.broadcast_to(gf[0:1, :], (8, 1024))
        gall = gf.astype(BF16)
        dcm_ref[...] += lax.dot_general(xall, dyb, tn_dims, preferred_element_type=F32)
        dbm_ref[...] += lax.dot_general(ub, gall, tn_dims, preferred_element_type=F32)
        du = lax.dot_general(gall, bm_ref[...], nt_dims, preferred_element_type=F32) + d_ref[...] * dyv
        du_ref[...] = du.astype(BF16)
        dd_ref[...] += _colsum(dyv * u)
        da_ref[0:1, :] += jnp.concatenate(dar, axis=1)
        da_ref[1:2, :] += jnp.concatenate(dai, axis=1)

    rev = lambda g, c: (nch - 1 - c, g)
    return pl.pallas_call(
        body, name=name,
        grid=(8, nch),
        in_specs=[
            pl.BlockSpec((t, 128), rev),
            pl.BlockSpec((t, 128), rev),
            pl.BlockSpec((None, None, 8, 1024), lambda g, c: (g, nch - 1 - c, 0, 0)),
            pl.BlockSpec((None, 128, 1024), lambda g, c: (g, 0, 0)),
            pl.BlockSpec((None, 1024, 128), lambda g, c: (g, 0, 0)),
            pl.BlockSpec((None, 8, 512), lambda g, c: (g, 0, 0)),
            pl.BlockSpec((None, 8, 512), lambda g, c: (g, 0, 0)),
            pl.BlockSpec((None, 1, 128), lambda g, c: (g, 0, 0)),
        ],
        out_specs=[
            pl.BlockSpec((t, 128), rev),
            pl.BlockSpec((None, 128, 1024), lambda g, c: (g, 0, 0)),
            pl.BlockSpec((None, 1024, 128), lambda g, c: (g, 0, 0)),
            pl.BlockSpec((None, 8, 512), lambda g, c: (g, 0, 0)),
            pl.BlockSpec((None, 1, 128), lambda g, c: (g, 0, 0)),
        ],
        out_shape=[_sds((n_rows, S5_WIDTH), BF16), _sds((8, 128, 1024), F32), _sds((8, 1024, 128), F32),
                   _sds((8, 8, 512), F32), _sds((8, 1, 128), F32)],
        scratch_shapes=[pltpu.VMEM((8, 1024), F32)],
        compiler_params=_params(("parallel", "arbitrary")),
    )(proj, dy, states, bm8, cm8, p1, p2, d8)


def _hgrn_lower_bounds(lb_param):
    p = jax.nn.softmax(lb_param, axis=0)
    return jnp.cumsum(p, axis=0) - p[0:1]


def _prefix16(x, r16):
    for s in (1, 2, 4, 8):
        x = x + jnp.where(r16 >= s, pltpu.roll(x, s, 0), 0.0)
    return x


def _suffix16(x, r16):
    n = x.shape[0]
    for s in (1, 2, 4, 8):
        x = x + jnp.where(r16 < HGRN_SUB - s, pltpu.roll(x, n - s, 0), 0.0)
    return x


_NT = (((1,), (1,)), ((), ()))
_TN = (((0,), (0,)), ((), ()))


def _dotf(a, b, dims=(((1,), (0,)), ((), ()))):
    return lax.dot_general(a.astype(BF16), b.astype(BF16), dims, preferred_element_type=F32)


def _hgrn_specs(n_blocks, rev):
    r = HGRN_BLOCK
    blk = (lambda b: n_blocks - 1 - b) if rev else (lambda b: b)
    proj_specs = [pl.BlockSpec((r, 128), (lambda h, b, c=c: (blk(b), 8 * c + h))) for c in (1, 2, 3, 4)]
    lb_spec = pl.BlockSpec((None, 1, 128), lambda h, b: (h, 0, 0))
    gain_spec = pl.BlockSpec((1, 128), lambda h, b: (0, 0))
    row_spec = pl.BlockSpec((r, 128), lambda h, b: (blk(b), h))
    st_spec = pl.BlockSpec((None, None, 128, 128), lambda h, b: (h, blk(b), 0, 0))
    return proj_specs, lb_spec, gain_spec, row_spec, st_spec, blk


def _hgrn_fwd(name, proj, lb, gain):
    n_rows = proj.shape[0]
    r = HGRN_BLOCK
    nb = n_rows // r
    nsub = r // HGRN_SUB
    proj_specs, lb_spec, gain_spec, row_spec, st_spec, _ = _hgrn_specs(nb, False)

    def body(q_ref, f_ref, i_ref, g_ref, lb_ref, gain_ref, o_ref, y_ref, st_ref, st_scr):
        @pl.when(pl.program_id(1) == 0)
        def _():
            st_scr[...] = jnp.zeros_like(st_scr)

        st_ref[...] = st_scr[...]
        q, f, v, g = q_ref[...], f_ref[...], i_ref[...], g_ref[...]
        lbv = lb_ref[...]
        qs = q * _sigmoid(q)
        fg = lbv + (1.0 - lbv) * _sigmoid(f)
        kk = 1.0 - fg
        r16 = lax.broadcasted_iota(jnp.int32, (r, 128), 0) & (HGRN_SUB - 1)
        b = _prefix16(jnp.log(fg), r16)
        qh = qs * jnp.exp(b)
        rs = lax.broadcasted_iota(jnp.int32, (HGRN_SUB, 128), 0)
        st = st_scr[...]
        outs = []
        for i in range(nsub):
            sl = slice(HGRN_SUB * i, HGRN_SUB * (i + 1))
            qsi, kki, vi, bi = qs[sl], kk[sl], v[sl], b[sl]
            o_i = _dotf(qh[sl], st, _NT)
            for s in range(HGRN_SUB):
                e = jnp.exp(jnp.where(rs >= s, bi - bi[s:s + 1], NEG))
                col = jnp.sum(qsi * e * kki[s:s + 1], axis=1, keepdims=True)
                o_i = o_i + col * vi[s:s + 1]
            bl = bi[HGRN_SUB - 1:HGRN_SUB]
            st = st * jnp.exp(bl) + _dotf(vi, kki * jnp.exp(bl - bi), _TN)
            outs.append(o_i)
        st_scr[...] = st
        o = jnp.concatenate(outs, axis=0)
        o_ref[...] = o
        rn = lax.rsqrt(jnp.mean(o * o, axis=1, keepdims=True) + EPS)
        y_ref[...] = (o * rn * gain_ref[...] * (g * _sigmoid(g))).astype(BF16)

    return pl.pallas_call(
        body, name=name,
        grid=(HGRN_HEADS, nb),
        in_specs=proj_specs + [lb_spec, gain_spec],
        out_specs=[row_spec, row_spec, st_spec],
        out_shape=[_sds((n_rows, HGRN_WIDTH), F32), _sds((n_rows, HGRN_WIDTH), BF16),
                   _sds((HGRN_HEADS, nb, 128, 128), F32)],
        scratch_shapes=[pltpu.VMEM((128, 128), F32)],
        compiler_params=_params(("parallel", "arbitrary")),
    )(proj, proj, proj, proj, lb, gain)


def _hgrn_bwd(name, proj, lb, gain, o_saved, states, dycat):
    n_rows = proj.shape[0]
    r = HGRN_BLOCK
    nb = n_rows // r
    nsub = r // HGRN_SUB
    proj_specs, lb_spec, gain_spec, row_spec, st_spec, blk = _hgrn_specs(nb, True)
    dy_spec = pl.BlockSpec((r, 128), lambda h, b: (blk(b), 8 + h))
    acc_spec = pl.BlockSpec((None, 1, 128), lambda h, b: (h, 0, 0))

    def body(q_ref, f_ref, i_ref, g_ref, lb_ref, gain_ref, o_ref, st_ref, dy_ref,
             dq_ref, df_ref, di_ref, dg_ref, dlb_ref, dgain_ref, dst_scr, sub_scr):
        @pl.when(pl.program_id(1) == 0)
        def _():
            dst_scr[...] = jnp.zeros_like(dst_scr)
            dlb_ref[...] = jnp.zeros_like(dlb_ref)
            dgain_ref[...] = jnp.zeros_like(dgain_ref)

        q, f, v, g = q_ref[...], f_ref[...], i_ref[...], g_ref[...]
        lbv, gain_v = lb_ref[...], gain_ref[...]
        sq = _sigmoid(q)
        qs = q * sq
        sf = _sigmoid(f)
        fg = lbv + (1.0 - lbv) * sf
        kk = 1.0 - fg
        r16 = lax.broadcasted_iota(jnp.int32, (r, 128), 0) & (HGRN_SUB - 1)
        b = _prefix16(jnp.log(fg), r16)
        eb = jnp.exp(b)
        qh = qs * eb

        o, dy = o_ref[...], dy_ref[...]
        rn = lax.rsqrt(jnp.mean(o * o, axis=1, keepdims=True) + EPS)
        on = o * rn
        sg = _sigmoid(g)
        sil = g * sg
        dgain_ref[...] += _colsum(dy * on * sil)
        dg_ref[...] = (dy * on * gain_v * (sg * (1.0 + g * (1.0 - sg)))).astype(BF16)
        don = dy * gain_v * sil
        do = rn * (don - on * jnp.mean(don * on, axis=1, keepdims=True))

        st = st_ref[...]
        for i in range(nsub):
            sl = slice(HGRN_SUB * i, HGRN_SUB * (i + 1))
            sub_scr[i] = st
            bi = b[sl]
            bl = bi[HGRN_SUB - 1:HGRN_SUB]
            st = st * jnp.exp(bl) + _dotf(v[sl], kk[sl] * jnp.exp(bl - bi), _TN)

        rs = lax.broadcasted_iota(jnp.int32, (HGRN_SUB, 128), 0)
        dst = dst_scr[...]
        parts = [None] * nsub
        for i in reversed(range(nsub)):
            sl = slice(HGRN_SUB * i, HGRN_SUB * (i + 1))
            sp = sub_scr[i]
            qsi, kki, vi, bi, doi, qhi = qs[sl], kk[sl], v[sl], b[sl], do[sl], qh[sl]
            bl = bi[HGRN_SUB - 1:HGRN_SUB]
            ebl = jnp.exp(bl)
            dec = jnp.exp(bl - bi)
            khat = kki * dec
            dqh = _dotf(doi, sp)
            dkhat = _dotf(vi, dst)
            dv = _dotf(khat, dst, _NT)
            zrow = _colsum(sp * dst) * ebl
            dq_in = jnp.zeros((HGRN_SUB, 128), F32)
            dk_in = jnp.zeros((HGRN_SUB, 128), F32)
            dv_in = jnp.zeros((HGRN_SUB, 128), F32)
            for s in range(HGRN_SUB):
                e = jnp.exp(jnp.where(rs >= s, bi - bi[s:s + 1], NEG))
                dpc = jnp.sum(doi * vi[s:s + 1], axis=1, keepdims=True)
                w = qsi * e
                pc = jnp.sum(w * kki[s:s + 1], axis=1, keepdims=True)
                dq_in = dq_in + dpc * e * kki[s:s + 1]
                dk_in = jnp.where(rs == s, _colsum(dpc * w), dk_in)
                dv_in = jnp.where(rs == s, _colsum(pc * doi), dv_in)
            kd = khat * dkhat
            parts[i] = (qsi * dq_in - kki * dk_in + qhi * dqh, kd, jnp.broadcast_to(zrow, (HGRN_SUB, 128)),
                        dq_in + dqh * eb[sl], dk_in + dkhat * dec, dv + dv_in)
            dst = dst * ebl + _dotf(doi, qhi, _TN)
        dst_scr[...] = dst

        cat = lambda j: jnp.concatenate([p[j] for p in parts], axis=0)
        d_b, kd, zr, dqs, dkk, dvv = (cat(j) for j in range(6))
        dlf = _suffix16(d_b, r16) + _prefix16(kd, r16) - kd + zr
        dfg = dlf / fg - dkk
        df_ref[...] = (dfg * (1.0 - lbv) * sf * (1.0 - sf)).astype(BF16)
        dlb_ref[...] += _colsum(dfg * (1.0 - sf))
        dq_ref[...] = (dqs * (sq * (1.0 + q * (1.0 - sq)))).astype(BF16)
        di_ref[...] = dvv.astype(BF16)

    return pl.pallas_call(
        body, name=name,
        grid=(HGRN_HEADS, nb),
        in_specs=proj_specs + [lb_spec, gain_spec, row_spec, st_spec, dy_spec],
        out_specs=[row_spec] * 4 + [acc_spec, acc_spec],
        out_shape=[_sds((n_rows, HGRN_WIDTH), BF16)] * 4 + [_sds((HGRN_HEADS, 1, 128), F32)] * 2,
        scratch_shapes=[pltpu.VMEM((128, 128), F32), pltpu.VMEM((nsub, 128, 128), F32)],
        compiler_params=_params(("parallel", "arbitrary")),
    )(proj, proj, proj, proj, lb, gain, o_saved, states, dycat)


def _alibi_slopes():
    return jnp.exp2(-8.0 * jnp.arange(1, ATT_HEADS + 1, dtype=F32) / ATT_HEADS)


def _swa_specs(n_blocks):
    blk = ATT_BLOCK
    prev = lambda i: jnp.maximum(i - 1, 0)
    smem = pl.BlockSpec(memory_space=pltpu.SMEM)
    return [
        smem, smem,
        pl.BlockSpec((blk, ATT_HEADS * ATT_DIM), lambda i: (i, 0)),
        pl.BlockSpec((blk, 256), lambda i: (i, 8)),
        pl.BlockSpec((blk, 256), lambda i: (prev(i), 8)),
        pl.BlockSpec((blk, 256), lambda i: (i, 9)),
        pl.BlockSpec((blk, 256), lambda i: (prev(i), 9)),
        pl.BlockSpec((1, ATT_DIM), lambda i: (0, 0)),
        pl.BlockSpec((1, ATT_DIM), lambda i: (0, 0)),
    ]


_ATT_GROUP = ATT_HEADS // ATT_KV
_ATT_ROWS = _ATT_GROUP * ATT_BLOCK


def _swa_mask(i):
    t_i = lax.broadcasted_iota(jnp.int32, (_ATT_ROWS, 2 * ATT_BLOCK), 0) & (ATT_BLOCK - 1)
    s_i = lax.broadcasted_iota(jnp.int32, (_ATT_ROWS, 2 * ATT_BLOCK), 1)
    dist = t_i + ATT_BLOCK - s_i
    valid = (dist >= 0) & (dist < ATT_BLOCK) & ((s_i >= ATT_BLOCK) | (i > 0))
    return valid, dist.astype(F32)


def _stack_heads(x):
    return jnp.concatenate([x[:, ATT_DIM * h:ATT_DIM * (h + 1)] for h in range(_ATT_GROUP)], axis=0)


def _unstack_heads(x):
    return jnp.concatenate([x[ATT_BLOCK * h:ATT_BLOCK * (h + 1)] for h in range(_ATT_GROUP)], axis=1)


def _head_column(ref, g):
    return jnp.concatenate([jnp.full((ATT_BLOCK, 1), ref[_ATT_GROUP * g + h], F32) for h in range(_ATT_GROUP)], axis=0)


def _swa_probs(qn, kn, slope, sink, valid, distf):
    s = lax.dot_general(qn, kn, _NT, preferred_element_type=F32) * (ATT_DIM ** -0.5) - slope * distf
    s = jnp.where(valid, s, NEG)
    m = jnp.maximum(jnp.max(s, axis=1, keepdims=True), sink)
    p = jnp.exp(s - m)
    es = jnp.exp(sink - m)
    inv = 1.0 / (jnp.sum(p, axis=1, keepdims=True) + es)
    return p * inv, es * inv


def _swa_fwd(name, qkv, q_gain, k_gain, sinks, slopes):
    n_rows = qkv.shape[0]
    nb = n_rows // ATT_BLOCK

    def body(sink_ref, slope_ref, q_ref, kc_ref, kp_ref, vc_ref, vp_ref, qg_ref, kg_ref, o_ref):
        i = pl.program_id(0)
        kb = jnp.concatenate([kp_ref[...], kc_ref[...]], axis=0)
        vb = jnp.concatenate([vp_ref[...], vc_ref[...]], axis=0)
        valid, distf = _swa_mask(i)
        qgv, kgv = qg_ref[...], kg_ref[...]
        gw = _ATT_GROUP * ATT_DIM
        for g in range(ATT_KV):
            kg = kb[:, 64 * g:64 * (g + 1)]
            rk = lax.rsqrt(jnp.mean(kg * kg, axis=1, keepdims=True) + EPS)
            kn = (kg * rk * kgv).astype(BF16)
            vv = vb[:, 64 * g:64 * (g + 1)].astype(BF16)
            qs = _stack_heads(q_ref[:, gw * g:gw * (g + 1)])
            rq = lax.rsqrt(jnp.mean(qs * qs, axis=1, keepdims=True) + EPS)
            pn, _ = _swa_probs((qs * rq * qgv).astype(BF16), kn, _head_column(slope_ref, g),
                               _head_column(sink_ref, g), valid, distf)
            out = jnp.dot(pn.astype(BF16), vv, preferred_element_type=F32)
            o_ref[:, gw * g:gw * (g + 1)] = _unstack_heads(out).astype(BF16)

    return pl.pallas_call(
        body, name=name,
        grid=(nb,),
        in_specs=_swa_specs(nb),
        out_specs=pl.BlockSpec((ATT_BLOCK, ATT_HEADS * ATT_DIM), lambda i: (i, 0)),
        out_shape=_sds((n_rows, ATT_HEADS * ATT_DIM), BF16),
        compiler_params=_params(("parallel",)),
    )(sinks, slopes, qkv, qkv, qkv, qkv, qkv, q_gain.reshape(1, -1), k_gain.reshape(1, -1))


def _swa_bwd(name, qkv, q_gain, k_gain, sinks, slopes, d_out):
    n_rows = qkv.shape[0]
    nb = n_rows // ATT_BLOCK
    blk = ATT_BLOCK

    def body(sink_ref, slope_ref, q_ref, kc_ref, kp_ref, vc_ref, vp_ref, qg_ref, kg_ref, do_ref,
             dq_ref, dkc_ref, dkp_ref, dvc_ref, dvp_ref, dsink_ref, dqg_ref, dkg_ref):
        i = pl.program_id(0)

        @pl.when(i == 0)
        def _():
            dsink_ref[...] = jnp.zeros_like(dsink_ref)
            dqg_ref[...] = jnp.zeros_like(dqg_ref)
            dkg_ref[...] = jnp.zeros_like(dkg_ref)

        kb = jnp.concatenate([kp_ref[...], kc_ref[...]], axis=0)
        vb = jnp.concatenate([vp_ref[...], vc_ref[...]], axis=0)
        kgv, qgv = kg_ref[...], qg_ref[...]
        valid, distf = _swa_mask(i)
        scale = ATT_DIM ** -0.5
        gw = _ATT_GROUP * ATT_DIM
        dks, dvs = [], []
        dqg, dkg = jnp.zeros((1, ATT_DIM), F32), jnp.zeros((1, ATT_DIM), F32)
        for g in range(ATT_KV):
            kg = kb[:, 64 * g:64 * (g + 1)]
            rk = lax.rsqrt(jnp.mean(kg * kg, axis=1, keepdims=True) + EPS)
            khat = kg * rk
            kn = (khat * kgv).astype(BF16)
            vv = vb[:, 64 * g:64 * (g + 1)].astype(BF16)
            qs = _stack_heads(q_ref[:, gw * g:gw * (g + 1)])
            rq = lax.rsqrt(jnp.mean(qs * qs, axis=1, keepdims=True) + EPS)
            qhat = qs * rq
            qn = (qhat * qgv).astype(BF16)
            pn, ps = _swa_probs(qn, kn, _head_column(slope_ref, g), _head_column(sink_ref, g), valid, distf)
            dos = _stack_heads(do_ref[:, gw * g:gw * (g + 1)]).astype(BF16)
            dp = lax.dot_general(dos, vv, _NT, preferred_element_type=F32)
            delta = jnp.sum(pn * dp, axis=1, keepdims=True)
            ds = (pn * (dp - delta)).astype(BF16)
            sd = ps * delta
            for h in range(_ATT_GROUP):
                hs = _ATT_GROUP * g + h
                dsink_ref[hs:hs + 1, :] += jnp.zeros((1, 128), F32) - jnp.sum(sd[blk * h:blk * (h + 1)])
            dvs.append(lax.dot_general(pn.astype(BF16), dos, _TN, preferred_element_type=F32))
            dkn = lax.dot_general(ds, qn, _TN, preferred_element_type=F32) * scale
            dqn = jnp.dot(ds, kn, preferred_element_type=F32) * scale
            dqg = dqg + _colsum(dqn * qhat)
            dqhat = dqn * qgv
            dqs = rq * (dqhat - qhat * jnp.mean(dqhat * qhat, axis=1, keepdims=True))
            dq_ref[:, gw * g:gw * (g + 1)] = _unstack_heads(dqs).astype(BF16)
            dkg = dkg + _colsum(dkn * khat)
            dkhat = dkn * kgv
            dks.append(rk * (dkhat - khat * jnp.mean(dkhat * khat, axis=1, keepdims=True)))
        dqg_ref[...] += dqg
        dkg_ref[...] += dkg
        dk = jnp.concatenate(dks, axis=1).astype(BF16)
        dv = jnp.concatenate(dvs, axis=1).astype(BF16)
        dkp_ref[...] = dk[:blk]
        dkc_ref[...] = dk[blk:]
        dvp_ref[...] = dv[:blk]
        dvc_ref[...] = dv[blk:]

    kv_spec = pl.BlockSpec((blk, 256), lambda i: (i, 0))
    full = pl.BlockSpec((blk, ATT_HEADS * ATT_DIM), lambda i: (i, 0))
    acc64 = pl.BlockSpec((1, ATT_DIM), lambda i: (0, 0))
    return pl.pallas_call(
        body, name=name,
        grid=(nb,),
        in_specs=_swa_specs(nb) + [full],
        out_specs=[full, kv_spec, kv_spec, kv_spec, kv_spec,
                   pl.BlockSpec((ATT_HEADS, 128), lambda i: (0, 0)), acc64, acc64],
        out_shape=[_sds((n_rows, ATT_HEADS * ATT_DIM), BF16)] + [_sds((n_rows, 256), BF16)] * 4
        + [_sds((ATT_HEADS, 128), F32), _sds((1, ATT_DIM), F32), _sds((1, ATT_DIM), F32)],
        compiler_params=_params(("arbitrary",)),
    )(sinks, slopes, qkv, qkv, qkv, qkv, qkv, q_gain.reshape(1, -1), k_gain.reshape(1, -1), d_out)


def _mesh_pos():
    return lax.axis_index("x"), lax.axis_index("y"), lax.axis_index("c")


def _allgather(name, shard):
    def body(x_ref, out_ref, send_sems, recv_sems, local_sem):
        x, y, c = _mesh_pos()
        me, sibling = (x, y, c), (x, y, 1 - c)
        chips = [(1 - x, y), (x, 1 - y), (1 - x, 1 - y)]

        def slot(px, py, pc):
            return out_ref.at[4 * px + 2 * py + pc]

        def copy(k, block, to, src=None):
            return pltpu.make_async_remote_copy(
                src_ref=slot(*block) if src is None else src, dst_ref=slot(*block),
                send_sem=send_sems.at[k], recv_sem=recv_sems.at[k], device_id=to, device_id_type=MESH)

        mine = pltpu.make_async_copy(x_ref, slot(*me), local_sem)
        mine.start()
        first = [copy(0, me, sibling, src=x_ref)]
        first += [copy(1 + j, me, (*chip, c), src=x_ref) for j, chip in enumerate(chips)]
        for cp in first:
            cp.start()
        passed = [copy(4 + j, (*chip, c), sibling) for j, chip in enumerate(chips)]
        for j, chip in enumerate(chips):
            copy(1 + j, (*chip, c), me).wait_recv()
            passed[j].start()
        copy(0, sibling, me).wait_recv()
        for j, chip in enumerate(chips):
            copy(4 + j, (*chip, 1 - c), me).wait_recv()
        for cp in first + passed:
            cp.wait_send()
        mine.wait()

    return pl.pallas_call(
        body, name=name,
        out_shape=_sds((N_DEV,) + shard.shape, shard.dtype),
        in_specs=[pl.BlockSpec(memory_space=pl.ANY)],
        out_specs=pl.BlockSpec(memory_space=pl.ANY),
        scratch_shapes=[pltpu.SemaphoreType.DMA((7,)), pltpu.SemaphoreType.DMA((7,)), pltpu.SemaphoreType.DMA],
    )(shard)


def _swap_with_sibling(name, blocks):
    def body(x_ref, out_ref, send_sem, recv_sem):
        x, y, c = _mesh_pos()
        copy = pltpu.make_async_remote_copy(src_ref=x_ref, dst_ref=out_ref, send_sem=send_sem, recv_sem=recv_sem,
                                            device_id=(x, y, 1 - c), device_id_type=MESH)
        copy.start()
        copy.wait()

    return pl.pallas_call(
        body, name=name,
        out_shape=_sds(blocks.shape, blocks.dtype),
        in_specs=[pl.BlockSpec(memory_space=pl.ANY)],
        out_specs=pl.BlockSpec(memory_space=pl.ANY),
        scratch_shapes=[pltpu.SemaphoreType.DMA, pltpu.SemaphoreType.DMA],
    )(blocks)


def _exchange_chips(name, blocks):
    def body(x_ref, out_ref, send_sems, recv_sems, local_sem):
        x, y, c = _mesh_pos()
        me = 2 * x + y
        mine = pltpu.make_async_copy(x_ref.at[me], out_ref.at[me], local_sem)
        mine.start()
        copies = []
        for k in range(1, 4):
            px, py = x ^ (k >> 1), y ^ (k & 1)
            peer = 2 * px + py
            send = pltpu.make_async_remote_copy(
                src_ref=x_ref.at[peer], dst_ref=out_ref.at[me],
                send_sem=send_sems.at[k - 1], recv_sem=recv_sems.at[k - 1],
                device_id=(px, py, c), device_id_type=MESH)
            recv = pltpu.make_async_remote_copy(
                src_ref=x_ref.at[peer], dst_ref=out_ref.at[peer],
                send_sem=send_sems.at[k - 1], recv_sem=recv_sems.at[k - 1],
                device_id=(px, py, c), device_id_type=MESH)
            send.start()
            copies.append((send, recv))
        for send, recv in copies:
            recv.wait_recv()
        for send, recv in copies:
            send.wait_send()
        mine.wait()

    return pl.pallas_call(
        body, name=name,
        out_shape=_sds(blocks.shape, blocks.dtype),
        in_specs=[pl.BlockSpec(memory_space=pl.ANY)],
        out_specs=pl.BlockSpec(memory_space=pl.ANY),
        scratch_shapes=[pltpu.SemaphoreType.DMA((3,)), pltpu.SemaphoreType.DMA((3,)), pltpu.SemaphoreType.DMA],
    )(blocks)


def _sum_blocks(name, blocks, out_dtype=F32, tr=512):
    n, n_rows, n_cols = blocks.shape
    tr = min(tr, n_rows)
    assert n_rows % tr == 0

    def body(x_ref, o_ref):
        acc = x_ref[0].astype(F32)
        for s in range(1, n):
            acc = acc + x_ref[s].astype(F32)
        o_ref[...] = acc.astype(o_ref.dtype)

    return pl.pallas_call(
        body, name=name,
        grid=(n_rows // tr,),
        in_specs=[pl.BlockSpec((n, tr, n_cols), lambda i: (0, i, 0))],
        out_specs=pl.BlockSpec((tr, n_cols), lambda i: (i, 0)),
        out_shape=_sds((n_rows, n_cols), out_dtype),
        compiler_params=_params(("parallel",)),
    )(blocks)


def _add_pair(name, a, b):
    n, n_rows, n_cols = a.shape
    tr = 512
    assert n_rows % tr == 0

    def body(a_ref, b_ref, o_ref):
        o_ref[...] = (a_ref[...].astype(F32) + b_ref[...].astype(F32)).astype(BF16)

    spec = pl.BlockSpec((None, tr, n_cols), lambda j, i: (j, i, 0))
    return pl.pallas_call(
        body, name=name,
        grid=(n, n_rows // tr),
        in_specs=[spec, spec],
        out_specs=spec,
        out_shape=_sds(a.shape, BF16),
        compiler_params=_params(("parallel", "parallel")),
    )(a, b)


_BIG = (
    ("even_w_in", 2, 2, D_MODEL, 5120),
    ("s5_w_glu", 1, 2, S5_WIDTH, S5_WIDTH),
    ("even_w_out", 1, 2, D_MODEL, D_MODEL),
    ("odd_w_qkv", 2, 2, D_MODEL, QKV_WIDTH),
    ("odd_w_out", 1, 2, D_MODEL, D_MODEL),
    ("mlp_w_up", 2, DEPTH, D_MODEL, D_FF),
    ("mlp_w_down", 1, DEPTH, D_FF, D_MODEL),
)
_PACK_COLS = 1024


def _big_rows(nl, rows, cols):
    return nl * rows * cols // N_DEV // _PACK_COLS


def _unpack_gathered(gathered):
    out, off = {}, 0
    for name, axis, nl, rows, cols in _BIG:
        n = _big_rows(nl, rows, cols)
        part = gathered[:, off:off + n]
        off += n
        if axis == 2:
            w = part.reshape(N_DEV, nl, rows, cols // N_DEV).transpose(1, 2, 0, 3)
        else:
            w = part.reshape(N_DEV, nl, rows // N_DEV, cols).transpose(1, 0, 2, 3)
        out[name] = w.reshape(nl, rows, cols)
    return out


def _pack_by_owner(full):
    parts = []
    for name, axis, nl, rows, cols in _BIG:
        g = full[name].astype(BF16)
        if axis == 2:
            g = g.reshape(nl, rows, N_DEV, cols // N_DEV).transpose(2, 0, 1, 3)
        else:
            g = g.reshape(nl, N_DEV, rows // N_DEV, cols).transpose(1, 0, 2, 3)
        parts.append(g.reshape(N_DEV, -1, _PACK_COLS))
    return jnp.concatenate(parts, axis=1)


def _pack_small(arrs, row_mult=512):
    parts = []
    for a in arrs:
        f = a.astype(F32).reshape(-1)
        parts.append(jnp.pad(f, (0, (-f.shape[0]) % 128)))
    f = jnp.concatenate(parts)
    f = jnp.pad(f, (0, (-f.shape[0]) % (128 * row_mult)))
    return f.reshape(-1, 128)


def _unpack_small(flat, shapes):
    f = flat.reshape(-1)
    out, off = [], 0
    for s in shapes:
        n = math.prod(s)
        out.append(f[off:off + n].reshape(s))
        off += n + (-n) % 128
    return out


_WEIGHTS = ("even_norm", "even_w_in", "s5_lambda_re", "s5_lambda_im", "s5_log_dt", "s5_b_re", "s5_b_im",
            "s5_c_re", "s5_c_im", "s5_d", "s5_w_glu", "s5_b_glu", "hgrn_lower_bound", "hgrn_o_norm",
            "even_w_out", "odd_norm", "odd_w_qkv", "q_norm", "k_norm", "att_sinks", "odd_w_out",
            "mlp_norm", "mlp_w_up", "mlp_w_down")
_BIG_NAMES = tuple(b[0] for b in _BIG)
_SMALL_NAMES = tuple(n for n in _WEIGHTS if n not in _BIG_NAMES)


def _add_res(acc, res):
    return (acc + res,)


def _mlp_fwd(h, gain, w_up, w_down):
    xn, rstd = _rms_fwd("rms_fwd", h, gain)
    up, act = _mm("mm_up", xn, w_up, "nn", out_dtypes=(F32, BF16),
                  epi=lambda acc: (acc, jnp.square(jnp.maximum(acc, 0.0))))
    out = _mm("mm_down", act, w_down, "nn", epi=_add_res, extras=(h,))
    return out, (h, gain, xn, rstd, up, act, w_up, w_down)


def _mlp_bwd(cache, dh, dhb):
    h, gain, xn, rstd, up, act, w_up, w_down = cache
    dup = _mm("mm_dact", dhb, w_down, "nt", out_dtypes=(BF16,),
              epi=lambda acc, u: (acc * (2.0 * jnp.maximum(u, 0.0)),), extras=(up,))
    dw_down = _mm("mm_dw_down", act, dhb, "tn", out_dtypes=(BF16,))
    dxn = _mm("mm_dxn_up", dup, w_up, "nt")
    dw_up = _mm("mm_dw_up", xn, dup, "tn", out_dtypes=(BF16,))
    dh_in, dhb_in, dgain = _rms_bwd("rms_bwd", h, rstd, gain, dxn, dh)
    return dh_in, dhb_in, dgain, dw_up, dw_down


def _even_fwd(h, p):
    xn, rstd = _rms_fwd("rms_fwd", h, p["norm"])
    proj = _mm("mm_w_in", xn, p["w_in"], "nn")
    y_pre, z, s5_states = _s5_fwd("s5_fwd", proj, p["mats"])
    gate = _mm("mm_glu", z, p["w_glu"], "nn")
    (ya,) = _rowwise("glu_fwd", lambda y, gt, b: ((_gelu(y) * _sigmoid(gt + b),), ()),
                     [y_pre, gate], [p["b_glu"].reshape(1, -1)], [(S5_WIDTH, BF16)])
    o, yb, h_states = _hgrn_fwd("hgrn_fwd", proj, p["lb"].reshape(8, 1, 128), p["o_gain"].reshape(1, 128))
    ycat = jnp.concatenate([ya, yb], axis=1)
    out = _mm("mm_w_out", ycat, p["w_out"], "nn", epi=_add_res, extras=(h,))
    return out, (h, xn, rstd, proj, y_pre, z, s5_states, gate, o, h_states, ycat)


def _even_bwd(cache, p, dh, dhb):
    h, xn, rstd, proj, y_pre, z, s5_states, gate, o, h_states, ycat = cache
    g = {}
    dycat = _mm("mm_dy_out", dhb, p["w_out"], "nt")
    g["w_out"] = _mm("mm_dw_out", ycat, dhb, "tn", out_dtypes=(BF16,))
    dq, df, di, dg, dlb, dgain = _hgrn_bwd("hgrn_bwd", proj, p["lb"].reshape(8, 1, 128),
                                           p["o_gain"].reshape(1, 128), o, h_states, dycat)
    g["lb"] = dlb.reshape(-1)
    g["o_gain"] = jnp.sum(dgain, axis=0).reshape(-1)

    def glu_bwd1(dyc, y, gt, b):
        zf = _gelu(y)
        s = _sigmoid(gt + b)
        dya = dyc[:, :S5_WIDTH]
        d_gate = dya * zf * s * (1.0 - s)
        return (d_gate, dya * s), (_colsum(d_gate),)

    d_gate, dz_direct, db_glu = _rowwise("glu_bwd_gate", glu_bwd1, [dycat, y_pre, gate], [p["b_glu"].reshape(1, -1)],
                                         [(S5_WIDTH, BF16), (S5_WIDTH, F32)], [S5_WIDTH])
    g["b_glu"] = db_glu.reshape(-1)
    dz_gate = _mm("mm_dz_glu", d_gate, p["w_glu"], "nt")
    g["w_glu"] = _mm("mm_dw_glu", z, d_gate, "tn", out_dtypes=(BF16,))
    (dy_pre,) = _rowwise("glu_bwd_gelu", lambda a, b, y: (((a + b) * _gelu_grad(y),), ()),
                         [dz_direct, dz_gate, y_pre], [], [(S5_WIDTH, F32)])
    du, dbm, dcm, da, dd = _s5_bwd("s5_bwd", proj, dy_pre, s5_states, p["mats"])
    g["s5"] = (dbm, dcm, da, dd)
    dproj = jnp.concatenate([du, dq, df, di, dg], axis=1)
    dxn = _mm("mm_dxn_in", dproj, p["w_in"], "nt", tk=2560)
    g["w_in"] = _mm("mm_dw_in", xn, dproj, "tn", out_dtypes=(BF16,))
    dh_in, dhb_in, dnorm = _rms_bwd("rms_bwd", h, rstd, p["norm"], dxn, dh)
    g["norm"] = dnorm.reshape(-1)
    return dh_in, dhb_in, g


def _odd_fwd(h, p):
    xn, rstd = _rms_fwd("rms_fwd", h, p["norm"])
    qkv = _mm("mm_w_qkv", xn, p["w_qkv"], "nn", tn=1280)
    o = _swa_fwd("swa_fwd", qkv, p["q_gain"], p["k_gain"], p["sinks"], p["slopes"])
    out = _mm("mm_w_out", o, p["w_out"], "nn", epi=_add_res, extras=(h,))
    return out, (h, xn, rstd, qkv, o)


def _shift_up_block(x):
    return jnp.concatenate([x[ATT_BLOCK:], jnp.zeros((ATT_BLOCK, x.shape[1]), x.dtype)], axis=0)


def _odd_bwd(cache, p, dh, dhb):
    h, xn, rstd, qkv, o = cache
    g = {}
    d_o = _mm("mm_dy_out", dhb, p["w_out"], "nt")
    g["w_out"] = _mm("mm_dw_out", o, dhb, "tn", out_dtypes=(BF16,))
    dq, dkc, dkp, dvc, dvp, dsink, dqg, dkg = _swa_bwd("swa_bwd", qkv, p["q_gain"], p["k_gain"], p["sinks"],
                                                       p["slopes"], d_o)
    dk = (dkc.astype(F32) + _shift_up_block(dkp).astype(F32)).astype(BF16)
    dv = (dvc.astype(F32) + _shift_up_block(dvp).astype(F32)).astype(BF16)
    g["sinks"], g["q_gain"], g["k_gain"] = dsink[:, 0], dqg.reshape(-1), dkg.reshape(-1)
    dqkv = jnp.concatenate([dq, dk, dv], axis=1)
    dxn = _mm("mm_dxn_qkv", dqkv, p["w_qkv"], "nt", tk=1280)
    g["w_qkv"] = _mm("mm_dw_qkv", xn, dqkv, "tn", out_dtypes=(BF16,), tn=1280)
    dh_in, dhb_in, dnorm = _rms_bwd("rms_bwd", h, rstd, p["norm"], dxn, dh)
    g["norm"] = dnorm.reshape(-1)
    return dh_in, dhb_in, g


def kernel(x, even_norm, even_w_in, s5_lambda_re, s5_lambda_im, s5_log_dt, s5_b_re, s5_b_im, s5_c_re, s5_c_im, s5_d, s5_w_glu, s5_b_glu, hgrn_lower_bound, hgrn_o_norm, even_w_out, odd_norm, odd_w_qkv, q_norm, k_norm, att_sinks, odd_w_out, mlp_norm, mlp_w_up, mlp_w_down, loss_target, m_even_norm, m_even_w_in, m_s5_lambda_re, m_s5_lambda_im, m_s5_log_dt, m_s5_b_re, m_s5_b_im, m_s5_c_re, m_s5_c_im, m_s5_d, m_s5_w_glu, m_s5_b_glu, m_hgrn_lower_bound, m_hgrn_o_norm, m_even_w_out, m_odd_norm, m_odd_w_qkv, m_q_norm, m_k_norm, m_att_sinks, m_odd_w_out, m_mlp_norm, m_mlp_w_up, m_mlp_w_down, v_even_norm, v_even_w_in, v_s5_lambda_re, v_s5_lambda_im, v_s5_log_dt, v_s5_b_re, v_s5_b_im, v_s5_c_re, v_s5_c_im, v_s5_d, v_s5_w_glu, v_s5_b_glu, v_hgrn_lower_bound, v_hgrn_o_norm, v_even_w_out, v_odd_norm, v_odd_w_qkv, v_q_norm, v_k_norm, v_att_sinks, v_odd_w_out, v_mlp_norm, v_mlp_w_up, v_mlp_w_down):
    a = dict(locals())
    n_rows = x.shape[1]
    xi, yi, ci = _mesh_pos()
    me = 4 * xi + 2 * yi + ci

    shard = jnp.concatenate([a[n].astype(BF16).reshape(-1, _PACK_COLS) for n in _BIG_NAMES], axis=0)
    w = _unpack_gathered(_allgather("gather_weights", shard))
    odd_shard = jnp.pad(odd_norm, ((0, 6), (0, 0)))
    odd_norm_full = _allgather("gather_odd_norm", odd_shard)[:, :2].transpose(1, 0, 2).reshape(2, D_MODEL)

    lower_bounds, lb_vjp = jax.vjp(_hgrn_lower_bounds, hgrn_lower_bound)
    slopes = _alibi_slopes()
    even_p, odd_p, s5_vjps = [], [], []
    for j in range(2):
        disc, vjp = jax.vjp(_s5_discretize, s5_lambda_re[j], s5_lambda_im[j], s5_log_dt[j], s5_b_re[j], s5_b_im[j])
        s5_vjps.append(vjp)
        even_p.append(dict(norm=even_norm[j], w_in=w["even_w_in"][j], w_glu=w["s5_w_glu"][j], b_glu=s5_b_glu[j],
                           mats=_s5_matrices(*disc, s5_c_re[j], s5_c_im[j], s5_d[j]),
                           lb=lower_bounds[j], o_gain=hgrn_o_norm[j], w_out=w["even_w_out"][j]))
        odd_p.append(dict(norm=odd_norm_full[j], w_qkv=w["odd_w_qkv"][j], q_gain=q_norm[j], k_gain=k_norm[j],
                          sinks=att_sinks[j], slopes=slopes, w_out=w["odd_w_out"][j]))

    h = x.reshape(n_rows, D_MODEL)
    caches = []
    for layer in range(DEPTH):
        j = layer // 2
        h, c_mix = (_even_fwd(h, even_p[j]) if layer % 2 == 0 else _odd_fwd(h, odd_p[j]))
        h, c_mlp = _mlp_fwd(h, mlp_norm[layer], w["mlp_w_up"][layer], w["mlp_w_down"][layer])
        caches.append((c_mix, c_mlp))
    dh, dhb, sq = _loss_head(h, loss_target.reshape(n_rows, D_MODEL))
    loss = lax.psum(0.5 * sq[0, 0] / D_MODEL, ("x", "y", "c"))

    mix_g, mlp_g = [None] * DEPTH, [None] * DEPTH
    for layer in reversed(range(DEPTH)):
        j = layer // 2
        c_mix, c_mlp = caches[layer]
        dh, dhb, d_mlp_norm, dw_up, dw_down = _mlp_bwd(c_mlp, dh, dhb)
        mlp_g[layer] = (d_mlp_norm.reshape(-1), dw_up, dw_down)
        if layer % 2 == 0:
            dh, dhb, mix_g[layer] = _even_bwd(c_mix, even_p[j], dh, dhb)
        else:
            dh, dhb, mix_g[layer] = _odd_bwd(c_mix, odd_p[j], dh, dhb)
    grad_x = dh.reshape(x.shape)

    ev, od = [mix_g[0], mix_g[2]], [mix_g[1], mix_g[3]]
    full = {
        "even_w_in": jnp.stack([g["w_in"] for g in ev]), "s5_w_glu": jnp.stack([g["w_glu"] for g in ev]),
        "even_w_out": jnp.stack([g["w_out"] for g in ev]), "odd_w_qkv": jnp.stack([g["w_qkv"] for g in od]),
        "odd_w_out": jnp.stack([g["w_out"] for g in od]), "mlp_w_up": jnp.stack([g[1] for g in mlp_g]),
        "mlp_w_down": jnp.stack([g[2] for g in mlp_g]),
    }
    by_owner = _pack_by_owner(full)
    by_chip = by_owner.reshape(4, 2, by_owner.shape[1], _PACK_COLS)
    keep = lax.dynamic_index_in_dim(by_chip, ci, axis=1, keepdims=False)
    give = lax.dynamic_index_in_dim(by_chip, 1 - ci, axis=1, keepdims=False)
    chip_sum = _add_pair("add_sibling_grads", keep, _swap_with_sibling("swap_big_grads", give))
    big_flat = _sum_blocks("sum_big_grads", _exchange_chips("exchange_big_grads", chip_sum))
    grads, off = {}, 0
    for name, axis, nl, rows, cols in _BIG:
        n = _big_rows(nl, rows, cols)
        grads[name] = big_flat[off:off + n].reshape(a[name].shape)
        off += n

    s5_g = []
    for j in range(2):
        dar, dai, dbbr, dbbi, dcr, dci, dd = _s5_unpack_grads(*ev[j]["s5"])
        s5_g.append(tuple(s5_vjps[j]((dar, dai, dbbr, dbbi))) + (dcr, dci, dd))
    (d_lb_param,) = lb_vjp(jnp.stack([g["lb"] for g in ev]))
    small = {
        "even_norm": jnp.stack([g["norm"] for g in ev]),
        "s5_lambda_re": jnp.stack([g[0] for g in s5_g]), "s5_lambda_im": jnp.stack([g[1] for g in s5_g]),
        "s5_log_dt": jnp.stack([g[2] for g in s5_g]), "s5_b_re": jnp.stack([g[3] for g in s5_g]),
        "s5_b_im": jnp.stack([g[4] for g in s5_g]), "s5_c_re": jnp.stack([g[5] for g in s5_g]),
        "s5_c_im": jnp.stack([g[6] for g in s5_g]), "s5_d": jnp.stack([g[7] for g in s5_g]),
        "s5_b_glu": jnp.stack([g["b_glu"] for g in ev]), "hgrn_lower_bound": d_lb_param,
        "hgrn_o_norm": jnp.stack([g["o_gain"] for g in ev]), "odd_norm": jnp.stack([g["norm"] for g in od]),
        "q_norm": jnp.stack([g["q_gain"] for g in od]), "k_norm": jnp.stack([g["k_gain"] for g in od]),
        "att_sinks": jnp.stack([g["sinks"] for g in od]), "mlp_norm": jnp.stack([g[0] for g in mlp_g]),
    }
    small_shapes = [small[n].shape for n in _SMALL_NAMES]
    small_sum = _sum_blocks("sum_small_grads",_allgather("gather_small_grads", _pack_small([small[n] for n in _SMALL_NAMES])))
    for n, g in zip(_SMALL_NAMES, _unpack_small(small_sum, small_shapes)):
        grads[n] = g
    grads["odd_norm"] = lax.dynamic_slice_in_dim(grads["odd_norm"], me * (D_MODEL // N_DEV), D_MODEL // N_DEV, axis=1)

    delta, new_m, new_v = {}, {}, {}
    for name in _BIG_NAMES:
        to2d = lambda t, c=a[name].shape[-1]: t.reshape(-1, c)
        d_, m_, v_ = _adamw("adamw_" + name, to2d(a[name]), to2d(grads[name]), to2d(a["m_" + name]), to2d(a["v_" + name]))
        delta[name], new_m[name], new_v[name] = (t.reshape(a[name].shape) for t in (d_, m_, v_))
    packed = [_pack_small([src[n] for n in _SMALL_NAMES])
              for src in (a, grads, {n: a["m_" + n] for n in _SMALL_NAMES}, {n: a["v_" + n] for n in _SMALL_NAMES})]
    shapes = [a[n].shape for n in _SMALL_NAMES]
    for dst, flat in zip((delta, new_m, new_v), _adamw("adamw_small", *packed)):
        for n, t in zip(_SMALL_NAMES, _unpack_small(flat, shapes)):
            dst[n] = t

    return (loss, grad_x, *[grads[n] for n in _WEIGHTS], *[delta[n] for n in _WEIGHTS],
            *[new_m[n] for n in _WEIGHTS], *[new_v[n] for n in _WEIGHTS])
```

```python
import math

import jax
import jax.numpy as jnp
from jax import lax
from jax.experimental import pallas as pl
from jax.experimental.pallas import tpu as pltpu

F32 = jnp.float32
BF16 = jnp.bfloat16
MESH = pl.DeviceIdType.MESH

D_MODEL = 2048
DEPTH = 4
EPS = 1e-6
S5_WIDTH = 1024
S5_GROUPS = 64
S5_STATE = 64
S5_GROUP_SIZE = 16
S5_MIN_DECAY = 1e-4
S5_CHUNK = 128
S5_LEVELS = 7
HGRN_WIDTH = 1024
HGRN_HEADS = 8
HGRN_DIM = 128
HGRN_SUB = 16
HGRN_BLOCK = 128
ATT_HEADS = 32
ATT_KV = 4
ATT_DIM = 64
ATT_BLOCK = 128
QKV_WIDTH = (ATT_HEADS + 2 * ATT_KV) * ATT_DIM
D_FF = 4 * D_MODEL
N_DEV = 8
NEG = -1e30
VMEM_LIMIT = 56 * 1024 * 1024

ADAM_LR, ADAM_B1, ADAM_B2, ADAM_EPS, ADAM_WD, ADAM_STEP = 0.001, 0.9, 0.999, 1e-08, 0.01, 10


def _params(sem=None):
    return pltpu.CompilerParams(dimension_semantics=sem, vmem_limit_bytes=VMEM_LIMIT)


def _sds(shape, dtype):
    return jax.ShapeDtypeStruct(shape, dtype)


def _mm(name, a, b, mode, out_dtypes=(F32,), epi=None, extras=(), tm=512, tn=1024, tk=2048,
        mkn=None, b_block=None, o_block=None):
    if mkn is not None:
        m, k, n = mkn
    elif mode == "nn":
        (m, k), n = a.shape, b.shape[1]
    elif mode == "nt":
        (m, k), n = a.shape, b.shape[0]
    else:
        (k, m), n = a.shape, b.shape[1]
    tm, tn, tk = min(tm, m), min(tn, n), min(tk, k)
    assert m % tm == 0 and n % tn == 0 and k % tk == 0, (name, m, n, k)
    nk = k // tk
    if mode == "nn":
        a_spec = pl.BlockSpec((tm, tk), lambda i, j, kk: (i, kk))
        b_spec = pl.BlockSpec((tk, tn), lambda i, j, kk: (kk, j))
        dims = (((1,), (0,)), ((), ()))
    elif mode == "nt":
        a_spec = pl.BlockSpec((tm, tk), lambda i, j, kk: (i, kk))
        b_spec = pl.BlockSpec((tn, tk), lambda i, j, kk: (j, kk))
        dims = (((1,), (1,)), ((), ()))
    else:
        a_spec = pl.BlockSpec((tk, tm), lambda i, j, kk: (kk, i))
        b_spec = pl.BlockSpec((tk, tn), lambda i, j, kk: (kk, j))
        dims = (((0,), (0,)), ((), ()))
    o_spec = pl.BlockSpec((tm, tn), lambda i, j, kk: (i, j))
    if b_block is not None:
        b_spec = b_block
    out_specs = [o_spec] * len(out_dtypes)
    out_shape = [_sds((m, n), dt) for dt in out_dtypes]
    if o_block is not None:
        assert len(out_dtypes) == 1 and not extras
        out_specs, out_shape = [o_block[0]], [_sds(o_block[1], out_dtypes[0])]
    n_ex, n_out = len(extras), len(out_dtypes)

    def body(*refs):
        a_ref, b_ref = refs[0], refs[1]
        ex_refs = refs[2:2 + n_ex]
        out_refs = refs[2 + n_ex:2 + n_ex + n_out]
        acc_ref = refs[2 + n_ex + n_out] if nk > 1 else None
        av, bv = a_ref[...], b_ref[...]
        if av.dtype != BF16:
            av = av.astype(BF16)
        if bv.dtype != BF16:
            bv = bv.astype(BF16)
        part = lax.dot_general(av, bv, dims, preferred_element_type=F32)

        def finish(acc):
            outs = epi(acc, *[r[...] for r in ex_refs]) if epi is not None else (acc,)
            for r, o in zip(out_refs, outs):
                r[...] = o.astype(r.dtype)

        if nk == 1:
            finish(part)
        else:
            kk = pl.program_id(2)

            @pl.when(kk == 0)
            def _():
                acc_ref[...] = part

            @pl.when(kk > 0)
            def _():
                acc_ref[...] += part

            @pl.when(kk == nk - 1)
            def _():
                finish(acc_ref[...])

    outs = pl.pallas_call(
        body, name=name,
        grid=(m // tm, n // tn, nk),
        in_specs=[a_spec, b_spec] + [o_spec] * n_ex,
        out_specs=out_specs,
        out_shape=out_shape,
        scratch_shapes=[pltpu.VMEM((tm, tn), F32)] if nk > 1 else [],
        compiler_params=_params(("parallel", "parallel", "arbitrary")),
    )(a, b, *extras)
    return outs[0] if n_out == 1 else outs


def _rowwise(name, fn, rows, vecs, outs, accs=(), tr=256):
    n_rows = rows[0].shape[0]
    tr = min(tr, n_rows)
    assert n_rows % tr == 0
    n_r, n_v, n_o, n_a = len(rows), len(vecs), len(outs), len(accs)

    def body(*refs):
        ins = [r[...] for r in refs[:n_r + n_v]]
        o_refs = refs[n_r + n_v:n_r + n_v + n_o]
        a_refs = refs[n_r + n_v + n_o:]
        ro, ao = fn(*ins)
        for r, o in zip(o_refs, ro):
            r[...] = o.astype(r.dtype)
        if n_a:
            step = pl.program_id(0)

            @pl.when(step == 0)
            def _():
                for r, o in zip(a_refs, ao):
                    r[...] = o

            @pl.when(step > 0)
            def _():
                for r, o in zip(a_refs, ao):
                    r[...] += o

    res = pl.pallas_call(
        body, name=name,
        grid=(n_rows // tr,),
        in_specs=[pl.BlockSpec((tr, r.shape[1]), lambda i: (i, 0)) for r in rows]
        + [pl.BlockSpec(v.shape, lambda i: (0, 0)) for v in vecs],
        out_specs=[pl.BlockSpec((tr, w), lambda i: (i, 0)) for w, _ in outs]
        + [pl.BlockSpec((1, w), lambda i: (0, 0)) for w in accs],
        out_shape=[_sds((n_rows, w), dt) for w, dt in outs] + [_sds((1, w), F32) for w in accs],
        compiler_params=_params(("arbitrary",)),
    )(*rows, *vecs)
    return res


def _colsum(x):
    return jnp.sum(x, axis=0, keepdims=True)


def _sigmoid(x):
    return 1.0 / (1.0 + jnp.exp(-x))


_GELU_C = math.sqrt(2.0 / math.pi)


def _gelu(y):
    return 0.5 * y * (1.0 + jnp.tanh(_GELU_C * (y + 0.044715 * y * y * y)))


def _gelu_grad(y):
    t = jnp.tanh(_GELU_C * (y + 0.044715 * y * y * y))
    return 0.5 * (1.0 + t) + 0.5 * y * (1.0 - t * t) * _GELU_C * (1.0 + 3.0 * 0.044715 * y * y)


def _rms_fwd(name, h, gain):
    def fn(x, g):
        r = lax.rsqrt(jnp.mean(x * x, axis=1, keepdims=True) + EPS)
        return (x * r * g, r), ()
    return _rowwise(name, fn, [h], [gain.reshape(1, -1)], [(h.shape[1], BF16), (1, F32)])


def _rms_bwd(name, h, rstd, gain, dxn, dres):
    def fn(x, r, dy, dr, g):
        xh = x * r
        gdy = dy * g
        dx = r * (gdy - xh * jnp.mean(gdy * xh, axis=1, keepdims=True)) + dr
        return (dx, dx), (_colsum(dy * xh),)
    w = h.shape[1]
    return _rowwise(name, fn, [h, rstd, dxn, dres], [gain.reshape(1, -1)], [(w, F32), (w, BF16)], [w])


def _loss_head(h, target):
    w = h.shape[1]

    def fn(x, t):
        e = x - t
        return (e * (1.0 / w), e * (1.0 / w)), (jnp.zeros((1, 128), F32) + jnp.sum(e * e),)
    return _rowwise("loss_head", fn, [h, target], [], [(w, F32), (w, BF16)], [128])


def _adamw(name, w, g, m, v):
    c1 = 1.0 - ADAM_B1 ** ADAM_STEP
    c2 = 1.0 - ADAM_B2 ** ADAM_STEP

    def fn(w_, g_, m_, v_):
        mn = ADAM_B1 * m_ + (1.0 - ADAM_B1) * g_
        vn = ADAM_B2 * v_ + (1.0 - ADAM_B2) * (g_ * g_)
        delta = -ADAM_LR * ((mn / c1) / (jnp.sqrt(vn / c2) + ADAM_EPS) + ADAM_WD * w_)
        return (delta, mn, vn), ()
    c = w.shape[1]
    return _rowwise(name, fn, [w, g, m, v], [], [(c, F32)] * 3)


def _s5_discretize(lam_re, lam_im, log_dt, b_re, b_im):
    lr = jnp.minimum(lam_re, -S5_MIN_DECAY)
    li = lam_im
    dt = jnp.exp(log_dt)[:, None]
    mag = jnp.exp(lr * dt)
    ar = mag * jnp.cos(li * dt)
    ai = mag * jnp.sin(li * dt)
    den = lr * lr + li * li
    zr = ((ar - 1.0) * lr + ai * li) / den
    zi = (ai * lr - (ar - 1.0) * li) / den
    bbr = zr[..., None] * b_re - zi[..., None] * b_im
    bbi = zr[..., None] * b_im + zi[..., None] * b_re
    return ar, ai, bbr, bbi


def _s5_matrices(ar, ai, bbr, bbi, c_re, c_im, d_skip):
    eye = jnp.eye(8, dtype=F32)
    bt = jnp.stack([bbr, bbi], axis=1).transpose(0, 3, 1, 2)
    bt = bt.reshape(8, 8, 16, 1, 2, 64) * eye[None, :, None, :, None, None]
    bm8 = bt.reshape(8, 8, 16, 4, 2, 2, 64).transpose(0, 1, 2, 3, 5, 4, 6).reshape(8, 128, 1024)
    ct = jnp.stack([c_re, -c_im], axis=1).transpose(0, 1, 3, 2)
    ct = ct.reshape(8, 8, 2, 64, 1, 16) * eye[None, :, None, None, :, None]
    cm8 = ct.reshape(8, 4, 2, 2, 64, 8, 16).transpose(0, 1, 3, 2, 4, 5, 6).reshape(8, 1024, 128)
    prs, pis = [], []
    pr, pi = ar, ai
    for _ in range(S5_LEVELS):
        prs.append(pr.reshape(8, 512))
        pis.append(pi.reshape(8, 512))
        pr, pi = pr * pr - pi * pi, 2.0 * pr * pi
    prs.append(jnp.zeros_like(prs[0]))
    pis.append(jnp.zeros_like(pis[0]))
    return (bm8.astype(BF16), cm8.astype(BF16), jnp.stack(prs, axis=1), jnp.stack(pis, axis=1),
            d_skip.reshape(8, 1, 128))


def _s5_unpack_grads(dbm8, dcm8, da, dd):
    db = dbm8.reshape(8, 8, 16, 4, 2, 2, 64).transpose(0, 1, 2, 3, 5, 4, 6).reshape(8, 8, 16, 8, 2, 64)
    db = jnp.einsum("agcgqp->agcqp", db).reshape(S5_GROUPS, 16, 2, 64)
    dc = dcm8.reshape(8, 4, 2, 2, 64, 8, 16).transpose(0, 1, 3, 2, 4, 5, 6).reshape(8, 8, 2, 64, 8, 16)
    dc = jnp.einsum("agqpgc->agqpc", dc).reshape(S5_GROUPS, 2, 64, 16)
    dar = da[:, 0, :].reshape(S5_GROUPS, 64)
    dai = da[:, 1, :].reshape(S5_GROUPS, 64)
    return (dar, dai, db[:, :, 0, :].transpose(0, 2, 1), db[:, :, 1, :].transpose(0, 2, 1),
            dc[:, 0].transpose(0, 2, 1), -dc[:, 1].transpose(0, 2, 1), dd.reshape(S5_GROUPS, 16))


def _shift_rows(x, s, row, down):
    t = x.shape[0]
    if s % 8 == 0:
        z = jnp.zeros((s, x.shape[1]), x.dtype)
        return jnp.concatenate([z, x[:t - s]], axis=0) if down else jnp.concatenate([x[s:], z], axis=0)
    if down:
        return jnp.where(row >= s, pltpu.roll(x, s, 0), 0.0)
    return jnp.where(row < t - s, pltpu.roll(x, t - s, 0), 0.0)


def _s5_scan(xr, xi, pr, pi, cr, ci, row, conj):
    t = xr[0].shape[0]
    sg = -1.0 if conj else 1.0
    edge = (t - 1) if conj else 0
    n = len(xr)
    for k in range(n):
        sl = slice(128 * k, 128 * (k + 1))
        p_r, p_i = pr[0:1, sl], sg * pi[0:1, sl]
        xr[k] = xr[k] + jnp.where(row == edge, p_r * cr[k] - p_i * ci[k], 0.0)
        xi[k] = xi[k] + jnp.where(row == edge, p_r * ci[k] + p_i * cr[k], 0.0)
    for lvl in range(S5_LEVELS):
        s = 1 << lvl
        for k in range(n):
            sl = slice(128 * k, 128 * (k + 1))
            p_r, p_i = pr[lvl:lvl + 1, sl], sg * pi[lvl:lvl + 1, sl]
            sr = _shift_rows(xr[k], s, row, not conj)
            si = _shift_rows(xi[k], s, row, not conj)
            xr[k] = xr[k] + p_r * sr - p_i * si
            xi[k] = xi[k] + p_r * si + p_i * sr
    return xr, xi


def _s5_fwd(name, proj, mats):
    bm8, cm8, p1, p2, d8 = mats
    n_rows = proj.shape[0]
    t = S5_CHUNK
    nch = n_rows // t

    def body(u_ref, bm_ref, cm_ref, pr_ref, pi_ref, d_ref, y_ref, z_ref, st_ref, carry):
        @pl.when(pl.program_id(1) == 0)
        def _():
            carry[...] = jnp.zeros_like(carry)

        cv = carry[...]
        st_ref[...] = cv
        u = u_ref[...]
        bu = jnp.dot(u.astype(BF16), bm_ref[...], preferred_element_type=F32)
        row = lax.broadcasted_iota(jnp.int32, (t, 128), 0)
        tile = lambda v, j: v[:, 128 * j:128 * (j + 1)]
        xr, xi = _s5_scan([tile(bu, 2 * k) for k in range(4)], [tile(bu, 2 * k + 1) for k in range(4)],
                          pr_ref[...], pi_ref[...], [tile(cv, 2 * k)[0:1] for k in range(4)],
                          [tile(cv, 2 * k + 1)[0:1] for k in range(4)], row, False)
        xall = jnp.concatenate([v for k in range(4) for v in (xr[k], xi[k])], axis=1)
        carry[...] = jnp.broadcast_to(xall[t - 1:t, :], (8, 1024))
        y = jnp.dot(xall.astype(BF16), cm_ref[...], preferred_element_type=F32) + d_ref[...] * u
        y_ref[...] = y
        z_ref[...] = _gelu(y).astype(BF16)

    return pl.pallas_call(
        body, name=name,
        grid=(8, nch),
        in_specs=[
            pl.BlockSpec((t, 128), lambda g, c: (c, g)),
            pl.BlockSpec((None, 128, 1024), lambda g, c: (g, 0, 0)),
            pl.BlockSpec((None, 1024, 128), lambda g, c: (g, 0, 0)),
            pl.BlockSpec((None, 8, 512), lambda g, c: (g, 0, 0)),
            pl.BlockSpec((None, 8, 512), lambda g, c: (g, 0, 0)),
            pl.BlockSpec((None, 1, 128), lambda g, c: (g, 0, 0)),
        ],
        out_specs=[
            pl.BlockSpec((t, 128), lambda g, c: (c, g)),
            pl.BlockSpec((t, 128), lambda g, c: (c, g)),
            pl.BlockSpec((None, None, 8, 1024), lambda g, c: (g, c, 0, 0)),
        ],
        out_shape=[_sds((n_rows, S5_WIDTH), F32), _sds((n_rows, S5_WIDTH), BF16), _sds((8, nch, 8, 1024), F32)],
        scratch_shapes=[pltpu.VMEM((8, 1024), F32)],
        compiler_params=_params(("parallel", "arbitrary")),
    )(proj, bm8, cm8, p1, p2, d8)


def _s5_bwd(name, proj, dy, states, mats):
    bm8, cm8, p1, p2, d8 = mats
    n_rows = proj.shape[0]
    t = S5_CHUNK
    nch = n_rows // t
    nt_dims = (((1,), (1,)), ((), ()))
    tn_dims = (((0,), (0,)), ((), ()))

    def body(u_ref, dy_ref, st_ref, bm_ref, cm_ref, pr_ref, pi_ref, d_ref,
             du_ref, dbm_ref, dcm_ref, da_ref, dd_ref, gcarry):
        @pl.when(pl.program_id(1) == 0)
        def _():
            gcarry[...] = jnp.zeros_like(gcarry)
            dbm_ref[...] = jnp.zeros_like(dbm_ref)
            dcm_ref[...] = jnp.zeros_like(dcm_ref)
            da_ref[...] = jnp.zeros_like(da_ref)
            dd_ref[...] = jnp.zeros_like(dd_ref)

        u = u_ref[...]
        dyv = dy_ref[...]
        ub, dyb = u.astype(BF16), dyv.astype(BF16)
        bu = jnp.dot(ub, bm_ref[...], preferred_element_type=F32)
        dxd = lax.dot_general(dyb, cm_ref[...], nt_dims, preferred_element_type=F32)
        row = lax.broadcasted_iota(jnp.int32, (t, 128), 0)
        tile = lambda v, j: v[:, 128 * j:128 * (j + 1)]
        prv, piv, cv, gv = pr_ref[...], pi_ref[...], st_ref[...], gcarry[...]
        cr = [tile(cv, 2 * k)[0:1] for k in range(4)]
        ci = [tile(cv, 2 * k + 1)[0:1] for k in range(4)]
        xr, xi = _s5_scan([tile(bu, 2 * k) for k in range(4)], [tile(bu, 2 * k + 1) for k in range(4)],
                          prv, piv, cr, ci, row, False)
        gr, gi = _s5_scan([tile(dxd, 2 * k) for k in range(4)], [tile(dxd, 2 * k + 1) for k in range(4)],
                          prv, piv, [tile(gv, 2 * k)[0:1] for k in range(4)],
                          [tile(gv, 2 * k + 1)[0:1] for k in range(4)], row, True)
        dar, dai = [], []
        for k in range(4):
            xpr = jnp.where(row >= 1, pltpu.roll(xr[k], 1, 0), cr[k])
            xpi = jnp.where(row >= 1, pltpu.roll(xi[k], 1, 0), ci[k])
            dar.append(_colsum(gr[k] * xpr + gi[k] * xpi))
            dai.append(_colsum(gi[k] * xpr - gr[k] * xpi))
        xall = jnp.concatenate([v for k in range(4) for v in (xr[k], xi[k])], axis=1).astype(BF16)
        gf = jnp.concatenate([v for k in range(4) for v in (gr[k], gi[k])], axis=1)
        gcarry[...] = jnp.broadcast_to(gf[0:1, :], (8, 1024))
        gall = gf.astype(BF16)
        dcm_ref[...] += lax.dot_general(xall, dyb, tn_dims, preferred_element_type=F32)
        dbm_ref[...] += lax.dot_general(ub, gall, tn_dims, preferred_element_type=F32)
        du = lax.dot_general(gall, bm_ref[...], nt_dims, preferred_element_type=F32) + d_ref[...] * dyv
        du_ref[...] = du.astype(BF16)
        dd_ref[...] += _colsum(dyv * u)
        da_ref[0:1, :] += jnp.concatenate(dar, axis=1)
        da_ref[1:2, :] += jnp.concatenate(dai, axis=1)

    rev = lambda g, c: (nch - 1 - c, g)
    return pl.pallas_call(
        body, name=name,
        grid=(8, nch),
        in_specs=[
            pl.BlockSpec((t, 128), rev),
            pl.BlockSpec((t, 128), rev),
            pl.BlockSpec((None, None, 8, 1024), lambda g, c: (g, nch - 1 - c, 0, 0)),
            pl.BlockSpec((None, 128, 1024), lambda g, c: (g, 0, 0)),
            pl.BlockSpec((None, 1024, 128), lambda g, c: (g, 0, 0)),
            pl.BlockSpec((None, 8, 512), lambda g, c: (g, 0, 0)),
            pl.BlockSpec((None, 8, 512), lambda g, c: (g, 0, 0)),
            pl.BlockSpec((None, 1, 128), lambda g, c: (g, 0, 0)),
        ],
        out_specs=[
            pl.BlockSpec((t, 128), rev),
            pl.BlockSpec((None, 128, 1024), lambda g, c: (g, 0, 0)),
            pl.BlockSpec((None, 1024, 128), lambda g, c: (g, 0, 0)),
            pl.BlockSpec((None, 8, 512), lambda g, c: (g, 0, 0)),
            pl.BlockSpec((None, 1, 128), lambda g, c: (g, 0, 0)),
        ],
        out_shape=[_sds((n_rows, S5_WIDTH), BF16), _sds((8, 128, 1024), F32), _sds((8, 1024, 128), F32),
                   _sds((8, 8, 512), F32), _sds((8, 1, 128), F32)],
        scratch_shapes=[pltpu.VMEM((8, 1024), F32)],
        compiler_params=_params(("parallel", "arbitrary")),
    )(proj, dy, states, bm8, cm8, p1, p2, d8)


def _hgrn_lower_bounds(lb_param):
    p = jax.nn.softmax(lb_param, axis=0)
    return jnp.cumsum(p, axis=0) - p[0:1]


def _prefix16(x, r16):
    for s in (1, 2, 4, 8):
        x = x + jnp.where(r16 >= s, pltpu.roll(x, s, 0), 0.0)
    return x


def _suffix16(x, r16):
    n = x.shape[0]
    for s in (1, 2, 4, 8):
        x = x + jnp.where(r16 < HGRN_SUB - s, pltpu.roll(x, n - s, 0), 0.0)
    return x


_NT = (((1,), (1,)), ((), ()))
_TN = (((0,), (0,)), ((), ()))


def _dotf(a, b, dims=(((1,), (0,)), ((), ()))):
    return lax.dot_general(a.astype(BF16), b.astype(BF16), dims, preferred_element_type=F32)


def _hgrn_specs(n_blocks, rev):
    r = HGRN_BLOCK
    blk = (lambda b: n_blocks - 1 - b) if rev else (lambda b: b)
    proj_specs = [pl.BlockSpec((r, 128), (lambda h, b, c=c: (blk(b), 8 * c + h))) for c in (1, 2, 3, 4)]
    lb_spec = pl.BlockSpec((None, 1, 128), lambda h, b: (h, 0, 0))
    gain_spec = pl.BlockSpec((1, 128), lambda h, b: (0, 0))
    row_spec = pl.BlockSpec((r, 128), lambda h, b: (blk(b), h))
    st_spec = pl.BlockSpec((None, None, 128, 128), lambda h, b: (h, blk(b), 0, 0))
    return proj_specs, lb_spec, gain_spec, row_spec, st_spec, blk


def _hgrn_fwd(name, proj, lb, gain):
    n_rows = proj.shape[0]
    r = HGRN_BLOCK
    nb = n_rows // r
    nsub = r // HGRN_SUB
    proj_specs, lb_spec, gain_spec, row_spec, st_spec, _ = _hgrn_specs(nb, False)

    def body(q_ref, f_ref, i_ref, g_ref, lb_ref, gain_ref, o_ref, y_ref, st_ref, st_scr):
        @pl.when(pl.program_id(1) == 0)
        def _():
            st_scr[...] = jnp.zeros_like(st_scr)

        st_ref[...] = st_scr[...]
        q, f, v, g = q_ref[...], f_ref[...], i_ref[...], g_ref[...]
        lbv = lb_ref[...]
        qs = q * _sigmoid(q)
        fg = lbv + (1.0 - lbv) * _sigmoid(f)
        kk = 1.0 - fg
        r16 = lax.broadcasted_iota(jnp.int32, (r, 128), 0) & (HGRN_SUB - 1)
        b = _prefix16(jnp.log(fg), r16)
        qh = qs * jnp.exp(b)
        rs = lax.broadcasted_iota(jnp.int32, (HGRN_SUB, 128), 0)
        st = st_scr[...]
        outs = []
        for i in range(nsub):
            sl = slice(HGRN_SUB * i, HGRN_SUB * (i + 1))
            qsi, kki, vi, bi = qs[sl], kk[sl], v[sl], b[sl]
            o_i = _dotf(qh[sl], st, _NT)
            for s in range(HGRN_SUB):
                e = jnp.exp(jnp.where(rs >= s, bi - bi[s:s + 1], NEG))
                col = jnp.sum(qsi * e * kki[s:s + 1], axis=1, keepdims=True)
                o_i = o_i + col * vi[s:s + 1]
            bl = bi[HGRN_SUB - 1:HGRN_SUB]
            st = st * jnp.exp(bl) + _dotf(vi, kki * jnp.exp(bl - bi), _TN)
            outs.append(o_i)
        st_scr[...] = st
        o = jnp.concatenate(outs, axis=0)
        o_ref[...] = o
        rn = lax.rsqrt(jnp.mean(o * o, axis=1, keepdims=True) + EPS)
        y_ref[...] = (o * rn * gain_ref[...] * (g * _sigmoid(g))).astype(BF16)

    return pl.pallas_call(
        body, name=name,
        grid=(HGRN_HEADS, nb),
        in_specs=proj_specs + [lb_spec, gain_spec],
        out_specs=[row_spec, row_spec, st_spec],
        out_shape=[_sds((n_rows, HGRN_WIDTH), F32), _sds((n_rows, HGRN_WIDTH), BF16),
                   _sds((HGRN_HEADS, nb, 128, 128), F32)],
        scratch_shapes=[pltpu.VMEM((128, 128), F32)],
        compiler_params=_params(("parallel", "arbitrary")),
    )(proj, proj, proj, proj, lb, gain)


def _hgrn_bwd(name, proj, lb, gain, o_saved, states, dycat):
    n_rows = proj.shape[0]
    r = HGRN_BLOCK
    nb = n_rows // r
    nsub = r // HGRN_SUB
    proj_specs, lb_spec, gain_spec, row_spec, st_spec, blk = _hgrn_specs(nb, True)
    dy_spec = pl.BlockSpec((r, 128), lambda h, b: (blk(b), 8 + h))
    acc_spec = pl.BlockSpec((None, 1, 128), lambda h, b: (h, 0, 0))

    def body(q_ref, f_ref, i_ref, g_ref, lb_ref, gain_ref, o_ref, st_ref, dy_ref,
             dq_ref, df_ref, di_ref, dg_ref, dlb_ref, dgain_ref, dst_scr, sub_scr):
        @pl.when(pl.program_id(1) == 0)
        def _():
            dst_scr[...] = jnp.zeros_like(dst_scr)
            dlb_ref[...] = jnp.zeros_like(dlb_ref)
            dgain_ref[...] = jnp.zeros_like(dgain_ref)

        q, f, v, g = q_ref[...], f_ref[...], i_ref[...], g_ref[...]
        lbv, gain_v = lb_ref[...], gain_ref[...]
        sq = _sigmoid(q)
        qs = q * sq
        sf = _sigmoid(f)
        fg = lbv + (1.0 - lbv) * sf
        kk = 1.0 - fg
        r16 = lax.broadcasted_iota(jnp.int32, (r, 128), 0) & (HGRN_SUB - 1)
        b = _prefix16(jnp.log(fg), r16)
        eb = jnp.exp(b)
        qh = qs * eb

        o, dy = o_ref[...], dy_ref[...]
        rn = lax.rsqrt(jnp.mean(o * o, axis=1, keepdims=True) + EPS)
        on = o * rn
        sg = _sigmoid(g)
        sil = g * sg
        dgain_ref[...] += _colsum(dy * on * sil)
        dg_ref[...] = (dy * on * gain_v * (sg * (1.0 + g * (1.0 - sg)))).astype(BF16)
        don = dy * gain_v * sil
        do = rn * (don - on * jnp.mean(don * on, axis=1, keepdims=True))

        st = st_ref[...]
        for i in range(nsub):
            sl = slice(HGRN_SUB * i, HGRN_SUB * (i + 1))
            sub_scr[i] = st
            bi = b[sl]
            bl = bi[HGRN_SUB - 1:HGRN_SUB]
            st = st * jnp.exp(bl) + _dotf(v[sl], kk[sl] * jnp.exp(bl - bi), _TN)

        rs = lax.broadcasted_iota(jnp.int32, (HGRN_SUB, 128), 0)
        dst = dst_scr[...]
        parts = [None] * nsub
        for i in reversed(range(nsub)):
            sl = slice(HGRN_SUB * i, HGRN_SUB * (i + 1))
            sp = sub_scr[i]
            qsi, kki, vi, bi, doi, qhi = qs[sl], kk[sl], v[sl], b[sl], do[sl], qh[sl]
            bl = bi[HGRN_SUB - 1:HGRN_SUB]
            ebl = jnp.exp(bl)
            dec = jnp.exp(bl - bi)
            khat = kki * dec
            dqh = _dotf(doi, sp)
            dkhat = _dotf(vi, dst)
            dv = _dotf(khat, dst, _NT)
            zrow = _colsum(sp * dst) * ebl
            dq_in = jnp.zeros((HGRN_SUB, 128), F32)
            dk_in = jnp.zeros((HGRN_SUB, 128), F32)
            dv_in = jnp.zeros((HGRN_SUB, 128), F32)
            for s in range(HGRN_SUB):
                e = jnp.exp(jnp.where(rs >= s, bi - bi[s:s + 1], NEG))
                dpc = jnp.sum(doi * vi[s:s + 1], axis=1, keepdims=True)
                w = qsi * e
                pc = jnp.sum(w * kki[s:s + 1], axis=1, keepdims=True)
                dq_in = dq_in + dpc * e * kki[s:s + 1]
                dk_in = jnp.where(rs == s, _colsum(dpc * w), dk_in)
                dv_in = jnp.where(rs == s, _colsum(pc * doi), dv_in)
            kd = khat * dkhat
            parts[i] = (qsi * dq_in - kki * dk_in + qhi * dqh, kd, jnp.broadcast_to(zrow, (HGRN_SUB, 128)),
                        dq_in + dqh * eb[sl], dk_in + dkhat * dec, dv + dv_in)
            dst = dst * ebl + _dotf(doi, qhi, _TN)
        dst_scr[...] = dst

        cat = lambda j: jnp.concatenate([p[j] for p in parts], axis=0)
        d_b, kd, zr, dqs, dkk, dvv = (cat(j) for j in range(6))
        dlf = _suffix16(d_b, r16) + _prefix16(kd, r16) - kd + zr
        dfg = dlf / fg - dkk
        df_ref[...] = (dfg * (1.0 - lbv) * sf * (1.0 - sf)).astype(BF16)
        dlb_ref[...] += _colsum(dfg * (1.0 - sf))
        dq_ref[...] = (dqs * (sq * (1.0 + q * (1.0 - sq)))).astype(BF16)
        di_ref[...] = dvv.astype(BF16)

    return pl.pallas_call(
        body, name=name,
        grid=(HGRN_HEADS, nb),
        in_specs=proj_specs + [lb_spec, gain_spec, row_spec, st_spec, dy_spec],
        out_specs=[row_spec] * 4 + [acc_spec, acc_spec],
        out_shape=[_sds((n_rows, HGRN_WIDTH), BF16)] * 4 + [_sds((HGRN_HEADS, 1, 128), F32)] * 2,
        scratch_shapes=[pltpu.VMEM((128, 128), F32), pltpu.VMEM((nsub, 128, 128), F32)],
        compiler_params=_params(("parallel", "arbitrary")),
    )(proj, proj, proj, proj, lb, gain, o_saved, states, dycat)


def _alibi_slopes():
    return jnp.exp2(-8.0 * jnp.arange(1, ATT_HEADS + 1, dtype=F32) / ATT_HEADS)


def _swa_specs(n_blocks):
    blk = ATT_BLOCK
    prev = lambda i: jnp.maximum(i - 1, 0)
    smem = pl.BlockSpec(memory_space=pltpu.SMEM)
    return [
        smem, smem,
        pl.BlockSpec((blk, ATT_HEADS * ATT_DIM), lambda i: (i, 0)),
        pl.BlockSpec((blk, 256), lambda i: (i, 8)),
        pl.BlockSpec((blk, 256), lambda i: (prev(i), 8)),
        pl.BlockSpec((blk, 256), lambda i: (i, 9)),
        pl.BlockSpec((blk, 256), lambda i: (prev(i), 9)),
        pl.BlockSpec((1, ATT_DIM), lambda i: (0, 0)),
        pl.BlockSpec((1, ATT_DIM), lambda i: (0, 0)),
    ]


_ATT_GROUP = ATT_HEADS // ATT_KV
_ATT_ROWS = _ATT_GROUP * ATT_BLOCK


def _swa_mask(i):
    t_i = lax.broadcasted_iota(jnp.int32, (_ATT_ROWS, 2 * ATT_BLOCK), 0) & (ATT_BLOCK - 1)
    s_i = lax.broadcasted_iota(jnp.int32, (_ATT_ROWS, 2 * ATT_BLOCK), 1)
    dist = t_i + ATT_BLOCK - s_i
    valid = (dist >= 0) & (dist < ATT_BLOCK) & ((s_i >= ATT_BLOCK) | (i > 0))
    return valid, dist.astype(F32)


def _stack_heads(x):
    return jnp.concatenate([x[:, ATT_DIM * h:ATT_DIM * (h + 1)] for h in range(_ATT_GROUP)], axis=0)


def _unstack_heads(x):
    return jnp.concatenate([x[ATT_BLOCK * h:ATT_BLOCK * (h + 1)] for h in range(_ATT_GROUP)], axis=1)


def _head_column(ref, g):
    return jnp.concatenate([jnp.full((ATT_BLOCK, 1), ref[_ATT_GROUP * g + h], F32) for h in range(_ATT_GROUP)], axis=0)


def _swa_probs(qn, kn, slope, sink, valid, distf):
    s = lax.dot_general(qn, kn, _NT, preferred_element_type=F32) * (ATT_DIM ** -0.5) - slope * distf
    s = jnp.where(valid, s, NEG)
    m = jnp.maximum(jnp.max(s, axis=1, keepdims=True), sink)
    p = jnp.exp(s - m)
    es = jnp.exp(sink - m)
    inv = 1.0 / (jnp.sum(p, axis=1, keepdims=True) + es)
    return p * inv, es * inv


def _swa_fwd(name, qkv, q_gain, k_gain, sinks, slopes):
    n_rows = qkv.shape[0]
    nb = n_rows // ATT_BLOCK

    def body(sink_ref, slope_ref, q_ref, kc_ref, kp_ref, vc_ref, vp_ref, qg_ref, kg_ref, o_ref):
        i = pl.program_id(0)
        kb = jnp.concatenate([kp_ref[...], kc_ref[...]], axis=0)
        vb = jnp.concatenate([vp_ref[...], vc_ref[...]], axis=0)
        valid, distf = _swa_mask(i)
        qgv, kgv = qg_ref[...], kg_ref[...]
        gw = _ATT_GROUP * ATT_DIM
        for g in range(ATT_KV):
            kg = kb[:, 64 * g:64 * (g + 1)]
            rk = lax.rsqrt(jnp.mean(kg * kg, axis=1, keepdims=True) + EPS)
            kn = (kg * rk * kgv).astype(BF16)
            vv = vb[:, 64 * g:64 * (g + 1)].astype(BF16)
            qs = _stack_heads(q_ref[:, gw * g:gw * (g + 1)])
            rq = lax.rsqrt(jnp.mean(qs * qs, axis=1, keepdims=True) + EPS)
            pn, _ = _swa_probs((qs * rq * qgv).astype(BF16), kn, _head_column(slope_ref, g),
                               _head_column(sink_ref, g), valid, distf)
            out = jnp.dot(pn.astype(BF16), vv, preferred_element_type=F32)
            o_ref[:, gw * g:gw * (g + 1)] = _unstack_heads(out).astype(BF16)

    return pl.pallas_call(
        body, name=name,
        grid=(nb,),
        in_specs=_swa_specs(nb),
        out_specs=pl.BlockSpec((ATT_BLOCK, ATT_HEADS * ATT_DIM), lambda i: (i, 0)),
        out_shape=_sds((n_rows, ATT_HEADS * ATT_DIM), BF16),
        compiler_params=_params(("parallel",)),
    )(sinks, slopes, qkv, qkv, qkv, qkv, qkv, q_gain.reshape(1, -1), k_gain.reshape(1, -1))


def _swa_bwd(name, qkv, q_gain, k_gain, sinks, slopes, d_out):
    n_rows = qkv.shape[0]
    nb = n_rows // ATT_BLOCK
    blk = ATT_BLOCK

    def body(sink_ref, slope_ref, q_ref, kc_ref, kp_ref, vc_ref, vp_ref, qg_ref, kg_ref, do_ref,
             dq_ref, dkc_ref, dkp_ref, dvc_ref, dvp_ref, dsink_ref, dqg_ref, dkg_ref):
        i = pl.program_id(0)

        @pl.when(i == 0)
        def _():
            dsink_ref[...] = jnp.zeros_like(dsink_ref)
            dqg_ref[...] = jnp.zeros_like(dqg_ref)
            dkg_ref[...] = jnp.zeros_like(dkg_ref)

        kb = jnp.concatenate([kp_ref[...], kc_ref[...]], axis=0)
        vb = jnp.concatenate([vp_ref[...], vc_ref[...]], axis=0)
        kgv, qgv = kg_ref[...], qg_ref[...]
        valid, distf = _swa_mask(i)
        scale = ATT_DIM ** -0.5
        gw = _ATT_GROUP * ATT_DIM
        dks, dvs = [], []
        dqg, dkg = jnp.zeros((1, ATT_DIM), F32), jnp.zeros((1, ATT_DIM), F32)
        for g in range(ATT_KV):
            kg = kb[:, 64 * g:64 * (g + 1)]
            rk = lax.rsqrt(jnp.mean(kg * kg, axis=1, keepdims=True) + EPS)
            khat = kg * rk
            kn = (khat * kgv).astype(BF16)
            vv = vb[:, 64 * g:64 * (g + 1)].astype(BF16)
            qs = _stack_heads(q_ref[:, gw * g:gw * (g + 1)])
            rq = lax.rsqrt(jnp.mean(qs * qs, axis=1, keepdims=True) + EPS)
            qhat = qs * rq
            qn = (qhat * qgv).astype(BF16)
            pn, ps = _swa_probs(qn, kn, _head_column(slope_ref, g), _head_column(sink_ref, g), valid, distf)
            dos = _stack_heads(do_ref[:, gw * g:gw * (g + 1)]).astype(BF16)
            dp = lax.dot_general(dos, vv, _NT, preferred_element_type=F32)
            delta = jnp.sum(pn * dp, axis=1, keepdims=True)
            ds = (pn * (dp - delta)).astype(BF16)
            sd = ps * delta
            for h in range(_ATT_GROUP):
                hs = _ATT_GROUP * g + h
                dsink_ref[hs:hs + 1, :] += jnp.zeros((1, 128), F32) - jnp.sum(sd[blk * h:blk * (h + 1)])
            dvs.append(lax.dot_general(pn.astype(BF16), dos, _TN, preferred_element_type=F32))
            dkn = lax.dot_general(ds, qn, _TN, preferred_element_type=F32) * scale
            dqn = jnp.dot(ds, kn, preferred_element_type=F32) * scale
            dqg = dqg + _colsum(dqn * qhat)
            dqhat = dqn * qgv
            dqs = rq * (dqhat - qhat * jnp.mean(dqhat * qhat, axis=1, keepdims=True))
            dq_ref[:, gw * g:gw * (g + 1)] = _unstack_heads(dqs).astype(BF16)
            dkg = dkg + _colsum(dkn * khat)
            dkhat = dkn * kgv
            dks.append(rk * (dkhat - khat * jnp.mean(dkhat * khat, axis=1, keepdims=True)))
        dqg_ref[...] += dqg
        dkg_ref[...] += dkg
        dk = jnp.concatenate(dks, axis=1).astype(BF16)
        dv = jnp.concatenate(dvs, axis=1).astype(BF16)
        dkp_ref[...] = dk[:blk]
        dkc_ref[...] = dk[blk:]
        dvp_ref[...] = dv[:blk]
        dvc_ref[...] = dv[blk:]

    kv_spec = pl.BlockSpec((blk, 256), lambda i: (i, 0))
    full = pl.BlockSpec((blk, ATT_HEADS * ATT_DIM), lambda i: (i, 0))
    acc64 = pl.BlockSpec((1, ATT_DIM), lambda i: (0, 0))
    return pl.pallas_call(
        body, name=name,
        grid=(nb,),
        in_specs=_swa_specs(nb) + [full],
        out_specs=[full, kv_spec, kv_spec, kv_spec, kv_spec,
                   pl.BlockSpec((ATT_HEADS, 128), lambda i: (0, 0)), acc64, acc64],
        out_shape=[_sds((n_rows, ATT_HEADS * ATT_DIM), BF16)] + [_sds((n_rows, 256), BF16)] * 4
        + [_sds((ATT_HEADS, 128), F32), _sds((1, ATT_DIM), F32), _sds((1, ATT_DIM), F32)],
        compiler_params=_params(("arbitrary",)),
    )(sinks, slopes, qkv, qkv, qkv, qkv, qkv, q_gain.reshape(1, -1), k_gain.reshape(1, -1), d_out)


def _mesh_pos():
    return lax.axis_index("x"), lax.axis_index("y"), lax.axis_index("c")


_ANY = pl.BlockSpec(memory_space=pl.ANY)


def _allgather(name, shards):
    n = len(shards)

    def body(*refs):
        x_refs, out_refs = refs[:n], refs[n:2 * n]
        send_sems, recv_sems, local_sems = refs[2 * n:]
        x, y, c = _mesh_pos()
        me, sibling = (x, y, c), (x, y, 1 - c)
        chips = [(1 - x, y), (x, 1 - y), (1 - x, 1 - y)]

        def slot(a, px, py, pc):
            return out_refs[a].at[4 * px + 2 * py + pc]

        def copy(a, k, block, to, src=None):
            return pltpu.make_async_remote_copy(
                src_ref=slot(a, *block) if src is None else src, dst_ref=slot(a, *block),
                send_sem=send_sems.at[7 * a + k], recv_sem=recv_sems.at[7 * a + k],
                device_id=to, device_id_type=MESH)

        started = []
        for a in range(n):
            mine = pltpu.make_async_copy(x_refs[a], slot(a, *me), local_sems.at[a])
            mine.start()
            first = [copy(a, 0, me, sibling, src=x_refs[a])]
            first += [copy(a, 1 + j, me, (*chip, c), src=x_refs[a]) for j, chip in enumerate(chips)]
            for cp in first:
                cp.start()
            started += first
        for a in range(n):
            for j, chip in enumerate(chips):
                copy(a, 1 + j, (*chip, c), me).wait_recv()
                fwd = copy(a, 4 + j, (*chip, c), sibling)
                fwd.start()
                started.append(fwd)
        for a in range(n):
            copy(a, 0, sibling, me).wait_recv()
            for j, chip in enumerate(chips):
                copy(a, 4 + j, (*chip, 1 - c), me).wait_recv()
        for cp in started:
            cp.wait_send()
        for a in range(n):
            pltpu.make_async_copy(x_refs[a], slot(a, *me), local_sems.at[a]).wait()

    return pl.pallas_call(
        body, name=name,
        out_shape=[_sds((N_DEV,) + s.shape, s.dtype) for s in shards],
        in_specs=[_ANY] * n,
        out_specs=[_ANY] * n,
        scratch_shapes=[pltpu.SemaphoreType.DMA((7 * n,)), pltpu.SemaphoreType.DMA((7 * n,)),
                        pltpu.SemaphoreType.DMA((n,))],
    )(*shards)


def _swap_with_sibling(name, arrs):
    n = len(arrs)

    def body(*refs):
        x_refs, got_refs, keep_refs = refs[:n], refs[n:2 * n], refs[2 * n:3 * n]
        send_sems, recv_sems, local_sems = refs[3 * n:]
        x, y, c = _mesh_pos()
        remote, local = [], []
        for a in range(n):
            for j in range(4):
                k = 4 * a + j
                cp = pltpu.make_async_remote_copy(
                    src_ref=x_refs[a].at[j, 1 - c], dst_ref=got_refs[a].at[j],
                    send_sem=send_sems.at[k], recv_sem=recv_sems.at[k],
                    device_id=(x, y, 1 - c), device_id_type=MESH)
                cp.start()
                remote.append(cp)
                lc = pltpu.make_async_copy(x_refs[a].at[j, c], keep_refs[a].at[j], local_sems.at[k])
                lc.start()
                local.append(lc)
        for cp in remote:
            cp.wait()
        for lc in local:
            lc.wait()

    outs = pl.pallas_call(
        body, name=name,
        out_shape=[_sds((4,) + t.shape[2:], t.dtype) for t in arrs] * 2,
        in_specs=[_ANY] * n,
        out_specs=[_ANY] * (2 * n),
        scratch_shapes=[pltpu.SemaphoreType.DMA((4 * n,))] * 3,
    )(*arrs)
    return outs[:n], outs[n:]


def _exchange_chips(name, arrs):
    n = len(arrs)

    def body(*refs):
        x_refs, out_refs = refs[:n], refs[n:2 * n]
        send_sems, recv_sems, local_sems = refs[2 * n:]
        x, y, c = _mesh_pos()
        me = 2 * x + y
        copies, local = [], []
        for a in range(n):
            mine = pltpu.make_async_copy(x_refs[a].at[me], out_refs[a].at[me], local_sems.at[a])
            mine.start()
            local.append(mine)
            for k in range(1, 4):
                px, py = x ^ (k >> 1), y ^ (k & 1)
                peer = 2 * px + py
                sem = 3 * a + k - 1
                send = pltpu.make_async_remote_copy(
                    src_ref=x_refs[a].at[peer], dst_ref=out_refs[a].at[me],
                    send_sem=send_sems.at[sem], recv_sem=recv_sems.at[sem],
                    device_id=(px, py, c), device_id_type=MESH)
                recv = pltpu.make_async_remote_copy(
                    src_ref=x_refs[a].at[peer], dst_ref=out_refs[a].at[peer],
                    send_sem=send_sems.at[sem], recv_sem=recv_sems.at[sem],
                    device_id=(px, py, c), device_id_type=MESH)
                send.start()
                copies.append((send, recv))
        for send, recv in copies:
            recv.wait_recv()
        for send, recv in copies:
            send.wait_send()
        for mine in local:
            mine.wait()

    return pl.pallas_call(
        body, name=name,
        out_shape=[_sds(t.shape, t.dtype) for t in arrs],
        in_specs=[_ANY] * n,
        out_specs=[_ANY] * n,
        scratch_shapes=[pltpu.SemaphoreType.DMA((3 * n,)), pltpu.SemaphoreType.DMA((3 * n,)),
                        pltpu.SemaphoreType.DMA((n,))],
    )(*arrs)


def _sum_blocks(name, blocks, out_dtype=F32, tr=512):
    n, n_rows, n_cols = blocks.shape
    tr = min(tr, n_rows)
    assert n_rows % tr == 0

    def body(x_ref, o_ref):
        acc = x_ref[0].astype(F32)
        for s in range(1, n):
            acc = acc + x_ref[s].astype(F32)
        o_ref[...] = acc.astype(o_ref.dtype)

    return pl.pallas_call(
        body, name=name,
        grid=(n_rows // tr,),
        in_specs=[pl.BlockSpec((n, tr, n_cols), lambda i: (0, i, 0))],
        out_specs=pl.BlockSpec((tr, n_cols), lambda i: (i, 0)),
        out_shape=_sds((n_rows, n_cols), out_dtype),
        compiler_params=_params(("parallel",)),
    )(blocks)


def _add_pair(name, a, b):
    n, n_rows, n_cols = a.shape
    tr = 512
    assert n_rows % tr == 0

    def body(a_ref, b_ref, o_ref):
        o_ref[...] = (a_ref[...].astype(F32) + b_ref[...].astype(F32)).astype(BF16)

    spec = pl.BlockSpec((None, tr, n_cols), lambda j, i: (j, i, 0))
    return pl.pallas_call(
        body, name=name,
        grid=(n, n_rows // tr),
        in_specs=[spec, spec],
        out_specs=spec,
        out_shape=_sds(a.shape, BF16),
        compiler_params=_params(("parallel", "parallel")),
    )(a, b)


_BIG = (
    ("even_w_in", 2, 2, D_MODEL, 5120),
    ("s5_w_glu", 1, 2, S5_WIDTH, S5_WIDTH),
    ("even_w_out", 1, 2, D_MODEL, D_MODEL),
    ("odd_w_qkv", 2, 2, D_MODEL, QKV_WIDTH),
    ("odd_w_out", 1, 2, D_MODEL, D_MODEL),
)
_PACK_COLS = 1024
_FF_SHARD = D_FF // N_DEV


def _big_rows(nl, rows, cols):
    return nl * rows * cols // N_DEV // _PACK_COLS


def _unpack_gathered(gathered):
    out, off = {}, 0
    for name, axis, nl, rows, cols in _BIG:
        n = _big_rows(nl, rows, cols)
        part = gathered[:, off:off + n]
        off += n
        if axis == 2:
            w = part.reshape(N_DEV, nl, rows, cols // N_DEV).transpose(1, 2, 0, 3)
        else:
            w = part.reshape(N_DEV, nl, rows // N_DEV, cols).transpose(1, 0, 2, 3)
        out[name] = w.reshape(nl, rows, cols)
    return out


def _pack_by_owner(full):
    parts = []
    for name, axis, nl, rows, cols in _BIG:
        g = full[name].astype(BF16)
        if axis == 2:
            g = g.reshape(nl, rows, N_DEV, cols // N_DEV).transpose(2, 0, 1, 3)
        else:
            g = g.reshape(nl, N_DEV, rows // N_DEV, cols).transpose(1, 0, 2, 3)
        parts.append(g.reshape(N_DEV, -1, _PACK_COLS))
    return jnp.concatenate(parts, axis=1)


def _pack_small(arrs, row_mult=512):
    parts = []
    for a in arrs:
        f = a.astype(F32).reshape(-1)
        parts.append(jnp.pad(f, (0, (-f.shape[0]) % 128)))
    f = jnp.concatenate(parts)
    f = jnp.pad(f, (0, (-f.shape[0]) % (128 * row_mult)))
    return f.reshape(-1, 128)


def _unpack_small(flat, shapes):
    f = flat.reshape(-1)
    out, off = [], 0
    for s in shapes:
        n = math.prod(s)
        out.append(f[off:off + n].reshape(s))
        off += n + (-n) % 128
    return out


_WEIGHTS = ("even_norm", "even_w_in", "s5_lambda_re", "s5_lambda_im", "s5_log_dt", "s5_b_re", "s5_b_im",
            "s5_c_re", "s5_c_im", "s5_d", "s5_w_glu", "s5_b_glu", "hgrn_lower_bound", "hgrn_o_norm",
            "even_w_out", "odd_norm", "odd_w_qkv", "q_norm", "k_norm", "att_sinks", "odd_w_out",
            "mlp_norm", "mlp_w_up", "mlp_w_down")
_BIG_NAMES = tuple(b[0] for b in _BIG)
_MLP_NAMES = ("mlp_w_up", "mlp_w_down")
_SMALL_NAMES = tuple(n for n in _WEIGHTS if n not in _BIG_NAMES + _MLP_NAMES)


def _add_res(acc, res):
    return (acc + res,)


def _mlp_fwd(h, gain, w_up, w_down):
    n_rows, fs = h.shape[0], _FF_SHARD
    xn, rstd = _rms_fwd("rms_fwd", h, gain)
    up, act = _mm("mm_up", xn, w_up, "nn", out_dtypes=(F32, BF16), mkn=(n_rows, D_MODEL, D_FF), tn=fs,
                  b_block=pl.BlockSpec((None, D_MODEL, fs), lambda i, j, kk: (j, kk, 0)),
                  epi=lambda acc: (acc, jnp.square(jnp.maximum(acc, 0.0))))
    out = _mm("mm_down", act, w_down, "nn", mkn=(n_rows, D_FF, D_MODEL), tk=fs,
              b_block=pl.BlockSpec((None, fs, 1024), lambda i, j, kk: (kk, 0, j)), epi=_add_res, extras=(h,))
    return out, (h, gain, xn, rstd, up, act, w_up, w_down)


def _mlp_bwd(cache, dh, dhb):
    h, gain, xn, rstd, up, act, w_up, w_down = cache
    n_rows, fs = h.shape[0], _FF_SHARD
    dup = _mm("mm_dact", dhb, w_down, "nt", out_dtypes=(BF16,), mkn=(n_rows, D_MODEL, D_FF), tn=fs,
              b_block=pl.BlockSpec((None, fs, D_MODEL), lambda i, j, kk: (j, 0, kk)),
              epi=lambda acc, u: (acc * (2.0 * jnp.maximum(u, 0.0)),), extras=(up,))
    dw_down = _mm("mm_dw_down", act, dhb, "tn", out_dtypes=(BF16,),
                  o_block=(pl.BlockSpec((None, 512, 1024), lambda i, j, kk: (i // 2, i % 2, j)),
                           (N_DEV, fs, D_MODEL)))
    dxn = _mm("mm_dxn_up", dup, w_up, "nt", mkn=(n_rows, D_FF, D_MODEL), tk=fs,
              b_block=pl.BlockSpec((None, 1024, fs), lambda i, j, kk: (kk, j, 0)))
    dw_up = _mm("mm_dw_up", xn, dup, "tn", out_dtypes=(BF16,),
                o_block=(pl.BlockSpec((None, 512, fs), lambda i, j, kk: (j, i, 0)), (N_DEV, D_MODEL, fs)))
    dh_in, dhb_in, dgain = _rms_bwd("rms_bwd", h, rstd, gain, dxn, dh)
    return dh_in, dhb_in, dgain, dw_up, dw_down


def _even_fwd(h, p):
    xn, rstd = _rms_fwd("rms_fwd", h, p["norm"])
    proj = _mm("mm_w_in", xn, p["w_in"], "nn")
    y_pre, z, s5_states = _s5_fwd("s5_fwd", proj, p["mats"])
    gate = _mm("mm_glu", z, p["w_glu"], "nn")
    (ya,) = _rowwise("glu_fwd", lambda y, gt, b: ((_gelu(y) * _sigmoid(gt + b),), ()),
                     [y_pre, gate], [p["b_glu"].reshape(1, -1)], [(S5_WIDTH, BF16)])
    o, yb, h_states = _hgrn_fwd("hgrn_fwd", proj, p["lb"].reshape(8, 1, 128), p["o_gain"].reshape(1, 128))
    ycat = jnp.concatenate([ya, yb], axis=1)
    out = _mm("mm_w_out", ycat, p["w_out"], "nn", epi=_add_res, extras=(h,))
    return out, (h, xn, rstd, proj, y_pre, z, s5_states, gate, o, h_states, ycat)


def _even_bwd(cache, p, dh, dhb):
    h, xn, rstd, proj, y_pre, z, s5_states, gate, o, h_states, ycat = cache
    g = {}
    dycat = _mm("mm_dy_out", dhb, p["w_out"], "nt")
    g["w_out"] = _mm("mm_dw_out", ycat, dhb, "tn", out_dtypes=(BF16,))
    dq, df, di, dg, dlb, dgain = _hgrn_bwd("hgrn_bwd", proj, p["lb"].reshape(8, 1, 128),
                                           p["o_gain"].reshape(1, 128), o, h_states, dycat)
    g["lb"] = dlb.reshape(-1)
    g["o_gain"] = jnp.sum(dgain, axis=0).reshape(-1)

    def glu_bwd1(dyc, y, gt, b):
        zf = _gelu(y)
        s = _sigmoid(gt + b)
        dya = dyc[:, :S5_WIDTH]
        d_gate = dya * zf * s * (1.0 - s)
        return (d_gate, dya * s), (_colsum(d_gate),)

    d_gate, dz_direct, db_glu = _rowwise("glu_bwd_gate", glu_bwd1, [dycat, y_pre, gate], [p["b_glu"].reshape(1, -1)],
                                         [(S5_WIDTH, BF16), (S5_WIDTH, F32)], [S5_WIDTH])
    g["b_glu"] = db_glu.reshape(-1)
    dz_gate = _mm("mm_dz_glu", d_gate, p["w_glu"], "nt")
    g["w_glu"] = _mm("mm_dw_glu", z, d_gate, "tn", out_dtypes=(BF16,))
    (dy_pre,) = _rowwise("glu_bwd_gelu", lambda a, b, y: (((a + b) * _gelu_grad(y),), ()),
                         [dz_direct, dz_gate, y_pre], [], [(S5_WIDTH, F32)])
    du, dbm, dcm, da, dd = _s5_bwd("s5_bwd", proj, dy_pre, s5_states, p["mats"])
    g["s5"] = (dbm, dcm, da, dd)
    dproj = jnp.concatenate([du, dq, df, di, dg], axis=1)
    dxn = _mm("mm_dxn_in", dproj, p["w_in"], "nt", tk=2560)
    g["w_in"] = _mm("mm_dw_in", xn, dproj, "tn", out_dtypes=(BF16,))
    dh_in, dhb_in, dnorm = _rms_bwd("rms_bwd", h, rstd, p["norm"], dxn, dh)
    g["norm"] = dnorm.reshape(-1)
    return dh_in, dhb_in, g


def _odd_fwd(h, p):
    xn, rstd = _rms_fwd("rms_fwd", h, p["norm"])
    qkv = _mm("mm_w_qkv", xn, p["w_qkv"], "nn", tn=1280)
    o = _swa_fwd("swa_fwd", qkv, p["q_gain"], p["k_gain"], p["sinks"], p["slopes"])
    out = _mm("mm_w_out", o, p["w_out"], "nn", epi=_add_res, extras=(h,))
    return out, (h, xn, rstd, qkv, o)


def _shift_up_block(x):
    return jnp.concatenate([x[ATT_BLOCK:], jnp.zeros((ATT_BLOCK, x.shape[1]), x.dtype)], axis=0)


def _odd_bwd(cache, p, dh, dhb):
    h, xn, rstd, qkv, o = cache
    g = {}
    d_o = _mm("mm_dy_out", dhb, p["w_out"], "nt")
    g["w_out"] = _mm("mm_dw_out", o, dhb, "tn", out_dtypes=(BF16,))
    dq, dkc, dkp, dvc, dvp, dsink, dqg, dkg = _swa_bwd("swa_bwd", qkv, p["q_gain"], p["k_gain"], p["sinks"],
                                                       p["slopes"], d_o)
    dk = (dkc.astype(F32) + _shift_up_block(dkp).astype(F32)).astype(BF16)
    dv = (dvc.astype(F32) + _shift_up_block(dvp).astype(F32)).astype(BF16)
    g["sinks"], g["q_gain"], g["k_gain"] = dsink[:, 0], dqg.reshape(-1), dkg.reshape(-1)
    dqkv = jnp.concatenate([dq, dk, dv], axis=1)
    dxn = _mm("mm_dxn_qkv", dqkv, p["w_qkv"], "nt", tk=1280)
    g["w_qkv"] = _mm("mm_dw_qkv", xn, dqkv, "tn", out_dtypes=(BF16,), tn=1280)
    dh_in, dhb_in, dnorm = _rms_bwd("rms_bwd", h, rstd, p["norm"], dxn, dh)
    g["norm"] = dnorm.reshape(-1)
    return dh_in, dhb_in, g


def kernel(x, even_norm, even_w_in, s5_lambda_re, s5_lambda_im, s5_log_dt, s5_b_re, s5_b_im, s5_c_re, s5_c_im, s5_d, s5_w_glu, s5_b_glu, hgrn_lower_bound, hgrn_o_norm, even_w_out, odd_norm, odd_w_qkv, q_norm, k_norm, att_sinks, odd_w_out, mlp_norm, mlp_w_up, mlp_w_down, loss_target, m_even_norm, m_even_w_in, m_s5_lambda_re, m_s5_lambda_im, m_s5_log_dt, m_s5_b_re, m_s5_b_im, m_s5_c_re, m_s5_c_im, m_s5_d, m_s5_w_glu, m_s5_b_glu, m_hgrn_lower_bound, m_hgrn_o_norm, m_even_w_out, m_odd_norm, m_odd_w_qkv, m_q_norm, m_k_norm, m_att_sinks, m_odd_w_out, m_mlp_norm, m_mlp_w_up, m_mlp_w_down, v_even_norm, v_even_w_in, v_s5_lambda_re, v_s5_lambda_im, v_s5_log_dt, v_s5_b_re, v_s5_b_im, v_s5_c_re, v_s5_c_im, v_s5_d, v_s5_w_glu, v_s5_b_glu, v_hgrn_lower_bound, v_hgrn_o_norm, v_even_w_out, v_odd_norm, v_odd_w_qkv, v_q_norm, v_k_norm, v_att_sinks, v_odd_w_out, v_mlp_norm, v_mlp_w_up, v_mlp_w_down):
    a = dict(locals())
    n_rows = x.shape[1]
    xi, yi, ci = _mesh_pos()
    me = 4 * xi + 2 * yi + ci

    shard = jnp.concatenate([a[n].astype(BF16).reshape(-1, _PACK_COLS) for n in _BIG_NAMES], axis=0)
    gathered = _allgather("gather_weights", [shard] + [mlp_w_up[l].astype(BF16) for l in range(DEPTH)]
                          + [mlp_w_down[l].astype(BF16) for l in range(DEPTH)])
    w = _unpack_gathered(gathered[0])
    w_up_g, w_down_g = gathered[1:1 + DEPTH], gathered[1 + DEPTH:]
    (odd_gathered,) = _allgather("gather_odd_norm", [jnp.pad(odd_norm, ((0, 6), (0, 0)))])
    odd_norm_full = odd_gathered[:, :2].transpose(1, 0, 2).reshape(2, D_MODEL)

    lower_bounds, lb_vjp = jax.vjp(_hgrn_lower_bounds, hgrn_lower_bound)
    slopes = _alibi_slopes()
    even_p, odd_p, s5_vjps = [], [], []
    for j in range(2):
        disc, vjp = jax.vjp(_s5_discretize, s5_lambda_re[j], s5_lambda_im[j], s5_log_dt[j], s5_b_re[j], s5_b_im[j])
        s5_vjps.append(vjp)
        even_p.append(dict(norm=even_norm[j], w_in=w["even_w_in"][j], w_glu=w["s5_w_glu"][j], b_glu=s5_b_glu[j],
                           mats=_s5_matrices(*disc, s5_c_re[j], s5_c_im[j], s5_d[j]),
                           lb=lower_bounds[j], o_gain=hgrn_o_norm[j], w_out=w["even_w_out"][j]))
        odd_p.append(dict(norm=odd_norm_full[j], w_qkv=w["odd_w_qkv"][j], q_gain=q_norm[j], k_gain=k_norm[j],
                          sinks=att_sinks[j], slopes=slopes, w_out=w["odd_w_out"][j]))

    h = x.reshape(n_rows, D_MODEL)
    caches = []
    for layer in range(DEPTH):
        j = layer // 2
        h, c_mix = (_even_fwd(h, even_p[j]) if layer % 2 == 0 else _odd_fwd(h, odd_p[j]))
        h, c_mlp = _mlp_fwd(h, mlp_norm[layer], w_up_g[layer], w_down_g[layer])
        caches.append((c_mix, c_mlp))
    dh, dhb, sq = _loss_head(h, loss_target.reshape(n_rows, D_MODEL))
    loss = lax.psum(0.5 * sq[0, 0] / D_MODEL, ("x", "y", "c"))

    mix_g, mlp_g = [None] * DEPTH, [None] * DEPTH
    for layer in reversed(range(DEPTH)):
        j = layer // 2
        c_mix, c_mlp = caches[layer]
        dh, dhb, d_mlp_norm, dw_up, dw_down = _mlp_bwd(c_mlp, dh, dhb)
        mlp_g[layer] = (d_mlp_norm.reshape(-1), dw_up, dw_down)
        if layer % 2 == 0:
            dh, dhb, mix_g[layer] = _even_bwd(c_mix, even_p[j], dh, dhb)
        else:
            dh, dhb, mix_g[layer] = _odd_bwd(c_mix, odd_p[j], dh, dhb)
    grad_x = dh.reshape(x.shape)

    ev, od = [mix_g[0], mix_g[2]], [mix_g[1], mix_g[3]]
    full = {
        "even_w_in": jnp.stack([g["w_in"] for g in ev]), "s5_w_glu": jnp.stack([g["w_glu"] for g in ev]),
        "even_w_out": jnp.stack([g["w_out"] for g in ev]), "odd_w_qkv": jnp.stack([g["w_qkv"] for g in od]),
        "odd_w_out": jnp.stack([g["w_out"] for g in od]),
    }
    by_owner = [_pack_by_owner(full)] + [g[1] for g in mlp_g] + [g[2] for g in mlp_g]
    got, keep = _swap_with_sibling("swap_big_grads", [t.reshape((4, 2) + t.shape[1:]) for t in by_owner])
    chip_sums = [_add_pair("add_sibling_grads", k, g) for k, g in zip(keep, got)]
    sums = [_sum_blocks("sum_big_grads", r) for r in _exchange_chips("exchange_big_grads", chip_sums)]
    big_flat = sums[0]
    grads, off = {"mlp_w_up": jnp.stack(sums[1:1 + DEPTH]), "mlp_w_down": jnp.stack(sums[1 + DEPTH:])}, 0
    for name, axis, nl, rows, cols in _BIG:
        n = _big_rows(nl, rows, cols)
        grads[name] = big_flat[off:off + n].reshape(a[name].shape)
        off += n

    s5_g = []
    for j in range(2):
        dar, dai, dbbr, dbbi, dcr, dci, dd = _s5_unpack_grads(*ev[j]["s5"])
        s5_g.append(tuple(s5_vjps[j]((dar, dai, dbbr, dbbi))) + (dcr, dci, dd))
    (d_lb_param,) = lb_vjp(jnp.stack([g["lb"] for g in ev]))
    small = {
        "even_norm": jnp.stack([g["norm"] for g in ev]),
        "s5_lambda_re": jnp.stack([g[0] for g in s5_g]), "s5_lambda_im": jnp.stack([g[1] for g in s5_g]),
        "s5_log_dt": jnp.stack([g[2] for g in s5_g]), "s5_b_re": jnp.stack([g[3] for g in s5_g]),
        "s5_b_im": jnp.stack([g[4] for g in s5_g]), "s5_c_re": jnp.stack([g[5] for g in s5_g]),
        "s5_c_im": jnp.stack([g[6] for g in s5_g]), "s5_d": jnp.stack([g[7] for g in s5_g]),
        "s5_b_glu": jnp.stack([g["b_glu"] for g in ev]), "hgrn_lower_bound": d_lb_param,
        "hgrn_o_norm": jnp.stack([g["o_gain"] for g in ev]), "odd_norm": jnp.stack([g["norm"] for g in od]),
        "q_norm": jnp.stack([g["q_gain"] for g in od]), "k_norm": jnp.stack([g["k_gain"] for g in od]),
        "att_sinks": jnp.stack([g["sinks"] for g in od]), "mlp_norm": jnp.stack([g[0] for g in mlp_g]),
    }
    small_shapes = [small[n].shape for n in _SMALL_NAMES]
    (small_all,) = _allgather("gather_small_grads", [_pack_small([small[n] for n in _SMALL_NAMES])])
    small_sum = _sum_blocks("sum_small_grads", small_all)
    for n, g in zip(_SMALL_NAMES, _unpack_small(small_sum, small_shapes)):
        grads[n] = g
    grads["odd_norm"] = lax.dynamic_slice_in_dim(grads["odd_norm"], me * (D_MODEL // N_DEV), D_MODEL // N_DEV, axis=1)

    delta, new_m, new_v = {}, {}, {}
    for name in _BIG_NAMES + _MLP_NAMES:
        to2d = lambda t, c=a[name].shape[-1]: t.reshape(-1, c)
        d_, m_, v_ = _adamw("adamw_" + name, to2d(a[name]), to2d(grads[name]), to2d(a["m_" + name]), to2d(a["v_" + name]))
        delta[name], new_m[name], new_v[name] = (t.reshape(a[name].shape) for t in (d_, m_, v_))
    packed = [_pack_small([src[n] for n in _SMALL_NAMES])
              for src in (a, grads, {n: a["m_" + n] for n in _SMALL_NAMES}, {n: a["v_" + n] for n in _SMALL_NAMES})]
    shapes = [a[n].shape for n in _SMALL_NAMES]
    for dst, flat in zip((delta, new_m, new_v), _adamw("adamw_small", *packed)):
        for n, t in zip(_SMALL_NAMES, _unpack_small(flat, shapes)):
            dst[n] = t

    return (loss, grad_x, *[grads[n] for n in _WEIGHTS], *[delta[n] for n in _WEIGHTS],
            *[new_m[n] for n in _WEIGHTS], *[new_v[n] for n in _WEIGHTS])
```

```python
import math

import jax
import jax.numpy as jnp
from jax import lax
from jax.experimental import pallas as pl
from jax.experimental.pallas import tpu as pltpu

F32 = jnp.float32
BF16 = jnp.bfloat16
MESH = pl.DeviceIdType.MESH

D_MODEL = 2048
DEPTH = 4
EPS = 1e-6
S5_WIDTH = 1024
S5_GROUPS = 64
S5_STATE = 64
S5_GROUP_SIZE = 16
S5_MIN_DECAY = 1e-4
S5_CHUNK = 128
S5_LEVELS = 7
HGRN_WIDTH = 1024
HGRN_HEADS = 8
HGRN_DIM = 128
HGRN_SUB = 16
HGRN_BLOCK = 128
ATT_HEADS = 32
ATT_KV = 4
ATT_DIM = 64
ATT_BLOCK = 128
QKV_WIDTH = (ATT_HEADS + 2 * ATT_KV) * ATT_DIM
D_FF = 4 * D_MODEL
N_DEV = 8
NEG = -1e30
VMEM_LIMIT = 56 * 1024 * 1024

ADAM_LR, ADAM_B1, ADAM_B2, ADAM_EPS, ADAM_WD, ADAM_STEP = 0.001, 0.9, 0.999, 1e-08, 0.01, 10


def _params(sem=None):
    return pltpu.CompilerParams(dimension_semantics=sem, vmem_limit_bytes=VMEM_LIMIT)


def _sds(shape, dtype):
    return jax.ShapeDtypeStruct(shape, dtype)


def _mm(name, a, b, mode, out_dtypes=(F32,), epi=None, extras=(), tm=512, tn=1024, tk=2048,
        mkn=None, b_block=None, o_block=None):
    if mkn is not None:
        m, k, n = mkn
    elif mode == "nn":
        (m, k), n = a.shape, b.shape[1]
    elif mode == "nt":
        (m, k), n = a.shape, b.shape[0]
    else:
        (k, m), n = a.shape, b.shape[1]
    tm, tn, tk = min(tm, m), min(tn, n), min(tk, k)
    assert m % tm == 0 and n % tn == 0 and k % tk == 0, (name, m, n, k)
    nk = k // tk
    if mode == "nn":
        a_spec = pl.BlockSpec((tm, tk), lambda i, j, kk: (i, kk))
        b_spec = pl.BlockSpec((tk, tn), lambda i, j, kk: (kk, j))
        dims = (((1,), (0,)), ((), ()))
    elif mode == "nt":
        a_spec = pl.BlockSpec((tm, tk), lambda i, j, kk: (i, kk))
        b_spec = pl.BlockSpec((tn, tk), lambda i, j, kk: (j, kk))
        dims = (((1,), (1,)), ((), ()))
    else:
        a_spec = pl.BlockSpec((tk, tm), lambda i, j, kk: (kk, i))
        b_spec = pl.BlockSpec((tk, tn), lambda i, j, kk: (kk, j))
        dims = (((0,), (0,)), ((), ()))
    o_spec = pl.BlockSpec((tm, tn), lambda i, j, kk: (i, j))
    if b_block is not None:
        b_spec = b_block
    out_specs = [o_spec] * len(out_dtypes)
    out_shape = [_sds((m, n), dt) for dt in out_dtypes]
    if o_block is not None:
        assert len(out_dtypes) == 1 and not extras
        out_specs, out_shape = [o_block[0]], [_sds(o_block[1], out_dtypes[0])]
    n_ex, n_out = len(extras), len(out_dtypes)

    def body(*refs):
        a_ref, b_ref = refs[0], refs[1]
        ex_refs = refs[2:2 + n_ex]
        out_refs = refs[2 + n_ex:2 + n_ex + n_out]
        acc_ref = refs[2 + n_ex + n_out] if nk > 1 else None
        av, bv = a_ref[...], b_ref[...]
        if av.dtype != BF16:
            av = av.astype(BF16)
        if bv.dtype != BF16:
            bv = bv.astype(BF16)
        part = lax.dot_general(av, bv, dims, preferred_element_type=F32)

        def finish(acc):
            outs = epi(acc, *[r[...] for r in ex_refs]) if epi is not None else (acc,)
            for r, o in zip(out_refs, outs):
                r[...] = o.astype(r.dtype)

        if nk == 1:
            finish(part)
        else:
            kk = pl.program_id(2)

            @pl.when(kk == 0)
            def _():
                acc_ref[...] = part

            @pl.when(kk > 0)
            def _():
                acc_ref[...] += part

            @pl.when(kk == nk - 1)
            def _():
                finish(acc_ref[...])

    outs = pl.pallas_call(
        body, name=name,
        grid=(m // tm, n // tn, nk),
        in_specs=[a_spec, b_spec] + [o_spec] * n_ex,
        out_specs=out_specs,
        out_shape=out_shape,
        scratch_shapes=[pltpu.VMEM((tm, tn), F32)] if nk > 1 else [],
        compiler_params=_params(("parallel", "parallel", "arbitrary")),
    )(a, b, *extras)
    return outs[0] if n_out == 1 else outs


def _rowwise(name, fn, rows, vecs, outs, accs=(), tr=256):
    n_rows = rows[0].shape[0]
    tr = min(tr, n_rows)
    assert n_rows % tr == 0
    n_r, n_v, n_o, n_a = len(rows), len(vecs), len(outs), len(accs)

    def body(*refs):
        ins = [r[...] for r in refs[:n_r + n_v]]
        o_refs = refs[n_r + n_v:n_r + n_v + n_o]
        a_refs = refs[n_r + n_v + n_o:]
        ro, ao = fn(*ins)
        for r, o in zip(o_refs, ro):
            r[...] = o.astype(r.dtype)
        if n_a:
            step = pl.program_id(0)

            @pl.when(step == 0)
            def _():
                for r, o in zip(a_refs, ao):
                    r[...] = o

            @pl.when(step > 0)
            def _():
                for r, o in zip(a_refs, ao):
                    r[...] += o

    res = pl.pallas_call(
        body, name=name,
        grid=(n_rows // tr,),
        in_specs=[pl.BlockSpec((tr, r.shape[1]), lambda i: (i, 0)) for r in rows]
        + [pl.BlockSpec(v.shape, lambda i: (0, 0)) for v in vecs],
        out_specs=[pl.BlockSpec((tr, w), lambda i: (i, 0)) for w, _ in outs]
        + [pl.BlockSpec((1, w), lambda i: (0, 0)) for w in accs],
        out_shape=[_sds((n_rows, w), dt) for w, dt in outs] + [_sds((1, w), F32) for w in accs],
        compiler_params=_params(("arbitrary",)),
    )(*rows, *vecs)
    return res


def _colsum(x):
    return jnp.sum(x, axis=0, keepdims=True)


def _sigmoid(x):
    return 1.0 / (1.0 + jnp.exp(-x))


_GELU_C = math.sqrt(2.0 / math.pi)


def _gelu(y):
    return 0.5 * y * (1.0 + jnp.tanh(_GELU_C * (y + 0.044715 * y * y * y)))


def _gelu_grad(y):
    t = jnp.tanh(_GELU_C * (y + 0.044715 * y * y * y))
    return 0.5 * (1.0 + t) + 0.5 * y * (1.0 - t * t) * _GELU_C * (1.0 + 3.0 * 0.044715 * y * y)


def _rms_fwd(name, h, gain):
    def fn(x, g):
        r = lax.rsqrt(jnp.mean(x * x, axis=1, keepdims=True) + EPS)
        return (x * r * g, r), ()
    return _rowwise(name, fn, [h], [gain.reshape(1, -1)], [(h.shape[1], BF16), (1, F32)])


def _rms_bwd(name, h, rstd, gain, dxn, dres):
    def fn(x, r, dy, dr, g):
        xh = x * r
        gdy = dy * g
        dx = r * (gdy - xh * jnp.mean(gdy * xh, axis=1, keepdims=True)) + dr
        return (dx, dx), (_colsum(dy * xh),)
    w = h.shape[1]
    return _rowwise(name, fn, [h, rstd, dxn, dres], [gain.reshape(1, -1)], [(w, F32), (w, BF16)], [w])


def _loss_head(h, target):
    w = h.shape[1]

    def fn(x, t):
        e = x - t
        return (e * (1.0 / w), e * (1.0 / w)), (jnp.zeros((1, 128), F32) + jnp.sum(e * e),)
    return _rowwise("loss_head", fn, [h, target], [], [(w, F32), (w, BF16)], [128])


def _adamw(name, w, g, m, v):
    c1 = 1.0 - ADAM_B1 ** ADAM_STEP
    c2 = 1.0 - ADAM_B2 ** ADAM_STEP

    def fn(w_, g_, m_, v_):
        mn = ADAM_B1 * m_ + (1.0 - ADAM_B1) * g_
        vn = ADAM_B2 * v_ + (1.0 - ADAM_B2) * (g_ * g_)
        delta = -ADAM_LR * ((mn / c1) / (jnp.sqrt(vn / c2) + ADAM_EPS) + ADAM_WD * w_)
        return (delta, mn, vn), ()
    c = w.shape[1]
    return _rowwise(name, fn, [w, g, m, v], [], [(c, F32)] * 3)


def _s5_discretize(lam_re, lam_im, log_dt, b_re, b_im):
    lr = jnp.minimum(lam_re, -S5_MIN_DECAY)
    li = lam_im
    dt = jnp.exp(log_dt)[:, None]
    mag = jnp.exp(lr * dt)
    ar = mag * jnp.cos(li * dt)
    ai = mag * jnp.sin(li * dt)
    den = lr * lr + li * li
    zr = ((ar - 1.0) * lr + ai * li) / den
    zi = (ai * lr - (ar - 1.0) * li) / den
    bbr = zr[..., None] * b_re - zi[..., None] * b_im
    bbi = zr[..., None] * b_im + zi[..., None] * b_re
    return ar, ai, bbr, bbi


def _s5_matrices(ar, ai, bbr, bbi, c_re, c_im, d_skip):
    eye = jnp.eye(8, dtype=F32)
    bt = jnp.stack([bbr, bbi], axis=1).transpose(0, 3, 1, 2)
    bt = bt.reshape(8, 8, 16, 1, 2, 64) * eye[None, :, None, :, None, None]
    bm8 = bt.reshape(8, 8, 16, 4, 2, 2, 64).transpose(0, 1, 2, 3, 5, 4, 6).reshape(8, 128, 1024)
    ct = jnp.stack([c_re, -c_im], axis=1).transpose(0, 1, 3, 2)
    ct = ct.reshape(8, 8, 2, 64, 1, 16) * eye[None, :, None, None, :, None]
    cm8 = ct.reshape(8, 4, 2, 2, 64, 8, 16).transpose(0, 1, 3, 2, 4, 5, 6).reshape(8, 1024, 128)
    prs, pis = [], []
    pr, pi = ar, ai
    for _ in range(S5_LEVELS):
        prs.append(pr.reshape(8, 512))
        pis.append(pi.reshape(8, 512))
        pr, pi = pr * pr - pi * pi, 2.0 * pr * pi
    prs.append(jnp.zeros_like(prs[0]))
    pis.append(jnp.zeros_like(pis[0]))
    return (bm8.astype(BF16), cm8.astype(BF16), jnp.stack(prs, axis=1), jnp.stack(pis, axis=1),
            d_skip.reshape(8, 1, 128))


def _s5_unpack_grads(dbm8, dcm8, da, dd):
    db = dbm8.reshape(8, 8, 16, 4, 2, 2, 64).transpose(0, 1, 2, 3, 5, 4, 6).reshape(8, 8, 16, 8, 2, 64)
    db = jnp.einsum("agcgqp->agcqp", db).reshape(S5_GROUPS, 16, 2, 64)
    dc = dcm8.reshape(8, 4, 2, 2, 64, 8, 16).transpose(0, 1, 3, 2, 4, 5, 6).reshape(8, 8, 2, 64, 8, 16)
    dc = jnp.einsum("agqpgc->agqpc", dc).reshape(S5_GROUPS, 2, 64, 16)
    dar = da[:, 0, :].reshape(S5_GROUPS, 64)
    dai = da[:, 1, :].reshape(S5_GROUPS, 64)
    return (dar, dai, db[:, :, 0, :].transpose(0, 2, 1), db[:, :, 1, :].transpose(0, 2, 1),
            dc[:, 0].transpose(0, 2, 1), -dc[:, 1].transpose(0, 2, 1), dd.reshape(S5_GROUPS, 16))


def _shift_rows(x, s, row, down):
    t = x.shape[0]
    if s % 8 == 0:
        z = jnp.zeros((s, x.shape[1]), x.dtype)
        return jnp.concatenate([z, x[:t - s]], axis=0) if down else jnp.concatenate([x[s:], z], axis=0)
    if down:
        return jnp.where(row >= s, pltpu.roll(x, s, 0), 0.0)
    return jnp.where(row < t - s, pltpu.roll(x, t - s, 0), 0.0)


def _s5_scan(xr, xi, pr, pi, cr, ci, row, conj):
    t = xr[0].shape[0]
    sg = -1.0 if conj else 1.0
    edge = (t - 1) if conj else 0
    n = len(xr)
    for k in range(n):
        sl = slice(128 * k, 128 * (k + 1))
        p_r, p_i = pr[0:1, sl], sg * pi[0:1, sl]
        xr[k] = xr[k] + jnp.where(row == edge, p_r * cr[k] - p_i * ci[k], 0.0)
        xi[k] = xi[k] + jnp.where(row == edge, p_r * ci[k] + p_i * cr[k], 0.0)
    for lvl in range(S5_LEVELS):
        s = 1 << lvl
        for k in range(n):
            sl = slice(128 * k, 128 * (k + 1))
            p_r, p_i = pr[lvl:lvl + 1, sl], sg * pi[lvl:lvl + 1, sl]
            sr = _shift_rows(xr[k], s, row, not conj)
            si = _shift_rows(xi[k], s, row, not conj)
            xr[k] = xr[k] + p_r * sr - p_i * si
            xi[k] = xi[k] + p_r * si + p_i * sr
    return xr, xi


def _s5_fwd(name, proj, mats):
    bm8, cm8, p1, p2, d8 = mats
    n_rows = proj.shape[0]
    t = S5_CHUNK
    nch = n_rows // t

    def body(u_ref, bm_ref, cm_ref, pr_ref, pi_ref, d_ref, y_ref, z_ref, st_ref, carry):
        @pl.when(pl.program_id(1) == 0)
        def _():
            carry[...] = jnp.zeros_like(carry)

        cv = carry[...]
        st_ref[...] = cv
        u = u_ref[...]
        bu = jnp.dot(u.astype(BF16), bm_ref[...], preferred_element_type=F32)
        row = lax.broadcasted_iota(jnp.int32, (t, 128), 0)
        tile = lambda v, j: v[:, 128 * j:128 * (j + 1)]
        xr, xi = _s5_scan([tile(bu, 2 * k) for k in range(4)], [tile(bu, 2 * k + 1) for k in range(4)],
                          pr_ref[...], pi_ref[...], [tile(cv, 2 * k)[0:1] for k in range(4)],
                          [tile(cv, 2 * k + 1)[0:1] for k in range(4)], row, False)
        xall = jnp.concatenate([v for k in range(4) for v in (xr[k], xi[k])], axis=1)
        carry[...] = jnp.broadcast_to(xall[t - 1:t, :], (8, 1024))
        y = jnp.dot(xall.astype(BF16), cm_ref[...], preferred_element_type=F32) + d_ref[...] * u
        y_ref[...] = y
        z_ref[...] = _gelu(y).astype(BF16)

    return pl.pallas_call(
        body, name=name,
        grid=(8, nch),
        in_specs=[
            pl.BlockSpec((t, 128), lambda g, c: (c, g)),
            pl.BlockSpec((None, 128, 1024), lambda g, c: (g, 0, 0)),
            pl.BlockSpec((None, 1024, 128), lambda g, c: (g, 0, 0)),
            pl.BlockSpec((None, 8, 512), lambda g, c: (g, 0, 0)),
            pl.BlockSpec((None, 8, 512), lambda g, c: (g, 0, 0)),
            pl.BlockSpec((None, 1, 128), lambda g, c: (g, 0, 0)),
        ],
        out_specs=[
            pl.BlockSpec((t, 128), lambda g, c: (c, g)),
            pl.BlockSpec((t, 128), lambda g, c: (c, g)),
            pl.BlockSpec((None, None, 8, 1024), lambda g, c: (g, c, 0, 0)),
        ],
        out_shape=[_sds((n_rows, S5_WIDTH), F32), _sds((n_rows, S5_WIDTH), BF16), _sds((8, nch, 8, 1024), F32)],
        scratch_shapes=[pltpu.VMEM((8, 1024), F32)],
        compiler_params=_params(("parallel", "arbitrary")),
    )(proj, bm8, cm8, p1, p2, d8)


def _s5_bwd(name, proj, dy, states, mats):
    bm8, cm8, p1, p2, d8 = mats
    n_rows = proj.shape[0]
    t = S5_CHUNK
    nch = n_rows // t
    nt_dims = (((1,), (1,)), ((), ()))
    tn_dims = (((0,), (0,)), ((), ()))

    def body(u_ref, dy_ref, st_ref, bm_ref, cm_ref, pr_ref, pi_ref, d_ref,
             du_ref, dbm_ref, dcm_ref, da_ref, dd_ref, gcarry):
        @pl.when(pl.program_id(1) == 0)
        def _():
            gcarry[...] = jnp.zeros_like(gcarry)
            dbm_ref[...] = jnp.zeros_like(dbm_ref)
            dcm_ref[...] = jnp.zeros_like(dcm_ref)
            da_ref[...] = jnp.zeros_like(da_ref)
            dd_ref[...] = jnp.zeros_like(dd_ref)

        u = u_ref[...]
        dyv = dy_ref[...]
        ub, dyb = u.astype(BF16), dyv.astype(BF16)
        bu = jnp.dot(ub, bm_ref[...], preferred_element_type=F32)
        dxd = lax.dot_general(dyb, cm_ref[...], nt_dims, preferred_element_type=F32)
        row = lax.broadcasted_iota(jnp.int32, (t, 128), 0)
        tile = lambda v, j: v[:, 128 * j:128 * (j + 1)]
        prv, piv, cv, gv = pr_ref[...], pi_ref[...], st_ref[...], gcarry[...]
        cr = [tile(cv, 2 * k)[0:1] for k in range(4)]
        ci = [tile(cv, 2 * k + 1)[0:1] for k in range(4)]
        xr, xi = _s5_scan([tile(bu, 2 * k) for k in range(4)], [tile(bu, 2 * k + 1) for k in range(4)],
                          prv, piv, cr, ci, row, False)
        gr, gi = _s5_scan([tile(dxd, 2 * k) for k in range(4)], [tile(dxd, 2 * k + 1) for k in range(4)],
                          prv, piv, [tile(gv, 2 * k)[0:1] for k in range(4)],
                          [tile(gv, 2 * k + 1)[0:1] for k in range(4)], row, True)
        dar, dai = [], []
        for k in range(4):
            xpr = jnp.where(row >= 1, pltpu.roll(xr[k], 1, 0), cr[k])
            xpi = jnp.where(row >= 1, pltpu.roll(xi[k], 1, 0), ci[k])
            dar.append(_colsum(gr[k] * xpr + gi[k] * xpi))
            dai.append(_colsum(gi[k] * xpr - gr[k] * xpi))
        xall = jnp.concatenate([v for k in range(4) for v in (xr[k], xi[k])], axis=1).astype(BF16)
        gf = jnp.concatenate([v for k in range(4) for v in (gr[k], gi[k])], axis=1)
        gcarry[...] = jnp.broadcast_to(gf[0:1, :], (8, 1024))
        gall = gf.astype(BF16)
        dcm_ref[...] += lax.dot_general(xall, dyb, tn_dims, preferred_element_type=F32)
        dbm_ref[...] += lax.dot_general(ub, gall, tn_dims, preferred_element_type=F32)
        du = lax.dot_general(gall, bm_ref[...], nt_dims, preferred_element_type=F32) + d_ref[...] * dyv
        du_ref[...] = du.astype(BF16)
        dd_ref[...] += _colsum(dyv * u)
        da_ref[0:1, :] += jnp.concatenate(dar, axis=1)
        da_ref[1:2, :] += jnp.concatenate(dai, axis=1)

    rev = lambda g, c: (nch - 1 - c, g)
    return pl.pallas_call(
        body, name=name,
        grid=(8, nch),
        in_specs=[
            pl.BlockSpec((t, 128), rev),
            pl.BlockSpec((t, 128), rev),
            pl.BlockSpec((None, None, 8, 1024), lambda g, c: (g, nch - 1 - c, 0, 0)),
            pl.BlockSpec((None, 128, 1024), lambda g, c: (g, 0, 0)),
            pl.BlockSpec((None, 1024, 128), lambda g, c: (g, 0, 0)),
            pl.BlockSpec((None, 8, 512), lambda g, c: (g, 0, 0)),
            pl.BlockSpec((None, 8, 512), lambda g, c: (g, 0, 0)),
            pl.BlockSpec((None, 1, 128), lambda g, c: (g, 0, 0)),
        ],
        out_specs=[
            pl.BlockSpec((t, 128), rev),
            pl.BlockSpec((None, 128, 1024), lambda g, c: (g, 0, 0)),
            pl.BlockSpec((None, 1024, 128), lambda g, c: (g, 0, 0)),
            pl.BlockSpec((None, 8, 512), lambda g, c: (g, 0, 0)),
            pl.BlockSpec((None, 1, 128), lambda g, c: (g, 0, 0)),
        ],
        out_shape=[_sds((n_rows, S5_WIDTH), BF16), _sds((8, 128, 1024), F32), _sds((8, 1024, 128), F32),
                   _sds((8, 8, 512), F32), _sds((8, 1, 128), F32)],
        scratch_shapes=[pltpu.VMEM((8, 1024), F32)],
        compiler_params=_params(("parallel", "arbitrary")),
    )(proj, dy, states, bm8, cm8, p1, p2, d8)


def _hgrn_lower_bounds(lb_param):
    p = jax.nn.softmax(lb_param, axis=0)
    return jnp.cumsum(p, axis=0) - p[0:1]


def _prefix16(x, r16):
    for s in (1, 2, 4, 8):
        x = x + jnp.where(r16 >= s, pltpu.roll(x, s, 0), 0.0)
    return x


def _suffix16(x, r16):
    n = x.shape[0]
    for s in (1, 2, 4, 8):
        x = x + jnp.where(r16 < HGRN_SUB - s, pltpu.roll(x, n - s, 0), 0.0)
    return x


_NT = (((1,), (1,)), ((), ()))
_TN = (((0,), (0,)), ((), ()))


def _dotf(a, b, dims=(((1,), (0,)), ((), ()))):
    return lax.dot_general(a.astype(BF16), b.astype(BF16), dims, preferred_element_type=F32)


def _hgrn_specs(n_blocks, rev):
    r = HGRN_BLOCK
    blk = (lambda b: n_blocks - 1 - b) if rev else (lambda b: b)
    proj_specs = [pl.BlockSpec((r, 128), (lambda h, b, c=c: (blk(b), 8 * c + h))) for c in (1, 2, 3, 4)]
    lb_spec = pl.BlockSpec((None, 1, 128), lambda h, b: (h, 0, 0))
    gain_spec = pl.BlockSpec((1, 128), lambda h, b: (0, 0))
    row_spec = pl.BlockSpec((r, 128), lambda h, b: (blk(b), h))
    st_spec = pl.BlockSpec((None, None, 128, 128), lambda h, b: (h, blk(b), 0, 0))
    return proj_specs, lb_spec, gain_spec, row_spec, st_spec, blk


def _hgrn_fwd(name, proj, lb, gain):
    n_rows = proj.shape[0]
    r = HGRN_BLOCK
    nb = n_rows // r
    nsub = r // HGRN_SUB
    proj_specs, lb_spec, gain_spec, row_spec, st_spec, _ = _hgrn_specs(nb, False)

    def body(q_ref, f_ref, i_ref, g_ref, lb_ref, gain_ref, o_ref, y_ref, st_ref, st_scr):
        @pl.when(pl.program_id(1) == 0)
        def _():
            st_scr[...] = jnp.zeros_like(st_scr)

        st_ref[...] = st_scr[...]
        q, f, v, g = q_ref[...], f_ref[...], i_ref[...], g_ref[...]
        lbv = lb_ref[...]
        qs = q * _sigmoid(q)
        fg = lbv + (1.0 - lbv) * _sigmoid(f)
        kk = 1.0 - fg
        r16 = lax.broadcasted_iota(jnp.int32, (r, 128), 0) & (HGRN_SUB - 1)
        b = _prefix16(jnp.log(fg), r16)
        qh = qs * jnp.exp(b)
        rs = lax.broadcasted_iota(jnp.int32, (HGRN_SUB, 128), 0)
        st = st_scr[...]
        outs = []
        for i in range(nsub):
            sl = slice(HGRN_SUB * i, HGRN_SUB * (i + 1))
            qsi, kki, vi, bi = qs[sl], kk[sl], v[sl], b[sl]
            o_i = _dotf(qh[sl], st, _NT)
            for s in range(HGRN_SUB):
                e = jnp.exp(jnp.where(rs >= s, bi - bi[s:s + 1], NEG))
                col = jnp.sum(qsi * e * kki[s:s + 1], axis=1, keepdims=True)
                o_i = o_i + col * vi[s:s + 1]
            bl = bi[HGRN_SUB - 1:HGRN_SUB]
            st = st * jnp.exp(bl) + _dotf(vi, kki * jnp.exp(bl - bi), _TN)
            outs.append(o_i)
        st_scr[...] = st
        o = jnp.concatenate(outs, axis=0)
        o_ref[...] = o
        rn = lax.rsqrt(jnp.mean(o * o, axis=1, keepdims=True) + EPS)
        y_ref[...] = (o * rn * gain_ref[...] * (g * _sigmoid(g))).astype(BF16)

    return pl.pallas_call(
        body, name=name,
        grid=(HGRN_HEADS, nb),
        in_specs=proj_specs + [lb_spec, gain_spec],
        out_specs=[row_spec, row_spec, st_spec],
        out_shape=[_sds((n_rows, HGRN_WIDTH), F32), _sds((n_rows, HGRN_WIDTH), BF16),
                   _sds((HGRN_HEADS, nb, 128, 128), F32)],
        scratch_shapes=[pltpu.VMEM((128, 128), F32)],
        compiler_params=_params(("parallel", "arbitrary")),
    )(proj, proj, proj, proj, lb, gain)


def _hgrn_bwd(name, proj, lb, gain, o_saved, states, dycat):
    n_rows = proj.shape[0]
    r = HGRN_BLOCK
    nb = n_rows // r
    nsub = r // HGRN_SUB
    proj_specs, lb_spec, gain_spec, row_spec, st_spec, blk = _hgrn_specs(nb, True)
    dy_spec = pl.BlockSpec((r, 128), lambda h, b: (blk(b), 8 + h))
    acc_spec = pl.BlockSpec((None, 1, 128), lambda h, b: (h, 0, 0))

    def body(q_ref, f_ref, i_ref, g_ref, lb_ref, gain_ref, o_ref, st_ref, dy_ref,
             dq_ref, df_ref, di_ref, dg_ref, dlb_ref, dgain_ref, dst_scr, sub_scr):
        @pl.when(pl.program_id(1) == 0)
        def _():
            dst_scr[...] = jnp.zeros_like(dst_scr)
            dlb_ref[...] = jnp.zeros_like(dlb_ref)
            dgain_ref[...] = jnp.zeros_like(dgain_ref)

        q, f, v, g = q_ref[...], f_ref[...], i_ref[...], g_ref[...]
        lbv, gain_v = lb_ref[...], gain_ref[...]
        sq = _sigmoid(q)
        qs = q * sq
        sf = _sigmoid(f)
        fg = lbv + (1.0 - lbv) * sf
        kk = 1.0 - fg
        r16 = lax.broadcasted_iota(jnp.int32, (r, 128), 0) & (HGRN_SUB - 1)
        b = _prefix16(jnp.log(fg), r16)
        eb = jnp.exp(b)
        qh = qs * eb

        o, dy = o_ref[...], dy_ref[...]
        rn = lax.rsqrt(jnp.mean(o * o, axis=1, keepdims=True) + EPS)
        on = o * rn
        sg = _sigmoid(g)
        sil = g * sg
        dgain_ref[...] += _colsum(dy * on * sil)
        dg_ref[...] = (dy * on * gain_v * (sg * (1.0 + g * (1.0 - sg)))).astype(BF16)
        don = dy * gain_v * sil
        do = rn * (don - on * jnp.mean(don * on, axis=1, keepdims=True))

        st = st_ref[...]
        for i in range(nsub):
            sl = slice(HGRN_SUB * i, HGRN_SUB * (i + 1))
            sub_scr[i] = st
            bi = b[sl]
            bl = bi[HGRN_SUB - 1:HGRN_SUB]
            st = st * jnp.exp(bl) + _dotf(v[sl], kk[sl] * jnp.exp(bl - bi), _TN)

        rs = lax.broadcasted_iota(jnp.int32, (HGRN_SUB, 128), 0)
        dst = dst_scr[...]
        parts = [None] * nsub
        for i in reversed(range(nsub)):
            sl = slice(HGRN_SUB * i, HGRN_SUB * (i + 1))
            sp = sub_scr[i]
            qsi, kki, vi, bi, doi, qhi = qs[sl], kk[sl], v[sl], b[sl], do[sl], qh[sl]
            bl = bi[HGRN_SUB - 1:HGRN_SUB]
            ebl = jnp.exp(bl)
            dec = jnp.exp(bl - bi)
            khat = kki * dec
            dqh = _dotf(doi, sp)
            dkhat = _dotf(vi, dst)
            dv = _dotf(khat, dst, _NT)
            zrow = _colsum(sp * dst) * ebl
            dq_in = jnp.zeros((HGRN_SUB, 128), F32)
            dk_in = jnp.zeros((HGRN_SUB, 128), F32)
            dv_in = jnp.zeros((HGRN_SUB, 128), F32)
            for s in range(HGRN_SUB):
                e = jnp.exp(jnp.where(rs >= s, bi - bi[s:s + 1], NEG))
                dpc = jnp.sum(doi * vi[s:s + 1], axis=1, keepdims=True)
                w = qsi * e
                pc = jnp.sum(w * kki[s:s + 1], axis=1, keepdims=True)
                dq_in = dq_in + dpc * e * kki[s:s + 1]
                dk_in = jnp.where(rs == s, _colsum(dpc * w), dk_in)
                dv_in = jnp.where(rs == s, _colsum(pc * doi), dv_in)
            kd = khat * dkhat
            parts[i] = (qsi * dq_in - kki * dk_in + qhi * dqh, kd, jnp.broadcast_to(zrow, (HGRN_SUB, 128)),
                        dq_in + dqh * eb[sl], dk_in + dkhat * dec, dv + dv_in)
            dst = dst * ebl + _dotf(doi, qhi, _TN)
        dst_scr[...] = dst

        cat = lambda j: jnp.concatenate([p[j] for p in parts], axis=0)
        d_b, kd, zr, dqs, dkk, dvv = (cat(j) for j in range(6))
        dlf = _suffix16(d_b, r16) + _prefix16(kd, r16) - kd + zr
        dfg = dlf / fg - dkk
        df_ref[...] = (dfg * (1.0 - lbv) * sf * (1.0 - sf)).astype(BF16)
        dlb_ref[...] += _colsum(dfg * (1.0 - sf))
        dq_ref[...] = (dqs * (sq * (1.0 + q * (1.0 - sq)))).astype(BF16)
        di_ref[...] = dvv.astype(BF16)

    return pl.pallas_call(
        body, name=name,
        grid=(HGRN_HEADS, nb),
        in_specs=proj_specs + [lb_spec, gain_spec, row_spec, st_spec, dy_spec],
        out_specs=[row_spec] * 4 + [acc_spec, acc_spec],
        out_shape=[_sds((n_rows, HGRN_WIDTH), BF16)] * 4 + [_sds((HGRN_HEADS, 1, 128), F32)] * 2,
        scratch_shapes=[pltpu.VMEM((128, 128), F32), pltpu.VMEM((nsub, 128, 128), F32)],
        compiler_params=_params(("parallel", "arbitrary")),
    )(proj, proj, proj, proj, lb, gain, o_saved, states, dycat)


def _alibi_slopes():
    return jnp.exp2(-8.0 * jnp.arange(1, ATT_HEADS + 1, dtype=F32) / ATT_HEADS)


def _swa_specs(n_blocks):
    blk = ATT_BLOCK
    prev = lambda i: jnp.maximum(i - 1, 0)
    smem = pl.BlockSpec(memory_space=pltpu.SMEM)
    return [
        smem, smem,
        pl.BlockSpec((blk, ATT_HEADS * ATT_DIM), lambda i: (i, 0)),
        pl.BlockSpec((blk, 256), lambda i: (i, 8)),
        pl.BlockSpec((blk, 256), lambda i: (prev(i), 8)),
        pl.BlockSpec((blk, 256), lambda i: (i, 9)),
        pl.BlockSpec((blk, 256), lambda i: (prev(i), 9)),
        pl.BlockSpec((1, ATT_DIM), lambda i: (0, 0)),
        pl.BlockSpec((1, ATT_DIM), lambda i: (0, 0)),
    ]


_ATT_GROUP = ATT_HEADS // ATT_KV
_ATT_ROWS = _ATT_GROUP * ATT_BLOCK


def _swa_mask(i):
    t_i = lax.broadcasted_iota(jnp.int32, (_ATT_ROWS, 2 * ATT_BLOCK), 0) & (ATT_BLOCK - 1)
    s_i = lax.broadcasted_iota(jnp.int32, (_ATT_ROWS, 2 * ATT_BLOCK), 1)
    dist = t_i + ATT_BLOCK - s_i
    valid = (dist >= 0) & (dist < ATT_BLOCK) & ((s_i >= ATT_BLOCK) | (i > 0))
    return valid, dist.astype(F32)


def _stack_heads(x):
    return jnp.concatenate([x[:, ATT_DIM * h:ATT_DIM * (h + 1)] for h in range(_ATT_GROUP)], axis=0)


def _unstack_heads(x):
    return jnp.concatenate([x[ATT_BLOCK * h:ATT_BLOCK * (h + 1)] for h in range(_ATT_GROUP)], axis=1)


def _head_column(ref, g):
    return jnp.concatenate([jnp.full((ATT_BLOCK, 1), ref[_ATT_GROUP * g + h], F32) for h in range(_ATT_GROUP)], axis=0)


def _swa_probs(qn, kn, slope, sink, valid, distf):
    s = lax.dot_general(qn, kn, _NT, preferred_element_type=F32) * (ATT_DIM ** -0.5) - slope * distf
    s = jnp.where(valid, s, NEG)
    m = jnp.maximum(jnp.max(s, axis=1, keepdims=True), sink)
    p = jnp.exp(s - m)
    es = jnp.exp(sink - m)
    inv = 1.0 / (jnp.sum(p, axis=1, keepdims=True) + es)
    return p * inv, es * inv


def _swa_fwd(name, qkv, q_gain, k_gain, sinks, slopes):
    n_rows = qkv.shape[0]
    nb = n_rows // ATT_BLOCK

    def body(sink_ref, slope_ref, q_ref, kc_ref, kp_ref, vc_ref, vp_ref, qg_ref, kg_ref, o_ref):
        i = pl.program_id(0)
        kb = jnp.concatenate([kp_ref[...], kc_ref[...]], axis=0)
        vb = jnp.concatenate([vp_ref[...], vc_ref[...]], axis=0)
        valid, distf = _swa_mask(i)
        qgv, kgv = qg_ref[...], kg_ref[...]
        gw = _ATT_GROUP * ATT_DIM
        for g in range(ATT_KV):
            kg = kb[:, 64 * g:64 * (g + 1)]
            rk = lax.rsqrt(jnp.mean(kg * kg, axis=1, keepdims=True) + EPS)
            kn = (kg * rk * kgv).astype(BF16)
            vv = vb[:, 64 * g:64 * (g + 1)].astype(BF16)
            qs = _stack_heads(q_ref[:, gw * g:gw * (g + 1)])
            rq = lax.rsqrt(jnp.mean(qs * qs, axis=1, keepdims=True) + EPS)
            pn, _ = _swa_probs((qs * rq * qgv).astype(BF16), kn, _head_column(slope_ref, g),
                               _head_column(sink_ref, g), valid, distf)
            out = jnp.dot(pn.astype(BF16), vv, preferred_element_type=F32)
            o_ref[:, gw * g:gw * (g + 1)] = _unstack_heads(out).astype(BF16)

    return pl.pallas_call(
        body, name=name,
        grid=(nb,),
        in_specs=_swa_specs(nb),
        out_specs=pl.BlockSpec((ATT_BLOCK, ATT_HEADS * ATT_DIM), lambda i: (i, 0)),
        out_shape=_sds((n_rows, ATT_HEADS * ATT_DIM), BF16),
        compiler_params=_params(("parallel",)),
    )(sinks, slopes, qkv, qkv, qkv, qkv, qkv, q_gain.reshape(1, -1), k_gain.reshape(1, -1))


def _swa_bwd(name, qkv, q_gain, k_gain, sinks, slopes, d_out):
    n_rows = qkv.shape[0]
    nb = n_rows // ATT_BLOCK
    blk = ATT_BLOCK

    def body(sink_ref, slope_ref, q_ref, kc_ref, kp_ref, vc_ref, vp_ref, qg_ref, kg_ref, do_ref,
             dq_ref, dkc_ref, dkp_ref, dvc_ref, dvp_ref, dsink_ref, dqg_ref, dkg_ref):
        i = pl.program_id(0)

        @pl.when(i == 0)
        def _():
            dsink_ref[...] = jnp.zeros_like(dsink_ref)
            dqg_ref[...] = jnp.zeros_like(dqg_ref)
            dkg_ref[...] = jnp.zeros_like(dkg_ref)

        kb = jnp.concatenate([kp_ref[...], kc_ref[...]], axis=0)
        vb = jnp.concatenate([vp_ref[...], vc_ref[...]], axis=0)
        kgv, qgv = kg_ref[...], qg_ref[...]
        valid, distf = _swa_mask(i)
        scale = ATT_DIM ** -0.5
        gw = _ATT_GROUP * ATT_DIM
        dks, dvs = [], []
        dqg, dkg = jnp.zeros((1, ATT_DIM), F32), jnp.zeros((1, ATT_DIM), F32)
        for g in range(ATT_KV):
            kg = kb[:, 64 * g:64 * (g + 1)]
            rk = lax.rsqrt(jnp.mean(kg * kg, axis=1, keepdims=True) + EPS)
            khat = kg * rk
            kn = (khat * kgv).astype(BF16)
            vv = vb[:, 64 * g:64 * (g + 1)].astype(BF16)
            qs = _stack_heads(q_ref[:, gw * g:gw * (g + 1)])
            rq = lax.rsqrt(jnp.mean(qs * qs, axis=1, keepdims=True) + EPS)
            qhat = qs * rq
            qn = (qhat * qgv).astype(BF16)
            pn, ps = _swa_probs(qn, kn, _head_column(slope_ref, g), _head_column(sink_ref, g), valid, distf)
            dos = _stack_heads(do_ref[:, gw * g:gw * (g + 1)]).astype(BF16)
            dp = lax.dot_general(dos, vv, _NT, preferred_element_type=F32)
            delta = jnp.sum(pn * dp, axis=1, keepdims=True)
            ds = (pn * (dp - delta)).astype(BF16)
            sd = ps * delta
            for h in range(_ATT_GROUP):
                hs = _ATT_GROUP * g + h
                dsink_ref[hs:hs + 1, :] += jnp.zeros((1, 128), F32) - jnp.sum(sd[blk * h:blk * (h + 1)])
            dvs.append(lax.dot_general(pn.astype(BF16), dos, _TN, preferred_element_type=F32))
            dkn = lax.dot_general(ds, qn, _TN, preferred_element_type=F32) * scale
            dqn = jnp.dot(ds, kn, preferred_element_type=F32) * scale
            dqg = dqg + _colsum(dqn * qhat)
            dqhat = dqn * qgv
            dqs = rq * (dqhat - qhat * jnp.mean(dqhat * qhat, axis=1, keepdims=True))
            dq_ref[:, gw * g:gw * (g + 1)] = _unstack_heads(dqs).astype(BF16)
            dkg = dkg + _colsum(dkn * khat)
            dkhat = dkn * kgv
            dks.append(rk * (dkhat - khat * jnp.mean(dkhat * khat, axis=1, keepdims=True)))
        dqg_ref[...] += dqg
        dkg_ref[...] += dkg
        dk = jnp.concatenate(dks, axis=1).astype(BF16)
        dv = jnp.concatenate(dvs, axis=1).astype(BF16)
        dkp_ref[...] = dk[:blk]
        dkc_ref[...] = dk[blk:]
        dvp_ref[...] = dv[:blk]
        dvc_ref[...] = dv[blk:]

    kv_spec = pl.BlockSpec((blk, 256), lambda i: (i, 0))
    full = pl.BlockSpec((blk, ATT_HEADS * ATT_DIM), lambda i: (i, 0))
    acc64 = pl.BlockSpec((1, ATT_DIM), lambda i: (0, 0))
    return pl.pallas_call(
        body, name=name,
        grid=(nb,),
        in_specs=_swa_specs(nb) + [full],
        out_specs=[full, kv_spec, kv_spec, kv_spec, kv_spec,
                   pl.BlockSpec((ATT_HEADS, 128), lambda i: (0, 0)), acc64, acc64],
        out_shape=[_sds((n_rows, ATT_HEADS * ATT_DIM), BF16)] + [_sds((n_rows, 256), BF16)] * 4
        + [_sds((ATT_HEADS, 128), F32), _sds((1, ATT_DIM), F32), _sds((1, ATT_DIM), F32)],
        compiler_params=_params(("arbitrary",)),
    )(sinks, slopes, qkv, qkv, qkv, qkv, qkv, q_gain.reshape(1, -1), k_gain.reshape(1, -1), d_out)


def _mesh_pos():
    return lax.axis_index("x"), lax.axis_index("y"), lax.axis_index("c")


_ANY = pl.BlockSpec(memory_space=pl.ANY)


def _allgather(name, shards):
    n = len(shards)

    def body(*refs):
        x_refs, out_refs = refs[:n], refs[n:2 * n]
        send_sems, recv_sems, local_sems = refs[2 * n:]
        x, y, c = _mesh_pos()
        me, sibling = (x, y, c), (x, y, 1 - c)
        chips = [(1 - x, y), (x, 1 - y), (1 - x, 1 - y)]

        def slot(a, px, py, pc):
            return out_refs[a].at[4 * px + 2 * py + pc]

        def copy(a, k, block, to, src=None):
            return pltpu.make_async_remote_copy(
                src_ref=slot(a, *block) if src is None else src, dst_ref=slot(a, *block),
                send_sem=send_sems.at[7 * a + k], recv_sem=recv_sems.at[7 * a + k],
                device_id=to, device_id_type=MESH)

        started = []
        for a in range(n):
            mine = pltpu.make_async_copy(x_refs[a], slot(a, *me), local_sems.at[a])
            mine.start()
            first = [copy(a, 0, me, sibling, src=x_refs[a])]
            first += [copy(a, 1 + j, me, (*chip, c), src=x_refs[a]) for j, chip in enumerate(chips)]
            for cp in first:
                cp.start()
            started += first
        for a in range(n):
            for j, chip in enumerate(chips):
                copy(a, 1 + j, (*chip, c), me).wait_recv()
                fwd = copy(a, 4 + j, (*chip, c), sibling)
                fwd.start()
                started.append(fwd)
        for a in range(n):
            copy(a, 0, sibling, me).wait_recv()
            for j, chip in enumerate(chips):
                copy(a, 4 + j, (*chip, 1 - c), me).wait_recv()
        for cp in started:
            cp.wait_send()
        for a in range(n):
            pltpu.make_async_copy(x_refs[a], slot(a, *me), local_sems.at[a]).wait()

    return pl.pallas_call(
        body, name=name,
        out_shape=[_sds((N_DEV,) + s.shape, s.dtype) for s in shards],
        in_specs=[_ANY] * n,
        out_specs=[_ANY] * n,
        scratch_shapes=[pltpu.SemaphoreType.DMA((7 * n,)), pltpu.SemaphoreType.DMA((7 * n,)),
                        pltpu.SemaphoreType.DMA((n,))],
    )(*shards)


def _swap_with_sibling(name, arrs):
    n = len(arrs)

    def body(*refs):
        x_refs, got_refs = refs[:n], refs[n:2 * n]
        send_sems, recv_sems = refs[2 * n:]
        x, y, c = _mesh_pos()
        copies = []
        for a in range(n):
            for j in range(4):
                k = 4 * a + j
                cp = pltpu.make_async_remote_copy(
                    src_ref=x_refs[a].at[j, 1 - c], dst_ref=got_refs[a].at[j],
                    send_sem=send_sems.at[k], recv_sem=recv_sems.at[k],
                    device_id=(x, y, 1 - c), device_id_type=MESH)
                cp.start()
                copies.append(cp)
        for cp in copies:
            cp.wait()

    return pl.pallas_call(
        body, name=name,
        out_shape=[_sds((4,) + t.shape[2:], t.dtype) for t in arrs],
        in_specs=[_ANY] * n,
        out_specs=[_ANY] * n,
        scratch_shapes=[pltpu.SemaphoreType.DMA((4 * n,))] * 2,
    )(*arrs)


def _exchange_chips(name, arrs):
    n = len(arrs)

    def body(*refs):
        x_refs, out_refs = refs[:n], refs[n:2 * n]
        send_sems, recv_sems, local_sems = refs[2 * n:]
        x, y, c = _mesh_pos()
        me = 2 * x + y
        copies, local = [], []
        for a in range(n):
            mine = pltpu.make_async_copy(x_refs[a].at[me], out_refs[a].at[me], local_sems.at[a])
            mine.start()
            local.append(mine)
            for k in range(1, 4):
                px, py = x ^ (k >> 1), y ^ (k & 1)
                peer = 2 * px + py
                sem = 3 * a + k - 1
                send = pltpu.make_async_remote_copy(
                    src_ref=x_refs[a].at[peer], dst_ref=out_refs[a].at[me],
                    send_sem=send_sems.at[sem], recv_sem=recv_sems.at[sem],
                    device_id=(px, py, c), device_id_type=MESH)
                recv = pltpu.make_async_remote_copy(
                    src_ref=x_refs[a].at[peer], dst_ref=out_refs[a].at[peer],
                    send_sem=send_sems.at[sem], recv_sem=recv_sems.at[sem],
                    device_id=(px, py, c), device_id_type=MESH)
                send.start()
                copies.append((send, recv))
        for send, recv in copies:
            recv.wait_recv()
        for send, recv in copies:
            send.wait_send()
        for mine in local:
            mine.wait()

    return pl.pallas_call(
        body, name=name,
        out_shape=[_sds(t.shape, t.dtype) for t in arrs],
        in_specs=[_ANY] * n,
        out_specs=[_ANY] * n,
        scratch_shapes=[pltpu.SemaphoreType.DMA((3 * n,)), pltpu.SemaphoreType.DMA((3 * n,)),
                        pltpu.SemaphoreType.DMA((n,))],
    )(*arrs)


def _sum_blocks(name, blocks, out_dtype=F32, tr=512):
    n, n_rows, n_cols = blocks.shape
    tr = min(tr, n_rows)
    assert n_rows % tr == 0

    def body(x_ref, o_ref):
        acc = x_ref[0].astype(F32)
        for s in range(1, n):
            acc = acc + x_ref[s].astype(F32)
        o_ref[...] = acc.astype(o_ref.dtype)

    return pl.pallas_call(
        body, name=name,
        grid=(n_rows // tr,),
        in_specs=[pl.BlockSpec((n, tr, n_cols), lambda i: (0, i, 0))],
        out_specs=pl.BlockSpec((tr, n_cols), lambda i: (i, 0)),
        out_shape=_sds((n_rows, n_cols), out_dtype),
        compiler_params=_params(("parallel",)),
    )(blocks)


def _add_pair(name, mine, got, core):
    n, n_rows, n_cols = got.shape
    tr = 512
    assert n_rows % tr == 0

    def body(core_ref, a_ref, b_ref, o_ref):
        o_ref[...] = (a_ref[...].astype(F32) + b_ref[...].astype(F32)).astype(BF16)

    spec = pl.BlockSpec((None, tr, n_cols), lambda j, i, core_ref: (j, i, 0))
    return pl.pallas_call(
        body, name=name,
        grid_spec=pltpu.PrefetchScalarGridSpec(
            num_scalar_prefetch=1,
            grid=(n, n_rows // tr),
            in_specs=[pl.BlockSpec((None, None, tr, n_cols), lambda j, i, core_ref: (j, core_ref[0], i, 0)), spec],
            out_specs=spec,
        ),
        out_shape=_sds(got.shape, BF16),
        compiler_params=_params(("parallel", "parallel")),
    )(core, mine, got)


_BIG = (
    ("even_w_in", 2, 2, D_MODEL, 5120),
    ("s5_w_glu", 1, 2, S5_WIDTH, S5_WIDTH),
    ("even_w_out", 1, 2, D_MODEL, D_MODEL),
    ("odd_w_qkv", 2, 2, D_MODEL, QKV_WIDTH),
    ("odd_w_out", 1, 2, D_MODEL, D_MODEL),
)
_PACK_COLS = 1024
_FF_SHARD = D_FF // N_DEV


def _big_rows(nl, rows, cols):
    return nl * rows * cols // N_DEV // _PACK_COLS


def _unpack_gathered(gathered):
    out, off = {}, 0
    for name, axis, nl, rows, cols in _BIG:
        n = _big_rows(nl, rows, cols)
        part = gathered[:, off:off + n]
        off += n
        if axis == 2:
            w = part.reshape(N_DEV, nl, rows, cols // N_DEV).transpose(1, 2, 0, 3)
        else:
            w = part.reshape(N_DEV, nl, rows // N_DEV, cols).transpose(1, 0, 2, 3)
        out[name] = w.reshape(nl, rows, cols)
    return out


def _pack_by_owner(full):
    parts = []
    for name, axis, nl, rows, cols in _BIG:
        g = full[name].astype(BF16)
        if axis == 2:
            g = g.reshape(nl, rows, N_DEV, cols // N_DEV).transpose(2, 0, 1, 3)
        else:
            g = g.reshape(nl, N_DEV, rows // N_DEV, cols).transpose(1, 0, 2, 3)
        parts.append(g.reshape(N_DEV, -1, _PACK_COLS))
    return jnp.concatenate(parts, axis=1)


def _pack_small(arrs, row_mult=512):
    parts = []
    for a in arrs:
        f = a.astype(F32).reshape(-1)
        parts.append(jnp.pad(f, (0, (-f.shape[0]) % 128)))
    f = jnp.concatenate(parts)
    f = jnp.pad(f, (0, (-f.shape[0]) % (128 * row_mult)))
    return f.reshape(-1, 128)


def _unpack_small(flat, shapes):
    f = flat.reshape(-1)
    out, off = [], 0
    for s in shapes:
        n = math.prod(s)
        out.append(f[off:off + n].reshape(s))
        off += n + (-n) % 128
    return out


_WEIGHTS = ("even_norm", "even_w_in", "s5_lambda_re", "s5_lambda_im", "s5_log_dt", "s5_b_re", "s5_b_im",
            "s5_c_re", "s5_c_im", "s5_d", "s5_w_glu", "s5_b_glu", "hgrn_lower_bound", "hgrn_o_norm",
            "even_w_out", "odd_norm", "odd_w_qkv", "q_norm", "k_norm", "att_sinks", "odd_w_out",
            "mlp_norm", "mlp_w_up", "mlp_w_down")
_BIG_NAMES = tuple(b[0] for b in _BIG)
_MLP_NAMES = ("mlp_w_up", "mlp_w_down")
_SMALL_NAMES = tuple(n for n in _WEIGHTS if n not in _BIG_NAMES + _MLP_NAMES)


def _add_res(acc, res):
    return (acc + res,)


def _mlp_fwd(h, gain, w_up, w_down):
    n_rows, fs = h.shape[0], _FF_SHARD
    xn, rstd = _rms_fwd("rms_fwd", h, gain)
    up, act = _mm("mm_up", xn, w_up, "nn", out_dtypes=(F32, BF16), mkn=(n_rows, D_MODEL, D_FF), tn=fs,
                  b_block=pl.BlockSpec((None, D_MODEL, fs), lambda i, j, kk: (j, kk, 0)),
                  epi=lambda acc: (acc, jnp.square(jnp.maximum(acc, 0.0))))
    out = _mm("mm_down", act, w_down.reshape(N_DEV // 2, 2 * fs, D_MODEL), "nn", mkn=(n_rows, D_FF, D_MODEL),
              tk=2 * fs, b_block=pl.BlockSpec((None, 2 * fs, 1024), lambda i, j, kk: (kk, 0, j)),
              epi=_add_res, extras=(h,))
    return out, (h, gain, xn, rstd, up, act, w_up, w_down)


def _mlp_bwd(cache, dh, dhb):
    h, gain, xn, rstd, up, act, w_up, w_down = cache
    n_rows, fs = h.shape[0], _FF_SHARD
    dup = _mm("mm_dact", dhb, w_down, "nt", out_dtypes=(BF16,), mkn=(n_rows, D_MODEL, D_FF), tn=fs,
              b_block=pl.BlockSpec((None, fs, D_MODEL), lambda i, j, kk: (j, 0, kk)),
              epi=lambda acc, u: (acc * (2.0 * jnp.maximum(u, 0.0)),), extras=(up,))
    dw_down = _mm("mm_dw_down", act, dhb, "tn", out_dtypes=(BF16,),
                  o_block=(pl.BlockSpec((None, 512, 1024), lambda i, j, kk: (i // 2, i % 2, j)),
                           (N_DEV, fs, D_MODEL)))
    dxn = _mm("mm_dxn_up", dup, w_up, "nt", mkn=(n_rows, D_FF, D_MODEL), tk=fs,
              b_block=pl.BlockSpec((None, 1024, fs), lambda i, j, kk: (kk, j, 0)))
    dw_up = _mm("mm_dw_up", xn, dup, "tn", out_dtypes=(BF16,),
                o_block=(pl.BlockSpec((None, 512, fs), lambda i, j, kk: (j, i, 0)), (N_DEV, D_MODEL, fs)))
    dh_in, dhb_in, dgain = _rms_bwd("rms_bwd", h, rstd, gain, dxn, dh)
    return dh_in, dhb_in, dgain, dw_up, dw_down


def _even_fwd(h, p):
    xn, rstd = _rms_fwd("rms_fwd", h, p["norm"])
    proj = _mm("mm_w_in", xn, p["w_in"], "nn")
    y_pre, z, s5_states = _s5_fwd("s5_fwd", proj, p["mats"])
    gate = _mm("mm_glu", z, p["w_glu"], "nn")
    (ya,) = _rowwise("glu_fwd", lambda y, gt, b: ((_gelu(y) * _sigmoid(gt + b),), ()),
                     [y_pre, gate], [p["b_glu"].reshape(1, -1)], [(S5_WIDTH, BF16)])
    o, yb, h_states = _hgrn_fwd("hgrn_fwd", proj, p["lb"].reshape(8, 1, 128), p["o_gain"].reshape(1, 128))
    ycat = jnp.concatenate([ya, yb], axis=1)
    out = _mm("mm_w_out", ycat, p["w_out"], "nn", epi=_add_res, extras=(h,))
    return out, (h, xn, rstd, proj, y_pre, z, s5_states, gate, o, h_states, ycat)


def _even_bwd(cache, p, dh, dhb):
    h, xn, rstd, proj, y_pre, z, s5_states, gate, o, h_states, ycat = cache
    g = {}
    dycat = _mm("mm_dy_out", dhb, p["w_out"], "nt")
    g["w_out"] = _mm("mm_dw_out", ycat, dhb, "tn", out_dtypes=(BF16,))
    dq, df, di, dg, dlb, dgain = _hgrn_bwd("hgrn_bwd", proj, p["lb"].reshape(8, 1, 128),
                                           p["o_gain"].reshape(1, 128), o, h_states, dycat)
    g["lb"] = dlb.reshape(-1)
    g["o_gain"] = jnp.sum(dgain, axis=0).reshape(-1)

    def glu_bwd1(dyc, y, gt, b):
        zf = _gelu(y)
        s = _sigmoid(gt + b)
        dya = dyc[:, :S5_WIDTH]
        d_gate = dya * zf * s * (1.0 - s)
        return (d_gate, dya * s), (_colsum(d_gate),)

    d_gate, dz_direct, db_glu = _rowwise("glu_bwd_gate", glu_bwd1, [dycat, y_pre, gate], [p["b_glu"].reshape(1, -1)],
                                         [(S5_WIDTH, BF16), (S5_WIDTH, F32)], [S5_WIDTH])
    g["b_glu"] = db_glu.reshape(-1)
    dz_gate = _mm("mm_dz_glu", d_gate, p["w_glu"], "nt")
    g["w_glu"] = _mm("mm_dw_glu", z, d_gate, "tn", out_dtypes=(BF16,))
    (dy_pre,) = _rowwise("glu_bwd_gelu", lambda a, b, y: (((a + b) * _gelu_grad(y),), ()),
                         [dz_direct, dz_gate, y_pre], [], [(S5_WIDTH, F32)])
    du, dbm, dcm, da, dd = _s5_bwd("s5_bwd", proj, dy_pre, s5_states, p["mats"])
    g["s5"] = (dbm, dcm, da, dd)
    dproj = jnp.concatenate([du, dq, df, di, dg], axis=1)
    dxn = _mm("mm_dxn_in", dproj, p["w_in"], "nt", tk=2560)
    g["w_in"] = _mm("mm_dw_in", xn, dproj, "tn", out_dtypes=(BF16,))
    dh_in, dhb_in, dnorm = _rms_bwd("rms_bwd", h, rstd, p["norm"], dxn, dh)
    g["norm"] = dnorm.reshape(-1)
    return dh_in, dhb_in, g


def _odd_fwd(h, p):
    xn, rstd = _rms_fwd("rms_fwd", h, p["norm"])
    qkv = _mm("mm_w_qkv", xn, p["w_qkv"], "nn", tn=1280)
    o = _swa_fwd("swa_fwd", qkv, p["q_gain"], p["k_gain"], p["sinks"], p["slopes"])
    out = _mm("mm_w_out", o, p["w_out"], "nn", epi=_add_res, extras=(h,))
    return out, (h, xn, rstd, qkv, o)


def _shift_up_block(x):
    return jnp.concatenate([x[ATT_BLOCK:], jnp.zeros((ATT_BLOCK, x.shape[1]), x.dtype)], axis=0)


def _odd_bwd(cache, p, dh, dhb):
    h, xn, rstd, qkv, o = cache
    g = {}
    d_o = _mm("mm_dy_out", dhb, p["w_out"], "nt")
    g["w_out"] = _mm("mm_dw_out", o, dhb, "tn", out_dtypes=(BF16,))
    dq, dkc, dkp, dvc, dvp, dsink, dqg, dkg = _swa_bwd("swa_bwd", qkv, p["q_gain"], p["k_gain"], p["sinks"],
                                                       p["slopes"], d_o)
    dk = (dkc.astype(F32) + _shift_up_block(dkp).astype(F32)).astype(BF16)
    dv = (dvc.astype(F32) + _shift_up_block(dvp).astype(F32)).astype(BF16)
    g["sinks"], g["q_gain"], g["k_gain"] = dsink[:, 0], dqg.reshape(-1), dkg.reshape(-1)
    dqkv = jnp.concatenate([dq, dk, dv], axis=1)
    dxn = _mm("mm_dxn_qkv", dqkv, p["w_qkv"], "nt", tk=1280)
    g["w_qkv"] = _mm("mm_dw_qkv", xn, dqkv, "tn", out_dtypes=(BF16,), tn=1280)
    dh_in, dhb_in, dnorm = _rms_bwd("rms_bwd", h, rstd, p["norm"], dxn, dh)
    g["norm"] = dnorm.reshape(-1)
    return dh_in, dhb_in, g


def kernel(x, even_norm, even_w_in, s5_lambda_re, s5_lambda_im, s5_log_dt, s5_b_re, s5_b_im, s5_c_re, s5_c_im, s5_d, s5_w_glu, s5_b_glu, hgrn_lower_bound, hgrn_o_norm, even_w_out, odd_norm, odd_w_qkv, q_norm, k_norm, att_sinks, odd_w_out, mlp_norm, mlp_w_up, mlp_w_down, loss_target, m_even_norm, m_even_w_in, m_s5_lambda_re, m_s5_lambda_im, m_s5_log_dt, m_s5_b_re, m_s5_b_im, m_s5_c_re, m_s5_c_im, m_s5_d, m_s5_w_glu, m_s5_b_glu, m_hgrn_lower_bound, m_hgrn_o_norm, m_even_w_out, m_odd_norm, m_odd_w_qkv, m_q_norm, m_k_norm, m_att_sinks, m_odd_w_out, m_mlp_norm, m_mlp_w_up, m_mlp_w_down, v_even_norm, v_even_w_in, v_s5_lambda_re, v_s5_lambda_im, v_s5_log_dt, v_s5_b_re, v_s5_b_im, v_s5_c_re, v_s5_c_im, v_s5_d, v_s5_w_glu, v_s5_b_glu, v_hgrn_lower_bound, v_hgrn_o_norm, v_even_w_out, v_odd_norm, v_odd_w_qkv, v_q_norm, v_k_norm, v_att_sinks, v_odd_w_out, v_mlp_norm, v_mlp_w_up, v_mlp_w_down):
    a = dict(locals())
    n_rows = x.shape[1]
    xi, yi, ci = _mesh_pos()
    me = 4 * xi + 2 * yi + ci

    shard = jnp.concatenate([a[n].astype(BF16).reshape(-1, _PACK_COLS) for n in _BIG_NAMES], axis=0)
    gathered = _allgather("gather_weights", [shard] + [mlp_w_up[l].astype(BF16) for l in range(DEPTH)]
                          + [mlp_w_down[l].astype(BF16) for l in range(DEPTH)])
    w = _unpack_gathered(gathered[0])
    w_up_g, w_down_g = gathered[1:1 + DEPTH], gathered[1 + DEPTH:]
    (odd_gathered,) = _allgather("gather_odd_norm", [jnp.pad(odd_norm, ((0, 6), (0, 0)))])
    odd_norm_full = odd_gathered[:, :2].transpose(1, 0, 2).reshape(2, D_MODEL)

    lower_bounds, lb_vjp = jax.vjp(_hgrn_lower_bounds, hgrn_lower_bound)
    slopes = _alibi_slopes()
    even_p, odd_p, s5_vjps = [], [], []
    for j in range(2):
        disc, vjp = jax.vjp(_s5_discretize, s5_lambda_re[j], s5_lambda_im[j], s5_log_dt[j], s5_b_re[j], s5_b_im[j])
        s5_vjps.append(vjp)
        even_p.append(dict(norm=even_norm[j], w_in=w["even_w_in"][j], w_glu=w["s5_w_glu"][j], b_glu=s5_b_glu[j],
                           mats=_s5_matrices(*disc, s5_c_re[j], s5_c_im[j], s5_d[j]),
                           lb=lower_bounds[j], o_gain=hgrn_o_norm[j], w_out=w["even_w_out"][j]))
        odd_p.append(dict(norm=odd_norm_full[j], w_qkv=w["odd_w_qkv"][j], q_gain=q_norm[j], k_gain=k_norm[j],
                          sinks=att_sinks[j], slopes=slopes, w_out=w["odd_w_out"][j]))

    h = x.reshape(n_rows, D_MODEL)
    caches = []
    for layer in range(DEPTH):
        j = layer // 2
        h, c_mix = (_even_fwd(h, even_p[j]) if layer % 2 == 0 else _odd_fwd(h, odd_p[j]))
        h, c_mlp = _mlp_fwd(h, mlp_norm[layer], w_up_g[layer], w_down_g[layer])
        caches.append((c_mix, c_mlp))
    dh, dhb, sq = _loss_head(h, loss_target.reshape(n_rows, D_MODEL))
    loss = lax.psum(0.5 * sq[0, 0] / D_MODEL, ("x", "y", "c"))

    mix_g, mlp_g = [None] * DEPTH, [None] * DEPTH
    for layer in reversed(range(DEPTH)):
        j = layer // 2
        c_mix, c_mlp = caches[layer]
        dh, dhb, d_mlp_norm, dw_up, dw_down = _mlp_bwd(c_mlp, dh, dhb)
        mlp_g[layer] = (d_mlp_norm.reshape(-1), dw_up, dw_down)
        if layer % 2 == 0:
            dh, dhb, mix_g[layer] = _even_bwd(c_mix, even_p[j], dh, dhb)
        else:
            dh, dhb, mix_g[layer] = _odd_bwd(c_mix, odd_p[j], dh, dhb)
    grad_x = dh.reshape(x.shape)

    ev, od = [mix_g[0], mix_g[2]], [mix_g[1], mix_g[3]]
    full = {
        "even_w_in": jnp.stack([g["w_in"] for g in ev]), "s5_w_glu": jnp.stack([g["w_glu"] for g in ev]),
        "even_w_out": jnp.stack([g["w_out"] for g in ev]), "odd_w_qkv": jnp.stack([g["w_qkv"] for g in od]),
        "odd_w_out": jnp.stack([g["w_out"] for g in od]),
    }
    by_owner = [_pack_by_owner(full)] + [g[1] for g in mlp_g] + [g[2] for g in mlp_g]
    by_chip = [t.reshape((4, 2) + t.shape[1:]) for t in by_owner]
    got = _swap_with_sibling("swap_big_grads", by_chip)
    core = ci.astype(jnp.int32).reshape(1)
    chip_sums = [_add_pair("add_sibling_grads", m, g, core) for m, g in zip(by_chip, got)]
    sums = [_sum_blocks("sum_big_grads", r) for r in _exchange_chips("exchange_big_grads", chip_sums)]
    big_flat = sums[0]
    grads, off = {"mlp_w_up": jnp.stack(sums[1:1 + DEPTH]), "mlp_w_down": jnp.stack(sums[1 + DEPTH:])}, 0
    for name, axis, nl, rows, cols in _BIG:
        n = _big_rows(nl, rows, cols)
        grads[name] = big_flat[off:off + n].reshape(a[name].shape)
        off += n

    s5_g = []
    for j in range(2):
        dar, dai, dbbr, dbbi, dcr, dci, dd = _s5_unpack_grads(*ev[j]["s5"])
        s5_g.append(tuple(s5_vjps[j]((dar, dai, dbbr, dbbi))) + (dcr, dci, dd))
    (d_lb_param,) = lb_vjp(jnp.stack([g["lb"] for g in ev]))
    small = {
        "even_norm": jnp.stack([g["norm"] for g in ev]),
        "s5_lambda_re": jnp.stack([g[0] for g in s5_g]), "s5_lambda_im": jnp.stack([g[1] for g in s5_g]),
        "s5_log_dt": jnp.stack([g[2] for g in s5_g]), "s5_b_re": jnp.stack([g[3] for g in s5_g]),
        "s5_b_im": jnp.stack([g[4] for g in s5_g]), "s5_c_re": jnp.stack([g[5] for g in s5_g]),
        "s5_c_im": jnp.stack([g[6] for g in s5_g]), "s5_d": jnp.stack([g[7] for g in s5_g]),
        "s5_b_glu": jnp.stack([g["b_glu"] for g in ev]), "hgrn_lower_bound": d_lb_param,
        "hgrn_o_norm": jnp.stack([g["o_gain"] for g in ev]), "odd_norm": jnp.stack([g["norm"] for g in od]),
        "q_norm": jnp.stack([g["q_gain"] for g in od]), "k_norm": jnp.stack([g["k_gain"] for g in od]),
        "att_sinks": jnp.stack([g["sinks"] for g in od]), "mlp_norm": jnp.stack([g[0] for g in mlp_g]),
    }
    small_shapes = [small[n].shape for n in _SMALL_NAMES]
    (small_all,) = _allgather("gather_small_grads", [_pack_small([small[n] for n in _SMALL_NAMES])])
    small_sum = _sum_blocks("sum_small_grads", small_all)
    for n, g in zip(_SMALL_NAMES, _unpack_small(small_sum, small_shapes)):
        grads[n] = g
    grads["odd_norm"] = lax.dynamic_slice_in_dim(grads["odd_norm"], me * (D_MODEL // N_DEV), D_MODEL // N_DEV, axis=1)

    delta, new_m, new_v = {}, {}, {}
    for name in _BIG_NAMES + _MLP_NAMES:
        to2d = lambda t, c=a[name].shape[-1]: t.reshape(-1, c)
        d_, m_, v_ = _adamw("adamw_" + name, to2d(a[name]), to2d(grads[name]), to2d(a["m_" + name]), to2d(a["v_" + name]))
        delta[name], new_m[name], new_v[name] = (t.reshape(a[name].shape) for t in (d_, m_, v_))
    packed = [_pack_small([src[n] for n in _SMALL_NAMES])
              for src in (a, grads, {n: a["m_" + n] for n in _SMALL_NAMES}, {n: a["v_" + n] for n in _SMALL_NAMES})]
    shapes = [a[n].shape for n in _SMALL_NAMES]
    for dst, flat in zip((delta, new_m, new_v), _adamw("adamw_small", *packed)):
        for n, t in zip(_SMALL_NAMES, _unpack_small(flat, shapes)):
            dst[n] = t

    return (loss, grad_x, *[grads[n] for n in _WEIGHTS], *[delta[n] for n in _WEIGHTS],
            *[new_m[n] for n in _WEIGHTS], *[new_v[n] for n in _WEIGHTS])
```

```python
import math

import jax
import jax.numpy as jnp
from jax import lax
from jax.experimental import pallas as pl
from jax.experimental.pallas import tpu as pltpu

F32 = jnp.float32
BF16 = jnp.bfloat16
MESH = pl.DeviceIdType.MESH

D_MODEL = 2048
DEPTH = 4
EPS = 1e-6
S5_WIDTH = 1024
S5_GROUPS = 64
S5_STATE = 64
S5_GROUP_SIZE = 16
S5_MIN_DECAY = 1e-4
S5_CHUNK = 128
S5_LEVELS = 7
HGRN_WIDTH = 1024
HGRN_HEADS = 8
HGRN_DIM = 128
HGRN_SUB = 16
HGRN_BLOCK = 128
ATT_HEADS = 32
ATT_KV = 4
ATT_DIM = 64
ATT_BLOCK = 128
QKV_WIDTH = (ATT_HEADS + 2 * ATT_KV) * ATT_DIM
D_FF = 4 * D_MODEL
N_DEV = 8
NEG = -1e30
VMEM_LIMIT = 56 * 1024 * 1024

ADAM_LR, ADAM_B1, ADAM_B2, ADAM_EPS, ADAM_WD, ADAM_STEP = 0.001, 0.9, 0.999, 1e-08, 0.01, 10


def _params(sem=None):
    return pltpu.CompilerParams(dimension_semantics=sem, vmem_limit_bytes=VMEM_LIMIT)


def _sds(shape, dtype):
    return jax.ShapeDtypeStruct(shape, dtype)


def _mm(name, a, b, mode, out_dtypes=(F32,), epi=None, extras=(), tm=512, tn=1024, tk=2048,
        mkn=None, b_block=None, o_block=None, side=None):
    if mkn is not None:
        m, k, n = mkn
    elif mode == "nn":
        (m, k), n = a.shape, b.shape[1]
    elif mode == "nt":
        (m, k), n = a.shape, b.shape[0]
    else:
        (k, m), n = a.shape, b.shape[1]
    tm, tn, tk = min(tm, m), min(tn, n), min(tk, k)
    assert m % tm == 0 and n % tn == 0 and k % tk == 0, (name, m, n, k)
    nk = k // tk
    if mode == "nn":
        a_spec = pl.BlockSpec((tm, tk), lambda i, j, kk: (i, kk))
        b_spec = pl.BlockSpec((tk, tn), lambda i, j, kk: (kk, j))
        dims = (((1,), (0,)), ((), ()))
    elif mode == "nt":
        a_spec = pl.BlockSpec((tm, tk), lambda i, j, kk: (i, kk))
        b_spec = pl.BlockSpec((tn, tk), lambda i, j, kk: (j, kk))
        dims = (((1,), (1,)), ((), ()))
    else:
        a_spec = pl.BlockSpec((tk, tm), lambda i, j, kk: (kk, i))
        b_spec = pl.BlockSpec((tk, tn), lambda i, j, kk: (kk, j))
        dims = (((0,), (0,)), ((), ()))
    o_spec = pl.BlockSpec((tm, tn), lambda i, j, kk: (i, j))
    if b_block is not None:
        b_spec = b_block
    out_specs = [o_spec] * len(out_dtypes)
    out_shape = [_sds((m, n), dt) for dt in out_dtypes]
    if o_block is not None:
        assert len(out_dtypes) == 1 and not extras
        out_specs, out_shape = [o_block[0]], [_sds(o_block[1], out_dtypes[0])]
    n_ex, n_out = len(extras), len(out_dtypes)
    n_sin = len(side.operands) if side is not None else 0
    n_sout = len(side.out_shapes) if side is not None else 0
    grid = (m // tm, n // tn, nk)

    def body(*refs):
        a_ref, b_ref = refs[0], refs[1]
        pos = 2
        ex_refs = refs[pos:pos + n_ex]
        pos += n_ex
        sin_refs = refs[pos:pos + n_sin]
        pos += n_sin
        out_refs = refs[pos:pos + n_out]
        pos += n_out
        sout_refs = refs[pos:pos + n_sout]
        pos += n_sout
        acc_ref = refs[pos] if nk > 1 else None
        sem_refs = refs[pos + (1 if nk > 1 else 0):]
        if side is not None:
            ids = [pl.program_id(d) for d in range(3)]

            @pl.when((ids[0] == 0) & (ids[1] == 0) & (ids[2] == 0))
            def _():
                side.start(sin_refs, sout_refs, sem_refs)

        av, bv = a_ref[...], b_ref[...]
        if av.dtype != BF16:
            av = av.astype(BF16)
        if bv.dtype != BF16:
            bv = bv.astype(BF16)
        part = lax.dot_general(av, bv, dims, preferred_element_type=F32)

        def finish(acc):
            outs = epi(acc, *[r[...] for r in ex_refs]) if epi is not None else (acc,)
            for r, o in zip(out_refs, outs):
                r[...] = o.astype(r.dtype)

        if nk == 1:
            finish(part)
        else:
            kk = pl.program_id(2)

            @pl.when(kk == 0)
            def _():
                acc_ref[...] = part

            @pl.when(kk > 0)
            def _():
                acc_ref[...] += part

            @pl.when(kk == nk - 1)
            def _():
                finish(acc_ref[...])

        if side is not None:
            @pl.when((ids[0] == grid[0] - 1) & (ids[1] == grid[1] - 1) & (ids[2] == grid[2] - 1))
            def _():
                side.finish(sin_refs, sout_refs, sem_refs)

    if side is None:
        outs = pl.pallas_call(
            body, name=name,
            grid=grid,
            in_specs=[a_spec, b_spec] + [o_spec] * n_ex,
            out_specs=out_specs,
            out_shape=out_shape,
            scratch_shapes=[pltpu.VMEM((tm, tn), F32)] if nk > 1 else [],
            compiler_params=_params(("parallel", "parallel", "arbitrary")),
        )(a, b, *extras)
        return outs[0] if n_out == 1 else outs
    outs = pl.pallas_call(
        body, name=name,
        grid=grid,
        in_specs=[a_spec, b_spec] + [o_spec] * n_ex + [_ANY] * n_sin,
        out_specs=out_specs + [_ANY] * n_sout,
        out_shape=out_shape + side.out_shapes,
        scratch_shapes=([pltpu.VMEM((tm, tn), F32)] if nk > 1 else []) + side.scratch,
        compiler_params=_params(("arbitrary", "arbitrary", "arbitrary")),
    )(a, b, *extras, *side.operands)
    main = outs[:n_out]
    return (main[0] if n_out == 1 else main), outs[n_out:]


def _rowwise(name, fn, rows, vecs, outs, accs=(), tr=256):
    n_rows = rows[0].shape[0]
    tr = min(tr, n_rows)
    assert n_rows % tr == 0
    n_r, n_v, n_o, n_a = len(rows), len(vecs), len(outs), len(accs)

    def body(*refs):
        ins = [r[...] for r in refs[:n_r + n_v]]
        o_refs = refs[n_r + n_v:n_r + n_v + n_o]
        a_refs = refs[n_r + n_v + n_o:]
        ro, ao = fn(*ins)
        for r, o in zip(o_refs, ro):
            r[...] = o.astype(r.dtype)
        if n_a:
            step = pl.program_id(0)

            @pl.when(step == 0)
            def _():
                for r, o in zip(a_refs, ao):
                    r[...] = o

            @pl.when(step > 0)
            def _():
                for r, o in zip(a_refs, ao):
                    r[...] += o

    res = pl.pallas_call(
        body, name=name,
        grid=(n_rows // tr,),
        in_specs=[pl.BlockSpec((tr, r.shape[1]), lambda i: (i, 0)) for r in rows]
        + [pl.BlockSpec(v.shape, lambda i: (0, 0)) for v in vecs],
        out_specs=[pl.BlockSpec((tr, w), lambda i: (i, 0)) for w, _ in outs]
        + [pl.BlockSpec((1, w), lambda i: (0, 0)) for w in accs],
        out_shape=[_sds((n_rows, w), dt) for w, dt in outs] + [_sds((1, w), F32) for w in accs],
        compiler_params=_params(("arbitrary",)),
    )(*rows, *vecs)
    return res


def _colsum(x):
    return jnp.sum(x, axis=0, keepdims=True)


def _sigmoid(x):
    return 1.0 / (1.0 + jnp.exp(-x))


_GELU_C = math.sqrt(2.0 / math.pi)


def _gelu(y):
    return 0.5 * y * (1.0 + jnp.tanh(_GELU_C * (y + 0.044715 * y * y * y)))


def _gelu_grad(y):
    t = jnp.tanh(_GELU_C * (y + 0.044715 * y * y * y))
    return 0.5 * (1.0 + t) + 0.5 * y * (1.0 - t * t) * _GELU_C * (1.0 + 3.0 * 0.044715 * y * y)


def _rms_fwd(name, h, gain):
    def fn(x, g):
        r = lax.rsqrt(jnp.mean(x * x, axis=1, keepdims=True) + EPS)
        return (x * r * g, r), ()
    return _rowwise(name, fn, [h], [gain.reshape(1, -1)], [(h.shape[1], BF16), (1, F32)])


def _rms_bwd(name, h, rstd, gain, dxn, dres):
    def fn(x, r, dy, dr, g):
        xh = x * r
        gdy = dy * g
        dx = r * (gdy - xh * jnp.mean(gdy * xh, axis=1, keepdims=True)) + dr
        return (dx, dx), (_colsum(dy * xh),)
    w = h.shape[1]
    return _rowwise(name, fn, [h, rstd, dxn, dres], [gain.reshape(1, -1)], [(w, F32), (w, BF16)], [w])


def _loss_head(h, target):
    w = h.shape[1]

    def fn(x, t):
        e = x - t
        return (e * (1.0 / w), e * (1.0 / w)), (jnp.zeros((1, 128), F32) + jnp.sum(e * e),)
    return _rowwise("loss_head", fn, [h, target], [], [(w, F32), (w, BF16)], [128])


def _adamw(name, w, g, m, v):
    c1 = 1.0 - ADAM_B1 ** ADAM_STEP
    c2 = 1.0 - ADAM_B2 ** ADAM_STEP

    def fn(w_, g_, m_, v_):
        mn = ADAM_B1 * m_ + (1.0 - ADAM_B1) * g_
        vn = ADAM_B2 * v_ + (1.0 - ADAM_B2) * (g_ * g_)
        delta = -ADAM_LR * ((mn / c1) / (jnp.sqrt(vn / c2) + ADAM_EPS) + ADAM_WD * w_)
        return (delta, mn, vn), ()
    c = w.shape[1]
    return _rowwise(name, fn, [w, g, m, v], [], [(c, F32)] * 3)


def _s5_discretize(lam_re, lam_im, log_dt, b_re, b_im):
    lr = jnp.minimum(lam_re, -S5_MIN_DECAY)
    li = lam_im
    dt = jnp.exp(log_dt)[:, None]
    mag = jnp.exp(lr * dt)
    ar = mag * jnp.cos(li * dt)
    ai = mag * jnp.sin(li * dt)
    den = lr * lr + li * li
    zr = ((ar - 1.0) * lr + ai * li) / den
    zi = (ai * lr - (ar - 1.0) * li) / den
    bbr = zr[..., None] * b_re - zi[..., None] * b_im
    bbi = zr[..., None] * b_im + zi[..., None] * b_re
    return ar, ai, bbr, bbi


def _s5_matrices(ar, ai, bbr, bbi, c_re, c_im, d_skip):
    eye = jnp.eye(8, dtype=F32)
    bt = jnp.stack([bbr, bbi], axis=1).transpose(0, 3, 1, 2)
    bt = bt.reshape(8, 8, 16, 1, 2, 64) * eye[None, :, None, :, None, None]
    bm8 = bt.reshape(8, 8, 16, 4, 2, 2, 64).transpose(0, 1, 2, 3, 5, 4, 6).reshape(8, 128, 1024)
    ct = jnp.stack([c_re, -c_im], axis=1).transpose(0, 1, 3, 2)
    ct = ct.reshape(8, 8, 2, 64, 1, 16) * eye[None, :, None, None, :, None]
    cm8 = ct.reshape(8, 4, 2, 2, 64, 8, 16).transpose(0, 1, 3, 2, 4, 5, 6).reshape(8, 1024, 128)
    prs, pis = [], []
    pr, pi = ar, ai
    for _ in range(S5_LEVELS):
        prs.append(pr.reshape(8, 512))
        pis.append(pi.reshape(8, 512))
        pr, pi = pr * pr - pi * pi, 2.0 * pr * pi
    prs.append(jnp.zeros_like(prs[0]))
    pis.append(jnp.zeros_like(pis[0]))
    return (bm8.astype(BF16), cm8.astype(BF16), jnp.stack(prs, axis=1), jnp.stack(pis, axis=1),
            d_skip.reshape(8, 1, 128))


def _s5_unpack_grads(dbm8, dcm8, da, dd):
    db = dbm8.reshape(8, 8, 16, 4, 2, 2, 64).transpose(0, 1, 2, 3, 5, 4, 6).reshape(8, 8, 16, 8, 2, 64)
    db = jnp.einsum("agcgqp->agcqp", db).reshape(S5_GROUPS, 16, 2, 64)
    dc = dcm8.reshape(8, 4, 2, 2, 64, 8, 16).transpose(0, 1, 3, 2, 4, 5, 6).reshape(8, 8, 2, 64, 8, 16)
    dc = jnp.einsum("agqpgc->agqpc", dc).reshape(S5_GROUPS, 2, 64, 16)
    dar = da[:, 0, :].reshape(S5_GROUPS, 64)
    dai = da[:, 1, :].reshape(S5_GROUPS, 64)
    return (dar, dai, db[:, :, 0, :].transpose(0, 2, 1), db[:, :, 1, :].transpose(0, 2, 1),
            dc[:, 0].transpose(0, 2, 1), -dc[:, 1].transpose(0, 2, 1), dd.reshape(S5_GROUPS, 16))


def _shift_rows(x, s, row, down):
    t = x.shape[0]
    if s % 8 == 0:
        z = jnp.zeros((s, x.shape[1]), x.dtype)
        return jnp.concatenate([z, x[:t - s]], axis=0) if down else jnp.concatenate([x[s:], z], axis=0)
    if down:
        return jnp.where(row >= s, pltpu.roll(x, s, 0), 0.0)
    return jnp.where(row < t - s, pltpu.roll(x, t - s, 0), 0.0)


def _s5_scan(xr, xi, pr, pi, cr, ci, row, conj):
    t = xr[0].shape[0]
    sg = -1.0 if conj else 1.0
    edge = (t - 1) if conj else 0
    n = len(xr)
    for k in range(n):
        sl = slice(128 * k, 128 * (k + 1))
        p_r, p_i = pr[0:1, sl], sg * pi[0:1, sl]
        xr[k] = xr[k] + jnp.where(row == edge, p_r * cr[k] - p_i * ci[k], 0.0)
        xi[k] = xi[k] + jnp.where(row == edge, p_r * ci[k] + p_i * cr[k], 0.0)
    for lvl in range(S5_LEVELS):
        s = 1 << lvl
        for k in range(n):
            sl = slice(128 * k, 128 * (k + 1))
            p_r, p_i = pr[lvl:lvl + 1, sl], sg * pi[lvl:lvl + 1, sl]
            sr = _shift_rows(xr[k], s, row, not conj)
            si = _shift_rows(xi[k], s, row, not conj)
            xr[k] = xr[k] + p_r * sr - p_i * si
            xi[k] = xi[k] + p_r * si + p_i * sr
    return xr, xi


def _s5_fwd(name, proj, mats):
    bm8, cm8, p1, p2, d8 = mats
    n_rows = proj.shape[0]
    t = S5_CHUNK
    nch = n_rows // t

    def body(u_ref, bm_ref, cm_ref, pr_ref, pi_ref, d_ref, y_ref, z_ref, st_ref, carry):
        @pl.when(pl.program_id(1) == 0)
        def _():
            carry[...] = jnp.zeros_like(carry)

        cv = carry[...]
        st_ref[...] = cv
        u = u_ref[...]
        bu = jnp.dot(u.astype(BF16), bm_ref[...], preferred_element_type=F32)
        row = lax.broadcasted_iota(jnp.int32, (t, 128), 0)
        tile = lambda v, j: v[:, 128 * j:128 * (j + 1)]
        xr, xi = _s5_scan([tile(bu, 2 * k) for k in range(4)], [tile(bu, 2 * k + 1) for k in range(4)],
                          pr_ref[...], pi_ref[...], [tile(cv, 2 * k)[0:1] for k in range(4)],
                          [tile(cv, 2 * k + 1)[0:1] for k in range(4)], row, False)
        xall = jnp.concatenate([v for k in range(4) for v in (xr[k], xi[k])], axis=1)
        carry[...] = jnp.broadcast_to(xall[t - 1:t, :], (8, 1024))
        y = jnp.dot(xall.astype(BF16), cm_ref[...], preferred_element_type=F32) + d_ref[...] * u
        y_ref[...] = y
        z_ref[...] = _gelu(y).astype(BF16)

    return pl.pallas_call(
        body, name=name,
        grid=(8, nch),
        in_specs=[
            pl.BlockSpec((t, 128), lambda g, c: (c, g)),
            pl.BlockSpec((None, 128, 1024), lambda g, c: (g, 0, 0)),
            pl.BlockSpec((None, 1024, 128), lambda g, c: (g, 0, 0)),
            pl.BlockSpec((None, 8, 512), lambda g, c: (g, 0, 0)),
            pl.BlockSpec((None, 8, 512), lambda g, c: (g, 0, 0)),
            pl.BlockSpec((None, 1, 128), lambda g, c: (g, 0, 0)),
        ],
        out_specs=[
            pl.BlockSpec((t, 128), lambda g, c: (c, g)),
            pl.BlockSpec((t, 128), lambda g, c: (c, g)),
            pl.BlockSpec((None, None, 8, 1024), lambda g, c: (g, c, 0, 0)),
        ],
        out_shape=[_sds((n_rows, S5_WIDTH), F32), _sds((n_rows, S5_WIDTH), BF16), _sds((8, nch, 8, 1024), F32)],
        scratch_shapes=[pltpu.VMEM((8, 1024), F32)],
        compiler_params=_params(("parallel", "arbitrary")),
    )(proj, bm8, cm8, p1, p2, d8)


def _s5_bwd(name, proj, dy, states, mats):
    bm8, cm8, p1, p2, d8 = mats
    n_rows = proj.shape[0]
    t = S5_CHUNK
    nch = n_rows // t
    nt_dims = (((1,), (1,)), ((), ()))
    tn_dims = (((0,), (0,)), ((), ()))

    def body(u_ref, dy_ref, st_ref, bm_ref, cm_ref, pr_ref, pi_ref, d_ref,
             du_ref, dbm_ref, dcm_ref, da_ref, dd_ref, gcarry):
        @pl.when(pl.program_id(1) == 0)
        def _():
            gcarry[...] = jnp.zeros_like(gcarry)
            dbm_ref[...] = jnp.zeros_like(dbm_ref)
            dcm_ref[...] = jnp.zeros_like(dcm_ref)
            da_ref[...] = jnp.zeros_like(da_ref)
            dd_ref[...] = jnp.zeros_like(dd_ref)

        u = u_ref[...]
        dyv = dy_ref[...]
        ub, dyb = u.astype(BF16), dyv.astype(BF16)
        bu = jnp.dot(ub, bm_ref[...], preferred_element_type=F32)
        dxd = lax.dot_general(dyb, cm_ref[...], nt_dims, preferred_element_type=F32)
        row = lax.broadcasted_iota(jnp.int32, (t, 128), 0)
        tile = lambda v, j: v[:, 128 * j:128 * (j + 1)]
        prv, piv, cv, gv = pr_ref[...], pi_ref[...], st_ref[...], gcarry[...]
        cr = [tile(cv, 2 * k)[0:1] for k in range(4)]
        ci = [tile(cv, 2 * k + 1)[0:1] for k in range(4)]
        xr, xi = _s5_scan([tile(bu, 2 * k) for k in range(4)], [tile(bu, 2 * k + 1) for k in range(4)],
                          prv, piv, cr, ci, row, False)
        gr, gi = _s5_scan([tile(dxd, 2 * k) for k in range(4)], [tile(dxd, 2 * k + 1) for k in range(4)],
                          prv, piv, [tile(gv, 2 * k)[0:1] for k in range(4)],
                          [tile(gv, 2 * k + 1)[0:1] for k in range(4)], row, True)
        dar, dai = [], []
        for k in range(4):
            xpr = jnp.where(row >= 1, pltpu.roll(xr[k], 1, 0), cr[k])
            xpi = jnp.where(row >= 1, pltpu.roll(xi[k], 1, 0), ci[k])
            dar.append(_colsum(gr[k] * xpr + gi[k] * xpi))
            dai.append(_colsum(gi[k] * xpr - gr[k] * xpi))
        xall = jnp.concatenate([v for k in range(4) for v in (xr[k], xi[k])], axis=1).astype(BF16)
        gf = jnp.concatenate([v for k in range(4) for v in (gr[k], gi[k])], axis=1)
        gcarry[...] = jnp.broadcast_to(gf[0:1, :], (8, 1024))
        gall = gf.astype(BF16)
        dcm_ref[...] += lax.dot_general(xall, dyb, tn_dims, preferred_element_type=F32)
        dbm_ref[...] += lax.dot_general(ub, gall, tn_dims, preferred_element_type=F32)
        du = lax.dot_general(gall, bm_ref[...], nt_dims, preferred_element_type=F32) + d_ref[...] * dyv
        du_ref[...] = du.astype(BF16)
        dd_ref[...] += _colsum(dyv * u)
        da_ref[0:1, :] += jnp.concatenate(dar, axis=1)
        da_ref[1:2, :] += jnp.concatenate(dai, axis=1)

    rev = lambda g, c: (nch - 1 - c, g)
    return pl.pallas_call(
        body, name=name,
        grid=(8, nch),
        in_specs=[
            pl.BlockSpec((t, 128), rev),
            pl.BlockSpec((t, 128), rev),
            pl.BlockSpec((None, None, 8, 1024), lambda g, c: (g, nch - 1 - c, 0, 0)),
            pl.BlockSpec((None, 128, 1024), lambda g, c: (g, 0, 0)),
            pl.BlockSpec((None, 1024, 128), lambda g, c: (g, 0, 0)),
            pl.BlockSpec((None, 8, 512), lambda g, c: (g, 0, 0)),
            pl.BlockSpec((None, 8, 512), lambda g, c: (g, 0, 0)),
            pl.BlockSpec((None, 1, 128), lambda g, c: (g, 0, 0)),
        ],
        out_specs=[
            pl.BlockSpec((t, 128), rev),
            pl.BlockSpec((None, 128, 1024), lambda g, c: (g, 0, 0)),
            pl.BlockSpec((None, 1024, 128), lambda g, c: (g, 0, 0)),
            pl.BlockSpec((None, 8, 512), lambda g, c: (g, 0, 0)),
            pl.BlockSpec((None, 1, 128), lambda g, c: (g, 0, 0)),
        ],
        out_shape=[_sds((n_rows, S5_WIDTH), BF16), _sds((8, 128, 1024), F32), _sds((8, 1024, 128), F32),
                   _sds((8, 8, 512), F32), _sds((8, 1, 128), F32)],
        scratch_shapes=[pltpu.VMEM((8, 1024), F32)],
        compiler_params=_params(("parallel", "arbitrary")),
    )(proj, dy, states, bm8, cm8, p1, p2, d8)


def _hgrn_lower_bounds(lb_param):
    p = jax.nn.softmax(lb_param, axis=0)
    return jnp.cumsum(p, axis=0) - p[0:1]


def _prefix16(x, r16):
    for s in (1, 2, 4, 8):
        x = x + jnp.where(r16 >= s, pltpu.roll(x, s, 0), 0.0)
    return x


def _suffix16(x, r16):
    n = x.shape[0]
    for s in (1, 2, 4, 8):
        x = x + jnp.where(r16 < HGRN_SUB - s, pltpu.roll(x, n - s, 0), 0.0)
    return x


_NT = (((1,), (1,)), ((), ()))
_TN = (((0,), (0,)), ((), ()))


def _dotf(a, b, dims=(((1,), (0,)), ((), ()))):
    return lax.dot_general(a.astype(BF16), b.astype(BF16), dims, preferred_element_type=F32)


def _hgrn_specs(n_blocks, rev):
    r = HGRN_BLOCK
    blk = (lambda b: n_blocks - 1 - b) if rev else (lambda b: b)
    proj_specs = [pl.BlockSpec((r, 128), (lambda h, b, c=c: (blk(b), 8 * c + h))) for c in (1, 2, 3, 4)]
    lb_spec = pl.BlockSpec((None, 1, 128), lambda h, b: (h, 0, 0))
    gain_spec = pl.BlockSpec((1, 128), lambda h, b: (0, 0))
    row_spec = pl.BlockSpec((r, 128), lambda h, b: (blk(b), h))
    st_spec = pl.BlockSpec((None, None, 128, 128), lambda h, b: (h, blk(b), 0, 0))
    return proj_specs, lb_spec, gain_spec, row_spec, st_spec, blk


def _hgrn_fwd(name, proj, lb, gain):
    n_rows = proj.shape[0]
    r = HGRN_BLOCK
    nb = n_rows // r
    nsub = r // HGRN_SUB
    proj_specs, lb_spec, gain_spec, row_spec, st_spec, _ = _hgrn_specs(nb, False)

    def body(q_ref, f_ref, i_ref, g_ref, lb_ref, gain_ref, o_ref, y_ref, st_ref, st_scr):
        @pl.when(pl.program_id(1) == 0)
        def _():
            st_scr[...] = jnp.zeros_like(st_scr)

        st_ref[...] = st_scr[...]
        q, f, v, g = q_ref[...], f_ref[...], i_ref[...], g_ref[...]
        lbv = lb_ref[...]
        qs = q * _sigmoid(q)
        fg = lbv + (1.0 - lbv) * _sigmoid(f)
        kk = 1.0 - fg
        r16 = lax.broadcasted_iota(jnp.int32, (r, 128), 0) & (HGRN_SUB - 1)
        b = _prefix16(jnp.log(fg), r16)
        qh = qs * jnp.exp(b)
        rs = lax.broadcasted_iota(jnp.int32, (HGRN_SUB, 128), 0)
        st = st_scr[...]
        outs = []
        for i in range(nsub):
            sl = slice(HGRN_SUB * i, HGRN_SUB * (i + 1))
            qsi, kki, vi, bi = qs[sl], kk[sl], v[sl], b[sl]
            o_i = _dotf(qh[sl], st, _NT)
            for s in range(HGRN_SUB):
                e = jnp.exp(jnp.where(rs >= s, bi - bi[s:s + 1], NEG))
                col = jnp.sum(qsi * e * kki[s:s + 1], axis=1, keepdims=True)
                o_i = o_i + col * vi[s:s + 1]
            bl = bi[HGRN_SUB - 1:HGRN_SUB]
            st = st * jnp.exp(bl) + _dotf(vi, kki * jnp.exp(bl - bi), _TN)
            outs.append(o_i)
        st_scr[...] = st
        o = jnp.concatenate(outs, axis=0)
        o_ref[...] = o
        rn = lax.rsqrt(jnp.mean(o * o, axis=1, keepdims=True) + EPS)
        y_ref[...] = (o * rn * gain_ref[...] * (g * _sigmoid(g))).astype(BF16)

    return pl.pallas_call(
        body, name=name,
        grid=(HGRN_HEADS, nb),
        in_specs=proj_specs + [lb_spec, gain_spec],
        out_specs=[row_spec, row_spec, st_spec],
        out_shape=[_sds((n_rows, HGRN_WIDTH), F32), _sds((n_rows, HGRN_WIDTH), BF16),
                   _sds((HGRN_HEADS, nb, 128, 128), F32)],
        scratch_shapes=[pltpu.VMEM((128, 128), F32)],
        compiler_params=_params(("parallel", "arbitrary")),
    )(proj, proj, proj, proj, lb, gain)


def _hgrn_bwd(name, proj, lb, gain, o_saved, states, dycat):
    n_rows = proj.shape[0]
    r = HGRN_BLOCK
    nb = n_rows // r
    nsub = r // HGRN_SUB
    proj_specs, lb_spec, gain_spec, row_spec, st_spec, blk = _hgrn_specs(nb, True)
    dy_spec = pl.BlockSpec((r, 128), lambda h, b: (blk(b), 8 + h))
    acc_spec = pl.BlockSpec((None, 1, 128), lambda h, b: (h, 0, 0))

    def body(q_ref, f_ref, i_ref, g_ref, lb_ref, gain_ref, o_ref, st_ref, dy_ref,
             dq_ref, df_ref, di_ref, dg_ref, dlb_ref, dgain_ref, dst_scr, sub_scr):
        @pl.when(pl.program_id(1) == 0)
        def _():
            dst_scr[...] = jnp.zeros_like(dst_scr)
            dlb_ref[...] = jnp.zeros_like(dlb_ref)
            dgain_ref[...] = jnp.zeros_like(dgain_ref)

        q, f, v, g = q_ref[...], f_ref[...], i_ref[...], g_ref[...]
        lbv, gain_v = lb_ref[...], gain_ref[...]
        sq = _sigmoid(q)
        qs = q * sq
        sf = _sigmoid(f)
        fg = lbv + (1.0 - lbv) * sf
        kk = 1.0 - fg
        r16 = lax.broadcasted_iota(jnp.int32, (r, 128), 0) & (HGRN_SUB - 1)
        b = _prefix16(jnp.log(fg), r16)
        eb = jnp.exp(b)
        qh = qs * eb

        o, dy = o_ref[...], dy_ref[...]
        rn = lax.rsqrt(jnp.mean(o * o, axis=1, keepdims=True) + EPS)
        on = o * rn
        sg = _sigmoid(g)
        sil = g * sg
        dgain_ref[...] += _colsum(dy * on * sil)
        dg_ref[...] = (dy * on * gain_v * (sg * (1.0 + g * (1.0 - sg)))).astype(BF16)
        don = dy * gain_v * sil
        do = rn * (don - on * jnp.mean(don * on, axis=1, keepdims=True))

        st = st_ref[...]
        for i in range(nsub):
            sl = slice(HGRN_SUB * i, HGRN_SUB * (i + 1))
            sub_scr[i] = st
            bi = b[sl]
            bl = bi[HGRN_SUB - 1:HGRN_SUB]
            st = st * jnp.exp(bl) + _dotf(v[sl], kk[sl] * jnp.exp(bl - bi), _TN)

        rs = lax.broadcasted_iota(jnp.int32, (HGRN_SUB, 128), 0)
        dst = dst_scr[...]
        parts = [None] * nsub
        for i in reversed(range(nsub)):
            sl = slice(HGRN_SUB * i, HGRN_SUB * (i + 1))
            sp = sub_scr[i]
            qsi, kki, vi, bi, doi, qhi = qs[sl], kk[sl], v[sl], b[sl], do[sl], qh[sl]
            bl = bi[HGRN_SUB - 1:HGRN_SUB]
            ebl = jnp.exp(bl)
            dec = jnp.exp(bl - bi)
            khat = kki * dec
            dqh = _dotf(doi, sp)
            dkhat = _dotf(vi, dst)
            dv = _dotf(khat, dst, _NT)
            zrow = _colsum(sp * dst) * ebl
            dq_in = jnp.zeros((HGRN_SUB, 128), F32)
            dk_in = jnp.zeros((HGRN_SUB, 128), F32)
            dv_in = jnp.zeros((HGRN_SUB, 128), F32)
            for s in range(HGRN_SUB):
                e = jnp.exp(jnp.where(rs >= s, bi - bi[s:s + 1], NEG))
                dpc = jnp.sum(doi * vi[s:s + 1], axis=1, keepdims=True)
                w = qsi * e
                pc = jnp.sum(w * kki[s:s + 1], axis=1, keepdims=True)
                dq_in = dq_in + dpc * e * kki[s:s + 1]
                dk_in = jnp.where(rs == s, _colsum(dpc * w), dk_in)
                dv_in = jnp.where(rs == s, _colsum(pc * doi), dv_in)
            kd = khat * dkhat
            parts[i] = (qsi * dq_in - kki * dk_in + qhi * dqh, kd, jnp.broadcast_to(zrow, (HGRN_SUB, 128)),
                        dq_in + dqh * eb[sl], dk_in + dkhat * dec, dv + dv_in)
            dst = dst * ebl + _dotf(doi, qhi, _TN)
        dst_scr[...] = dst

        cat = lambda j: jnp.concatenate([p[j] for p in parts], axis=0)
        d_b, kd, zr, dqs, dkk, dvv = (cat(j) for j in range(6))
        dlf = _suffix16(d_b, r16) + _prefix16(kd, r16) - kd + zr
        dfg = dlf / fg - dkk
        df_ref[...] = (dfg * (1.0 - lbv) * sf * (1.0 - sf)).astype(BF16)
        dlb_ref[...] += _colsum(dfg * (1.0 - sf))
        dq_ref[...] = (dqs * (sq * (1.0 + q * (1.0 - sq)))).astype(BF16)
        di_ref[...] = dvv.astype(BF16)

    return pl.pallas_call(
        body, name=name,
        grid=(HGRN_HEADS, nb),
        in_specs=proj_specs + [lb_spec, gain_spec, row_spec, st_spec, dy_spec],
        out_specs=[row_spec] * 4 + [acc_spec, acc_spec],
        out_shape=[_sds((n_rows, HGRN_WIDTH), BF16)] * 4 + [_sds((HGRN_HEADS, 1, 128), F32)] * 2,
        scratch_shapes=[pltpu.VMEM((128, 128), F32), pltpu.VMEM((nsub, 128, 128), F32)],
        compiler_params=_params(("parallel", "arbitrary")),
    )(proj, proj, proj, proj, lb, gain, o_saved, states, dycat)


def _alibi_slopes():
    return jnp.exp2(-8.0 * jnp.arange(1, ATT_HEADS + 1, dtype=F32) / ATT_HEADS)


def _swa_specs(n_blocks):
    blk = ATT_BLOCK
    prev = lambda i: jnp.maximum(i - 1, 0)
    smem = pl.BlockSpec(memory_space=pltpu.SMEM)
    return [
        smem, smem,
        pl.BlockSpec((blk, ATT_HEADS * ATT_DIM), lambda i: (i, 0)),
        pl.BlockSpec((blk, 256), lambda i: (i, 8)),
        pl.BlockSpec((blk, 256), lambda i: (prev(i), 8)),
        pl.BlockSpec((blk, 256), lambda i: (i, 9)),
        pl.BlockSpec((blk, 256), lambda i: (prev(i), 9)),
        pl.BlockSpec((1, ATT_DIM), lambda i: (0, 0)),
        pl.BlockSpec((1, ATT_DIM), lambda i: (0, 0)),
    ]


_ATT_GROUP = ATT_HEADS // ATT_KV
_ATT_ROWS = _ATT_GROUP * ATT_BLOCK


def _swa_mask(i):
    t_i = lax.broadcasted_iota(jnp.int32, (_ATT_ROWS, 2 * ATT_BLOCK), 0) & (ATT_BLOCK - 1)
    s_i = lax.broadcasted_iota(jnp.int32, (_ATT_ROWS, 2 * ATT_BLOCK), 1)
    dist = t_i + ATT_BLOCK - s_i
    valid = (dist >= 0) & (dist < ATT_BLOCK) & ((s_i >= ATT_BLOCK) | (i > 0))
    return valid, dist.astype(F32)


def _stack_heads(x):
    return jnp.concatenate([x[:, ATT_DIM * h:ATT_DIM * (h + 1)] for h in range(_ATT_GROUP)], axis=0)


def _unstack_heads(x):
    return jnp.concatenate([x[ATT_BLOCK * h:ATT_BLOCK * (h + 1)] for h in range(_ATT_GROUP)], axis=1)


def _head_column(ref, g):
    return jnp.concatenate([jnp.full((ATT_BLOCK, 1), ref[_ATT_GROUP * g + h], F32) for h in range(_ATT_GROUP)], axis=0)


def _swa_probs(qn, kn, slope, sink, valid, distf):
    s = lax.dot_general(qn, kn, _NT, preferred_element_type=F32) * (ATT_DIM ** -0.5) - slope * distf
    s = jnp.where(valid, s, NEG)
    m = jnp.maximum(jnp.max(s, axis=1, keepdims=True), sink)
    p = jnp.exp(s - m)
    es = jnp.exp(sink - m)
    inv = 1.0 / (jnp.sum(p, axis=1, keepdims=True) + es)
    return p * inv, es * inv


def _swa_fwd(name, qkv, q_gain, k_gain, sinks, slopes):
    n_rows = qkv.shape[0]
    nb = n_rows // ATT_BLOCK

    def body(sink_ref, slope_ref, q_ref, kc_ref, kp_ref, vc_ref, vp_ref, qg_ref, kg_ref, o_ref):
        i = pl.program_id(0)
        kb = jnp.concatenate([kp_ref[...], kc_ref[...]], axis=0)
        vb = jnp.concatenate([vp_ref[...], vc_ref[...]], axis=0)
        valid, distf = _swa_mask(i)
        qgv, kgv = qg_ref[...], kg_ref[...]
        gw = _ATT_GROUP * ATT_DIM
        for g in range(ATT_KV):
            kg = kb[:, 64 * g:64 * (g + 1)]
            rk = lax.rsqrt(jnp.mean(kg * kg, axis=1, keepdims=True) + EPS)
            kn = (kg * rk * kgv).astype(BF16)
            vv = vb[:, 64 * g:64 * (g + 1)].astype(BF16)
            qs = _stack_heads(q_ref[:, gw * g:gw * (g + 1)])
            rq = lax.rsqrt(jnp.mean(qs * qs, axis=1, keepdims=True) + EPS)
            pn, _ = _swa_probs((qs * rq * qgv).astype(BF16), kn, _head_column(slope_ref, g),
                               _head_column(sink_ref, g), valid, distf)
            out = jnp.dot(pn.astype(BF16), vv, preferred_element_type=F32)
            o_ref[:, gw * g:gw * (g + 1)] = _unstack_heads(out).astype(BF16)

    return pl.pallas_call(
        body, name=name,
        grid=(nb,),
        in_specs=_swa_specs(nb),
        out_specs=pl.BlockSpec((ATT_BLOCK, ATT_HEADS * ATT_DIM), lambda i: (i, 0)),
        out_shape=_sds((n_rows, ATT_HEADS * ATT_DIM), BF16),
        compiler_params=_params(("parallel",)),
    )(sinks, slopes, qkv, qkv, qkv, qkv, qkv, q_gain.reshape(1, -1), k_gain.reshape(1, -1))


def _swa_bwd(name, qkv, q_gain, k_gain, sinks, slopes, d_out):
    n_rows = qkv.shape[0]
    nb = n_rows // ATT_BLOCK
    blk = ATT_BLOCK

    def body(sink_ref, slope_ref, q_ref, kc_ref, kp_ref, vc_ref, vp_ref, qg_ref, kg_ref, do_ref,
             dq_ref, dkc_ref, dkp_ref, dvc_ref, dvp_ref, dsink_ref, dqg_ref, dkg_ref):
        i = pl.program_id(0)

        @pl.when(i == 0)
        def _():
            dsink_ref[...] = jnp.zeros_like(dsink_ref)
            dqg_ref[...] = jnp.zeros_like(dqg_ref)
            dkg_ref[...] = jnp.zeros_like(dkg_ref)

        kb = jnp.concatenate([kp_ref[...], kc_ref[...]], axis=0)
        vb = jnp.concatenate([vp_ref[...], vc_ref[...]], axis=0)
        kgv, qgv = kg_ref[...], qg_ref[...]
        valid, distf = _swa_mask(i)
        scale = ATT_DIM ** -0.5
        gw = _ATT_GROUP * ATT_DIM
        dks, dvs = [], []
        dqg, dkg = jnp.zeros((1, ATT_DIM), F32), jnp.zeros((1, ATT_DIM), F32)
        for g in range(ATT_KV):
            kg = kb[:, 64 * g:64 * (g + 1)]
            rk = lax.rsqrt(jnp.mean(kg * kg, axis=1, keepdims=True) + EPS)
            khat = kg * rk
            kn = (khat * kgv).astype(BF16)
            vv = vb[:, 64 * g:64 * (g + 1)].astype(BF16)
            qs = _stack_heads(q_ref[:, gw * g:gw * (g + 1)])
            rq = lax.rsqrt(jnp.mean(qs * qs, axis=1, keepdims=True) + EPS)
            qhat = qs * rq
            qn = (qhat * qgv).astype(BF16)
            pn, ps = _swa_probs(qn, kn, _head_column(slope_ref, g), _head_column(sink_ref, g), valid, distf)
            dos = _stack_heads(do_ref[:, gw * g:gw * (g + 1)]).astype(BF16)
            dp = lax.dot_general(dos, vv, _NT, preferred_element_type=F32)
            delta = jnp.sum(pn * dp, axis=1, keepdims=True)
            ds = (pn * (dp - delta)).astype(BF16)
            sd = ps * delta
            for h in range(_ATT_GROUP):
                hs = _ATT_GROUP * g + h
                dsink_ref[hs:hs + 1, :] += jnp.zeros((1, 128), F32) - jnp.sum(sd[blk * h:blk * (h + 1)])
            dvs.append(lax.dot_general(pn.astype(BF16), dos, _TN, preferred_element_type=F32))
            dkn = lax.dot_general(ds, qn, _TN, preferred_element_type=F32) * scale
            dqn = jnp.dot(ds, kn, preferred_element_type=F32) * scale
            dqg = dqg + _colsum(dqn * qhat)
            dqhat = dqn * qgv
            dqs = rq * (dqhat - qhat * jnp.mean(dqhat * qhat, axis=1, keepdims=True))
            dq_ref[:, gw * g:gw * (g + 1)] = _unstack_heads(dqs).astype(BF16)
            dkg = dkg + _colsum(dkn * khat)
            dkhat = dkn * kgv
            dks.append(rk * (dkhat - khat * jnp.mean(dkhat * khat, axis=1, keepdims=True)))
        dqg_ref[...] += dqg
        dkg_ref[...] += dkg
        dk = jnp.concatenate(dks, axis=1).astype(BF16)
        dv = jnp.concatenate(dvs, axis=1).astype(BF16)
        dkp_ref[...] = dk[:blk]
        dkc_ref[...] = dk[blk:]
        dvp_ref[...] = dv[:blk]
        dvc_ref[...] = dv[blk:]

    kv_spec = pl.BlockSpec((blk, 256), lambda i: (i, 0))
    full = pl.BlockSpec((blk, ATT_HEADS * ATT_DIM), lambda i: (i, 0))
    acc64 = pl.BlockSpec((1, ATT_DIM), lambda i: (0, 0))
    return pl.pallas_call(
        body, name=name,
        grid=(nb,),
        in_specs=_swa_specs(nb) + [full],
        out_specs=[full, kv_spec, kv_spec, kv_spec, kv_spec,
                   pl.BlockSpec((ATT_HEADS, 128), lambda i: (0, 0)), acc64, acc64],
        out_shape=[_sds((n_rows, ATT_HEADS * ATT_DIM), BF16)] + [_sds((n_rows, 256), BF16)] * 4
        + [_sds((ATT_HEADS, 128), F32), _sds((1, ATT_DIM), F32), _sds((1, ATT_DIM), F32)],
        compiler_params=_params(("arbitrary",)),
    )(sinks, slopes, qkv, qkv, qkv, qkv, qkv, q_gain.reshape(1, -1), k_gain.reshape(1, -1), d_out)


def _mesh_pos():
    return lax.axis_index("x"), lax.axis_index("y"), lax.axis_index("c")


_ANY = pl.BlockSpec(memory_space=pl.ANY)


def _allgather(name, shards):
    side = _gather_side(shards)
    n = len(shards)

    def body(*refs):
        side.start(refs[:n], refs[n:2 * n], refs[2 * n:])
        side.finish(refs[:n], refs[n:2 * n], refs[2 * n:])

    return pl.pallas_call(
        body, name=name,
        out_shape=side.out_shapes,
        in_specs=[_ANY] * n,
        out_specs=[_ANY] * n,
        scratch_shapes=side.scratch,
    )(*shards)


class _Side:
    def __init__(self, operands, out_shapes, scratch, start, finish):
        self.operands, self.out_shapes, self.scratch = list(operands), list(out_shapes), list(scratch)
        self.start, self.finish = start, finish


def _gather_side(shards):
    n = len(shards)

    def plan(x_refs, out_refs, sems):
        send_sems, recv_sems, local_sems = sems
        x, y, c = _mesh_pos()
        me, sibling = (x, y, c), (x, y, 1 - c)
        chips = [(1 - x, y), (x, 1 - y), (1 - x, 1 - y)]

        def slot(a, px, py, pc):
            return out_refs[a].at[4 * px + 2 * py + pc]

        def copy(a, k, block, to, src=None):
            return pltpu.make_async_remote_copy(
                src_ref=slot(a, *block) if src is None else src, dst_ref=slot(a, *block),
                send_sem=send_sems.at[7 * a + k], recv_sem=recv_sems.at[7 * a + k],
                device_id=to, device_id_type=MESH)

        local = [pltpu.make_async_copy(x_refs[a], slot(a, *me), local_sems.at[a]) for a in range(n)]
        first = [[copy(a, 0, me, sibling, src=x_refs[a])]
                 + [copy(a, 1 + j, me, (*chip, c), src=x_refs[a]) for j, chip in enumerate(chips)] for a in range(n)]
        from_chips = [[copy(a, 1 + j, (*chip, c), me) for j, chip in enumerate(chips)] for a in range(n)]
        forward = [[copy(a, 4 + j, (*chip, c), sibling) for j, chip in enumerate(chips)] for a in range(n)]
        from_sibling = [[copy(a, 0, sibling, me)] + [copy(a, 4 + j, (*chip, 1 - c), me) for j, chip in enumerate(chips)]
                        for a in range(n)]
        return local, first, from_chips, forward, from_sibling

    def start(x_refs, out_refs, sems):
        local, first, _, _, _ = plan(x_refs, out_refs, sems)
        for a in range(n):
            local[a].start()
            for cp in first[a]:
                cp.start()

    def finish(x_refs, out_refs, sems):
        local, first, from_chips, forward, from_sibling = plan(x_refs, out_refs, sems)
        for a in range(n):
            for j in range(3):
                from_chips[a][j].wait_recv()
                forward[a][j].start()
        for a in range(n):
            for cp in from_sibling[a]:
                cp.wait_recv()
        for a in range(n):
            for cp in first[a] + forward[a]:
                cp.wait_send()
            local[a].wait()

    return _Side(shards, [_sds((N_DEV,) + s.shape, s.dtype) for s in shards],
                 [pltpu.SemaphoreType.DMA((7 * n,)), pltpu.SemaphoreType.DMA((7 * n,)), pltpu.SemaphoreType.DMA((n,))],
                 start, finish)


def _swap_with_sibling(name, arrs):
    n = len(arrs)

    def body(*refs):
        x_refs, got_refs = refs[:n], refs[n:2 * n]
        send_sems, recv_sems = refs[2 * n:]
        x, y, c = _mesh_pos()
        copies = []
        for a in range(n):
            for j in range(4):
                k = 4 * a + j
                cp = pltpu.make_async_remote_copy(
                    src_ref=x_refs[a].at[j, 1 - c], dst_ref=got_refs[a].at[j],
                    send_sem=send_sems.at[k], recv_sem=recv_sems.at[k],
                    device_id=(x, y, 1 - c), device_id_type=MESH)
                cp.start()
                copies.append(cp)
        for cp in copies:
            cp.wait()

    return pl.pallas_call(
        body, name=name,
        out_shape=[_sds((4,) + t.shape[2:], t.dtype) for t in arrs],
        in_specs=[_ANY] * n,
        out_specs=[_ANY] * n,
        scratch_shapes=[pltpu.SemaphoreType.DMA((4 * n,))] * 2,
    )(*arrs)


def _exchange_chips(name, arrs):
    side = _exchange_side(arrs)
    n = len(arrs)

    def body(*refs):
        side.start(refs[:n], refs[n:2 * n], refs[2 * n:])
        side.finish(refs[:n], refs[n:2 * n], refs[2 * n:])

    return pl.pallas_call(
        body, name=name,
        out_shape=side.out_shapes,
        in_specs=[_ANY] * n,
        out_specs=[_ANY] * n,
        scratch_shapes=side.scratch,
    )(*arrs)


def _exchange_side(arrs):
    n = len(arrs)

    def plan(x_refs, out_refs, sems):
        send_sems, recv_sems, local_sems = sems
        x, y, c = _mesh_pos()
        me = 2 * x + y
        local = [pltpu.make_async_copy(x_refs[a].at[me], out_refs[a].at[me], local_sems.at[a]) for a in range(n)]
        sends, recvs = [], []
        for a in range(n):
            for k in range(1, 4):
                px, py = x ^ (k >> 1), y ^ (k & 1)
                peer = 2 * px + py
                sem = 3 * a + k - 1
                sends.append(pltpu.make_async_remote_copy(
                    src_ref=x_refs[a].at[peer], dst_ref=out_refs[a].at[me],
                    send_sem=send_sems.at[sem], recv_sem=recv_sems.at[sem],
                    device_id=(px, py, c), device_id_type=MESH))
                recvs.append(pltpu.make_async_remote_copy(
                    src_ref=x_refs[a].at[peer], dst_ref=out_refs[a].at[peer],
                    send_sem=send_sems.at[sem], recv_sem=recv_sems.at[sem],
                    device_id=(px, py, c), device_id_type=MESH))
        return local, sends, recvs

    def start(x_refs, out_refs, sems):
        local, sends, _ = plan(x_refs, out_refs, sems)
        for cp in local + sends:
            cp.start()

    def finish(x_refs, out_refs, sems):
        local, sends, recvs = plan(x_refs, out_refs, sems)
        for cp in recvs:
            cp.wait_recv()
        for cp in sends:
            cp.wait_send()
        for cp in local:
            cp.wait()

    return _Side(arrs, [_sds(t.shape, t.dtype) for t in arrs],
                 [pltpu.SemaphoreType.DMA((3 * n,)), pltpu.SemaphoreType.DMA((3 * n,)), pltpu.SemaphoreType.DMA((n,))],
                 start, finish)


def _sum_blocks(name, blocks, out_dtype=F32):
    n, n_rows, n_cols = blocks.shape
    tr = _row_tile(n_rows)

    def body(x_ref, o_ref):
        acc = x_ref[0].astype(F32)
        for s in range(1, n):
            acc = acc + x_ref[s].astype(F32)
        o_ref[...] = acc.astype(o_ref.dtype)

    return pl.pallas_call(
        body, name=name,
        grid=(n_rows // tr,),
        in_specs=[pl.BlockSpec((n, tr, n_cols), lambda i: (0, i, 0))],
        out_specs=pl.BlockSpec((tr, n_cols), lambda i: (i, 0)),
        out_shape=_sds((n_rows, n_cols), out_dtype),
        compiler_params=_params(("parallel",)),
    )(blocks)


def _add_pair(name, mine, got, core):
    n, n_rows, n_cols = got.shape
    tr = _row_tile(n_rows)

    def body(core_ref, a_ref, b_ref, o_ref):
        o_ref[...] = (a_ref[...].astype(F32) + b_ref[...].astype(F32)).astype(BF16)

    spec = pl.BlockSpec((None, tr, n_cols), lambda j, i, core_ref: (j, i, 0))
    return pl.pallas_call(
        body, name=name,
        grid_spec=pltpu.PrefetchScalarGridSpec(
            num_scalar_prefetch=1,
            grid=(n, n_rows // tr),
            in_specs=[pl.BlockSpec((None, None, tr, n_cols), lambda j, i, core_ref: (j, core_ref[0], i, 0)), spec],
            out_specs=spec,
        ),
        out_shape=_sds(got.shape, BF16),
        compiler_params=_params(("parallel", "parallel")),
    )(core, mine, got)


_MIX_PARTS = (
    (("w_in", "even_w_in", 1, D_MODEL, 5120), ("w_glu", "s5_w_glu", 0, S5_WIDTH, S5_WIDTH),
     ("w_out", "even_w_out", 0, D_MODEL, D_MODEL)),
    (("w_qkv", "odd_w_qkv", 1, D_MODEL, QKV_WIDTH), ("w_out", "odd_w_out", 0, D_MODEL, D_MODEL)),
)
_PACK_COLS = 1024
_FF_SHARD = D_FF // N_DEV
_BIG_NAMES = ("even_w_in", "s5_w_glu", "even_w_out", "odd_w_qkv", "odd_w_out")


def _part_rows(rows, cols):
    return rows * cols // N_DEV // _PACK_COLS


def _row_tile(n_rows):
    return next(t for t in (512, 480, 384, 256, 128) if n_rows % t == 0)


def _pack_mixer_shard(kind, j, args):
    return jnp.concatenate([args[name][j].astype(BF16).reshape(-1, _PACK_COLS) for _, name, _, _, _ in _MIX_PARTS[kind]],
                           axis=0)


def _unpack_mixer(kind, gathered):
    out, off = {}, 0
    for key, _, axis, rows, cols in _MIX_PARTS[kind]:
        n = _part_rows(rows, cols)
        part = gathered[:, off:off + n]
        off += n
        if axis == 1:
            part = part.reshape(N_DEV, rows, cols // N_DEV).transpose(1, 0, 2)
        out[key] = part.reshape(rows, cols)
    return out


def _pack_mixer_grads(kind, g):
    parts = []
    for key, _, axis, rows, cols in _MIX_PARTS[kind]:
        t = g[key]
        if axis == 1:
            t = t.reshape(rows, N_DEV, cols // N_DEV).transpose(1, 0, 2)
        parts.append(t.reshape(N_DEV, -1, _PACK_COLS))
    return jnp.concatenate(parts, axis=1)


def _unpack_mixer_grads(kind, flat):
    out, off = {}, 0
    for _, name, axis, rows, cols in _MIX_PARTS[kind]:
        n = _part_rows(rows, cols)
        shape = (rows, cols // N_DEV) if axis == 1 else (rows // N_DEV, cols)
        out[name] = flat[off:off + n].reshape(shape)
        off += n
    return out


def _pack_small(arrs, row_mult=512):
    parts = []
    for a in arrs:
        f = a.astype(F32).reshape(-1)
        parts.append(jnp.pad(f, (0, (-f.shape[0]) % 128)))
    f = jnp.concatenate(parts)
    f = jnp.pad(f, (0, (-f.shape[0]) % (128 * row_mult)))
    return f.reshape(-1, 128)


def _unpack_small(flat, shapes):
    f = flat.reshape(-1)
    out, off = [], 0
    for s in shapes:
        n = math.prod(s)
        out.append(f[off:off + n].reshape(s))
        off += n + (-n) % 128
    return out


_WEIGHTS = ("even_norm", "even_w_in", "s5_lambda_re", "s5_lambda_im", "s5_log_dt", "s5_b_re", "s5_b_im",
            "s5_c_re", "s5_c_im", "s5_d", "s5_w_glu", "s5_b_glu", "hgrn_lower_bound", "hgrn_o_norm",
            "even_w_out", "odd_norm", "odd_w_qkv", "q_norm", "k_norm", "att_sinks", "odd_w_out",
            "mlp_norm", "mlp_w_up", "mlp_w_down")
_MLP_NAMES = ("mlp_w_up", "mlp_w_down")
_SMALL_NAMES = tuple(n for n in _WEIGHTS if n not in _BIG_NAMES + _MLP_NAMES)


def _add_res(acc, res):
    return (acc + res,)


def _mm_hosting(side, *args, **kw):
    if side is None:
        return _mm(*args, **kw), None
    return _mm(*args, side=side, **kw)


def _mlp_fwd(h, gain, w_up, w_down, sides=(None, None)):
    n_rows, fs = h.shape[0], _FF_SHARD
    xn, rstd = _rms_fwd("rms_fwd", h, gain)
    (up, act), got_up = _mm_hosting(
        sides[0], "mm_up", xn, w_up, "nn", out_dtypes=(F32, BF16), mkn=(n_rows, D_MODEL, D_FF), tn=fs,
        b_block=pl.BlockSpec((None, D_MODEL, fs), lambda i, j, kk: (j, kk, 0)),
        epi=lambda acc: (acc, jnp.square(jnp.maximum(acc, 0.0))))
    out, got_down = _mm_hosting(
        sides[1], "mm_down", act, w_down.reshape(N_DEV // 2, 2 * fs, D_MODEL), "nn", mkn=(n_rows, D_FF, D_MODEL),
        tk=2 * fs, b_block=pl.BlockSpec((None, 2 * fs, 1024), lambda i, j, kk: (kk, 0, j)),
        epi=_add_res, extras=(h,))
    return out, (h, gain, xn, rstd, up, act, w_up, w_down), got_up, got_down


def _mlp_bwd(cache, dh, dhb, sides=(None, None, None)):
    h, gain, xn, rstd, up, act, w_up, w_down = cache
    n_rows, fs = h.shape[0], _FF_SHARD
    dup, got0 = _mm_hosting(
        sides[0], "mm_dact", dhb, w_down, "nt", out_dtypes=(BF16,), mkn=(n_rows, D_MODEL, D_FF), tn=fs,
        b_block=pl.BlockSpec((None, fs, D_MODEL), lambda i, j, kk: (j, 0, kk)),
        epi=lambda acc, u: (acc * (2.0 * jnp.maximum(u, 0.0)),), extras=(up,))
    dw_down, got1 = _mm_hosting(
        sides[1], "mm_dw_down", act, dhb, "tn", out_dtypes=(BF16,),
        o_block=(pl.BlockSpec((None, 512, 1024), lambda i, j, kk: (i // 2, i % 2, j)), (N_DEV, fs, D_MODEL)))
    dxn, got2 = _mm_hosting(
        sides[2], "mm_dxn_up", dup, w_up, "nt", mkn=(n_rows, D_FF, D_MODEL), tk=fs,
        b_block=pl.BlockSpec((None, 1024, fs), lambda i, j, kk: (kk, j, 0)))
    dw_up = _mm("mm_dw_up", xn, dup, "tn", out_dtypes=(BF16,),
                o_block=(pl.BlockSpec((None, 512, fs), lambda i, j, kk: (j, i, 0)), (N_DEV, D_MODEL, fs)))
    dh_in, dhb_in, dgain = _rms_bwd("rms_bwd", h, rstd, gain, dxn, dh)
    return dh_in, dhb_in, dgain, dw_up, dw_down, (got0, got1, got2)


def _even_fwd(h, p, side=None):
    xn, rstd = _rms_fwd("rms_fwd", h, p["norm"])
    proj, got = _mm_hosting(side, "mm_w_in", xn, p["w_in"], "nn")
    y_pre, z, s5_states = _s5_fwd("s5_fwd", proj, p["mats"])
    gate = _mm("mm_glu", z, p["w_glu"], "nn")
    (ya,) = _rowwise("glu_fwd", lambda y, gt, b: ((_gelu(y) * _sigmoid(gt + b),), ()),
                     [y_pre, gate], [p["b_glu"].reshape(1, -1)], [(S5_WIDTH, BF16)])
    o, yb, h_states = _hgrn_fwd("hgrn_fwd", proj, p["lb"].reshape(8, 1, 128), p["o_gain"].reshape(1, 128))
    ycat = jnp.concatenate([ya, yb], axis=1)
    out = _mm("mm_w_out", ycat, p["w_out"], "nn", epi=_add_res, extras=(h,))
    return out, (h, xn, rstd, proj, y_pre, z, s5_states, gate, o, h_states, ycat), got


def _even_bwd(cache, p, dh, dhb):
    h, xn, rstd, proj, y_pre, z, s5_states, gate, o, h_states, ycat = cache
    g = {}
    dycat = _mm("mm_dy_out", dhb, p["w_out"], "nt")
    g["w_out"] = _mm("mm_dw_out", ycat, dhb, "tn", out_dtypes=(BF16,))
    dq, df, di, dg, dlb, dgain = _hgrn_bwd("hgrn_bwd", proj, p["lb"].reshape(8, 1, 128),
                                           p["o_gain"].reshape(1, 128), o, h_states, dycat)
    g["lb"] = dlb.reshape(-1)
    g["o_gain"] = jnp.sum(dgain, axis=0).reshape(-1)

    def glu_bwd1(dyc, y, gt, b):
        zf = _gelu(y)
        s = _sigmoid(gt + b)
        dya = dyc[:, :S5_WIDTH]
        d_gate = dya * zf * s * (1.0 - s)
        return (d_gate, dya * s), (_colsum(d_gate),)

    d_gate, dz_direct, db_glu = _rowwise("glu_bwd_gate", glu_bwd1, [dycat, y_pre, gate], [p["b_glu"].reshape(1, -1)],
                                         [(S5_WIDTH, BF16), (S5_WIDTH, F32)], [S5_WIDTH])
    g["b_glu"] = db_glu.reshape(-1)
    dz_gate = _mm("mm_dz_glu", d_gate, p["w_glu"], "nt")
    g["w_glu"] = _mm("mm_dw_glu", z, d_gate, "tn", out_dtypes=(BF16,))
    (dy_pre,) = _rowwise("glu_bwd_gelu", lambda a, b, y: (((a + b) * _gelu_grad(y),), ()),
                         [dz_direct, dz_gate, y_pre], [], [(S5_WIDTH, F32)])
    du, dbm, dcm, da, dd = _s5_bwd("s5_bwd", proj, dy_pre, s5_states, p["mats"])
    g["s5"] = (dbm, dcm, da, dd)
    dproj = jnp.concatenate([du, dq, df, di, dg], axis=1)
    dxn = _mm("mm_dxn_in", dproj, p["w_in"], "nt", tk=2560)
    g["w_in"] = _mm("mm_dw_in", xn, dproj, "tn", out_dtypes=(BF16,))
    dh_in, dhb_in, dnorm = _rms_bwd("rms_bwd", h, rstd, p["norm"], dxn, dh)
    g["norm"] = dnorm.reshape(-1)
    return dh_in, dhb_in, g


def _odd_fwd(h, p, side=None):
    xn, rstd = _rms_fwd("rms_fwd", h, p["norm"])
    qkv, got = _mm_hosting(side, "mm_w_qkv", xn, p["w_qkv"], "nn", tn=1280)
    o = _swa_fwd("swa_fwd", qkv, p["q_gain"], p["k_gain"], p["sinks"], p["slopes"])
    out = _mm("mm_w_out", o, p["w_out"], "nn", epi=_add_res, extras=(h,))
    return out, (h, xn, rstd, qkv, o), got


def _shift_up_block(x):
    return jnp.concatenate([x[ATT_BLOCK:], jnp.zeros((ATT_BLOCK, x.shape[1]), x.dtype)], axis=0)


def _odd_bwd(cache, p, dh, dhb):
    h, xn, rstd, qkv, o = cache
    g = {}
    d_o = _mm("mm_dy_out", dhb, p["w_out"], "nt")
    g["w_out"] = _mm("mm_dw_out", o, dhb, "tn", out_dtypes=(BF16,))
    dq, dkc, dkp, dvc, dvp, dsink, dqg, dkg = _swa_bwd("swa_bwd", qkv, p["q_gain"], p["k_gain"], p["sinks"],
                                                       p["slopes"], d_o)
    dk = (dkc.astype(F32) + _shift_up_block(dkp).astype(F32)).astype(BF16)
    dv = (dvc.astype(F32) + _shift_up_block(dvp).astype(F32)).astype(BF16)
    g["sinks"], g["q_gain"], g["k_gain"] = dsink[:, 0], dqg.reshape(-1), dkg.reshape(-1)
    dqkv = jnp.concatenate([dq, dk, dv], axis=1)
    dxn = _mm("mm_dxn_qkv", dqkv, p["w_qkv"], "nt", tk=1280)
    g["w_qkv"] = _mm("mm_dw_qkv", xn, dqkv, "tn", out_dtypes=(BF16,), tn=1280)
    dh_in, dhb_in, dnorm = _rms_bwd("rms_bwd", h, rstd, p["norm"], dxn, dh)
    g["norm"] = dnorm.reshape(-1)
    return dh_in, dhb_in, g


def kernel(x, even_norm, even_w_in, s5_lambda_re, s5_lambda_im, s5_log_dt, s5_b_re, s5_b_im, s5_c_re, s5_c_im, s5_d, s5_w_glu, s5_b_glu, hgrn_lower_bound, hgrn_o_norm, even_w_out, odd_norm, odd_w_qkv, q_norm, k_norm, att_sinks, odd_w_out, mlp_norm, mlp_w_up, mlp_w_down, loss_target, m_even_norm, m_even_w_in, m_s5_lambda_re, m_s5_lambda_im, m_s5_log_dt, m_s5_b_re, m_s5_b_im, m_s5_c_re, m_s5_c_im, m_s5_d, m_s5_w_glu, m_s5_b_glu, m_hgrn_lower_bound, m_hgrn_o_norm, m_even_w_out, m_odd_norm, m_odd_w_qkv, m_q_norm, m_k_norm, m_att_sinks, m_odd_w_out, m_mlp_norm, m_mlp_w_up, m_mlp_w_down, v_even_norm, v_even_w_in, v_s5_lambda_re, v_s5_lambda_im, v_s5_log_dt, v_s5_b_re, v_s5_b_im, v_s5_c_re, v_s5_c_im, v_s5_d, v_s5_w_glu, v_s5_b_glu, v_hgrn_lower_bound, v_hgrn_o_norm, v_even_w_out, v_odd_norm, v_odd_w_qkv, v_q_norm, v_k_norm, v_att_sinks, v_odd_w_out, v_mlp_norm, v_mlp_w_up, v_mlp_w_down):
    a = dict(locals())
    n_rows = x.shape[1]
    xi, yi, ci = _mesh_pos()
    me = 4 * xi + 2 * yi + ci

    chunks = [[_pack_mixer_shard(layer % 2, layer // 2, a), mlp_w_up[layer].astype(BF16),
               mlp_w_down[layer].astype(BF16)] for layer in range(DEPTH)]
    gathered = _allgather("gather_layer", chunks[0])
    (odd_gathered,) = _allgather("gather_odd_norm", [jnp.pad(odd_norm, ((0, 6), (0, 0)))])
    odd_norm_full = odd_gathered[:, :2].transpose(1, 0, 2).reshape(2, D_MODEL)

    lower_bounds, lb_vjp = jax.vjp(_hgrn_lower_bounds, hgrn_lower_bound)
    slopes = _alibi_slopes()
    s5_vjps = []

    h = x.reshape(n_rows, D_MODEL)
    caches, layer_p = [], []
    for layer in range(DEPTH):
        kind, j = layer % 2, layer // 2
        wl = _unpack_mixer(kind, gathered[0])
        w_up_g, w_down_g = gathered[1], gathered[2]
        nxt = chunks[layer + 1] if layer + 1 < DEPTH else None
        sides = [_gather_side([t]) for t in nxt] if nxt is not None else [None] * 3
        if kind == 0:
            disc, vjp = jax.vjp(_s5_discretize, s5_lambda_re[j], s5_lambda_im[j], s5_log_dt[j], s5_b_re[j], s5_b_im[j])
            s5_vjps.append(vjp)
            p = dict(norm=even_norm[j], w_in=wl["w_in"], w_glu=wl["w_glu"], b_glu=s5_b_glu[j],
                     mats=_s5_matrices(*disc, s5_c_re[j], s5_c_im[j], s5_d[j]),
                     lb=lower_bounds[j], o_gain=hgrn_o_norm[j], w_out=wl["w_out"])
            h, c_mix, got_mix = _even_fwd(h, p, sides[0])
        else:
            p = dict(norm=odd_norm_full[j], w_qkv=wl["w_qkv"], q_gain=q_norm[j], k_gain=k_norm[j],
                     sinks=att_sinks[j], slopes=slopes, w_out=wl["w_out"])
            h, c_mix, got_mix = _odd_fwd(h, p, sides[0])
        h, c_mlp, got_up, got_down = _mlp_fwd(h, mlp_norm[layer], w_up_g, w_down_g, sides[1:])
        caches.append((c_mix, c_mlp))
        layer_p.append(p)
        if nxt is not None:
            gathered = [got_mix[0], got_up[0], got_down[0]]
    dh, dhb, sq = _loss_head(h, loss_target.reshape(n_rows, D_MODEL))
    loss = lax.psum(0.5 * sq[0, 0] / D_MODEL, ("x", "y", "c"))

    core = ci.astype(jnp.int32).reshape(1)
    mix_g, mlp_norm_g, received = [None] * DEPTH, [None] * DEPTH, [None] * DEPTH
    pending = None
    for layer in reversed(range(DEPTH)):
        kind = layer % 2
        c_mix, c_mlp = caches[layer]
        sides = [_exchange_side([t]) for t in pending] if pending is not None else [None] * 3
        dh, dhb, d_mlp_norm, dw_up, dw_down, got = _mlp_bwd(c_mlp, dh, dhb, sides)
        if pending is not None:
            received[layer + 1] = [g[0] for g in got]
        mlp_norm_g[layer] = d_mlp_norm.reshape(-1)
        bwd = _even_bwd if kind == 0 else _odd_bwd
        dh, dhb, mix_g[layer] = bwd(c_mix, layer_p[layer], dh, dhb)
        by_chip = [t.reshape((4, 2) + t.shape[1:]) for t in (_pack_mixer_grads(kind, mix_g[layer]), dw_up, dw_down)]
        arrived = _swap_with_sibling("swap_grads", by_chip)
        pending = [_add_pair("add_sibling_grads", m, g, core) for m, g in zip(by_chip, arrived)]
    received[0] = _exchange_chips("exchange_grads", pending)
    grad_x = dh.reshape(x.shape)

    ev, od = [mix_g[0], mix_g[2]], [mix_g[1], mix_g[3]]
    sums = [[_sum_blocks("sum_grads", r) for r in received[layer]] for layer in range(DEPTH)]
    grads = {"mlp_w_up": jnp.stack([s[1] for s in sums]), "mlp_w_down": jnp.stack([s[2] for s in sums])}
    per_layer = [_unpack_mixer_grads(layer % 2, sums[layer][0]) for layer in range(DEPTH)]
    for name in _BIG_NAMES:
        grads[name] = jnp.stack([g[name] for g in per_layer if name in g])

    s5_g = []
    for j in range(2):
        dar, dai, dbbr, dbbi, dcr, dci, dd = _s5_unpack_grads(*ev[j]["s5"])
        s5_g.append(tuple(s5_vjps[j]((dar, dai, dbbr, dbbi))) + (dcr, dci, dd))
    (d_lb_param,) = lb_vjp(jnp.stack([g["lb"] for g in ev]))
    small = {
        "even_norm": jnp.stack([g["norm"] for g in ev]),
        "s5_lambda_re": jnp.stack([g[0] for g in s5_g]), "s5_lambda_im": jnp.stack([g[1] for g in s5_g]),
        "s5_log_dt": jnp.stack([g[2] for g in s5_g]), "s5_b_re": jnp.stack([g[3] for g in s5_g]),
        "s5_b_im": jnp.stack([g[4] for g in s5_g]), "s5_c_re": jnp.stack([g[5] for g in s5_g]),
        "s5_c_im": jnp.stack([g[6] for g in s5_g]), "s5_d": jnp.stack([g[7] for g in s5_g]),
        "s5_b_glu": jnp.stack([g["b_glu"] for g in ev]), "hgrn_lower_bound": d_lb_param,
        "hgrn_o_norm": jnp.stack([g["o_gain"] for g in ev]), "odd_norm": jnp.stack([g["norm"] for g in od]),
        "q_norm": jnp.stack([g["q_gain"] for g in od]), "k_norm": jnp.stack([g["k_gain"] for g in od]),
        "att_sinks": jnp.stack([g["sinks"] for g in od]), "mlp_norm": jnp.stack(mlp_norm_g),
    }
    small_shapes = [small[n].shape for n in _SMALL_NAMES]
    (small_all,) = _allgather("gather_small_grads", [_pack_small([small[n] for n in _SMALL_NAMES])])
    small_sum = _sum_blocks("sum_small_grads", small_all)
    for n, g in zip(_SMALL_NAMES, _unpack_small(small_sum, small_shapes)):
        grads[n] = g
    grads["odd_norm"] = lax.dynamic_slice_in_dim(grads["odd_norm"], me * (D_MODEL // N_DEV), D_MODEL // N_DEV, axis=1)

    delta, new_m, new_v = {}, {}, {}
    for name in _BIG_NAMES + _MLP_NAMES:
        to2d = lambda t, c=a[name].shape[-1]: t.reshape(-1, c)
        d_, m_, v_ = _adamw("adamw_" + name, to2d(a[name]), to2d(grads[name]), to2d(a["m_" + name]), to2d(a["v_" + name]))
        delta[name], new_m[name], new_v[name] = (t.reshape(a[name].shape) for t in (d_, m_, v_))
    packed = [_pack_small([src[n] for n in _SMALL_NAMES])
              for src in (a, grads, {n: a["m_" + n] for n in _SMALL_NAMES}, {n: a["v_" + n] for n in _SMALL_NAMES})]
    shapes = [a[n].shape for n in _SMALL_NAMES]
    for dst, flat in zip((delta, new_m, new_v), _adamw("adamw_small", *packed)):
        for n, t in zip(_SMALL_NAMES, _unpack_small(flat, shapes)):
            dst[n] = t

    return (loss, grad_x, *[grads[n] for n in _WEIGHTS], *[delta[n] for n in _WEIGHTS],
            *[new_m[n] for n in _WEIGHTS], *[new_v[n] for n in _WEIGHTS])
```

```python
import math

import jax
import jax.numpy as jnp
from jax import lax
from jax.experimental import pallas as pl
from jax.experimental.pallas import tpu as pltpu

F32 = jnp.float32
BF16 = jnp.bfloat16
MESH = pl.DeviceIdType.MESH

D_MODEL = 2048
DEPTH = 4
EPS = 1e-6
S5_WIDTH = 1024
S5_GROUPS = 64
S5_STATE = 64
S5_GROUP_SIZE = 16
S5_MIN_DECAY = 1e-4
S5_CHUNK = 128
S5_LEVELS = 7
HGRN_WIDTH = 1024
HGRN_HEADS = 8
HGRN_DIM = 128
HGRN_SUB = 16
HGRN_BLOCK = 128
ATT_HEADS = 32
ATT_KV = 4
ATT_DIM = 64
ATT_BLOCK = 128
QKV_WIDTH = (ATT_HEADS + 2 * ATT_KV) * ATT_DIM
D_FF = 4 * D_MODEL
N_DEV = 8
NEG = -1e30
VMEM_LIMIT = 56 * 1024 * 1024

ADAM_LR, ADAM_B1, ADAM_B2, ADAM_EPS, ADAM_WD, ADAM_STEP = 0.001, 0.9, 0.999, 1e-08, 0.01, 10


def _params(sem=None):
    return pltpu.CompilerParams(dimension_semantics=sem, vmem_limit_bytes=VMEM_LIMIT)


def _sds(shape, dtype):
    return jax.ShapeDtypeStruct(shape, dtype)


def _call_hosting(side, body, name, grid, in_specs, out_specs, out_shape, scratch_shapes, sem, operands):
    if side is None:
        return pl.pallas_call(body, name=name, grid=grid, in_specs=in_specs, out_specs=out_specs, out_shape=out_shape,
                              scratch_shapes=scratch_shapes, compiler_params=_params(sem))(*operands), None
    n_in, n_out, n_scr = len(in_specs), len(out_specs), len(scratch_shapes)
    n_sin, n_sout = len(side.operands), len(side.out_shapes)
    any_spec = pl.BlockSpec(memory_space=pl.ANY)

    def hosting(*refs):
        ins, refs = refs[:n_in], refs[n_in:]
        sin, refs = refs[:n_sin], refs[n_sin:]
        outs, refs = refs[:n_out], refs[n_out:]
        sout, refs = refs[:n_sout], refs[n_sout:]
        scr, sems = refs[:n_scr], refs[n_scr:]
        ids = [pl.program_id(d) for d in range(len(grid))]
        first, last = ids[0] == 0, ids[0] == grid[0] - 1
        for d in range(1, len(grid)):
            first, last = first & (ids[d] == 0), last & (ids[d] == grid[d] - 1)

        @pl.when(first)
        def _():
            side.start(sin, sout, sems)

        body(*ins, *outs, *scr)

        @pl.when(last)
        def _():
            side.finish(sin, sout, sems)

    outs = pl.pallas_call(
        hosting, name=name, grid=grid,
        in_specs=list(in_specs) + [any_spec] * n_sin,
        out_specs=list(out_specs) + [any_spec] * n_sout,
        out_shape=list(out_shape) + side.out_shapes,
        scratch_shapes=list(scratch_shapes) + side.scratch,
        compiler_params=_params(("arbitrary",) * len(grid)),
    )(*operands, *side.operands)
    return outs[:n_out], outs[n_out:]


def _mm(name, a, b, mode, out_dtypes=(F32,), epi=None, extras=(), tm=512, tn=1024, tk=2048,
        mkn=None, b_block=None, b2_block=None, o_block=None, side=None):
    if mkn is not None:
        m, k, n = mkn
    elif mode == "nn":
        (m, k), n = a.shape, b.shape[1]
    elif mode == "nt":
        (m, k), n = a.shape, b.shape[0]
    else:
        (k, m), n = a.shape, b.shape[1]
    tm, tn, tk = min(tm, m), min(tn, n), min(tk, k)
    assert m % tm == 0 and n % tn == 0 and k % tk == 0, (name, m, n, k)
    nk = k // tk
    if mode == "nn":
        a_spec = pl.BlockSpec((tm, tk), lambda i, j, kk: (i, kk))
        b_spec = pl.BlockSpec((tk, tn), lambda i, j, kk: (kk, j))
        dims = (((1,), (0,)), ((), ()))
    elif mode == "nt":
        a_spec = pl.BlockSpec((tm, tk), lambda i, j, kk: (i, kk))
        b_spec = pl.BlockSpec((tn, tk), lambda i, j, kk: (j, kk))
        dims = (((1,), (1,)), ((), ()))
    else:
        a_spec = pl.BlockSpec((tk, tm), lambda i, j, kk: (kk, i))
        b_spec = pl.BlockSpec((tk, tn), lambda i, j, kk: (kk, j))
        dims = (((0,), (0,)), ((), ()))
    o_spec = pl.BlockSpec((tm, tn), lambda i, j, kk: (i, j))
    if b_block is not None:
        b_spec = b_block
    out_specs = [o_spec] * len(out_dtypes)
    out_shape = [_sds((m, n), dt) for dt in out_dtypes]
    if o_block is not None:
        assert len(out_dtypes) == 1 and not extras
        out_specs, out_shape = [o_block[0]], [_sds(o_block[1], out_dtypes[0])]
    n_ex, n_out = len(extras), len(out_dtypes)
    n_b = 1 if b2_block is None else 2
    grid = (m // tm, n // tn, nk)

    def body(*refs):
        a_ref, b_refs = refs[0], refs[1:1 + n_b]
        pos = 1 + n_b
        ex_refs = refs[pos:pos + n_ex]
        pos += n_ex
        out_refs = refs[pos:pos + n_out]
        pos += n_out
        acc_ref = refs[pos] if nk > 1 else None
        av = a_ref[...]
        if av.dtype != BF16:
            av = av.astype(BF16)
        part = None
        for q, b_ref in enumerate(b_refs):
            bv = b_ref[...]
            if bv.dtype != BF16:
                bv = bv.astype(BF16)
            aq = av if n_b == 1 else av[:, q * (tk // 2):(q + 1) * (tk // 2)]
            d = lax.dot_general(aq, bv, dims, preferred_element_type=F32)
            part = d if part is None else part + d

        def finish(acc):
            outs = epi(acc, *[r[...] for r in ex_refs]) if epi is not None else (acc,)
            for r, o in zip(out_refs, outs):
                r[...] = o.astype(r.dtype)

        if nk == 1:
            finish(part)
        else:
            kk = pl.program_id(2)

            @pl.when(kk == 0)
            def _():
                acc_ref[...] = part

            @pl.when(kk > 0)
            def _():
                acc_ref[...] += part

            @pl.when(kk == nk - 1)
            def _():
                finish(acc_ref[...])

    b_specs = [b_spec] if b2_block is None else [b_spec, b2_block]
    outs, side_outs = _call_hosting(
        side, body, name, grid, [a_spec] + b_specs + [o_spec] * n_ex, out_specs, out_shape,
        [pltpu.VMEM((tm, tn), F32)] if nk > 1 else [], ("parallel", "parallel", "arbitrary"),
        (a,) + (b,) * n_b + tuple(extras))
    main = outs[0] if n_out == 1 else outs
    return main if side is None else (main, side_outs)


def _rowwise(name, fn, rows, vecs, outs, accs=(), tr=256):
    n_rows = rows[0].shape[0]
    tr = min(tr, n_rows)
    assert n_rows % tr == 0
    n_r, n_v, n_o, n_a = len(rows), len(vecs), len(outs), len(accs)

    def body(*refs):
        ins = [r[...] for r in refs[:n_r + n_v]]
        o_refs = refs[n_r + n_v:n_r + n_v + n_o]
        a_refs = refs[n_r + n_v + n_o:]
        ro, ao = fn(*ins)
        for r, o in zip(o_refs, ro):
            r[...] = o.astype(r.dtype)
        if n_a:
            step = pl.program_id(0)

            @pl.when(step == 0)
            def _():
                for r, o in zip(a_refs, ao):
                    r[...] = o

            @pl.when(step > 0)
            def _():
                for r, o in zip(a_refs, ao):
                    r[...] += o

    res = pl.pallas_call(
        body, name=name,
        grid=(n_rows // tr,),
        in_specs=[pl.BlockSpec((tr, r.shape[1]), lambda i: (i, 0)) for r in rows]
        + [pl.BlockSpec(v.shape, lambda i: (0, 0)) for v in vecs],
        out_specs=[pl.BlockSpec((tr, w), lambda i: (i, 0)) for w, _ in outs]
        + [pl.BlockSpec((1, w), lambda i: (0, 0)) for w in accs],
        out_shape=[_sds((n_rows, w), dt) for w, dt in outs] + [_sds((1, w), F32) for w in accs],
        compiler_params=_params(("arbitrary",)),
    )(*rows, *vecs)
    return res


def _colsum(x):
    return jnp.sum(x, axis=0, keepdims=True)


def _sigmoid(x):
    return 1.0 / (1.0 + jnp.exp(-x))


_GELU_C = math.sqrt(2.0 / math.pi)


def _gelu(y):
    return 0.5 * y * (1.0 + jnp.tanh(_GELU_C * (y + 0.044715 * y * y * y)))


def _gelu_grad(y):
    t = jnp.tanh(_GELU_C * (y + 0.044715 * y * y * y))
    return 0.5 * (1.0 + t) + 0.5 * y * (1.0 - t * t) * _GELU_C * (1.0 + 3.0 * 0.044715 * y * y)


def _rms_fwd(name, h, gain):
    def fn(x, g):
        r = lax.rsqrt(jnp.mean(x * x, axis=1, keepdims=True) + EPS)
        return (x * r * g, r), ()
    return _rowwise(name, fn, [h], [gain.reshape(1, -1)], [(h.shape[1], BF16), (1, F32)])


def _rms_bwd(name, h, rstd, gain, dxn, dres):
    def fn(x, r, dy, dr, g):
        xh = x * r
        gdy = dy * g
        dx = r * (gdy - xh * jnp.mean(gdy * xh, axis=1, keepdims=True)) + dr
        return (dx, dx), (_colsum(dy * xh),)
    w = h.shape[1]
    return _rowwise(name, fn, [h, rstd, dxn, dres], [gain.reshape(1, -1)], [(w, F32), (w, BF16)], [w])


def _loss_head(h, target):
    w = h.shape[1]

    def fn(x, t):
        e = x - t
        return (e * (1.0 / w), e * (1.0 / w)), (jnp.zeros((1, 128), F32) + jnp.sum(e * e),)
    return _rowwise("loss_head", fn, [h, target], [], [(w, F32), (w, BF16)], [128])


def _adamw(name, w, g, m, v):
    c1 = 1.0 - ADAM_B1 ** ADAM_STEP
    c2 = 1.0 - ADAM_B2 ** ADAM_STEP

    def fn(w_, g_, m_, v_):
        mn = ADAM_B1 * m_ + (1.0 - ADAM_B1) * g_
        vn = ADAM_B2 * v_ + (1.0 - ADAM_B2) * (g_ * g_)
        delta = -ADAM_LR * ((mn / c1) / (jnp.sqrt(vn / c2) + ADAM_EPS) + ADAM_WD * w_)
        return (delta, mn, vn), ()
    c = w.shape[1]
    return _rowwise(name, fn, [w, g, m, v], [], [(c, F32)] * 3)


def _s5_discretize(lam_re, lam_im, log_dt, b_re, b_im):
    lr = jnp.minimum(lam_re, -S5_MIN_DECAY)
    li = lam_im
    dt = jnp.exp(log_dt)[:, None]
    mag = jnp.exp(lr * dt)
    ar = mag * jnp.cos(li * dt)
    ai = mag * jnp.sin(li * dt)
    den = lr * lr + li * li
    zr = ((ar - 1.0) * lr + ai * li) / den
    zi = (ai * lr - (ar - 1.0) * li) / den
    bbr = zr[..., None] * b_re - zi[..., None] * b_im
    bbi = zr[..., None] * b_im + zi[..., None] * b_re
    return ar, ai, bbr, bbi


def _s5_matrices(ar, ai, bbr, bbi, c_re, c_im, d_skip):
    eye = jnp.eye(8, dtype=F32)
    bt = jnp.stack([bbr, bbi], axis=1).transpose(0, 3, 1, 2)
    bt = bt.reshape(8, 8, 16, 1, 2, 64) * eye[None, :, None, :, None, None]
    bm8 = bt.reshape(8, 8, 16, 4, 2, 2, 64).transpose(0, 1, 2, 3, 5, 4, 6).reshape(8, 128, 1024)
    ct = jnp.stack([c_re, -c_im], axis=1).transpose(0, 1, 3, 2)
    ct = ct.reshape(8, 8, 2, 64, 1, 16) * eye[None, :, None, None, :, None]
    cm8 = ct.reshape(8, 4, 2, 2, 64, 8, 16).transpose(0, 1, 3, 2, 4, 5, 6).reshape(8, 1024, 128)
    prs, pis = [], []
    pr, pi = ar, ai
    for _ in range(S5_LEVELS):
        prs.append(pr.reshape(8, 512))
        pis.append(pi.reshape(8, 512))
        pr, pi = pr * pr - pi * pi, 2.0 * pr * pi
    prs.append(jnp.zeros_like(prs[0]))
    pis.append(jnp.zeros_like(pis[0]))
    return (bm8.astype(BF16), cm8.astype(BF16), jnp.stack(prs, axis=1), jnp.stack(pis, axis=1),
            d_skip.reshape(8, 1, 128))


def _s5_unpack_grads(dbm8, dcm8, da, dd):
    db = dbm8.reshape(8, 8, 16, 4, 2, 2, 64).transpose(0, 1, 2, 3, 5, 4, 6).reshape(8, 8, 16, 8, 2, 64)
    db = jnp.einsum("agcgqp->agcqp", db).reshape(S5_GROUPS, 16, 2, 64)
    dc = dcm8.reshape(8, 4, 2, 2, 64, 8, 16).transpose(0, 1, 3, 2, 4, 5, 6).reshape(8, 8, 2, 64, 8, 16)
    dc = jnp.einsum("agqpgc->agqpc", dc).reshape(S5_GROUPS, 2, 64, 16)
    dar = da[:, 0, :].reshape(S5_GROUPS, 64)
    dai = da[:, 1, :].reshape(S5_GROUPS, 64)
    return (dar, dai, db[:, :, 0, :].transpose(0, 2, 1), db[:, :, 1, :].transpose(0, 2, 1),
            dc[:, 0].transpose(0, 2, 1), -dc[:, 1].transpose(0, 2, 1), dd.reshape(S5_GROUPS, 16))


def _shift_rows(x, s, row, down):
    t = x.shape[0]
    if s % 8 == 0:
        z = jnp.zeros((s, x.shape[1]), x.dtype)
        return jnp.concatenate([z, x[:t - s]], axis=0) if down else jnp.concatenate([x[s:], z], axis=0)
    if down:
        return jnp.where(row >= s, pltpu.roll(x, s, 0), 0.0)
    return jnp.where(row < t - s, pltpu.roll(x, t - s, 0), 0.0)


def _s5_scan(xr, xi, pr, pi, cr, ci, row, conj):
    t = xr[0].shape[0]
    sg = -1.0 if conj else 1.0
    edge = (t - 1) if conj else 0
    n = len(xr)
    for k in range(n):
        sl = slice(128 * k, 128 * (k + 1))
        p_r, p_i = pr[0:1, sl], sg * pi[0:1, sl]
        xr[k] = xr[k] + jnp.where(row == edge, p_r * cr[k] - p_i * ci[k], 0.0)
        xi[k] = xi[k] + jnp.where(row == edge, p_r * ci[k] + p_i * cr[k], 0.0)
    for lvl in range(S5_LEVELS):
        s = 1 << lvl
        for k in range(n):
            sl = slice(128 * k, 128 * (k + 1))
            p_r, p_i = pr[lvl:lvl + 1, sl], sg * pi[lvl:lvl + 1, sl]
            sr = _shift_rows(xr[k], s, row, not conj)
            si = _shift_rows(xi[k], s, row, not conj)
            xr[k] = xr[k] + p_r * sr - p_i * si
            xi[k] = xi[k] + p_r * si + p_i * sr
    return xr, xi


def _s5_fwd(name, proj, mats, side=None):
    bm8, cm8, p1, p2, d8 = mats
    n_rows = proj.shape[0]
    t = S5_CHUNK
    nch = n_rows // t

    def body(u_ref, bm_ref, cm_ref, pr_ref, pi_ref, d_ref, y_ref, z_ref, st_ref, carry):
        @pl.when(pl.program_id(1) == 0)
        def _():
            carry[...] = jnp.zeros_like(carry)

        cv = carry[...]
        st_ref[...] = cv
        u = u_ref[...]
        bu = jnp.dot(u.astype(BF16), bm_ref[...], preferred_element_type=F32)
        row = lax.broadcasted_iota(jnp.int32, (t, 128), 0)
        tile = lambda v, j: v[:, 128 * j:128 * (j + 1)]
        xr, xi = _s5_scan([tile(bu, 2 * k) for k in range(4)], [tile(bu, 2 * k + 1) for k in range(4)],
                          pr_ref[...], pi_ref[...], [tile(cv, 2 * k)[0:1] for k in range(4)],
                          [tile(cv, 2 * k + 1)[0:1] for k in range(4)], row, False)
        xall = jnp.concatenate([v for k in range(4) for v in (xr[k], xi[k])], axis=1)
        carry[...] = jnp.broadcast_to(xall[t - 1:t, :], (8, 1024))
        y = jnp.dot(xall.astype(BF16), cm_ref[...], preferred_element_type=F32) + d_ref[...] * u
        y_ref[...] = y
        z_ref[...] = _gelu(y).astype(BF16)

    return _call_hosting(
        side, body, name, (8, nch),
        [
            pl.BlockSpec((t, 128), lambda g, c: (c, g)),
            pl.BlockSpec((None, 128, 1024), lambda g, c: (g, 0, 0)),
            pl.BlockSpec((None, 1024, 128), lambda g, c: (g, 0, 0)),
            pl.BlockSpec((None, 8, 512), lambda g, c: (g, 0, 0)),
            pl.BlockSpec((None, 8, 512), lambda g, c: (g, 0, 0)),
            pl.BlockSpec((None, 1, 128), lambda g, c: (g, 0, 0)),
        ],
        [
            pl.BlockSpec((t, 128), lambda g, c: (c, g)),
            pl.BlockSpec((t, 128), lambda g, c: (c, g)),
            pl.BlockSpec((None, None, 8, 1024), lambda g, c: (g, c, 0, 0)),
        ],
        [_sds((n_rows, S5_WIDTH), F32), _sds((n_rows, S5_WIDTH), BF16), _sds((8, nch, 8, 1024), F32)],
        [pltpu.VMEM((8, 1024), F32)], ("parallel", "arbitrary"), (proj, bm8, cm8, p1, p2, d8))


def _s5_bwd(name, proj, dy, states, mats):
    bm8, cm8, p1, p2, d8 = mats
    n_rows = proj.shape[0]
    t = S5_CHUNK
    nch = n_rows // t
    nt_dims = (((1,), (1,)), ((), ()))
    tn_dims = (((0,), (0,)), ((), ()))

    def body(u_ref, dy_ref, st_ref, bm_ref, cm_ref, pr_ref, pi_ref, d_ref,
             du_ref, dbm_ref, dcm_ref, da_ref, dd_ref, gcarry):
        @pl.when(pl.program_id(1) == 0)
        def _():
            gcarry[...] = jnp.zeros_like(gcarry)
            dbm_ref[...] = jnp.zeros_like(dbm_ref)
            dcm_ref[...] = jnp.zeros_like(dcm_ref)
            da_ref[...] = jnp.zeros_like(da_ref)
            dd_ref[...] = jnp.zeros_like(dd_ref)

        u = u_ref[...]
        dyv = dy_ref[...]
        ub, dyb = u.astype(BF16), dyv.astype(BF16)
        bu = jnp.dot(ub, bm_ref[...], preferred_element_type=F32)
        dxd = lax.dot_general(dyb, cm_ref[...], nt_dims, preferred_element_type=F32)
        row = lax.broadcasted_iota(jnp.int32, (t, 128), 0)
        tile = lambda v, j: v[:, 128 * j:128 * (j + 1)]
        prv, piv, cv, gv = pr_ref[...], pi_ref[...], st_ref[...], gcarry[...]
        cr = [tile(cv, 2 * k)[0:1] for k in range(4)]
        ci = [tile(cv, 2 * k + 1)[0:1] for k in range(4)]
        xr, xi = _s5_scan([tile(bu, 2 * k) for k in range(4)], [tile(bu, 2 * k + 1) for k in range(4)],
                          prv, piv, cr, ci, row, False)
        gr, gi = _s5_scan([tile(dxd, 2 * k) for k in range(4)], [tile(dxd, 2 * k + 1) for k in range(4)],
                          prv, piv, [tile(gv, 2 * k)[0:1] for k in range(4)],
                          [tile(gv, 2 * k + 1)[0:1] for k in range(4)], row, True)
        dar, dai = [], []
        for k in range(4):
            xpr = jnp.where(row >= 1, pltpu.roll(xr[k], 1, 0), cr[k])
            xpi = jnp.where(row >= 1, pltpu.roll(xi[k], 1, 0), ci[k])
            dar.append(_colsum(gr[k] * xpr + gi[k] * xpi))
            dai.append(_colsum(gi[k] * xpr - gr[k] * xpi))
        xall = jnp.concatenate([v for k in range(4) for v in (xr[k], xi[k])], axis=1).astype(BF16)
        gf = jnp.concatenate([v for k in range(4) for v in (gr[k], gi[k])], axis=1)
        gcarry[...] = jnp.broadcast_to(gf[0:1, :], (8, 1024))
        gall = gf.astype(BF16)
        dcm_ref[...] += lax.dot_general(xall, dyb, tn_dims, preferred_element_type=F32)
        dbm_ref[...] += lax.dot_general(ub, gall, tn_dims, preferred_element_type=F32)
        du = lax.dot_general(gall, bm_ref[...], nt_dims, preferred_element_type=F32) + d_ref[...] * dyv
        du_ref[...] = du.astype(BF16)
        dd_ref[...] += _colsum(dyv * u)
        da_ref[0:1, :] += jnp.concatenate(dar, axis=1)
        da_ref[1:2, :] += jnp.concatenate(dai, axis=1)

    rev = lambda g, c: (nch - 1 - c, g)
    return pl.pallas_call(
        body, name=name,
        grid=(8, nch),
        in_specs=[
            pl.BlockSpec((t, 128), rev),
            pl.BlockSpec((t, 128), rev),
            pl.BlockSpec((None, None, 8, 1024), lambda g, c: (g, nch - 1 - c, 0, 0)),
            pl.BlockSpec((None, 128, 1024), lambda g, c: (g, 0, 0)),
            pl.BlockSpec((None, 1024, 128), lambda g, c: (g, 0, 0)),
            pl.BlockSpec((None, 8, 512), lambda g, c: (g, 0, 0)),
            pl.BlockSpec((None, 8, 512), lambda g, c: (g, 0, 0)),
            pl.BlockSpec((None, 1, 128), lambda g, c: (g, 0, 0)),
        ],
        out_specs=[
            pl.BlockSpec((t, 128), rev),
            pl.BlockSpec((None, 128, 1024), lambda g, c: (g, 0, 0)),
            pl.BlockSpec((None, 1024, 128), lambda g, c: (g, 0, 0)),
            pl.BlockSpec((None, 8, 512), lambda g, c: (g, 0, 0)),
            pl.BlockSpec((None, 1, 128), lambda g, c: (g, 0, 0)),
        ],
        out_shape=[_sds((n_rows, S5_WIDTH), BF16), _sds((8, 128, 1024), F32), _sds((8, 1024, 128), F32),
                   _sds((8, 8, 512), F32), _sds((8, 1, 128), F32)],
        scratch_shapes=[pltpu.VMEM((8, 1024), F32)],
        compiler_params=_params(("parallel", "arbitrary")),
    )(proj, dy, states, bm8, cm8, p1, p2, d8)


def _hgrn_lower_bounds(lb_param):
    p = jax.nn.softmax(lb_param, axis=0)
    return jnp.cumsum(p, axis=0) - p[0:1]


def _prefix16(x, r16):
    for s in (1, 2, 4, 8):
        x = x + jnp.where(r16 >= s, pltpu.roll(x, s, 0), 0.0)
    return x


def _suffix16(x, r16):
    n = x.shape[0]
    for s in (1, 2, 4, 8):
        x = x + jnp.where(r16 < HGRN_SUB - s, pltpu.roll(x, n - s, 0), 0.0)
    return x


_NT = (((1,), (1,)), ((), ()))
_TN = (((0,), (0,)), ((), ()))


def _dotf(a, b, dims=(((1,), (0,)), ((), ()))):
    return lax.dot_general(a.astype(BF16), b.astype(BF16), dims, preferred_element_type=F32)


def _hgrn_specs(n_blocks, rev):
    r = HGRN_BLOCK
    blk = (lambda b: n_blocks - 1 - b) if rev else (lambda b: b)
    proj_specs = [pl.BlockSpec((r, 128), (lambda h, b, c=c: (blk(b), 8 * c + h))) for c in (1, 2, 3, 4)]
    lb_spec = pl.BlockSpec((None, 1, 128), lambda h, b: (h, 0, 0))
    gain_spec = pl.BlockSpec((1, 128), lambda h, b: (0, 0))
    row_spec = pl.BlockSpec((r, 128), lambda h, b: (blk(b), h))
    st_spec = pl.BlockSpec((None, None, 128, 128), lambda h, b: (h, blk(b), 0, 0))
    return proj_specs, lb_spec, gain_spec, row_spec, st_spec, blk


def _hgrn_fwd(name, proj, lb, gain, side=None):
    n_rows = proj.shape[0]
    r = HGRN_BLOCK
    nb = n_rows // r
    nsub = r // HGRN_SUB
    proj_specs, lb_spec, gain_spec, row_spec, st_spec, _ = _hgrn_specs(nb, False)

    def body(q_ref, f_ref, i_ref, g_ref, lb_ref, gain_ref, o_ref, y_ref, st_ref, st_scr):
        @pl.when(pl.program_id(1) == 0)
        def _():
            st_scr[...] = jnp.zeros_like(st_scr)

        st_ref[...] = st_scr[...]
        q, f, v, g = q_ref[...], f_ref[...], i_ref[...], g_ref[...]
        lbv = lb_ref[...]
        qs = q * _sigmoid(q)
        fg = lbv + (1.0 - lbv) * _sigmoid(f)
        kk = 1.0 - fg
        r16 = lax.broadcasted_iota(jnp.int32, (r, 128), 0) & (HGRN_SUB - 1)
        b = _prefix16(jnp.log(fg), r16)
        qh = qs * jnp.exp(b)
        rs = lax.broadcasted_iota(jnp.int32, (HGRN_SUB, 128), 0)
        st = st_scr[...]
        outs = []
        for i in range(nsub):
            sl = slice(HGRN_SUB * i, HGRN_SUB * (i + 1))
            qsi, kki, vi, bi = qs[sl], kk[sl], v[sl], b[sl]
            o_i = _dotf(qh[sl], st, _NT)
            for s in range(HGRN_SUB):
                e = jnp.exp(jnp.where(rs >= s, bi - bi[s:s + 1], NEG))
                col = jnp.sum(qsi * e * kki[s:s + 1], axis=1, keepdims=True)
                o_i = o_i + col * vi[s:s + 1]
            bl = bi[HGRN_SUB - 1:HGRN_SUB]
            st = st * jnp.exp(bl) + _dotf(vi, kki * jnp.exp(bl - bi), _TN)
            outs.append(o_i)
        st_scr[...] = st
        o = jnp.concatenate(outs, axis=0)
        o_ref[...] = o
        rn = lax.rsqrt(jnp.mean(o * o, axis=1, keepdims=True) + EPS)
        y_ref[...] = (o * rn * gain_ref[...] * (g * _sigmoid(g))).astype(BF16)

    return _call_hosting(
        side, body, name, (HGRN_HEADS, nb), proj_specs + [lb_spec, gain_spec], [row_spec, row_spec, st_spec],
        [_sds((n_rows, HGRN_WIDTH), F32), _sds((n_rows, HGRN_WIDTH), BF16), _sds((HGRN_HEADS, nb, 128, 128), F32)],
        [pltpu.VMEM((128, 128), F32)], ("parallel", "arbitrary"), (proj, proj, proj, proj, lb, gain))


def _hgrn_bwd(name, proj, lb, gain, o_saved, states, dycat):
    n_rows = proj.shape[0]
    r = HGRN_BLOCK
    nb = n_rows // r
    nsub = r // HGRN_SUB
    proj_specs, lb_spec, gain_spec, row_spec, st_spec, blk = _hgrn_specs(nb, True)
    dy_spec = pl.BlockSpec((r, 128), lambda h, b: (blk(b), 8 + h))
    acc_spec = pl.BlockSpec((None, 1, 128), lambda h, b: (h, 0, 0))

    def body(q_ref, f_ref, i_ref, g_ref, lb_ref, gain_ref, o_ref, st_ref, dy_ref,
             dq_ref, df_ref, di_ref, dg_ref, dlb_ref, dgain_ref, dst_scr, sub_scr):
        @pl.when(pl.program_id(1) == 0)
        def _():
            dst_scr[...] = jnp.zeros_like(dst_scr)
            dlb_ref[...] = jnp.zeros_like(dlb_ref)
            dgain_ref[...] = jnp.zeros_like(dgain_ref)

        q, f, v, g = q_ref[...], f_ref[...], i_ref[...], g_ref[...]
        lbv, gain_v = lb_ref[...], gain_ref[...]
        sq = _sigmoid(q)
        qs = q * sq
        sf = _sigmoid(f)
        fg = lbv + (1.0 - lbv) * sf
        kk = 1.0 - fg
        r16 = lax.broadcasted_iota(jnp.int32, (r, 128), 0) & (HGRN_SUB - 1)
        b = _prefix16(jnp.log(fg), r16)
        eb = jnp.exp(b)
        qh = qs * eb

        o, dy = o_ref[...], dy_ref[...]
        rn = lax.rsqrt(jnp.mean(o * o, axis=1, keepdims=True) + EPS)
        on = o * rn
        sg = _sigmoid(g)
        sil = g * sg
        dgain_ref[...] += _colsum(dy * on * sil)
        dg_ref[...] = (dy * on * gain_v * (sg * (1.0 + g * (1.0 - sg)))).astype(BF16)
        don = dy * gain_v * sil
        do = rn * (don - on * jnp.mean(don * on, axis=1, keepdims=True))

        st = st_ref[...]
        for i in range(nsub):
            sl = slice(HGRN_SUB * i, HGRN_SUB * (i + 1))
            sub_scr[i] = st
            bi = b[sl]
            bl = bi[HGRN_SUB - 1:HGRN_SUB]
            st = st * jnp.exp(bl) + _dotf(v[sl], kk[sl] * jnp.exp(bl - bi), _TN)

        rs = lax.broadcasted_iota(jnp.int32, (HGRN_SUB, 128), 0)
        dst = dst_scr[...]
        parts = [None] * nsub
        for i in reversed(range(nsub)):
            sl = slice(HGRN_SUB * i, HGRN_SUB * (i + 1))
            sp = sub_scr[i]
            qsi, kki, vi, bi, doi, qhi = qs[sl], kk[sl], v[sl], b[sl], do[sl], qh[sl]
            bl = bi[HGRN_SUB - 1:HGRN_SUB]
            ebl = jnp.exp(bl)
            dec = jnp.exp(bl - bi)
            khat = kki * dec
            dqh = _dotf(doi, sp)
            dkhat = _dotf(vi, dst)
            dv = _dotf(khat, dst, _NT)
            zrow = _colsum(sp * dst) * ebl
            dq_in = jnp.zeros((HGRN_SUB, 128), F32)
            dk_in = jnp.zeros((HGRN_SUB, 128), F32)
            dv_in = jnp.zeros((HGRN_SUB, 128), F32)
            for s in range(HGRN_SUB):
                e = jnp.exp(jnp.where(rs >= s, bi - bi[s:s + 1], NEG))
                dpc = jnp.sum(doi * vi[s:s + 1], axis=1, keepdims=True)
                w = qsi * e
                pc = jnp.sum(w * kki[s:s + 1], axis=1, keepdims=True)
                dq_in = dq_in + dpc * e * kki[s:s + 1]
                dk_in = jnp.where(rs == s, _colsum(dpc * w), dk_in)
                dv_in = jnp.where(rs == s, _colsum(pc * doi), dv_in)
            kd = khat * dkhat
            parts[i] = (qsi * dq_in - kki * dk_in + qhi * dqh, kd, jnp.broadcast_to(zrow, (HGRN_SUB, 128)),
                        dq_in + dqh * eb[sl], dk_in + dkhat * dec, dv + dv_in)
            dst = dst * ebl + _dotf(doi, qhi, _TN)
        dst_scr[...] = dst

        cat = lambda j: jnp.concatenate([p[j] for p in parts], axis=0)
        d_b, kd, zr, dqs, dkk, dvv = (cat(j) for j in range(6))
        dlf = _suffix16(d_b, r16) + _prefix16(kd, r16) - kd + zr
        dfg = dlf / fg - dkk
        df_ref[...] = (dfg * (1.0 - lbv) * sf * (1.0 - sf)).astype(BF16)
        dlb_ref[...] += _colsum(dfg * (1.0 - sf))
        dq_ref[...] = (dqs * (sq * (1.0 + q * (1.0 - sq)))).astype(BF16)
        di_ref[...] = dvv.astype(BF16)

    return pl.pallas_call(
        body, name=name,
        grid=(HGRN_HEADS, nb),
        in_specs=proj_specs + [lb_spec, gain_spec, row_spec, st_spec, dy_spec],
        out_specs=[row_spec] * 4 + [acc_spec, acc_spec],
        out_shape=[_sds((n_rows, HGRN_WIDTH), BF16)] * 4 + [_sds((HGRN_HEADS, 1, 128), F32)] * 2,
        scratch_shapes=[pltpu.VMEM((128, 128), F32), pltpu.VMEM((nsub, 128, 128), F32)],
        compiler_params=_params(("parallel", "arbitrary")),
    )(proj, proj, proj, proj, lb, gain, o_saved, states, dycat)


def _alibi_slopes():
    return jnp.exp2(-8.0 * jnp.arange(1, ATT_HEADS + 1, dtype=F32) / ATT_HEADS)


def _swa_specs(n_blocks):
    blk = ATT_BLOCK
    prev = lambda i: jnp.maximum(i - 1, 0)
    smem = pl.BlockSpec(memory_space=pltpu.SMEM)
    return [
        smem, smem,
        pl.BlockSpec((blk, ATT_HEADS * ATT_DIM), lambda i: (i, 0)),
        pl.BlockSpec((blk, 256), lambda i: (i, 8)),
        pl.BlockSpec((blk, 256), lambda i: (prev(i), 8)),
        pl.BlockSpec((blk, 256), lambda i: (i, 9)),
        pl.BlockSpec((blk, 256), lambda i: (prev(i), 9)),
        pl.BlockSpec((1, ATT_DIM), lambda i: (0, 0)),
        pl.BlockSpec((1, ATT_DIM), lambda i: (0, 0)),
    ]


_ATT_GROUP = ATT_HEADS // ATT_KV
_ATT_ROWS = _ATT_GROUP * ATT_BLOCK


def _swa_mask(i):
    t_i = lax.broadcasted_iota(jnp.int32, (_ATT_ROWS, 2 * ATT_BLOCK), 0) & (ATT_BLOCK - 1)
    s_i = lax.broadcasted_iota(jnp.int32, (_ATT_ROWS, 2 * ATT_BLOCK), 1)
    dist = t_i + ATT_BLOCK - s_i
    valid = (dist >= 0) & (dist < ATT_BLOCK) & ((s_i >= ATT_BLOCK) | (i > 0))
    return valid, dist.astype(F32)


def _stack_heads(x):
    return jnp.concatenate([x[:, ATT_DIM * h:ATT_DIM * (h + 1)] for h in range(_ATT_GROUP)], axis=0)


def _unstack_heads(x):
    return jnp.concatenate([x[ATT_BLOCK * h:ATT_BLOCK * (h + 1)] for h in range(_ATT_GROUP)], axis=1)


def _head_column(ref, g):
    return jnp.concatenate([jnp.full((ATT_BLOCK, 1), ref[_ATT_GROUP * g + h], F32) for h in range(_ATT_GROUP)], axis=0)


def _swa_probs(qn, kn, slope, sink, valid, distf):
    s = lax.dot_general(qn, kn, _NT, preferred_element_type=F32) * (ATT_DIM ** -0.5) - slope * distf
    s = jnp.where(valid, s, NEG)
    m = jnp.maximum(jnp.max(s, axis=1, keepdims=True), sink)
    p = jnp.exp(s - m)
    es = jnp.exp(sink - m)
    inv = 1.0 / (jnp.sum(p, axis=1, keepdims=True) + es)
    return p * inv, es * inv


def _swa_fwd(name, qkv, q_gain, k_gain, sinks, slopes):
    n_rows = qkv.shape[0]
    nb = n_rows // ATT_BLOCK

    def body(sink_ref, slope_ref, q_ref, kc_ref, kp_ref, vc_ref, vp_ref, qg_ref, kg_ref, o_ref):
        i = pl.program_id(0)
        kb = jnp.concatenate([kp_ref[...], kc_ref[...]], axis=0)
        vb = jnp.concatenate([vp_ref[...], vc_ref[...]], axis=0)
        valid, distf = _swa_mask(i)
        qgv, kgv = qg_ref[...], kg_ref[...]
        gw = _ATT_GROUP * ATT_DIM
        for g in range(ATT_KV):
            kg = kb[:, 64 * g:64 * (g + 1)]
            rk = lax.rsqrt(jnp.mean(kg * kg, axis=1, keepdims=True) + EPS)
            kn = (kg * rk * kgv).astype(BF16)
            vv = vb[:, 64 * g:64 * (g + 1)].astype(BF16)
            qs = _stack_heads(q_ref[:, gw * g:gw * (g + 1)])
            rq = lax.rsqrt(jnp.mean(qs * qs, axis=1, keepdims=True) + EPS)
            pn, _ = _swa_probs((qs * rq * qgv).astype(BF16), kn, _head_column(slope_ref, g),
                               _head_column(sink_ref, g), valid, distf)
            out = jnp.dot(pn.astype(BF16), vv, preferred_element_type=F32)
            o_ref[:, gw * g:gw * (g + 1)] = _unstack_heads(out).astype(BF16)

    return pl.pallas_call(
        body, name=name,
        grid=(nb,),
        in_specs=_swa_specs(nb),
        out_specs=pl.BlockSpec((ATT_BLOCK, ATT_HEADS * ATT_DIM), lambda i: (i, 0)),
        out_shape=_sds((n_rows, ATT_HEADS * ATT_DIM), BF16),
        compiler_params=_params(("parallel",)),
    )(sinks, slopes, qkv, qkv, qkv, qkv, qkv, q_gain.reshape(1, -1), k_gain.reshape(1, -1))


def _swa_bwd(name, qkv, q_gain, k_gain, sinks, slopes, d_out):
    n_rows = qkv.shape[0]
    nb = n_rows // ATT_BLOCK
    blk = ATT_BLOCK

    def body(sink_ref, slope_ref, q_ref, kc_ref, kp_ref, vc_ref, vp_ref, qg_ref, kg_ref, do_ref,
             dq_ref, dkc_ref, dkp_ref, dvc_ref, dvp_ref, dsink_ref, dqg_ref, dkg_ref):
        i = pl.program_id(0)

        @pl.when(i == 0)
        def _():
            dsink_ref[...] = jnp.zeros_like(dsink_ref)
            dqg_ref[...] = jnp.zeros_like(dqg_ref)
            dkg_ref[...] = jnp.zeros_like(dkg_ref)

        kb = jnp.concatenate([kp_ref[...], kc_ref[...]], axis=0)
        vb = jnp.concatenate([vp_ref[...], vc_ref[...]], axis=0)
        kgv, qgv = kg_ref[...], qg_ref[...]
        valid, distf = _swa_mask(i)
        scale = ATT_DIM ** -0.5
        gw = _ATT_GROUP * ATT_DIM
        dks, dvs = [], []
        dqg, dkg = jnp.zeros((1, ATT_DIM), F32), jnp.zeros((1, ATT_DIM), F32)
        for g in range(ATT_KV):
            kg = kb[:, 64 * g:64 * (g + 1)]
            rk = lax.rsqrt(jnp.mean(kg * kg, axis=1, keepdims=True) + EPS)
            khat = kg * rk
            kn = (khat * kgv).astype(BF16)
            vv = vb[:, 64 * g:64 * (g + 1)].astype(BF16)
            qs = _stack_heads(q_ref[:, gw * g:gw * (g + 1)])
            rq = lax.rsqrt(jnp.mean(qs * qs, axis=1, keepdims=True) + EPS)
            qhat = qs * rq
            qn = (qhat * qgv).astype(BF16)
            pn, ps = _swa_probs(qn, kn, _head_column(slope_ref, g), _head_column(sink_ref, g), valid, distf)
            dos = _stack_heads(do_ref[:, gw * g:gw * (g + 1)]).astype(BF16)
            dp = lax.dot_general(dos, vv, _NT, preferred_element_type=F32)
            delta = jnp.sum(pn * dp, axis=1, keepdims=True)
            ds = (pn * (dp - delta)).astype(BF16)
            sd = ps * delta
            for h in range(_ATT_GROUP):
                hs = _ATT_GROUP * g + h
                dsink_ref[hs:hs + 1, :] += jnp.zeros((1, 128), F32) - jnp.sum(sd[blk * h:blk * (h + 1)])
            dvs.append(lax.dot_general(pn.astype(BF16), dos, _TN, preferred_element_type=F32))
            dkn = lax.dot_general(ds, qn, _TN, preferred_element_type=F32) * scale
            dqn = jnp.dot(ds, kn, preferred_element_type=F32) * scale
            dqg = dqg + _colsum(dqn * qhat)
            dqhat = dqn * qgv
            dqs = rq * (dqhat - qhat * jnp.mean(dqhat * qhat, axis=1, keepdims=True))
            dq_ref[:, gw * g:gw * (g + 1)] = _unstack_heads(dqs).astype(BF16)
            dkg = dkg + _colsum(dkn * khat)
            dkhat = dkn * kgv
            dks.append(rk * (dkhat - khat * jnp.mean(dkhat * khat, axis=1, keepdims=True)))
        dqg_ref[...] += dqg
        dkg_ref[...] += dkg
        dk = jnp.concatenate(dks, axis=1).astype(BF16)
        dv = jnp.concatenate(dvs, axis=1).astype(BF16)
        dkp_ref[...] = dk[:blk]
        dkc_ref[...] = dk[blk:]
        dvp_ref[...] = dv[:blk]
        dvc_ref[...] = dv[blk:]

    kv_spec = pl.BlockSpec((blk, 256), lambda i: (i, 0))
    full = pl.BlockSpec((blk, ATT_HEADS * ATT_DIM), lambda i: (i, 0))
    acc64 = pl.BlockSpec((1, ATT_DIM), lambda i: (0, 0))
    return pl.pallas_call(
        body, name=name,
        grid=(nb,),
        in_specs=_swa_specs(nb) + [full],
        out_specs=[full, kv_spec, kv_spec, kv_spec, kv_spec,
                   pl.BlockSpec((ATT_HEADS, 128), lambda i: (0, 0)), acc64, acc64],
        out_shape=[_sds((n_rows, ATT_HEADS * ATT_DIM), BF16)] + [_sds((n_rows, 256), BF16)] * 4
        + [_sds((ATT_HEADS, 128), F32), _sds((1, ATT_DIM), F32), _sds((1, ATT_DIM), F32)],
        compiler_params=_params(("arbitrary",)),
    )(sinks, slopes, qkv, qkv, qkv, qkv, qkv, q_gain.reshape(1, -1), k_gain.reshape(1, -1), d_out)


def _mesh_pos():
    return lax.axis_index("x"), lax.axis_index("y"), lax.axis_index("c")


_ANY = pl.BlockSpec(memory_space=pl.ANY)


def _allgather(name, shards):
    side = _gather_side(shards)
    n = len(shards)

    def body(*refs):
        side.start(refs[:n], refs[n:2 * n], refs[2 * n:])
        side.finish(refs[:n], refs[n:2 * n], refs[2 * n:])

    return pl.pallas_call(
        body, name=name,
        out_shape=side.out_shapes,
        in_specs=[_ANY] * n,
        out_specs=[_ANY] * n,
        scratch_shapes=side.scratch,
    )(*shards)


class _Side:
    def __init__(self, operands, out_shapes, scratch, start, finish):
        self.operands, self.out_shapes, self.scratch = list(operands), list(out_shapes), list(scratch)
        self.start, self.finish = start, finish


def _gather_side(shards):
    n = len(shards)

    def plan(x_refs, out_refs, sems):
        send_sems, recv_sems, local_sems = sems
        x, y, c = _mesh_pos()
        me, sibling = (x, y, c), (x, y, 1 - c)
        chips = [(1 - x, y), (x, 1 - y), (1 - x, 1 - y)]

        def slot(a, px, py, pc):
            return out_refs[a].at[4 * px + 2 * py + pc]

        def copy(a, k, block, to, src=None):
            return pltpu.make_async_remote_copy(
                src_ref=slot(a, *block) if src is None else src, dst_ref=slot(a, *block),
                send_sem=send_sems.at[7 * a + k], recv_sem=recv_sems.at[7 * a + k],
                device_id=to, device_id_type=MESH)

        local = [pltpu.make_async_copy(x_refs[a], slot(a, *me), local_sems.at[a]) for a in range(n)]
        first = [[copy(a, 0, me, sibling, src=x_refs[a])]
                 + [copy(a, 1 + j, me, (*chip, c), src=x_refs[a]) for j, chip in enumerate(chips)] for a in range(n)]
        from_chips = [[copy(a, 1 + j, (*chip, c), me) for j, chip in enumerate(chips)] for a in range(n)]
        forward = [[copy(a, 4 + j, (*chip, c), sibling) for j, chip in enumerate(chips)] for a in range(n)]
        from_sibling = [[copy(a, 0, sibling, me)] + [copy(a, 4 + j, (*chip, 1 - c), me) for j, chip in enumerate(chips)]
                        for a in range(n)]
        return local, first, from_chips, forward, from_sibling

    def start(x_refs, out_refs, sems):
        local, first, _, _, _ = plan(x_refs, out_refs, sems)
        for a in range(n):
            local[a].start()
            for cp in first[a]:
                cp.start()

    def finish(x_refs, out_refs, sems):
        local, first, from_chips, forward, from_sibling = plan(x_refs, out_refs, sems)
        for a in range(n):
            for j in range(3):
                from_chips[a][j].wait_recv()
                forward[a][j].start()
        for a in range(n):
            for cp in from_sibling[a]:
                cp.wait_recv()
        for a in range(n):
            for cp in first[a] + forward[a]:
                cp.wait_send()
            local[a].wait()

    return _Side(shards, [_sds((N_DEV,) + s.shape, s.dtype) for s in shards],
                 [pltpu.SemaphoreType.DMA((7 * n,)), pltpu.SemaphoreType.DMA((7 * n,)), pltpu.SemaphoreType.DMA((n,))],
                 start, finish)


def _swap_with_sibling(name, arrs):
    n = len(arrs)

    def body(*refs):
        x_refs, got_refs = refs[:n], refs[n:2 * n]
        send_sems, recv_sems = refs[2 * n:]
        x, y, c = _mesh_pos()
        copies = []
        for a in range(n):
            for j in range(4):
                k = 4 * a + j
                cp = pltpu.make_async_remote_copy(
                    src_ref=x_refs[a].at[j, 1 - c], dst_ref=got_refs[a].at[j],
                    send_sem=send_sems.at[k], recv_sem=recv_sems.at[k],
                    device_id=(x, y, 1 - c), device_id_type=MESH)
                cp.start()
                copies.append(cp)
        for cp in copies:
            cp.wait()

    return pl.pallas_call(
        body, name=name,
        out_shape=[_sds((4,) + t.shape[2:], t.dtype) for t in arrs],
        in_specs=[_ANY] * n,
        out_specs=[_ANY] * n,
        scratch_shapes=[pltpu.SemaphoreType.DMA((4 * n,))] * 2,
    )(*arrs)


def _exchange_chips(name, arrs):
    side = _exchange_side(arrs)
    n = len(arrs)

    def body(*refs):
        side.start(refs[:n], refs[n:2 * n], refs[2 * n:])
        side.finish(refs[:n], refs[n:2 * n], refs[2 * n:])

    return pl.pallas_call(
        body, name=name,
        out_shape=side.out_shapes,
        in_specs=[_ANY] * n,
        out_specs=[_ANY] * n,
        scratch_shapes=side.scratch,
    )(*arrs)


def _exchange_side(arrs):
    n = len(arrs)

    def plan(x_refs, out_refs, sems):
        send_sems, recv_sems, local_sems = sems
        x, y, c = _mesh_pos()
        me = 2 * x + y
        local = [pltpu.make_async_copy(x_refs[a].at[me], out_refs[a].at[me], local_sems.at[a]) for a in range(n)]
        sends, recvs = [], []
        for a in range(n):
            for k in range(1, 4):
                px, py = x ^ (k >> 1), y ^ (k & 1)
                peer = 2 * px + py
                sem = 3 * a + k - 1
                sends.append(pltpu.make_async_remote_copy(
                    src_ref=x_refs[a].at[peer], dst_ref=out_refs[a].at[me],
                    send_sem=send_sems.at[sem], recv_sem=recv_sems.at[sem],
                    device_id=(px, py, c), device_id_type=MESH))
                recvs.append(pltpu.make_async_remote_copy(
                    src_ref=x_refs[a].at[peer], dst_ref=out_refs[a].at[peer],
                    send_sem=send_sems.at[sem], recv_sem=recv_sems.at[sem],
                    device_id=(px, py, c), device_id_type=MESH))
        return local, sends, recvs

    def start(x_refs, out_refs, sems):
        local, sends, _ = plan(x_refs, out_refs, sems)
        for cp in local + sends:
            cp.start()

    def finish(x_refs, out_refs, sems):
        local, sends, recvs = plan(x_refs, out_refs, sems)
        for cp in recvs:
            cp.wait_recv()
        for cp in sends:
            cp.wait_send()
        for cp in local:
            cp.wait()

    return _Side(arrs, [_sds(t.shape, t.dtype) for t in arrs],
                 [pltpu.SemaphoreType.DMA((3 * n,)), pltpu.SemaphoreType.DMA((3 * n,)), pltpu.SemaphoreType.DMA((n,))],
                 start, finish)


def _sum_blocks(name, blocks, out_dtype=F32):
    n, n_rows, n_cols = blocks.shape
    tr = _row_tile(n_rows)

    def body(x_ref, o_ref):
        acc = x_ref[0].astype(F32)
        for s in range(1, n):
            acc = acc + x_ref[s].astype(F32)
        o_ref[...] = acc.astype(o_ref.dtype)

    return pl.pallas_call(
        body, name=name,
        grid=(n_rows // tr,),
        in_specs=[pl.BlockSpec((n, tr, n_cols), lambda i: (0, i, 0))],
        out_specs=pl.BlockSpec((tr, n_cols), lambda i: (i, 0)),
        out_shape=_sds((n_rows, n_cols), out_dtype),
        compiler_params=_params(("parallel",)),
    )(blocks)


def _add_pair(name, mine, got, core):
    n, n_rows, n_cols = got.shape
    tr = _row_tile(n_rows)

    def body(core_ref, a_ref, b_ref, o_ref):
        o_ref[...] = (a_ref[...].astype(F32) + b_ref[...].astype(F32)).astype(BF16)

    spec = pl.BlockSpec((None, tr, n_cols), lambda j, i, core_ref: (j, i, 0))
    return pl.pallas_call(
        body, name=name,
        grid_spec=pltpu.PrefetchScalarGridSpec(
            num_scalar_prefetch=1,
            grid=(n, n_rows // tr),
            in_specs=[pl.BlockSpec((None, None, tr, n_cols), lambda j, i, core_ref: (j, core_ref[0], i, 0)), spec],
            out_specs=spec,
        ),
        out_shape=_sds(got.shape, BF16),
        compiler_params=_params(("parallel", "parallel")),
    )(core, mine, got)


_MIX_PARTS = (
    (("w_in", "even_w_in", 1, D_MODEL, 5120), ("w_glu", "s5_w_glu", 0, S5_WIDTH, S5_WIDTH),
     ("w_out", "even_w_out", 0, D_MODEL, D_MODEL)),
    (("w_qkv", "odd_w_qkv", 1, D_MODEL, QKV_WIDTH), ("w_out", "odd_w_out", 0, D_MODEL, D_MODEL)),
)
_PACK_COLS = 1024
_FF_SHARD = D_FF // N_DEV
_BIG_NAMES = ("even_w_in", "s5_w_glu", "even_w_out", "odd_w_qkv", "odd_w_out")


def _part_rows(rows, cols):
    return rows * cols // N_DEV // _PACK_COLS


def _row_tile(n_rows):
    return next(t for t in (512, 480, 384, 256, 128) if n_rows % t == 0)


def _pack_mixer_shard(kind, j, args):
    return jnp.concatenate([args[name][j].astype(BF16).reshape(-1, _PACK_COLS) for _, name, _, _, _ in _MIX_PARTS[kind]],
                           axis=0)


def _unpack_mixer(kind, gathered):
    out, off = {}, 0
    for key, _, axis, rows, cols in _MIX_PARTS[kind]:
        n = _part_rows(rows, cols)
        part = gathered[:, off:off + n]
        off += n
        if axis == 1:
            part = part.reshape(N_DEV, rows, cols // N_DEV).transpose(1, 0, 2)
        out[key] = part.reshape(rows, cols)
    return out


def _pack_mixer_grads(kind, g):
    parts = []
    for key, _, axis, rows, cols in _MIX_PARTS[kind]:
        t = g[key]
        if axis == 1:
            t = t.reshape(rows, N_DEV, cols // N_DEV).transpose(1, 0, 2)
        parts.append(t.reshape(N_DEV, -1, _PACK_COLS))
    return jnp.concatenate(parts, axis=1)


def _unpack_mixer_grads(kind, flat):
    out, off = {}, 0
    for _, name, axis, rows, cols in _MIX_PARTS[kind]:
        n = _part_rows(rows, cols)
        shape = (rows, cols // N_DEV) if axis == 1 else (rows // N_DEV, cols)
        out[name] = flat[off:off + n].reshape(shape)
        off += n
    return out


def _pack_small(arrs, row_mult=512):
    parts = []
    for a in arrs:
        f = a.astype(F32).reshape(-1)
        parts.append(jnp.pad(f, (0, (-f.shape[0]) % 128)))
    f = jnp.concatenate(parts)
    f = jnp.pad(f, (0, (-f.shape[0]) % (128 * row_mult)))
    return f.reshape(-1, 128)


def _unpack_small(flat, shapes):
    f = flat.reshape(-1)
    out, off = [], 0
    for s in shapes:
        n = math.prod(s)
        out.append(f[off:off + n].reshape(s))
        off += n + (-n) % 128
    return out


_WEIGHTS = ("even_norm", "even_w_in", "s5_lambda_re", "s5_lambda_im", "s5_log_dt", "s5_b_re", "s5_b_im",
            "s5_c_re", "s5_c_im", "s5_d", "s5_w_glu", "s5_b_glu", "hgrn_lower_bound", "hgrn_o_norm",
            "even_w_out", "odd_norm", "odd_w_qkv", "q_norm", "k_norm", "att_sinks", "odd_w_out",
            "mlp_norm", "mlp_w_up", "mlp_w_down")
_MLP_NAMES = ("mlp_w_up", "mlp_w_down")
_SMALL_NAMES = tuple(n for n in _WEIGHTS if n not in _BIG_NAMES + _MLP_NAMES)


def _add_res(acc, res):
    return (acc + res,)


def _mm_hosting(side, *args, **kw):
    if side is None:
        return _mm(*args, **kw), None
    return _mm(*args, side=side, **kw)


def _mlp_fwd(h, gain, w_up, w_down, sides=(None, None)):
    n_rows, fs = h.shape[0], _FF_SHARD
    xn, rstd = _rms_fwd("rms_fwd", h, gain)
    (up, act), got_up = _mm_hosting(
        sides[0], "mm_up", xn, w_up, "nn", out_dtypes=(F32, BF16), mkn=(n_rows, D_MODEL, D_FF), tn=fs,
        b_block=pl.BlockSpec((None, D_MODEL, fs), lambda i, j, kk: (j, kk, 0)),
        epi=lambda acc: (acc, jnp.square(jnp.maximum(acc, 0.0))))
    out, got_down = _mm_hosting(
        sides[1], "mm_down", act, w_down.reshape(N_DEV // 2, 2 * fs, D_MODEL), "nn", mkn=(n_rows, D_FF, D_MODEL),
        tk=2 * fs, b_block=pl.BlockSpec((None, 2 * fs, 1024), lambda i, j, kk: (kk, 0, j)),
        epi=_add_res, extras=(h,))
    return out, (h, gain, xn, rstd, up, act, w_up, w_down), got_up, got_down


def _mlp_bwd(cache, dh, dhb, sides=(None, None, None)):
    h, gain, xn, rstd, up, act, w_up, w_down = cache
    n_rows, fs = h.shape[0], _FF_SHARD
    dup, got0 = _mm_hosting(
        sides[0], "mm_dact", dhb, w_down, "nt", out_dtypes=(BF16,), mkn=(n_rows, D_MODEL, D_FF), tn=fs,
        b_block=pl.BlockSpec((None, fs, D_MODEL), lambda i, j, kk: (j, 0, kk)),
        epi=lambda acc, u: (acc * (2.0 * jnp.maximum(u, 0.0)),), extras=(up,))
    dw_down, got1 = _mm_hosting(
        sides[1], "mm_dw_down", act, dhb, "tn", out_dtypes=(BF16,),
        o_block=(pl.BlockSpec((None, 512, 1024), lambda i, j, kk: (i // 2, i % 2, j)), (N_DEV, fs, D_MODEL)))
    dxn, got2 = _mm_hosting(
        sides[2], "mm_dxn_up", dup, w_up, "nt", mkn=(n_rows, D_FF, D_MODEL), tk=2 * fs,
        b_block=pl.BlockSpec((None, 1024, fs), lambda i, j, kk: (2 * kk, j, 0)),
        b2_block=pl.BlockSpec((None, 1024, fs), lambda i, j, kk: (2 * kk + 1, j, 0)))
    dw_up = _mm("mm_dw_up", xn, dup, "tn", out_dtypes=(BF16,),
                o_block=(pl.BlockSpec((None, 512, fs), lambda i, j, kk: (j, i, 0)), (N_DEV, D_MODEL, fs)))
    dh_in, dhb_in, dgain = _rms_bwd("rms_bwd", h, rstd, gain, dxn, dh)
    return dh_in, dhb_in, dgain, dw_up, dw_down, (got0, got1, got2)


def _even_fwd(h, p, sides=(None, None, None)):
    xn, rstd = _rms_fwd("rms_fwd", h, p["norm"])
    proj, got_in = _mm_hosting(sides[0], "mm_w_in", xn, p["w_in"], "nn")
    (y_pre, z, s5_states), got_s5 = _s5_fwd("s5_fwd", proj, p["mats"], sides[1])
    gate = _mm("mm_glu", z, p["w_glu"], "nn")
    (ya,) = _rowwise("glu_fwd", lambda y, gt, b: ((_gelu(y) * _sigmoid(gt + b),), ()),
                     [y_pre, gate], [p["b_glu"].reshape(1, -1)], [(S5_WIDTH, BF16)])
    (o, yb, h_states), got_h = _hgrn_fwd("hgrn_fwd", proj, p["lb"].reshape(8, 1, 128), p["o_gain"].reshape(1, 128),
                                         sides[2])
    ycat = jnp.concatenate([ya, yb], axis=1)
    out = _mm("mm_w_out", ycat, p["w_out"], "nn", epi=_add_res, extras=(h,))
    return out, (h, xn, rstd, proj, y_pre, z, s5_states, gate, o, h_states, ycat), (got_in, got_s5, got_h)


def _even_bwd(cache, p, dh, dhb, sides=(None, None)):
    h, xn, rstd, proj, y_pre, z, s5_states, gate, o, h_states, ycat = cache
    g = {}
    dycat = _mm("mm_dy_out", dhb, p["w_out"], "nt")
    g["w_out"] = _mm("mm_dw_out", ycat, dhb, "tn", out_dtypes=(BF16,))
    dq, df, di, dg, dlb, dgain = _hgrn_bwd("hgrn_bwd", proj, p["lb"].reshape(8, 1, 128),
                                           p["o_gain"].reshape(1, 128), o, h_states, dycat)
    g["lb"] = dlb.reshape(-1)
    g["o_gain"] = jnp.sum(dgain, axis=0).reshape(-1)

    def glu_bwd1(dyc, y, gt, b):
        zf = _gelu(y)
        s = _sigmoid(gt + b)
        dya = dyc[:, :S5_WIDTH]
        d_gate = dya * zf * s * (1.0 - s)
        return (d_gate, dya * s), (_colsum(d_gate),)

    d_gate, dz_direct, db_glu = _rowwise("glu_bwd_gate", glu_bwd1, [dycat, y_pre, gate], [p["b_glu"].reshape(1, -1)],
                                         [(S5_WIDTH, BF16), (S5_WIDTH, F32)], [S5_WIDTH])
    g["b_glu"] = db_glu.reshape(-1)
    dz_gate = _mm("mm_dz_glu", d_gate, p["w_glu"], "nt")
    g["w_glu"] = _mm("mm_dw_glu", z, d_gate, "tn", out_dtypes=(BF16,))
    (dy_pre,) = _rowwise("glu_bwd_gelu", lambda a, b, y: (((a + b) * _gelu_grad(y),), ()),
                         [dz_direct, dz_gate, y_pre], [], [(S5_WIDTH, F32)])
    du, dbm, dcm, da, dd = _s5_bwd("s5_bwd", proj, dy_pre, s5_states, p["mats"])
    g["s5"] = (dbm, dcm, da, dd)
    dproj = jnp.concatenate([du, dq, df, di, dg], axis=1)
    dxn, got0 = _mm_hosting(sides[0], "mm_dxn_in", dproj, p["w_in"], "nt", tk=2560)
    g["w_in"], got1 = _mm_hosting(sides[1], "mm_dw_in", xn, dproj, "tn", out_dtypes=(BF16,))
    dh_in, dhb_in, dnorm = _rms_bwd("rms_bwd", h, rstd, p["norm"], dxn, dh)
    g["norm"] = dnorm.reshape(-1)
    return dh_in, dhb_in, g, (got0, got1)


def _odd_fwd(h, p, sides=(None, None, None)):
    xn, rstd = _rms_fwd("rms_fwd", h, p["norm"])
    qkv, got = _mm_hosting(sides[0], "mm_w_qkv", xn, p["w_qkv"], "nn", tn=1280)
    o = _swa_fwd("swa_fwd", qkv, p["q_gain"], p["k_gain"], p["sinks"], p["slopes"])
    out = _mm("mm_w_out", o, p["w_out"], "nn", epi=_add_res, extras=(h,))
    return out, (h, xn, rstd, qkv, o), (got, None, None)


def _shift_up_block(x):
    return jnp.concatenate([x[ATT_BLOCK:], jnp.zeros((ATT_BLOCK, x.shape[1]), x.dtype)], axis=0)


def _odd_bwd(cache, p, dh, dhb, sides=(None, None)):
    h, xn, rstd, qkv, o = cache
    g = {}
    d_o = _mm("mm_dy_out", dhb, p["w_out"], "nt")
    g["w_out"] = _mm("mm_dw_out", o, dhb, "tn", out_dtypes=(BF16,))
    dq, dkc, dkp, dvc, dvp, dsink, dqg, dkg = _swa_bwd("swa_bwd", qkv, p["q_gain"], p["k_gain"], p["sinks"],
                                                       p["slopes"], d_o)
    dk = (dkc.astype(F32) + _shift_up_block(dkp).astype(F32)).astype(BF16)
    dv = (dvc.astype(F32) + _shift_up_block(dvp).astype(F32)).astype(BF16)
    g["sinks"], g["q_gain"], g["k_gain"] = dsink[:, 0], dqg.reshape(-1), dkg.reshape(-1)
    dqkv = jnp.concatenate([dq, dk, dv], axis=1)
    dxn = _mm("mm_dxn_qkv", dqkv, p["w_qkv"], "nt", tk=1280)
    g["w_qkv"] = _mm("mm_dw_qkv", xn, dqkv, "tn", out_dtypes=(BF16,), tn=1280)
    dh_in, dhb_in, dnorm = _rms_bwd("rms_bwd", h, rstd, p["norm"], dxn, dh)
    g["norm"] = dnorm.reshape(-1)
    return dh_in, dhb_in, g, (None, None)


def kernel(x, even_norm, even_w_in, s5_lambda_re, s5_lambda_im, s5_log_dt, s5_b_re, s5_b_im, s5_c_re, s5_c_im, s5_d, s5_w_glu, s5_b_glu, hgrn_lower_bound, hgrn_o_norm, even_w_out, odd_norm, odd_w_qkv, q_norm, k_norm, att_sinks, odd_w_out, mlp_norm, mlp_w_up, mlp_w_down, loss_target, m_even_norm, m_even_w_in, m_s5_lambda_re, m_s5_lambda_im, m_s5_log_dt, m_s5_b_re, m_s5_b_im, m_s5_c_re, m_s5_c_im, m_s5_d, m_s5_w_glu, m_s5_b_glu, m_hgrn_lower_bound, m_hgrn_o_norm, m_even_w_out, m_odd_norm, m_odd_w_qkv, m_q_norm, m_k_norm, m_att_sinks, m_odd_w_out, m_mlp_norm, m_mlp_w_up, m_mlp_w_down, v_even_norm, v_even_w_in, v_s5_lambda_re, v_s5_lambda_im, v_s5_log_dt, v_s5_b_re, v_s5_b_im, v_s5_c_re, v_s5_c_im, v_s5_d, v_s5_w_glu, v_s5_b_glu, v_hgrn_lower_bound, v_hgrn_o_norm, v_even_w_out, v_odd_norm, v_odd_w_qkv, v_q_norm, v_k_norm, v_att_sinks, v_odd_w_out, v_mlp_norm, v_mlp_w_up, v_mlp_w_down):
    a = dict(locals())
    n_rows = x.shape[1]
    xi, yi, ci = _mesh_pos()
    me = 4 * xi + 2 * yi + ci

    chunks = [[_pack_mixer_shard(layer % 2, layer // 2, a), mlp_w_up[layer].astype(BF16),
               mlp_w_down[layer].astype(BF16)] for layer in range(DEPTH)]
    gathered = list(_allgather("gather_layer", chunks[0][:1])) + [None, None]
    (odd_gathered,) = _allgather("gather_odd_norm", [jnp.pad(odd_norm, ((0, 6), (0, 0)))])
    odd_norm_full = odd_gathered[:, :2].transpose(1, 0, 2).reshape(2, D_MODEL)

    lower_bounds, lb_vjp = jax.vjp(_hgrn_lower_bounds, hgrn_lower_bound)
    slopes = _alibi_slopes()
    s5_vjps = []

    h = x.reshape(n_rows, D_MODEL)
    caches, layer_p = [], []
    for layer in range(DEPTH):
        kind, j = layer % 2, layer // 2
        wl = _unpack_mixer(kind, gathered[0])
        nxt = chunks[layer + 1] if layer + 1 < DEPTH else None
        sides = [_gather_side([t]) for t in nxt] if nxt is not None else [None] * 3
        late = [_gather_side([t]) if gathered[i] is None else None for i, t in ((1, chunks[layer][1]), (2, chunks[layer][2]))]
        if kind == 0:
            disc, vjp = jax.vjp(_s5_discretize, s5_lambda_re[j], s5_lambda_im[j], s5_log_dt[j], s5_b_re[j], s5_b_im[j])
            s5_vjps.append(vjp)
            p = dict(norm=even_norm[j], w_in=wl["w_in"], w_glu=wl["w_glu"], b_glu=s5_b_glu[j],
                     mats=_s5_matrices(*disc, s5_c_re[j], s5_c_im[j], s5_d[j]),
                     lb=lower_bounds[j], o_gain=hgrn_o_norm[j], w_out=wl["w_out"])
            h, c_mix, got_mix = _even_fwd(h, p, [sides[0]] + late)
        else:
            p = dict(norm=odd_norm_full[j], w_qkv=wl["w_qkv"], q_gain=q_norm[j], k_gain=k_norm[j],
                     sinks=att_sinks[j], slopes=slopes, w_out=wl["w_out"])
            h, c_mix, got_mix = _odd_fwd(h, p, [sides[0], None, None])
        w_up_g = gathered[1] if gathered[1] is not None else got_mix[1][0]
        w_down_g = gathered[2] if gathered[2] is not None else got_mix[2][0]
        h, c_mlp, got_up, got_down = _mlp_fwd(h, mlp_norm[layer], w_up_g, w_down_g, sides[1:])
        caches.append((c_mix, c_mlp))
        layer_p.append(p)
        if nxt is not None:
            gathered = [got_mix[0][0], got_up[0], got_down[0]]
    dh, dhb, sq = _loss_head(h, loss_target.reshape(n_rows, D_MODEL))
    loss = lax.psum(0.5 * sq[0, 0] / D_MODEL, ("x", "y", "c"))

    core = ci.astype(jnp.int32).reshape(1)
    mix_g, mlp_norm_g, received = [None] * DEPTH, [None] * DEPTH, [None] * DEPTH
    pending = None
    for layer in reversed(range(DEPTH)):
        kind = layer % 2
        c_mix, c_mlp = caches[layer]
        sides = [_exchange_side([t]) for t in pending] if pending is not None else [None] * 3
        dh, dhb, d_mlp_norm, dw_up, dw_down, got = _mlp_bwd(c_mlp, dh, dhb, sides)
        if pending is not None:
            received[layer + 1] = [g[0] for g in got]
        mlp_norm_g[layer] = d_mlp_norm.reshape(-1)
        by_chip = [t.reshape((4, 2) + t.shape[1:]) for t in (dw_up, dw_down)]
        early = [None, None]
        if layer == 0:
            arrived = _swap_with_sibling("swap_grads", by_chip)
            early = [_exchange_side([_add_pair("add_sibling_grads", m, g, core)]) for m, g in zip(by_chip, arrived)]
        bwd = _even_bwd if kind == 0 else _odd_bwd
        dh, dhb, mix_g[layer], got = bwd(c_mix, layer_p[layer], dh, dhb, early)
        by_chip = [_pack_mixer_grads(kind, mix_g[layer]).reshape(4, 2, -1, _PACK_COLS)] + (by_chip if layer > 0 else [])
        arrived = _swap_with_sibling("swap_grads", by_chip)
        pending = [_add_pair("add_sibling_grads", m, g, core) for m, g in zip(by_chip, arrived)]
    received[0] = list(_exchange_chips("exchange_grads", pending)) + [got[0][0], got[1][0]]
    grad_x = dh.reshape(x.shape)

    ev, od = [mix_g[0], mix_g[2]], [mix_g[1], mix_g[3]]
    sums = [[_sum_blocks("sum_grads", r) for r in received[layer]] for layer in range(DEPTH)]
    grads = {"mlp_w_up": jnp.stack([s[1] for s in sums]), "mlp_w_down": jnp.stack([s[2] for s in sums])}
    per_layer = [_unpack_mixer_grads(layer % 2, sums[layer][0]) for layer in range(DEPTH)]
    for name in _BIG_NAMES:
        grads[name] = jnp.stack([g[name] for g in per_layer if name in g])

    s5_g = []
    for j in range(2):
        dar, dai, dbbr, dbbi, dcr, dci, dd = _s5_unpack_grads(*ev[j]["s5"])
        s5_g.append(tuple(s5_vjps[j]((dar, dai, dbbr, dbbi))) + (dcr, dci, dd))
    (d_lb_param,) = lb_vjp(jnp.stack([g["lb"] for g in ev]))
    small = {
        "even_norm": jnp.stack([g["norm"] for g in ev]),
        "s5_lambda_re": jnp.stack([g[0] for g in s5_g]), "s5_lambda_im": jnp.stack([g[1] for g in s5_g]),
        "s5_log_dt": jnp.stack([g[2] for g in s5_g]), "s5_b_re": jnp.stack([g[3] for g in s5_g]),
        "s5_b_im": jnp.stack([g[4] for g in s5_g]), "s5_c_re": jnp.stack([g[5] for g in s5_g]),
        "s5_c_im": jnp.stack([g[6] for g in s5_g]), "s5_d": jnp.stack([g[7] for g in s5_g]),
        "s5_b_glu": jnp.stack([g["b_glu"] for g in ev]), "hgrn_lower_bound": d_lb_param,
        "hgrn_o_norm": jnp.stack([g["o_gain"] for g in ev]), "odd_norm": jnp.stack([g["norm"] for g in od]),
        "q_norm": jnp.stack([g["q_gain"] for g in od]), "k_norm": jnp.stack([g["k_gain"] for g in od]),
        "att_sinks": jnp.stack([g["sinks"] for g in od]), "mlp_norm": jnp.stack(mlp_norm_g),
    }
    small_shapes = [small[n].shape for n in _SMALL_NAMES]
    (small_all,) = _allgather("gather_small_grads", [_pack_small([small[n] for n in _SMALL_NAMES])])
    small_sum = _sum_blocks("sum_small_grads", small_all)
    for n, g in zip(_SMALL_NAMES, _unpack_small(small_sum, small_shapes)):
        grads[n] = g
    grads["odd_norm"] = lax.dynamic_slice_in_dim(grads["odd_norm"], me * (D_MODEL // N_DEV), D_MODEL // N_DEV, axis=1)

    delta, new_m, new_v = {}, {}, {}
    for name in _BIG_NAMES + _MLP_NAMES:
        to2d = lambda t, c=a[name].shape[-1]: t.reshape(-1, c)
        d_, m_, v_ = _adamw("adamw_" + name, to2d(a[name]), to2d(grads[name]), to2d(a["m_" + name]), to2d(a["v_" + name]))
        delta[name], new_m[name], new_v[name] = (t.reshape(a[name].shape) for t in (d_, m_, v_))
    packed = [_pack_small([src[n] for n in _SMALL_NAMES])
              for src in (a, grads, {n: a["m_" + n] for n in _SMALL_NAMES}, {n: a["v_" + n] for n in _SMALL_NAMES})]
    shapes = [a[n].shape for n in _SMALL_NAMES]
    for dst, flat in zip((delta, new_m, new_v), _adamw("adamw_small", *packed)):
        for n, t in zip(_SMALL_NAMES, _unpack_small(flat, shapes)):
            dst[n] = t

    return (loss, grad_x, *[grads[n] for n in _WEIGHTS], *[delta[n] for n in _WEIGHTS],
            *[new_m[n] for n in _WEIGHTS], *[new_v[n] for n in _WEIGHTS])
```

```python
import math

import jax
import jax.numpy as jnp
from jax import lax
from jax.experimental import pallas as pl
from jax.experimental.pallas import tpu as pltpu

F32 = jnp.float32
BF16 = jnp.bfloat16
MESH = pl.DeviceIdType.MESH

D_MODEL = 2048
DEPTH = 4
EPS = 1e-6
S5_WIDTH = 1024
S5_GROUPS = 64
S5_STATE = 64
S5_GROUP_SIZE = 16
S5_MIN_DECAY = 1e-4
S5_CHUNK = 128
S5_LEVELS = 7
HGRN_WIDTH = 1024
HGRN_HEADS = 8
HGRN_DIM = 128
HGRN_SUB = 16
HGRN_BLOCK = 128
ATT_HEADS = 32
ATT_KV = 4
ATT_DIM = 64
ATT_BLOCK = 128
QKV_WIDTH = (ATT_HEADS + 2 * ATT_KV) * ATT_DIM
D_FF = 4 * D_MODEL
N_DEV = 8
NEG = -1e30
VMEM_LIMIT = 56 * 1024 * 1024

ADAM_LR, ADAM_B1, ADAM_B2, ADAM_EPS, ADAM_WD, ADAM_STEP = 0.001, 0.9, 0.999, 1e-08, 0.01, 10


def _params(sem=None):
    return pltpu.CompilerParams(dimension_semantics=sem, vmem_limit_bytes=VMEM_LIMIT)


def _sds(shape, dtype):
    return jax.ShapeDtypeStruct(shape, dtype)


def _call_hosting(side, body, name, grid, in_specs, out_specs, out_shape, scratch_shapes, sem, operands):
    if side is None:
        return pl.pallas_call(body, name=name, grid=grid, in_specs=in_specs, out_specs=out_specs, out_shape=out_shape,
                              scratch_shapes=scratch_shapes, compiler_params=_params(sem))(*operands), None
    n_in, n_out, n_scr = len(in_specs), len(out_specs), len(scratch_shapes)
    n_sin, n_sout = len(side.operands), len(side.out_shapes)
    any_spec = pl.BlockSpec(memory_space=pl.ANY)

    def hosting(*refs):
        ins, refs = refs[:n_in], refs[n_in:]
        sin, refs = refs[:n_sin], refs[n_sin:]
        outs, refs = refs[:n_out], refs[n_out:]
        sout, refs = refs[:n_sout], refs[n_sout:]
        scr, sems = refs[:n_scr], refs[n_scr:]
        ids = [pl.program_id(d) for d in range(len(grid))]
        first, last = ids[0] == 0, ids[0] == grid[0] - 1
        for d in range(1, len(grid)):
            first, last = first & (ids[d] == 0), last & (ids[d] == grid[d] - 1)

        @pl.when(first)
        def _():
            side.start(sin, sout, sems)

        body(*ins, *outs, *scr)

        @pl.when(last)
        def _():
            side.finish(sin, sout, sems)

    outs = pl.pallas_call(
        hosting, name=name, grid=grid,
        in_specs=list(in_specs) + [any_spec] * n_sin,
        out_specs=list(out_specs) + [any_spec] * n_sout,
        out_shape=list(out_shape) + side.out_shapes,
        scratch_shapes=list(scratch_shapes) + side.scratch,
        compiler_params=_params(("arbitrary",) * len(grid)),
    )(*operands, *side.operands)
    return outs[:n_out], outs[n_out:]


def _mm(name, a, b, mode, out_dtypes=(F32,), epi=None, extras=(), tm=512, tn=1024, tk=2048,
        mkn=None, b_block=None, b2_block=None, o_block=None, side=None, row_vecs=()):
    if mkn is not None:
        m, k, n = mkn
    elif mode == "nn":
        (m, k), n = a.shape, b.shape[1]
    elif mode == "nt":
        (m, k), n = a.shape, b.shape[0]
    else:
        (k, m), n = a.shape, b.shape[1]
    tm, tn, tk = min(tm, m), min(tn, n), min(tk, k)
    assert m % tm == 0 and n % tn == 0 and k % tk == 0, (name, m, n, k)
    nk = k // tk
    if mode == "nn":
        a_spec = pl.BlockSpec((tm, tk), lambda i, j, kk: (i, kk))
        b_spec = pl.BlockSpec((tk, tn), lambda i, j, kk: (kk, j))
        dims = (((1,), (0,)), ((), ()))
    elif mode == "nt":
        a_spec = pl.BlockSpec((tm, tk), lambda i, j, kk: (i, kk))
        b_spec = pl.BlockSpec((tn, tk), lambda i, j, kk: (j, kk))
        dims = (((1,), (1,)), ((), ()))
    else:
        a_spec = pl.BlockSpec((tk, tm), lambda i, j, kk: (kk, i))
        b_spec = pl.BlockSpec((tk, tn), lambda i, j, kk: (kk, j))
        dims = (((0,), (0,)), ((), ()))
    o_spec = pl.BlockSpec((tm, tn), lambda i, j, kk: (i, j))
    if b_block is not None:
        b_spec = b_block
    out_specs = [o_spec] * len(out_dtypes)
    out_shape = [_sds((m, n), dt) for dt in out_dtypes]
    if o_block is not None:
        assert len(out_dtypes) == 1 and not extras
        out_specs, out_shape = [o_block[0]], [_sds(o_block[1], out_dtypes[0])]
    n_ex, n_out = len(extras) + len(row_vecs), len(out_dtypes)
    n_b = 1 if b2_block is None else 2
    grid = (m // tm, n // tn, nk)
    vec_spec = pl.BlockSpec((1, tn), lambda i, j, kk: (0, j))

    def body(*refs):
        a_ref, b_refs = refs[0], refs[1:1 + n_b]
        pos = 1 + n_b
        ex_refs = refs[pos:pos + n_ex]
        pos += n_ex
        out_refs = refs[pos:pos + n_out]
        pos += n_out
        acc_ref = refs[pos] if nk > 1 else None
        av = a_ref[...]
        if av.dtype != BF16:
            av = av.astype(BF16)
        part = None
        for q, b_ref in enumerate(b_refs):
            bv = b_ref[...]
            if bv.dtype != BF16:
                bv = bv.astype(BF16)
            aq = av if n_b == 1 else av[:, q * (tk // 2):(q + 1) * (tk // 2)]
            d = lax.dot_general(aq, bv, dims, preferred_element_type=F32)
            part = d if part is None else part + d

        def finish(acc):
            outs = epi(acc, *[r[...] for r in ex_refs]) if epi is not None else (acc,)
            for r, o in zip(out_refs, outs):
                r[...] = o.astype(r.dtype)

        if nk == 1:
            finish(part)
        else:
            kk = pl.program_id(2)

            @pl.when(kk == 0)
            def _():
                acc_ref[...] = part

            @pl.when(kk > 0)
            def _():
                acc_ref[...] += part

            @pl.when(kk == nk - 1)
            def _():
                finish(acc_ref[...])

    b_specs = [b_spec] if b2_block is None else [b_spec, b2_block]
    outs, side_outs = _call_hosting(
        side, body, name, grid, [a_spec] + b_specs + [o_spec] * len(extras) + [vec_spec] * len(row_vecs),
        out_specs, out_shape, [pltpu.VMEM((tm, tn), F32)] if nk > 1 else [], ("parallel", "parallel", "arbitrary"),
        (a,) + (b,) * n_b + tuple(extras) + tuple(v.reshape(1, -1) for v in row_vecs))
    main = outs[0] if n_out == 1 else outs
    return main if side is None else (main, side_outs)


def _rowwise(name, fn, rows, vecs, outs, accs=(), tr=256):
    n_rows = rows[0].shape[0]
    tr = min(tr, n_rows)
    assert n_rows % tr == 0
    n_r, n_v, n_o, n_a = len(rows), len(vecs), len(outs), len(accs)

    def body(*refs):
        ins = [r[...] for r in refs[:n_r + n_v]]
        o_refs = refs[n_r + n_v:n_r + n_v + n_o]
        a_refs = refs[n_r + n_v + n_o:]
        ro, ao = fn(*ins)
        for r, o in zip(o_refs, ro):
            r[...] = o.astype(r.dtype)
        if n_a:
            step = pl.program_id(0)

            @pl.when(step == 0)
            def _():
                for r, o in zip(a_refs, ao):
                    r[...] = o

            @pl.when(step > 0)
            def _():
                for r, o in zip(a_refs, ao):
                    r[...] += o

    res = pl.pallas_call(
        body, name=name,
        grid=(n_rows // tr,),
        in_specs=[pl.BlockSpec((tr, r.shape[1]), lambda i: (i, 0)) for r in rows]
        + [pl.BlockSpec(v.shape, lambda i: (0, 0)) for v in vecs],
        out_specs=[pl.BlockSpec((tr, w), lambda i: (i, 0)) for w, _ in outs]
        + [pl.BlockSpec((1, w), lambda i: (0, 0)) for w in accs],
        out_shape=[_sds((n_rows, w), dt) for w, dt in outs] + [_sds((1, w), F32) for w in accs],
        compiler_params=_params(("arbitrary",)),
    )(*rows, *vecs)
    return res


def _colsum(x):
    return jnp.sum(x, axis=0, keepdims=True)


def _sigmoid(x):
    return 1.0 / (1.0 + jnp.exp(-x))


_GELU_C = math.sqrt(2.0 / math.pi)


def _gelu(y):
    return 0.5 * y * (1.0 + jnp.tanh(_GELU_C * (y + 0.044715 * y * y * y)))


def _gelu_grad(y):
    t = jnp.tanh(_GELU_C * (y + 0.044715 * y * y * y))
    return 0.5 * (1.0 + t) + 0.5 * y * (1.0 - t * t) * _GELU_C * (1.0 + 3.0 * 0.044715 * y * y)


def _rms_fwd(name, h, gain):
    def fn(x, g):
        r = lax.rsqrt(jnp.mean(x * x, axis=1, keepdims=True) + EPS)
        return (x * r * g,), ()
    return _rowwise(name, fn, [h], [gain.reshape(1, -1)], [(h.shape[1], BF16)])[0]


def _res_norm(acc, res, gain):
    hn = acc + res
    r = lax.rsqrt(jnp.mean(hn * hn, axis=1, keepdims=True) + EPS)
    return hn, hn * r * gain


def _rms_bwd(name, h, gain, dxn, dres):
    def fn(x, dy, dr, g):
        r = lax.rsqrt(jnp.mean(x * x, axis=1, keepdims=True) + EPS)
        xh = x * r
        gdy = dy * g
        dx = r * (gdy - xh * jnp.mean(gdy * xh, axis=1, keepdims=True)) + dr
        return (dx, dx), (_colsum(dy * xh),)
    w = h.shape[1]
    return _rowwise(name, fn, [h, dxn, dres], [gain.reshape(1, -1)], [(w, F32), (w, BF16)], [w])


def _loss_head(h, target):
    w = h.shape[1]

    def fn(x, t):
        e = x - t
        return (e * (1.0 / w), e * (1.0 / w)), (jnp.zeros((1, 128), F32) + jnp.sum(e * e),)
    return _rowwise("loss_head", fn, [h, target], [], [(w, F32), (w, BF16)], [128])


def _adamw(name, w, g, m, v):
    c1 = 1.0 - ADAM_B1 ** ADAM_STEP
    c2 = 1.0 - ADAM_B2 ** ADAM_STEP

    def fn(w_, g_, m_, v_):
        mn = ADAM_B1 * m_ + (1.0 - ADAM_B1) * g_
        vn = ADAM_B2 * v_ + (1.0 - ADAM_B2) * (g_ * g_)
        delta = -ADAM_LR * ((mn / c1) / (jnp.sqrt(vn / c2) + ADAM_EPS) + ADAM_WD * w_)
        return (delta, mn, vn), ()
    c = w.shape[1]
    return _rowwise(name, fn, [w, g, m, v], [], [(c, F32)] * 3)


def _s5_discretize(lam_re, lam_im, log_dt, b_re, b_im):
    lr = jnp.minimum(lam_re, -S5_MIN_DECAY)
    li = lam_im
    dt = jnp.exp(log_dt)[:, None]
    mag = jnp.exp(lr * dt)
    ar = mag * jnp.cos(li * dt)
    ai = mag * jnp.sin(li * dt)
    den = lr * lr + li * li
    zr = ((ar - 1.0) * lr + ai * li) / den
    zi = (ai * lr - (ar - 1.0) * li) / den
    bbr = zr[..., None] * b_re - zi[..., None] * b_im
    bbi = zr[..., None] * b_im + zi[..., None] * b_re
    return ar, ai, bbr, bbi


def _s5_matrices(ar, ai, bbr, bbi, c_re, c_im, d_skip):
    eye = jnp.eye(8, dtype=F32)
    bt = jnp.stack([bbr, bbi], axis=1).transpose(0, 3, 1, 2)
    bt = bt.reshape(8, 8, 16, 1, 2, 64) * eye[None, :, None, :, None, None]
    bm8 = bt.reshape(8, 8, 16, 4, 2, 2, 64).transpose(0, 1, 2, 3, 5, 4, 6).reshape(8, 128, 1024)
    ct = jnp.stack([c_re, -c_im], axis=1).transpose(0, 1, 3, 2)
    ct = ct.reshape(8, 8, 2, 64, 1, 16) * eye[None, :, None, None, :, None]
    cm8 = ct.reshape(8, 4, 2, 2, 64, 8, 16).transpose(0, 1, 3, 2, 4, 5, 6).reshape(8, 1024, 128)
    prs, pis = [], []
    pr, pi = ar, ai
    for _ in range(S5_LEVELS):
        prs.append(pr.reshape(8, 512))
        pis.append(pi.reshape(8, 512))
        pr, pi = pr * pr - pi * pi, 2.0 * pr * pi
    prs.append(jnp.zeros_like(prs[0]))
    pis.append(jnp.zeros_like(pis[0]))
    return (bm8.astype(BF16), cm8.astype(BF16), jnp.stack(prs, axis=1), jnp.stack(pis, axis=1),
            d_skip.reshape(8, 1, 128))


def _s5_unpack_grads(dbm8, dcm8, da, dd):
    db = dbm8.reshape(8, 8, 16, 4, 2, 2, 64).transpose(0, 1, 2, 3, 5, 4, 6).reshape(8, 8, 16, 8, 2, 64)
    db = jnp.einsum("agcgqp->agcqp", db).reshape(S5_GROUPS, 16, 2, 64)
    dc = dcm8.reshape(8, 4, 2, 2, 64, 8, 16).transpose(0, 1, 3, 2, 4, 5, 6).reshape(8, 8, 2, 64, 8, 16)
    dc = jnp.einsum("agqpgc->agqpc", dc).reshape(S5_GROUPS, 2, 64, 16)
    dar = da[:, 0, :].reshape(S5_GROUPS, 64)
    dai = da[:, 1, :].reshape(S5_GROUPS, 64)
    return (dar, dai, db[:, :, 0, :].transpose(0, 2, 1), db[:, :, 1, :].transpose(0, 2, 1),
            dc[:, 0].transpose(0, 2, 1), -dc[:, 1].transpose(0, 2, 1), dd.reshape(S5_GROUPS, 16))


def _shift_rows(x, s, row, down):
    t = x.shape[0]
    if s % 8 == 0:
        z = jnp.zeros((s, x.shape[1]), x.dtype)
        return jnp.concatenate([z, x[:t - s]], axis=0) if down else jnp.concatenate([x[s:], z], axis=0)
    if down:
        return jnp.where(row >= s, pltpu.roll(x, s, 0), 0.0)
    return jnp.where(row < t - s, pltpu.roll(x, t - s, 0), 0.0)


def _s5_scan(xr, xi, pr, pi, cr, ci, row, conj):
    t = xr[0].shape[0]
    sg = -1.0 if conj else 1.0
    edge = (t - 1) if conj else 0
    n = len(xr)
    for k in range(n):
        sl = slice(128 * k, 128 * (k + 1))
        p_r, p_i = pr[0:1, sl], sg * pi[0:1, sl]
        xr[k] = xr[k] + jnp.where(row == edge, p_r * cr[k] - p_i * ci[k], 0.0)
        xi[k] = xi[k] + jnp.where(row == edge, p_r * ci[k] + p_i * cr[k], 0.0)
    for lvl in range(S5_LEVELS):
        s = 1 << lvl
        for k in range(n):
            sl = slice(128 * k, 128 * (k + 1))
            p_r, p_i = pr[lvl:lvl + 1, sl], sg * pi[lvl:lvl + 1, sl]
            sr = _shift_rows(xr[k], s, row, not conj)
            si = _shift_rows(xi[k], s, row, not conj)
            xr[k] = xr[k] + p_r * sr - p_i * si
            xi[k] = xi[k] + p_r * si + p_i * sr
    return xr, xi


def _s5_fwd(name, proj, mats, side=None):
    bm8, cm8, p1, p2, d8 = mats
    n_rows = proj.shape[0]
    t = S5_CHUNK
    nch = n_rows // t

    def body(u_ref, bm_ref, cm_ref, pr_ref, pi_ref, d_ref, y_ref, z_ref, st_ref, carry):
        @pl.when(pl.program_id(1) == 0)
        def _():
            carry[...] = jnp.zeros_like(carry)

        cv = carry[...]
        st_ref[...] = cv
        u = u_ref[...]
        bu = jnp.dot(u.astype(BF16), bm_ref[...], preferred_element_type=F32)
        row = lax.broadcasted_iota(jnp.int32, (t, 128), 0)
        tile = lambda v, j: v[:, 128 * j:128 * (j + 1)]
        xr, xi = _s5_scan([tile(bu, 2 * k) for k in range(4)], [tile(bu, 2 * k + 1) for k in range(4)],
                          pr_ref[...], pi_ref[...], [tile(cv, 2 * k)[0:1] for k in range(4)],
                          [tile(cv, 2 * k + 1)[0:1] for k in range(4)], row, False)
        xall = jnp.concatenate([v for k in range(4) for v in (xr[k], xi[k])], axis=1)
        carry[...] = jnp.broadcast_to(xall[t - 1:t, :], (8, 1024))
        y = jnp.dot(xall.astype(BF16), cm_ref[...], preferred_element_type=F32) + d_ref[...] * u
        y_ref[...] = y
        z_ref[...] = _gelu(y).astype(BF16)

    return _call_hosting(
        side, body, name, (8, nch),
        [
            pl.BlockSpec((t, 128), lambda g, c: (c, g)),
            pl.BlockSpec((None, 128, 1024), lambda g, c: (g, 0, 0)),
            pl.BlockSpec((None, 1024, 128), lambda g, c: (g, 0, 0)),
            pl.BlockSpec((None, 8, 512), lambda g, c: (g, 0, 0)),
            pl.BlockSpec((None, 8, 512), lambda g, c: (g, 0, 0)),
            pl.BlockSpec((None, 1, 128), lambda g, c: (g, 0, 0)),
        ],
        [
            pl.BlockSpec((t, 128), lambda g, c: (c, g)),
            pl.BlockSpec((t, 128), lambda g, c: (c, g)),
            pl.BlockSpec((None, None, 8, 1024), lambda g, c: (g, c, 0, 0)),
        ],
        [_sds((n_rows, S5_WIDTH), F32), _sds((n_rows, S5_WIDTH), BF16), _sds((8, nch, 8, 1024), F32)],
        [pltpu.VMEM((8, 1024), F32)], ("parallel", "arbitrary"), (proj, bm8, cm8, p1, p2, d8))


def _s5_bwd(name, proj, dy, states, mats):
    bm8, cm8, p1, p2, d8 = mats
    n_rows = proj.shape[0]
    t = S5_CHUNK
    nch = n_rows // t
    nt_dims = (((1,), (1,)), ((), ()))
    tn_dims = (((0,), (0,)), ((), ()))

    def body(u_ref, dy_ref, st_ref, bm_ref, cm_ref, pr_ref, pi_ref, d_ref,
             du_ref, dbm_ref, dcm_ref, da_ref, dd_ref, gcarry):
        @pl.when(pl.program_id(1) == 0)
        def _():
            gcarry[...] = jnp.zeros_like(gcarry)
            dbm_ref[...] = jnp.zeros_like(dbm_ref)
            dcm_ref[...] = jnp.zeros_like(dcm_ref)
            da_ref[...] = jnp.zeros_like(da_ref)
            dd_ref[...] = jnp.zeros_like(dd_ref)

        u = u_ref[...]
        dyv = dy_ref[...]
        ub, dyb = u.astype(BF16), dyv.astype(BF16)
        bu = jnp.dot(ub, bm_ref[...], preferred_element_type=F32)
        dxd = lax.dot_general(dyb, cm_ref[...], nt_dims, preferred_element_type=F32)
        row = lax.broadcasted_iota(jnp.int32, (t, 128), 0)
        tile = lambda v, j: v[:, 128 * j:128 * (j + 1)]
        prv, piv, cv, gv = pr_ref[...], pi_ref[...], st_ref[...], gcarry[...]
        cr = [tile(cv, 2 * k)[0:1] for k in range(4)]
        ci = [tile(cv, 2 * k + 1)[0:1] for k in range(4)]
        xr, xi = _s5_scan([tile(bu, 2 * k) for k in range(4)], [tile(bu, 2 * k + 1) for k in range(4)],
                          prv, piv, cr, ci, row, False)
        gr, gi = _s5_scan([tile(dxd, 2 * k) for k in range(4)], [tile(dxd, 2 * k + 1) for k in range(4)],
                          prv, piv, [tile(gv, 2 * k)[0:1] for k in range(4)],
                          [tile(gv, 2 * k + 1)[0:1] for k in range(4)], row, True)
        dar, dai = [], []
        for k in range(4):
            xpr = jnp.where(row >= 1, pltpu.roll(xr[k], 1, 0), cr[k])
            xpi = jnp.where(row >= 1, pltpu.roll(xi[k], 1, 0), ci[k])
            dar.append(_colsum(gr[k] * xpr + gi[k] * xpi))
            dai.append(_colsum(gi[k] * xpr - gr[k] * xpi))
        xall = jnp.concatenate([v for k in range(4) for v in (xr[k], xi[k])], axis=1).astype(BF16)
        gf = jnp.concatenate([v for k in range(4) for v in (gr[k], gi[k])], axis=1)
        gcarry[...] = jnp.broadcast_to(gf[0:1, :], (8, 1024))
        gall = gf.astype(BF16)
        dcm_ref[...] += lax.dot_general(xall, dyb, tn_dims, preferred_element_type=F32)
        dbm_ref[...] += lax.dot_general(ub, gall, tn_dims, preferred_element_type=F32)
        du = lax.dot_general(gall, bm_ref[...], nt_dims, preferred_element_type=F32) + d_ref[...] * dyv
        du_ref[...] = du.astype(BF16)
        dd_ref[...] += _colsum(dyv * u)
        da_ref[0:1, :] += jnp.concatenate(dar, axis=1)
        da_ref[1:2, :] += jnp.concatenate(dai, axis=1)

    rev = lambda g, c: (nch - 1 - c, g)
    return pl.pallas_call(
        body, name=name,
        grid=(8, nch),
        in_specs=[
            pl.BlockSpec((t, 128), rev),
            pl.BlockSpec((t, 128), rev),
            pl.BlockSpec((None, None, 8, 1024), lambda g, c: (g, nch - 1 - c, 0, 0)),
            pl.BlockSpec((None, 128, 1024), lambda g, c: (g, 0, 0)),
            pl.BlockSpec((None, 1024, 128), lambda g, c: (g, 0, 0)),
            pl.BlockSpec((None, 8, 512), lambda g, c: (g, 0, 0)),
            pl.BlockSpec((None, 8, 512), lambda g, c: (g, 0, 0)),
            pl.BlockSpec((None, 1, 128), lambda g, c: (g, 0, 0)),
        ],
        out_specs=[
            pl.BlockSpec((t, 128), rev),
            pl.BlockSpec((None, 128, 1024), lambda g, c: (g, 0, 0)),
            pl.BlockSpec((None, 1024, 128), lambda g, c: (g, 0, 0)),
            pl.BlockSpec((None, 8, 512), lambda g, c: (g, 0, 0)),
            pl.BlockSpec((None, 1, 128), lambda g, c: (g, 0, 0)),
        ],
        out_shape=[_sds((n_rows, S5_WIDTH), BF16), _sds((8, 128, 1024), F32), _sds((8, 1024, 128), F32),
                   _sds((8, 8, 512), F32), _sds((8, 1, 128), F32)],
        scratch_shapes=[pltpu.VMEM((8, 1024), F32)],
        compiler_params=_params(("parallel", "arbitrary")),
    )(proj, dy, states, bm8, cm8, p1, p2, d8)


def _hgrn_lower_bounds(lb_param):
    p = jax.nn.softmax(lb_param, axis=0)
    return jnp.cumsum(p, axis=0) - p[0:1]


def _prefix16(x, r16):
    for s in (1, 2, 4, 8):
        x = x + jnp.where(r16 >= s, pltpu.roll(x, s, 0), 0.0)
    return x


def _suffix16(x, r16):
    n = x.shape[0]
    for s in (1, 2, 4, 8):
        x = x + jnp.where(r16 < HGRN_SUB - s, pltpu.roll(x, n - s, 0), 0.0)
    return x


_NT = (((1,), (1,)), ((), ()))
_TN = (((0,), (0,)), ((), ()))


def _dotf(a, b, dims=(((1,), (0,)), ((), ()))):
    return lax.dot_general(a.astype(BF16), b.astype(BF16), dims, preferred_element_type=F32)


def _hgrn_specs(n_blocks, rev):
    r = HGRN_BLOCK
    blk = (lambda b: n_blocks - 1 - b) if rev else (lambda b: b)
    proj_specs = [pl.BlockSpec((r, 128), (lambda h, b, c=c: (blk(b), 8 * c + h))) for c in (1, 2, 3, 4)]
    lb_spec = pl.BlockSpec((None, 1, 128), lambda h, b: (h, 0, 0))
    gain_spec = pl.BlockSpec((1, 128), lambda h, b: (0, 0))
    row_spec = pl.BlockSpec((r, 128), lambda h, b: (blk(b), h))
    st_spec = pl.BlockSpec((None, None, 128, 128), lambda h, b: (h, blk(b), 0, 0))
    return proj_specs, lb_spec, gain_spec, row_spec, st_spec, blk


def _hgrn_fwd(name, proj, lb, gain, side=None):
    n_rows = proj.shape[0]
    r = HGRN_BLOCK
    nb = n_rows // r
    nsub = r // HGRN_SUB
    proj_specs, lb_spec, gain_spec, row_spec, st_spec, _ = _hgrn_specs(nb, False)

    def body(q_ref, f_ref, i_ref, g_ref, lb_ref, gain_ref, o_ref, y_ref, st_ref, st_scr):
        @pl.when(pl.program_id(1) == 0)
        def _():
            st_scr[...] = jnp.zeros_like(st_scr)

        st_ref[...] = st_scr[...]
        q, f, v, g = q_ref[...], f_ref[...], i_ref[...], g_ref[...]
        lbv = lb_ref[...]
        qs = q * _sigmoid(q)
        fg = lbv + (1.0 - lbv) * _sigmoid(f)
        kk = 1.0 - fg
        r16 = lax.broadcasted_iota(jnp.int32, (r, 128), 0) & (HGRN_SUB - 1)
        b = _prefix16(jnp.log(fg), r16)
        qh = qs * jnp.exp(b)
        rs = lax.broadcasted_iota(jnp.int32, (HGRN_SUB, 128), 0)
        st = st_scr[...]
        outs = []
        for i in range(nsub):
            sl = slice(HGRN_SUB * i, HGRN_SUB * (i + 1))
            qsi, kki, vi, bi = qs[sl], kk[sl], v[sl], b[sl]
            o_i = _dotf(qh[sl], st, _NT)
            for s in range(HGRN_SUB):
                e = jnp.exp(jnp.where(rs >= s, bi - bi[s:s + 1], NEG))
                col = jnp.sum(qsi * e * kki[s:s + 1], axis=1, keepdims=True)
                o_i = o_i + col * vi[s:s + 1]
            bl = bi[HGRN_SUB - 1:HGRN_SUB]
            st = st * jnp.exp(bl) + _dotf(vi, kki * jnp.exp(bl - bi), _TN)
            outs.append(o_i)
        st_scr[...] = st
        o = jnp.concatenate(outs, axis=0)
        o_ref[...] = o
        rn = lax.rsqrt(jnp.mean(o * o, axis=1, keepdims=True) + EPS)
        y_ref[...] = (o * rn * gain_ref[...] * (g * _sigmoid(g))).astype(BF16)

    return _call_hosting(
        side, body, name, (HGRN_HEADS, nb), proj_specs + [lb_spec, gain_spec], [row_spec, row_spec, st_spec],
        [_sds((n_rows, HGRN_WIDTH), F32), _sds((n_rows, HGRN_WIDTH), BF16), _sds((HGRN_HEADS, nb, 128, 128), F32)],
        [pltpu.VMEM((128, 128), F32)], ("parallel", "arbitrary"), (proj, proj, proj, proj, lb, gain))


def _hgrn_bwd(name, proj, lb, gain, o_saved, states, dycat):
    n_rows = proj.shape[0]
    r = HGRN_BLOCK
    nb = n_rows // r
    nsub = r // HGRN_SUB
    proj_specs, lb_spec, gain_spec, row_spec, st_spec, blk = _hgrn_specs(nb, True)
    dy_spec = pl.BlockSpec((r, 128), lambda h, b: (blk(b), 8 + h))
    acc_spec = pl.BlockSpec((None, 1, 128), lambda h, b: (h, 0, 0))

    def body(q_ref, f_ref, i_ref, g_ref, lb_ref, gain_ref, o_ref, st_ref, dy_ref,
             dq_ref, df_ref, di_ref, dg_ref, dlb_ref, dgain_ref, dst_scr, sub_scr):
        @pl.when(pl.program_id(1) == 0)
        def _():
            dst_scr[...] = jnp.zeros_like(dst_scr)
            dlb_ref[...] = jnp.zeros_like(dlb_ref)
            dgain_ref[...] = jnp.zeros_like(dgain_ref)

        q, f, v, g = q_ref[...], f_ref[...], i_ref[...], g_ref[...]
        lbv, gain_v = lb_ref[...], gain_ref[...]
        sq = _sigmoid(q)
        qs = q * sq
        sf = _sigmoid(f)
        fg = lbv + (1.0 - lbv) * sf
        kk = 1.0 - fg
        r16 = lax.broadcasted_iota(jnp.int32, (r, 128), 0) & (HGRN_SUB - 1)
        b = _prefix16(jnp.log(fg), r16)
        eb = jnp.exp(b)
        qh = qs * eb

        o, dy = o_ref[...], dy_ref[...]
        rn = lax.rsqrt(jnp.mean(o * o, axis=1, keepdims=True) + EPS)
        on = o * rn
        sg = _sigmoid(g)
        sil = g * sg
        dgain_ref[...] += _colsum(dy * on * sil)
        dg_ref[...] = (dy * on * gain_v * (sg * (1.0 + g * (1.0 - sg)))).astype(BF16)
        don = dy * gain_v * sil
        do = rn * (don - on * jnp.mean(don * on, axis=1, keepdims=True))

        st = st_ref[...]
        for i in range(nsub):
            sl = slice(HGRN_SUB * i, HGRN_SUB * (i + 1))
            sub_scr[i] = st
            bi = b[sl]
            bl = bi[HGRN_SUB - 1:HGRN_SUB]
            st = st * jnp.exp(bl) + _dotf(v[sl], kk[sl] * jnp.exp(bl - bi), _TN)

        rs = lax.broadcasted_iota(jnp.int32, (HGRN_SUB, 128), 0)
        dst = dst_scr[...]
        parts = [None] * nsub
        for i in reversed(range(nsub)):
            sl = slice(HGRN_SUB * i, HGRN_SUB * (i + 1))
            sp = sub_scr[i]
            qsi, kki, vi, bi, doi, qhi = qs[sl], kk[sl], v[sl], b[sl], do[sl], qh[sl]
            bl = bi[HGRN_SUB - 1:HGRN_SUB]
            ebl = jnp.exp(bl)
            dec = jnp.exp(bl - bi)
            khat = kki * dec
            dqh = _dotf(doi, sp)
            dkhat = _dotf(vi, dst)
            dv = _dotf(khat, dst, _NT)
            zrow = _colsum(sp * dst) * ebl
            dq_in = jnp.zeros((HGRN_SUB, 128), F32)
            dk_in = jnp.zeros((HGRN_SUB, 128), F32)
            dv_in = jnp.zeros((HGRN_SUB, 128), F32)
            for s in range(HGRN_SUB):
                e = jnp.exp(jnp.where(rs >= s, bi - bi[s:s + 1], NEG))
                dpc = jnp.sum(doi * vi[s:s + 1], axis=1, keepdims=True)
                w = qsi * e
                pc = jnp.sum(w * kki[s:s + 1], axis=1, keepdims=True)
                dq_in = dq_in + dpc * e * kki[s:s + 1]
                dk_in = jnp.where(rs == s, _colsum(dpc * w), dk_in)
                dv_in = jnp.where(rs == s, _colsum(pc * doi), dv_in)
            kd = khat * dkhat
            parts[i] = (qsi * dq_in - kki * dk_in + qhi * dqh, kd, jnp.broadcast_to(zrow, (HGRN_SUB, 128)),
                        dq_in + dqh * eb[sl], dk_in + dkhat * dec, dv + dv_in)
            dst = dst * ebl + _dotf(doi, qhi, _TN)
        dst_scr[...] = dst

        cat = lambda j: jnp.concatenate([p[j] for p in parts], axis=0)
        d_b, kd, zr, dqs, dkk, dvv = (cat(j) for j in range(6))
        dlf = _suffix16(d_b, r16) + _prefix16(kd, r16) - kd + zr
        dfg = dlf / fg - dkk
        df_ref[...] = (dfg * (1.0 - lbv) * sf * (1.0 - sf)).astype(BF16)
        dlb_ref[...] += _colsum(dfg * (1.0 - sf))
        dq_ref[...] = (dqs * (sq * (1.0 + q * (1.0 - sq)))).astype(BF16)
        di_ref[...] = dvv.astype(BF16)

    return pl.pallas_call(
        body, name=name,
        grid=(HGRN_HEADS, nb),
        in_specs=proj_specs + [lb_spec, gain_spec, row_spec, st_spec, dy_spec],
        out_specs=[row_spec] * 4 + [acc_spec, acc_spec],
        out_shape=[_sds((n_rows, HGRN_WIDTH), BF16)] * 4 + [_sds((HGRN_HEADS, 1, 128), F32)] * 2,
        scratch_shapes=[pltpu.VMEM((128, 128), F32), pltpu.VMEM((nsub, 128, 128), F32)],
        compiler_params=_params(("parallel", "arbitrary")),
    )(proj, proj, proj, proj, lb, gain, o_saved, states, dycat)


def _alibi_slopes():
    return jnp.exp2(-8.0 * jnp.arange(1, ATT_HEADS + 1, dtype=F32) / ATT_HEADS)


def _swa_specs(n_blocks):
    blk = ATT_BLOCK
    prev = lambda i: jnp.maximum(i - 1, 0)
    smem = pl.BlockSpec(memory_space=pltpu.SMEM)
    return [
        smem, smem,
        pl.BlockSpec((blk, ATT_HEADS * ATT_DIM), lambda i: (i, 0)),
        pl.BlockSpec((blk, 256), lambda i: (i, 8)),
        pl.BlockSpec((blk, 256), lambda i: (prev(i), 8)),
        pl.BlockSpec((blk, 256), lambda i: (i, 9)),
        pl.BlockSpec((blk, 256), lambda i: (prev(i), 9)),
        pl.BlockSpec((1, ATT_DIM), lambda i: (0, 0)),
        pl.BlockSpec((1, ATT_DIM), lambda i: (0, 0)),
    ]


_ATT_GROUP = ATT_HEADS // ATT_KV
_ATT_ROWS = _ATT_GROUP * ATT_BLOCK


def _swa_mask(i):
    t_i = lax.broadcasted_iota(jnp.int32, (_ATT_ROWS, 2 * ATT_BLOCK), 0) & (ATT_BLOCK - 1)
    s_i = lax.broadcasted_iota(jnp.int32, (_ATT_ROWS, 2 * ATT_BLOCK), 1)
    dist = t_i + ATT_BLOCK - s_i
    valid = (dist >= 0) & (dist < ATT_BLOCK) & ((s_i >= ATT_BLOCK) | (i > 0))
    return valid, dist.astype(F32)


def _stack_heads(x):
    return jnp.concatenate([x[:, ATT_DIM * h:ATT_DIM * (h + 1)] for h in range(_ATT_GROUP)], axis=0)


def _unstack_heads(x):
    return jnp.concatenate([x[ATT_BLOCK * h:ATT_BLOCK * (h + 1)] for h in range(_ATT_GROUP)], axis=1)


def _head_column(ref, g):
    return jnp.concatenate([jnp.full((ATT_BLOCK, 1), ref[_ATT_GROUP * g + h], F32) for h in range(_ATT_GROUP)], axis=0)


def _swa_probs(qn, kn, slope, sink, valid, distf):
    s = lax.dot_general(qn, kn, _NT, preferred_element_type=F32) * (ATT_DIM ** -0.5) - slope * distf
    s = jnp.where(valid, s, NEG)
    m = jnp.maximum(jnp.max(s, axis=1, keepdims=True), sink)
    p = jnp.exp(s - m)
    es = jnp.exp(sink - m)
    inv = 1.0 / (jnp.sum(p, axis=1, keepdims=True) + es)
    return p * inv, es * inv


def _swa_fwd(name, qkv, q_gain, k_gain, sinks, slopes):
    n_rows = qkv.shape[0]
    nb = n_rows // ATT_BLOCK

    def body(sink_ref, slope_ref, q_ref, kc_ref, kp_ref, vc_ref, vp_ref, qg_ref, kg_ref, o_ref):
        i = pl.program_id(0)
        kb = jnp.concatenate([kp_ref[...], kc_ref[...]], axis=0)
        vb = jnp.concatenate([vp_ref[...], vc_ref[...]], axis=0)
        valid, distf = _swa_mask(i)
        qgv, kgv = qg_ref[...], kg_ref[...]
        gw = _ATT_GROUP * ATT_DIM
        for g in range(ATT_KV):
            kg = kb[:, 64 * g:64 * (g + 1)]
            rk = lax.rsqrt(jnp.mean(kg * kg, axis=1, keepdims=True) + EPS)
            kn = (kg * rk * kgv).astype(BF16)
            vv = vb[:, 64 * g:64 * (g + 1)].astype(BF16)
            qs = _stack_heads(q_ref[:, gw * g:gw * (g + 1)])
            rq = lax.rsqrt(jnp.mean(qs * qs, axis=1, keepdims=True) + EPS)
            pn, _ = _swa_probs((qs * rq * qgv).astype(BF16), kn, _head_column(slope_ref, g),
                               _head_column(sink_ref, g), valid, distf)
            out = jnp.dot(pn.astype(BF16), vv, preferred_element_type=F32)
            o_ref[:, gw * g:gw * (g + 1)] = _unstack_heads(out).astype(BF16)

    return pl.pallas_call(
        body, name=name,
        grid=(nb,),
        in_specs=_swa_specs(nb),
        out_specs=pl.BlockSpec((ATT_BLOCK, ATT_HEADS * ATT_DIM), lambda i: (i, 0)),
        out_shape=_sds((n_rows, ATT_HEADS * ATT_DIM), BF16),
        compiler_params=_params(("parallel",)),
    )(sinks, slopes, qkv, qkv, qkv, qkv, qkv, q_gain.reshape(1, -1), k_gain.reshape(1, -1))


def _swa_bwd(name, qkv, q_gain, k_gain, sinks, slopes, d_out):
    n_rows = qkv.shape[0]
    nb = n_rows // ATT_BLOCK
    blk = ATT_BLOCK

    def body(sink_ref, slope_ref, q_ref, kc_ref, kp_ref, vc_ref, vp_ref, qg_ref, kg_ref, do_ref,
             dq_ref, dkc_ref, dkp_ref, dvc_ref, dvp_ref, dsink_ref, dqg_ref, dkg_ref):
        i = pl.program_id(0)

        @pl.when(i == 0)
        def _():
            dsink_ref[...] = jnp.zeros_like(dsink_ref)
            dqg_ref[...] = jnp.zeros_like(dqg_ref)
            dkg_ref[...] = jnp.zeros_like(dkg_ref)

        kb = jnp.concatenate([kp_ref[...], kc_ref[...]], axis=0)
        vb = jnp.concatenate([vp_ref[...], vc_ref[...]], axis=0)
        kgv, qgv = kg_ref[...], qg_ref[...]
        valid, distf = _swa_mask(i)
        scale = ATT_DIM ** -0.5
        gw = _ATT_GROUP * ATT_DIM
        dks, dvs = [], []
        dqg, dkg = jnp.zeros((1, ATT_DIM), F32), jnp.zeros((1, ATT_DIM), F32)
        for g in range(ATT_KV):
            kg = kb[:, 64 * g:64 * (g + 1)]
            rk = lax.rsqrt(jnp.mean(kg * kg, axis=1, keepdims=True) + EPS)
            khat = kg * rk
            kn = (khat * kgv).astype(BF16)
            vv = vb[:, 64 * g:64 * (g + 1)].astype(BF16)
            qs = _stack_heads(q_ref[:, gw * g:gw * (g + 1)])
            rq = lax.rsqrt(jnp.mean(qs * qs, axis=1, keepdims=True) + EPS)
            qhat = qs * rq
            qn = (qhat * qgv).astype(BF16)
            pn, ps = _swa_probs(qn, kn, _head_column(slope_ref, g), _head_column(sink_ref, g), valid, distf)
            dos = _stack_heads(do_ref[:, gw * g:gw * (g + 1)]).astype(BF16)
            dp = lax.dot_general(dos, vv, _NT, preferred_element_type=F32)
            delta = jnp.sum(pn * dp, axis=1, keepdims=True)
            ds = (pn * (dp - delta)).astype(BF16)
            sd = ps * delta
            for h in range(_ATT_GROUP):
                hs = _ATT_GROUP * g + h
                dsink_ref[hs:hs + 1, :] += jnp.zeros((1, 128), F32) - jnp.sum(sd[blk * h:blk * (h + 1)])
            dvs.append(lax.dot_general(pn.astype(BF16), dos, _TN, preferred_element_type=F32))
            dkn = lax.dot_general(ds, qn, _TN, preferred_element_type=F32) * scale
            dqn = jnp.dot(ds, kn, preferred_element_type=F32) * scale
            dqg = dqg + _colsum(dqn * qhat)
            dqhat = dqn * qgv
            dqs = rq * (dqhat - qhat * jnp.mean(dqhat * qhat, axis=1, keepdims=True))
            dq_ref[:, gw * g:gw * (g + 1)] = _unstack_heads(dqs).astype(BF16)
            dkg = dkg + _colsum(dkn * khat)
            dkhat = dkn * kgv
            dks.append(rk * (dkhat - khat * jnp.mean(dkhat * khat, axis=1, keepdims=True)))
        dqg_ref[...] += dqg
        dkg_ref[...] += dkg
        dk = jnp.concatenate(dks, axis=1).astype(BF16)
        dv = jnp.concatenate(dvs, axis=1).astype(BF16)
        dkp_ref[...] = dk[:blk]
        dkc_ref[...] = dk[blk:]
        dvp_ref[...] = dv[:blk]
        dvc_ref[...] = dv[blk:]

    kv_spec = pl.BlockSpec((blk, 256), lambda i: (i, 0))
    full = pl.BlockSpec((blk, ATT_HEADS * ATT_DIM), lambda i: (i, 0))
    acc64 = pl.BlockSpec((1, ATT_DIM), lambda i: (0, 0))
    return pl.pallas_call(
        body, name=name,
        grid=(nb,),
        in_specs=_swa_specs(nb) + [full],
        out_specs=[full, kv_spec, kv_spec, kv_spec, kv_spec,
                   pl.BlockSpec((ATT_HEADS, 128), lambda i: (0, 0)), acc64, acc64],
        out_shape=[_sds((n_rows, ATT_HEADS * ATT_DIM), BF16)] + [_sds((n_rows, 256), BF16)] * 4
        + [_sds((ATT_HEADS, 128), F32), _sds((1, ATT_DIM), F32), _sds((1, ATT_DIM), F32)],
        compiler_params=_params(("arbitrary",)),
    )(sinks, slopes, qkv, qkv, qkv, qkv, qkv, q_gain.reshape(1, -1), k_gain.reshape(1, -1), d_out)


def _mesh_pos():
    return lax.axis_index("x"), lax.axis_index("y"), lax.axis_index("c")


_ANY = pl.BlockSpec(memory_space=pl.ANY)


def _allgather(name, shards):
    side = _gather_side(shards)
    n = len(shards)

    def body(*refs):
        side.start(refs[:n], refs[n:2 * n], refs[2 * n:])
        side.finish(refs[:n], refs[n:2 * n], refs[2 * n:])

    return pl.pallas_call(
        body, name=name,
        out_shape=side.out_shapes,
        in_specs=[_ANY] * n,
        out_specs=[_ANY] * n,
        scratch_shapes=side.scratch,
    )(*shards)


class _Side:
    def __init__(self, operands, out_shapes, scratch, start, finish):
        self.operands, self.out_shapes, self.scratch = list(operands), list(out_shapes), list(scratch)
        self.start, self.finish = start, finish


def _gather_side(shards):
    n = len(shards)

    def plan(x_refs, out_refs, sems):
        send_sems, recv_sems, local_sems = sems
        x, y, c = _mesh_pos()
        me, sibling = (x, y, c), (x, y, 1 - c)
        chips = [(1 - x, y), (x, 1 - y), (1 - x, 1 - y)]

        def slot(a, px, py, pc):
            return out_refs[a].at[4 * px + 2 * py + pc]

        def copy(a, k, block, to, src=None):
            return pltpu.make_async_remote_copy(
                src_ref=slot(a, *block) if src is None else src, dst_ref=slot(a, *block),
                send_sem=send_sems.at[7 * a + k], recv_sem=recv_sems.at[7 * a + k],
                device_id=to, device_id_type=MESH)

        local = [pltpu.make_async_copy(x_refs[a], slot(a, *me), local_sems.at[a]) for a in range(n)]
        first = [[copy(a, 0, me, sibling, src=x_refs[a])]
                 + [copy(a, 1 + j, me, (*chip, c), src=x_refs[a]) for j, chip in enumerate(chips)] for a in range(n)]
        from_chips = [[copy(a, 1 + j, (*chip, c), me) for j, chip in enumerate(chips)] for a in range(n)]
        forward = [[copy(a, 4 + j, (*chip, c), sibling) for j, chip in enumerate(chips)] for a in range(n)]
        from_sibling = [[copy(a, 0, sibling, me)] + [copy(a, 4 + j, (*chip, 1 - c), me) for j, chip in enumerate(chips)]
                        for a in range(n)]
        return local, first, from_chips, forward, from_sibling

    def start(x_refs, out_refs, sems):
        local, first, _, _, _ = plan(x_refs, out_refs, sems)
        for a in range(n):
            local[a].start()
            for cp in first[a]:
                cp.start()

    def finish(x_refs, out_refs, sems):
        local, first, from_chips, forward, from_sibling = plan(x_refs, out_refs, sems)
        for a in range(n):
            for j in range(3):
                from_chips[a][j].wait_recv()
                forward[a][j].start()
        for a in range(n):
            for cp in from_sibling[a]:
                cp.wait_recv()
        for a in range(n):
            for cp in first[a] + forward[a]:
                cp.wait_send()
            local[a].wait()

    return _Side(shards, [_sds((N_DEV,) + s.shape, s.dtype) for s in shards],
                 [pltpu.SemaphoreType.DMA((7 * n,)), pltpu.SemaphoreType.DMA((7 * n,)), pltpu.SemaphoreType.DMA((n,))],
                 start, finish)


def _swap_with_sibling(name, arrs):
    n = len(arrs)

    def body(*refs):
        x_refs, got_refs = refs[:n], refs[n:2 * n]
        send_sems, recv_sems = refs[2 * n:]
        x, y, c = _mesh_pos()
        copies = []
        for a in range(n):
            for j in range(4):
                k = 4 * a + j
                cp = pltpu.make_async_remote_copy(
                    src_ref=x_refs[a].at[j, 1 - c], dst_ref=got_refs[a].at[j],
                    send_sem=send_sems.at[k], recv_sem=recv_sems.at[k],
                    device_id=(x, y, 1 - c), device_id_type=MESH)
                cp.start()
                copies.append(cp)
        for cp in copies:
            cp.wait()

    return pl.pallas_call(
        body, name=name,
        out_shape=[_sds((4,) + t.shape[2:], t.dtype) for t in arrs],
        in_specs=[_ANY] * n,
        out_specs=[_ANY] * n,
        scratch_shapes=[pltpu.SemaphoreType.DMA((4 * n,))] * 2,
    )(*arrs)


def _exchange_chips(name, arrs):
    side = _exchange_side(arrs)
    n = len(arrs)

    def body(*refs):
        side.start(refs[:n], refs[n:2 * n], refs[2 * n:])
        side.finish(refs[:n], refs[n:2 * n], refs[2 * n:])

    return pl.pallas_call(
        body, name=name,
        out_shape=side.out_shapes,
        in_specs=[_ANY] * n,
        out_specs=[_ANY] * n,
        scratch_shapes=side.scratch,
    )(*arrs)


def _exchange_side(arrs):
    n = len(arrs)

    def plan(x_refs, out_refs, sems):
        send_sems, recv_sems, local_sems = sems
        x, y, c = _mesh_pos()
        me = 2 * x + y
        local = [pltpu.make_async_copy(x_refs[a].at[me], out_refs[a].at[me], local_sems.at[a]) for a in range(n)]
        sends, recvs = [], []
        for a in range(n):
            for k in range(1, 4):
                px, py = x ^ (k >> 1), y ^ (k & 1)
                peer = 2 * px + py
                sem = 3 * a + k - 1
                sends.append(pltpu.make_async_remote_copy(
                    src_ref=x_refs[a].at[peer], dst_ref=out_refs[a].at[me],
                    send_sem=send_sems.at[sem], recv_sem=recv_sems.at[sem],
                    device_id=(px, py, c), device_id_type=MESH))
                recvs.append(pltpu.make_async_remote_copy(
                    src_ref=x_refs[a].at[peer], dst_ref=out_refs[a].at[peer],
                    send_sem=send_sems.at[sem], recv_sem=recv_sems.at[sem],
                    device_id=(px, py, c), device_id_type=MESH))
        return local, sends, recvs

    def start(x_refs, out_refs, sems):
        local, sends, _ = plan(x_refs, out_refs, sems)
        for cp in local + sends:
            cp.start()

    def finish(x_refs, out_refs, sems):
        local, sends, recvs = plan(x_refs, out_refs, sems)
        for cp in recvs:
            cp.wait_recv()
        for cp in sends:
            cp.wait_send()
        for cp in local:
            cp.wait()

    return _Side(arrs, [_sds(t.shape, t.dtype) for t in arrs],
                 [pltpu.SemaphoreType.DMA((3 * n,)), pltpu.SemaphoreType.DMA((3 * n,)), pltpu.SemaphoreType.DMA((n,))],
                 start, finish)


def _sum_blocks(name, blocks, out_dtype=F32):
    n, n_rows, n_cols = blocks.shape
    tr = _row_tile(n_rows)

    def body(x_ref, o_ref):
        acc = x_ref[0].astype(F32)
        for s in range(1, n):
            acc = acc + x_ref[s].astype(F32)
        o_ref[...] = acc.astype(o_ref.dtype)

    return pl.pallas_call(
        body, name=name,
        grid=(n_rows // tr,),
        in_specs=[pl.BlockSpec((n, tr, n_cols), lambda i: (0, i, 0))],
        out_specs=pl.BlockSpec((tr, n_cols), lambda i: (i, 0)),
        out_shape=_sds((n_rows, n_cols), out_dtype),
        compiler_params=_params(("parallel",)),
    )(blocks)


def _add_pair(name, mine, got, core):
    n, n_rows, n_cols = got.shape
    tr = _row_tile(n_rows)

    def body(core_ref, a_ref, b_ref, o_ref):
        o_ref[...] = (a_ref[...].astype(F32) + b_ref[...].astype(F32)).astype(BF16)

    spec = pl.BlockSpec((None, tr, n_cols), lambda j, i, core_ref: (j, i, 0))
    return pl.pallas_call(
        body, name=name,
        grid_spec=pltpu.PrefetchScalarGridSpec(
            num_scalar_prefetch=1,
            grid=(n, n_rows // tr),
            in_specs=[pl.BlockSpec((None, None, tr, n_cols), lambda j, i, core_ref: (j, core_ref[0], i, 0)), spec],
            out_specs=spec,
        ),
        out_shape=_sds(got.shape, BF16),
        compiler_params=_params(("parallel", "parallel")),
    )(core, mine, got)


_MIX_PARTS = (
    (("w_in", "even_w_in", 1, D_MODEL, 5120), ("w_glu", "s5_w_glu", 0, S5_WIDTH, S5_WIDTH),
     ("w_out", "even_w_out", 0, D_MODEL, D_MODEL)),
    (("w_qkv", "odd_w_qkv", 1, D_MODEL, QKV_WIDTH), ("w_out", "odd_w_out", 0, D_MODEL, D_MODEL)),
)
_PACK_COLS = 1024
_FF_SHARD = D_FF // N_DEV
_BIG_NAMES = ("even_w_in", "s5_w_glu", "even_w_out", "odd_w_qkv", "odd_w_out")


def _part_rows(rows, cols):
    return rows * cols // N_DEV // _PACK_COLS


def _row_tile(n_rows):
    return next(t for t in (512, 480, 384, 256, 128) if n_rows % t == 0)


def _pack_mixer_shard(kind, j, args):
    return jnp.concatenate([args[name][j].astype(BF16).reshape(-1, _PACK_COLS) for _, name, _, _, _ in _MIX_PARTS[kind]],
                           axis=0)


def _unpack_mixer(kind, gathered):
    out, off = {}, 0
    for key, _, axis, rows, cols in _MIX_PARTS[kind]:
        n = _part_rows(rows, cols)
        part = gathered[:, off:off + n]
        off += n
        if axis == 1:
            part = part.reshape(N_DEV, rows, cols // N_DEV).transpose(1, 0, 2)
        out[key] = part.reshape(rows, cols)
    return out


def _pack_mixer_grads(kind, g):
    parts = []
    for key, _, axis, rows, cols in _MIX_PARTS[kind]:
        t = g[key]
        if axis == 1:
            t = t.reshape(rows, N_DEV, cols // N_DEV).transpose(1, 0, 2)
        parts.append(t.reshape(N_DEV, -1, _PACK_COLS))
    return jnp.concatenate(parts, axis=1)


def _unpack_mixer_grads(kind, flat):
    out, off = {}, 0
    for _, name, axis, rows, cols in _MIX_PARTS[kind]:
        n = _part_rows(rows, cols)
        shape = (rows, cols // N_DEV) if axis == 1 else (rows // N_DEV, cols)
        out[name] = flat[off:off + n].reshape(shape)
        off += n
    return out


def _pack_small(arrs, row_mult=512):
    parts = []
    for a in arrs:
        f = a.astype(F32).reshape(-1)
        parts.append(jnp.pad(f, (0, (-f.shape[0]) % 128)))
    f = jnp.concatenate(parts)
    f = jnp.pad(f, (0, (-f.shape[0]) % (128 * row_mult)))
    return f.reshape(-1, 128)


def _unpack_small(flat, shapes):
    f = flat.reshape(-1)
    out, off = [], 0
    for s in shapes:
        n = math.prod(s)
        out.append(f[off:off + n].reshape(s))
        off += n + (-n) % 128
    return out


_WEIGHTS = ("even_norm", "even_w_in", "s5_lambda_re", "s5_lambda_im", "s5_log_dt", "s5_b_re", "s5_b_im",
            "s5_c_re", "s5_c_im", "s5_d", "s5_w_glu", "s5_b_glu", "hgrn_lower_bound", "hgrn_o_norm",
            "even_w_out", "odd_norm", "odd_w_qkv", "q_norm", "k_norm", "att_sinks", "odd_w_out",
            "mlp_norm", "mlp_w_up", "mlp_w_down")
_MLP_NAMES = ("mlp_w_up", "mlp_w_down")
_SMALL_NAMES = tuple(n for n in _WEIGHTS if n not in _BIG_NAMES + _MLP_NAMES)


def _add_res(acc, res):
    return (acc + res,)


def _mm_hosting(side, *args, **kw):
    if side is None:
        return _mm(*args, **kw), None
    return _mm(*args, side=side, **kw)


def _mlp_fwd(h, xn, gain, next_gain, w_up, w_down, sides=(None, None)):
    n_rows, fs = h.shape[0], _FF_SHARD
    (up, act), got_up = _mm_hosting(
        sides[0], "mm_up", xn, w_up, "nn", out_dtypes=(F32, BF16), mkn=(n_rows, D_MODEL, D_FF), tn=fs,
        b_block=pl.BlockSpec((None, D_MODEL, fs), lambda i, j, kk: (j, kk, 0)),
        epi=lambda acc: (acc, jnp.square(jnp.maximum(acc, 0.0))))
    w_down4 = w_down.reshape(N_DEV // 2, 2 * fs, D_MODEL)
    if next_gain is None:
        out, got_down = _mm_hosting(
            sides[1], "mm_down_last", act, w_down4, "nn", mkn=(n_rows, D_FF, D_MODEL), tk=2 * fs,
            b_block=pl.BlockSpec((None, 2 * fs, 1024), lambda i, j, kk: (kk, 0, j)), epi=_add_res, extras=(h,))
        xn_next = None
    else:
        (out, xn_next), got_down = _mm_hosting(
            sides[1], "mm_down", act, w_down4, "nn", out_dtypes=(F32, BF16), mkn=(n_rows, D_FF, D_MODEL),
            tk=2 * fs, tn=D_MODEL, b_block=pl.BlockSpec((None, 2 * fs, D_MODEL), lambda i, j, kk: (kk, 0, 0)),
            epi=_res_norm, extras=(h,), row_vecs=(next_gain,))
    return out, xn_next, (h, gain, xn, up, act, w_up, w_down), got_up, got_down


def _mlp_bwd(cache, dh, dhb, sides=(None, None, None)):
    h, gain, xn, up, act, w_up, w_down = cache
    n_rows, fs = h.shape[0], _FF_SHARD
    dup, got0 = _mm_hosting(
        sides[0], "mm_dact", dhb, w_down, "nt", out_dtypes=(BF16,), mkn=(n_rows, D_MODEL, D_FF), tn=fs,
        b_block=pl.BlockSpec((None, fs, D_MODEL), lambda i, j, kk: (j, 0, kk)),
        epi=lambda acc, u: (acc * (2.0 * jnp.maximum(u, 0.0)),), extras=(up,))
    dw_down, got1 = _mm_hosting(
        sides[1], "mm_dw_down", act, dhb, "tn", out_dtypes=(BF16,),
        o_block=(pl.BlockSpec((None, 512, 1024), lambda i, j, kk: (i // 2, i % 2, j)), (N_DEV, fs, D_MODEL)))
    dxn, got2 = _mm_hosting(
        sides[2], "mm_dxn_up", dup, w_up, "nt", mkn=(n_rows, D_FF, D_MODEL), tk=2 * fs,
        b_block=pl.BlockSpec((None, 1024, fs), lambda i, j, kk: (2 * kk, j, 0)),
        b2_block=pl.BlockSpec((None, 1024, fs), lambda i, j, kk: (2 * kk + 1, j, 0)))
    dw_up = _mm("mm_dw_up", xn, dup, "tn", out_dtypes=(BF16,),
                o_block=(pl.BlockSpec((None, 512, fs), lambda i, j, kk: (j, i, 0)), (N_DEV, D_MODEL, fs)))
    dh_in, dhb_in, dgain = _rms_bwd("rms_bwd", h, gain, dxn, dh)
    return dh_in, dhb_in, dgain, dw_up, dw_down, (got0, got1, got2)


def _even_fwd(h, xn, p, sides=(None, None, None)):
    proj, got_in = _mm_hosting(sides[0], "mm_w_in", xn, p["w_in"], "nn")
    (y_pre, z, s5_states), got_s5 = _s5_fwd("s5_fwd", proj, p["mats"], sides[1])
    gate, ya = _mm("mm_glu", z, p["w_glu"], "nn", out_dtypes=(F32, BF16), extras=(y_pre,), row_vecs=(p["b_glu"],),
                   epi=lambda acc, y, b: (acc, _gelu(y) * _sigmoid(acc + b)))
    (o, yb, h_states), got_h = _hgrn_fwd("hgrn_fwd", proj, p["lb"].reshape(8, 1, 128), p["o_gain"].reshape(1, 128),
                                         sides[2])
    ycat = jnp.concatenate([ya, yb], axis=1)
    out, xn_mlp = _mm("mm_w_out", ycat, p["w_out"], "nn", out_dtypes=(F32, BF16), tn=D_MODEL, epi=_res_norm,
                      extras=(h,), row_vecs=(p["mlp_gain"],))
    return out, xn_mlp, (h, xn, proj, y_pre, z, s5_states, gate, o, h_states, ycat), (got_in, got_s5, got_h)


def _even_bwd(cache, p, dh, dhb, sides=(None, None)):
    h, xn, proj, y_pre, z, s5_states, gate, o, h_states, ycat = cache
    g = {}
    dycat = _mm("mm_dy_out", dhb, p["w_out"], "nt")
    g["w_out"] = _mm("mm_dw_out", ycat, dhb, "tn", out_dtypes=(BF16,))
    dq, df, di, dg, dlb, dgain = _hgrn_bwd("hgrn_bwd", proj, p["lb"].reshape(8, 1, 128),
                                           p["o_gain"].reshape(1, 128), o, h_states, dycat)
    g["lb"] = dlb.reshape(-1)
    g["o_gain"] = jnp.sum(dgain, axis=0).reshape(-1)

    def glu_bwd1(dyc, y, gt, b):
        zf = _gelu(y)
        s = _sigmoid(gt + b)
        dya = dyc[:, :S5_WIDTH]
        d_gate = dya * zf * s * (1.0 - s)
        return (d_gate, dya * s), (_colsum(d_gate),)

    d_gate, dz_direct, db_glu = _rowwise("glu_bwd_gate", glu_bwd1, [dycat, y_pre, gate], [p["b_glu"].reshape(1, -1)],
                                         [(S5_WIDTH, BF16), (S5_WIDTH, F32)], [S5_WIDTH])
    g["b_glu"] = db_glu.reshape(-1)
    dy_pre = _mm("mm_dz_glu", d_gate, p["w_glu"], "nt", extras=(dz_direct, y_pre),
                 epi=lambda acc, dzd, y: ((acc + dzd) * _gelu_grad(y),))
    g["w_glu"] = _mm("mm_dw_glu", z, d_gate, "tn", out_dtypes=(BF16,))
    du, dbm, dcm, da, dd = _s5_bwd("s5_bwd", proj, dy_pre, s5_states, p["mats"])
    g["s5"] = (dbm, dcm, da, dd)
    dproj = jnp.concatenate([du, dq, df, di, dg], axis=1)
    dxn, got0 = _mm_hosting(sides[0], "mm_dxn_in", dproj, p["w_in"], "nt", tk=2560)
    g["w_in"], got1 = _mm_hosting(sides[1], "mm_dw_in", xn, dproj, "tn", out_dtypes=(BF16,))
    dh_in, dhb_in, dnorm = _rms_bwd("rms_bwd", h, p["norm"], dxn, dh)
    g["norm"] = dnorm.reshape(-1)
    return dh_in, dhb_in, g, (got0, got1)


def _odd_fwd(h, xn, p, sides=(None, None, None)):
    qkv, got = _mm_hosting(sides[0], "mm_w_qkv", xn, p["w_qkv"], "nn", tn=1280)
    o = _swa_fwd("swa_fwd", qkv, p["q_gain"], p["k_gain"], p["sinks"], p["slopes"])
    out, xn_mlp = _mm("mm_w_out", o, p["w_out"], "nn", out_dtypes=(F32, BF16), tn=D_MODEL, epi=_res_norm,
                      extras=(h,), row_vecs=(p["mlp_gain"],))
    return out, xn_mlp, (h, xn, qkv, o), (got, None, None)


def _shift_up_block(x):
    return jnp.concatenate([x[ATT_BLOCK:], jnp.zeros((ATT_BLOCK, x.shape[1]), x.dtype)], axis=0)


def _odd_bwd(cache, p, dh, dhb, sides=(None, None)):
    h, xn, qkv, o = cache
    g = {}
    d_o = _mm("mm_dy_out", dhb, p["w_out"], "nt")
    g["w_out"] = _mm("mm_dw_out", o, dhb, "tn", out_dtypes=(BF16,))
    dq, dkc, dkp, dvc, dvp, dsink, dqg, dkg = _swa_bwd("swa_bwd", qkv, p["q_gain"], p["k_gain"], p["sinks"],
                                                       p["slopes"], d_o)
    dk = (dkc.astype(F32) + _shift_up_block(dkp).astype(F32)).astype(BF16)
    dv = (dvc.astype(F32) + _shift_up_block(dvp).astype(F32)).astype(BF16)
    g["sinks"], g["q_gain"], g["k_gain"] = dsink[:, 0], dqg.reshape(-1), dkg.reshape(-1)
    dqkv = jnp.concatenate([dq, dk, dv], axis=1)
    dxn = _mm("mm_dxn_qkv", dqkv, p["w_qkv"], "nt", tk=1280)
    g["w_qkv"] = _mm("mm_dw_qkv", xn, dqkv, "tn", out_dtypes=(BF16,), tn=1280)
    dh_in, dhb_in, dnorm = _rms_bwd("rms_bwd", h, p["norm"], dxn, dh)
    g["norm"] = dnorm.reshape(-1)
    return dh_in, dhb_in, g, (None, None)


def kernel(x, even_norm, even_w_in, s5_lambda_re, s5_lambda_im, s5_log_dt, s5_b_re, s5_b_im, s5_c_re, s5_c_im, s5_d, s5_w_glu, s5_b_glu, hgrn_lower_bound, hgrn_o_norm, even_w_out, odd_norm, odd_w_qkv, q_norm, k_norm, att_sinks, odd_w_out, mlp_norm, mlp_w_up, mlp_w_down, loss_target, m_even_norm, m_even_w_in, m_s5_lambda_re, m_s5_lambda_im, m_s5_log_dt, m_s5_b_re, m_s5_b_im, m_s5_c_re, m_s5_c_im, m_s5_d, m_s5_w_glu, m_s5_b_glu, m_hgrn_lower_bound, m_hgrn_o_norm, m_even_w_out, m_odd_norm, m_odd_w_qkv, m_q_norm, m_k_norm, m_att_sinks, m_odd_w_out, m_mlp_norm, m_mlp_w_up, m_mlp_w_down, v_even_norm, v_even_w_in, v_s5_lambda_re, v_s5_lambda_im, v_s5_log_dt, v_s5_b_re, v_s5_b_im, v_s5_c_re, v_s5_c_im, v_s5_d, v_s5_w_glu, v_s5_b_glu, v_hgrn_lower_bound, v_hgrn_o_norm, v_even_w_out, v_odd_norm, v_odd_w_qkv, v_q_norm, v_k_norm, v_att_sinks, v_odd_w_out, v_mlp_norm, v_mlp_w_up, v_mlp_w_down):
    a = dict(locals())
    n_rows = x.shape[1]
    xi, yi, ci = _mesh_pos()
    me = 4 * xi + 2 * yi + ci

    chunks = [[_pack_mixer_shard(layer % 2, layer // 2, a), mlp_w_up[layer].astype(BF16),
               mlp_w_down[layer].astype(BF16)] for layer in range(DEPTH)]
    gathered = list(_allgather("gather_layer", chunks[0][:1])) + [None, None]
    (odd_gathered,) = _allgather("gather_odd_norm", [jnp.pad(odd_norm, ((0, 6), (0, 0)))])
    odd_norm_full = odd_gathered[:, :2].transpose(1, 0, 2).reshape(2, D_MODEL)

    lower_bounds, lb_vjp = jax.vjp(_hgrn_lower_bounds, hgrn_lower_bound)
    slopes = _alibi_slopes()
    s5_vjps = []

    h = x.reshape(n_rows, D_MODEL)
    mixer_gain = [even_norm[layer // 2] if layer % 2 == 0 else odd_norm_full[layer // 2] for layer in range(DEPTH)]
    xn = _rms_fwd("rms_fwd", h, mixer_gain[0])
    caches, layer_p = [], []
    for layer in range(DEPTH):
        kind, j = layer % 2, layer // 2
        wl = _unpack_mixer(kind, gathered[0])
        nxt = chunks[layer + 1] if layer + 1 < DEPTH else None
        sides = [_gather_side([t]) for t in nxt] if nxt is not None else [None] * 3
        late = [_gather_side([t]) if gathered[i] is None else None for i, t in ((1, chunks[layer][1]), (2, chunks[layer][2]))]
        if kind == 0:
            disc, vjp = jax.vjp(_s5_discretize, s5_lambda_re[j], s5_lambda_im[j], s5_log_dt[j], s5_b_re[j], s5_b_im[j])
            s5_vjps.append(vjp)
            p = dict(norm=mixer_gain[layer], w_in=wl["w_in"], w_glu=wl["w_glu"], b_glu=s5_b_glu[j],
                     mats=_s5_matrices(*disc, s5_c_re[j], s5_c_im[j], s5_d[j]),
                     lb=lower_bounds[j], o_gain=hgrn_o_norm[j], w_out=wl["w_out"], mlp_gain=mlp_norm[layer])
            h, xn, c_mix, got_mix = _even_fwd(h, xn, p, [sides[0]] + late)
        else:
            p = dict(norm=mixer_gain[layer], w_qkv=wl["w_qkv"], q_gain=q_norm[j], k_gain=k_norm[j],
                     sinks=att_sinks[j], slopes=slopes, w_out=wl["w_out"], mlp_gain=mlp_norm[layer])
            h, xn, c_mix, got_mix = _odd_fwd(h, xn, p, [sides[0], None, None])
        w_up_g = gathered[1] if gathered[1] is not None else got_mix[1][0]
        w_down_g = gathered[2] if gathered[2] is not None else got_mix[2][0]
        next_gain = mixer_gain[layer + 1] if layer + 1 < DEPTH else None
        h, xn, c_mlp, got_up, got_down = _mlp_fwd(h, xn, mlp_norm[layer], next_gain, w_up_g, w_down_g, sides[1:])
        caches.append((c_mix, c_mlp))
        layer_p.append(p)
        if nxt is not None:
            gathered = [got_mix[0][0], got_up[0], got_down[0]]
    dh, dhb, sq = _loss_head(h, loss_target.reshape(n_rows, D_MODEL))
    loss = lax.psum(0.5 * sq[0, 0] / D_MODEL, ("x", "y", "c"))

    core = ci.astype(jnp.int32).reshape(1)
    mix_g, mlp_norm_g, received = [None] * DEPTH, [None] * DEPTH, [None] * DEPTH
    pending = None
    for layer in reversed(range(DEPTH)):
        kind = layer % 2
        c_mix, c_mlp = caches[layer]
        sides = [_exchange_side([t]) for t in pending] if pending is not None else [None] * 3
        dh, dhb, d_mlp_norm, dw_up, dw_down, got = _mlp_bwd(c_mlp, dh, dhb, sides)
        if pending is not None:
            received[layer + 1] = [g[0] for g in got]
        mlp_norm_g[layer] = d_mlp_norm.reshape(-1)
        by_chip = [t.reshape((4, 2) + t.shape[1:]) for t in (dw_up, dw_down)]
        early = [None, None]
        if layer == 0:
            arrived = _swap_with_sibling("swap_grads", by_chip)
            early = [_exchange_side([_add_pair("add_sibling_grads", m, g, core)]) for m, g in zip(by_chip, arrived)]
        bwd = _even_bwd if kind == 0 else _odd_bwd
        dh, dhb, mix_g[layer], got = bwd(c_mix, layer_p[layer], dh, dhb, early)
        by_chip = [_pack_mixer_grads(kind, mix_g[layer]).reshape(4, 2, -1, _PACK_COLS)] + (by_chip if layer > 0 else [])
        arrived = _swap_with_sibling("swap_grads", by_chip)
        pending = [_add_pair("add_sibling_grads", m, g, core) for m, g in zip(by_chip, arrived)]
    received[0] = list(_exchange_chips("exchange_grads", pending)) + [got[0][0], got[1][0]]
    grad_x = dh.reshape(x.shape)

    ev, od = [mix_g[0], mix_g[2]], [mix_g[1], mix_g[3]]
    sums = [[_sum_blocks("sum_grads", r) for r in received[layer]] for layer in range(DEPTH)]
    grads = {"mlp_w_up": jnp.stack([s[1] for s in sums]), "mlp_w_down": jnp.stack([s[2] for s in sums])}
    per_layer = [_unpack_mixer_grads(layer % 2, sums[layer][0]) for layer in range(DEPTH)]
    for name in _BIG_NAMES:
        grads[name] = jnp.stack([g[name] for g in per_layer if name in g])

    s5_g = []
    for j in range(2):
        dar, dai, dbbr, dbbi, dcr, dci, dd = _s5_unpack_grads(*ev[j]["s5"])
        s5_g.append(tuple(s5_vjps[j]((dar, dai, dbbr, dbbi))) + (dcr, dci, dd))
    (d_lb_param,) = lb_vjp(jnp.stack([g["lb"] for g in ev]))
    small = {
        "even_norm": jnp.stack([g["norm"] for g in ev]),
        "s5_lambda_re": jnp.stack([g[0] for g in s5_g]), "s5_lambda_im": jnp.stack([g[1] for g in s5_g]),
        "s5_log_dt": jnp.stack([g[2] for g in s5_g]), "s5_b_re": jnp.stack([g[3] for g in s5_g]),
        "s5_b_im": jnp.stack([g[4] for g in s5_g]), "s5_c_re": jnp.stack([g[5] for g in s5_g]),
        "s5_c_im": jnp.stack([g[6] for g in s5_g]), "s5_d": jnp.stack([g[7] for g in s5_g]),
        "s5_b_glu": jnp.stack([g["b_glu"] for g in ev]), "hgrn_lower_bound": d_lb_param,
        "hgrn_o_norm": jnp.stack([g["o_gain"] for g in ev]), "odd_norm": jnp.stack([g["norm"] for g in od]),
        "q_norm": jnp.stack([g["q_gain"] for g in od]), "k_norm": jnp.stack([g["k_gain"] for g in od]),
        "att_sinks": jnp.stack([g["sinks"] for g in od]), "mlp_norm": jnp.stack(mlp_norm_g),
    }
    small_shapes = [small[n].shape for n in _SMALL_NAMES]
    (small_all,) = _allgather("gather_small_grads", [_pack_small([small[n] for n in _SMALL_NAMES])])
    small_sum = _sum_blocks("sum_small_grads", small_all)
    for n, g in zip(_SMALL_NAMES, _unpack_small(small_sum, small_shapes)):
        grads[n] = g
    grads["odd_norm"] = lax.dynamic_slice_in_dim(grads["odd_norm"], me * (D_MODEL // N_DEV), D_MODEL // N_DEV, axis=1)

    delta, new_m, new_v = {}, {}, {}
    for name in _BIG_NAMES + _MLP_NAMES:
        to2d = lambda t, c=a[name].shape[-1]: t.reshape(-1, c)
        d_, m_, v_ = _adamw("adamw_" + name, to2d(a[name]), to2d(grads[name]), to2d(a["m_" + name]), to2d(a["v_" + name]))
        delta[name], new_m[name], new_v[name] = (t.reshape(a[name].shape) for t in (d_, m_, v_))
    packed = [_pack_small([src[n] for n in _SMALL_NAMES])
              for src in (a, grads, {n: a["m_" + n] for n in _SMALL_NAMES}, {n: a["v_" + n] for n in _SMALL_NAMES})]
    shapes = [a[n].shape for n in _SMALL_NAMES]
    for dst, flat in zip((delta, new_m, new_v), _adamw("adamw_small", *packed)):
        for n, t in zip(_SMALL_NAMES, _unpack_small(flat, shapes)):
            dst[n] = t

    return (loss, grad_x, *[grads[n] for n in _WEIGHTS], *[delta[n] for n in _WEIGHTS],
            *[new_m[n] for n in _WEIGHTS], *[new_v[n] for n in _WEIGHTS])
```

```python
import math

import jax
import jax.numpy as jnp
from jax import lax
from jax.experimental import pallas as pl
from jax.experimental.pallas import tpu as pltpu

F32 = jnp.float32
BF16 = jnp.bfloat16
MESH = pl.DeviceIdType.MESH

D_MODEL = 2048
DEPTH = 4
EPS = 1e-6
S5_WIDTH = 1024
S5_GROUPS = 64
S5_STATE = 64
S5_GROUP_SIZE = 16
S5_MIN_DECAY = 1e-4
S5_CHUNK = 128
S5_LEVELS = 7
HGRN_WIDTH = 1024
HGRN_HEADS = 8
HGRN_DIM = 128
HGRN_SUB = 16
HGRN_BLOCK = 128
ATT_HEADS = 32
ATT_KV = 4
ATT_DIM = 64
ATT_BLOCK = 128
QKV_WIDTH = (ATT_HEADS + 2 * ATT_KV) * ATT_DIM
D_FF = 4 * D_MODEL
N_DEV = 8
NEG = -1e30
VMEM_LIMIT = 56 * 1024 * 1024

ADAM_LR, ADAM_B1, ADAM_B2, ADAM_EPS, ADAM_WD, ADAM_STEP = 0.001, 0.9, 0.999, 1e-08, 0.01, 10


def _params(sem=None):
    return pltpu.CompilerParams(dimension_semantics=sem, vmem_limit_bytes=VMEM_LIMIT)


def _sds(shape, dtype):
    return jax.ShapeDtypeStruct(shape, dtype)


def _call_hosting(side, body, name, grid, in_specs, out_specs, out_shape, scratch_shapes, sem, operands):
    if side is None:
        return pl.pallas_call(body, name=name, grid=grid, in_specs=in_specs, out_specs=out_specs, out_shape=out_shape,
                              scratch_shapes=scratch_shapes, compiler_params=_params(sem))(*operands), None
    n_in, n_out, n_scr = len(in_specs), len(out_specs), len(scratch_shapes)
    n_sin, n_sout = len(side.operands), len(side.out_shapes)
    any_spec = pl.BlockSpec(memory_space=pl.ANY)

    def hosting(*refs):
        ins, refs = refs[:n_in], refs[n_in:]
        sin, refs = refs[:n_sin], refs[n_sin:]
        outs, refs = refs[:n_out], refs[n_out:]
        sout, refs = refs[:n_sout], refs[n_sout:]
        scr, sems = refs[:n_scr], refs[n_scr:]
        ids = [pl.program_id(d) for d in range(len(grid))]
        first, last = ids[0] == 0, ids[0] == grid[0] - 1
        for d in range(1, len(grid)):
            first, last = first & (ids[d] == 0), last & (ids[d] == grid[d] - 1)

        @pl.when(first)
        def _():
            side.start(sin, sout, sems)

        body(*ins, *outs, *scr)

        @pl.when(last)
        def _():
            side.finish(sin, sout, sems)

    outs = pl.pallas_call(
        hosting, name=name, grid=grid,
        in_specs=list(in_specs) + [any_spec] * n_sin,
        out_specs=list(out_specs) + [any_spec] * n_sout,
        out_shape=list(out_shape) + side.out_shapes,
        scratch_shapes=list(scratch_shapes) + side.scratch,
        compiler_params=_params(("arbitrary",) * len(grid)),
    )(*operands, *side.operands)
    return outs[:n_out], outs[n_out:]


def _mm(name, a, b, mode, out_dtypes=(F32,), epi=None, extras=(), tm=512, tn=1024, tk=2048,
        mkn=None, b_block=None, b2_block=None, o_block=None, side=None, row_vecs=(), col_acc=False):
    if mkn is not None:
        m, k, n = mkn
    elif mode == "nn":
        (m, k), n = a.shape, b.shape[1]
    elif mode == "nt":
        (m, k), n = a.shape, b.shape[0]
    else:
        (k, m), n = a.shape, b.shape[1]
    tm, tn, tk = min(tm, m), min(tn, n), min(tk, k)
    assert m % tm == 0 and n % tn == 0 and k % tk == 0, (name, m, n, k)
    nk = k // tk
    if mode == "nn":
        a_spec = pl.BlockSpec((tm, tk), lambda i, j, kk: (i, kk))
        b_spec = pl.BlockSpec((tk, tn), lambda i, j, kk: (kk, j))
        dims = (((1,), (0,)), ((), ()))
    elif mode == "nt":
        a_spec = pl.BlockSpec((tm, tk), lambda i, j, kk: (i, kk))
        b_spec = pl.BlockSpec((tn, tk), lambda i, j, kk: (j, kk))
        dims = (((1,), (1,)), ((), ()))
    else:
        a_spec = pl.BlockSpec((tk, tm), lambda i, j, kk: (kk, i))
        b_spec = pl.BlockSpec((tk, tn), lambda i, j, kk: (kk, j))
        dims = (((0,), (0,)), ((), ()))
    o_spec = pl.BlockSpec((tm, tn), lambda i, j, kk: (i, j))
    if b_block is not None:
        b_spec = b_block
    out_specs = [o_spec] * len(out_dtypes)
    out_shape = [_sds((m, n), dt) for dt in out_dtypes]
    if o_block is not None:
        assert len(out_dtypes) == 1 and not extras
        out_specs, out_shape = [o_block[0]], [_sds(o_block[1], out_dtypes[0])]
    n_ex, n_out = len(extras) + len(row_vecs), len(out_dtypes)
    n_b = 1 if b2_block is None else 2
    grid = (m // tm, n // tn, nk)
    vec_spec = pl.BlockSpec((1, tn), lambda i, j, kk: (0, j))
    if col_acc:
        assert tn == n
        out_specs, out_shape = out_specs + [vec_spec], out_shape + [_sds((1, n), F32)]
    n_all = n_out + (1 if col_acc else 0)

    def body(*refs):
        a_ref, b_refs = refs[0], refs[1:1 + n_b]
        pos = 1 + n_b
        ex_refs = refs[pos:pos + n_ex]
        pos += n_ex
        out_refs = refs[pos:pos + n_out]
        col_ref = refs[pos + n_out] if col_acc else None
        pos += n_all
        acc_ref = refs[pos] if nk > 1 else None
        av = a_ref[...]
        if av.dtype != BF16:
            av = av.astype(BF16)
        part = None
        for q, b_ref in enumerate(b_refs):
            bv = b_ref[...]
            if bv.dtype != BF16:
                bv = bv.astype(BF16)
            aq = av if n_b == 1 else av[:, q * (tk // 2):(q + 1) * (tk // 2)]
            d = lax.dot_general(aq, bv, dims, preferred_element_type=F32)
            part = d if part is None else part + d

        def finish(acc):
            outs = epi(acc, *[r[...] for r in ex_refs]) if epi is not None else (acc,)
            for r, o in zip(out_refs, outs):
                r[...] = o.astype(r.dtype)
            if col_acc:
                row_tile = pl.program_id(0)

                @pl.when(row_tile == 0)
                def _():
                    col_ref[...] = outs[n_out]

                @pl.when(row_tile > 0)
                def _():
                    col_ref[...] += outs[n_out]

        if nk == 1:
            finish(part)
        else:
            kk = pl.program_id(2)

            @pl.when(kk == 0)
            def _():
                acc_ref[...] = part

            @pl.when(kk > 0)
            def _():
                acc_ref[...] += part

            @pl.when(kk == nk - 1)
            def _():
                finish(acc_ref[...])

    b_specs = [b_spec] if b2_block is None else [b_spec, b2_block]
    outs, side_outs = _call_hosting(
        side, body, name, grid, [a_spec] + b_specs + [o_spec] * len(extras) + [vec_spec] * len(row_vecs),
        out_specs, out_shape, [pltpu.VMEM((tm, tn), F32)] if nk > 1 else [],
        ("arbitrary",) * 3 if col_acc else ("parallel", "parallel", "arbitrary"),
        (a,) + (b,) * n_b + tuple(extras) + tuple(v.reshape(1, -1) for v in row_vecs))
    main = outs[0] if n_all == 1 else outs
    return main if side is None else (main, side_outs)


def _rowwise(name, fn, rows, vecs, outs, accs=(), tr=256):
    n_rows = rows[0].shape[0]
    tr = min(tr, n_rows)
    assert n_rows % tr == 0
    n_r, n_v, n_o, n_a = len(rows), len(vecs), len(outs), len(accs)

    def body(*refs):
        ins = [r[...] for r in refs[:n_r + n_v]]
        o_refs = refs[n_r + n_v:n_r + n_v + n_o]
        a_refs = refs[n_r + n_v + n_o:]
        ro, ao = fn(*ins)
        for r, o in zip(o_refs, ro):
            r[...] = o.astype(r.dtype)
        if n_a:
            step = pl.program_id(0)

            @pl.when(step == 0)
            def _():
                for r, o in zip(a_refs, ao):
                    r[...] = o

            @pl.when(step > 0)
            def _():
                for r, o in zip(a_refs, ao):
                    r[...] += o

    res = pl.pallas_call(
        body, name=name,
        grid=(n_rows // tr,),
        in_specs=[pl.BlockSpec((tr, r.shape[1]), lambda i: (i, 0)) for r in rows]
        + [pl.BlockSpec(v.shape, lambda i: (0, 0)) for v in vecs],
        out_specs=[pl.BlockSpec((tr, w), lambda i: (i, 0)) for w, _ in outs]
        + [pl.BlockSpec((1, w), lambda i: (0, 0)) for w in accs],
        out_shape=[_sds((n_rows, w), dt) for w, dt in outs] + [_sds((1, w), F32) for w in accs],
        compiler_params=_params(("arbitrary",)),
    )(*rows, *vecs)
    return res


def _colsum(x):
    return jnp.sum(x, axis=0, keepdims=True)


def _sigmoid(x):
    return 1.0 / (1.0 + jnp.exp(-x))


_GELU_C = math.sqrt(2.0 / math.pi)


def _gelu(y):
    return 0.5 * y * (1.0 + jnp.tanh(_GELU_C * (y + 0.044715 * y * y * y)))


def _gelu_grad(y):
    t = jnp.tanh(_GELU_C * (y + 0.044715 * y * y * y))
    return 0.5 * (1.0 + t) + 0.5 * y * (1.0 - t * t) * _GELU_C * (1.0 + 3.0 * 0.044715 * y * y)


def _rms_fwd(name, h, gain):
    def fn(x, g):
        r = lax.rsqrt(jnp.mean(x * x, axis=1, keepdims=True) + EPS)
        return (x * r * g,), ()
    return _rowwise(name, fn, [h], [gain.reshape(1, -1)], [(h.shape[1], BF16)])[0]


def _res_norm(acc, res, gain):
    hn = acc + res
    r = lax.rsqrt(jnp.mean(hn * hn, axis=1, keepdims=True) + EPS)
    return hn, hn * r * gain


def _rms_bwd_epi(dxn, h, dres, gain):
    r = lax.rsqrt(jnp.mean(h * h, axis=1, keepdims=True) + EPS)
    xh = h * r
    gdy = dxn * gain
    dx = r * (gdy - xh * jnp.mean(gdy * xh, axis=1, keepdims=True)) + dres
    return dx, dx, _colsum(dxn * xh)


def _loss_head(h, target):
    w = h.shape[1]

    def fn(x, t):
        e = x - t
        return (e * (1.0 / w), e * (1.0 / w)), (jnp.zeros((1, 128), F32) + jnp.sum(e * e),)
    return _rowwise("loss_head", fn, [h, target], [], [(w, F32), (w, BF16)], [128])


def _adamw(name, w, g, m, v):
    c1 = 1.0 - ADAM_B1 ** ADAM_STEP
    c2 = 1.0 - ADAM_B2 ** ADAM_STEP

    def fn(w_, g_, m_, v_):
        mn = ADAM_B1 * m_ + (1.0 - ADAM_B1) * g_
        vn = ADAM_B2 * v_ + (1.0 - ADAM_B2) * (g_ * g_)
        delta = -ADAM_LR * ((mn / c1) / (jnp.sqrt(vn / c2) + ADAM_EPS) + ADAM_WD * w_)
        return (delta, mn, vn), ()
    c = w.shape[1]
    return _rowwise(name, fn, [w, g, m, v], [], [(c, F32)] * 3)


def _s5_discretize(lam_re, lam_im, log_dt, b_re, b_im):
    lr = jnp.minimum(lam_re, -S5_MIN_DECAY)
    li = lam_im
    dt = jnp.exp(log_dt)[:, None]
    mag = jnp.exp(lr * dt)
    ar = mag * jnp.cos(li * dt)
    ai = mag * jnp.sin(li * dt)
    den = lr * lr + li * li
    zr = ((ar - 1.0) * lr + ai * li) / den
    zi = (ai * lr - (ar - 1.0) * li) / den
    bbr = zr[..., None] * b_re - zi[..., None] * b_im
    bbi = zr[..., None] * b_im + zi[..., None] * b_re
    return ar, ai, bbr, bbi


def _s5_matrices(ar, ai, bbr, bbi, c_re, c_im, d_skip):
    eye = jnp.eye(8, dtype=F32)
    bt = jnp.stack([bbr, bbi], axis=1).transpose(0, 3, 1, 2)
    bt = bt.reshape(8, 8, 16, 1, 2, 64) * eye[None, :, None, :, None, None]
    bm8 = bt.reshape(8, 8, 16, 4, 2, 2, 64).transpose(0, 1, 2, 3, 5, 4, 6).reshape(8, 128, 1024)
    ct = jnp.stack([c_re, -c_im], axis=1).transpose(0, 1, 3, 2)
    ct = ct.reshape(8, 8, 2, 64, 1, 16) * eye[None, :, None, None, :, None]
    cm8 = ct.reshape(8, 4, 2, 2, 64, 8, 16).transpose(0, 1, 3, 2, 4, 5, 6).reshape(8, 1024, 128)
    prs, pis = [], []
    pr, pi = ar, ai
    for _ in range(S5_LEVELS):
        prs.append(pr.reshape(8, 512))
        pis.append(pi.reshape(8, 512))
        pr, pi = pr * pr - pi * pi, 2.0 * pr * pi
    prs.append(jnp.zeros_like(prs[0]))
    pis.append(jnp.zeros_like(pis[0]))
    return (bm8.astype(BF16), cm8.astype(BF16), jnp.stack(prs, axis=1), jnp.stack(pis, axis=1),
            d_skip.reshape(8, 1, 128))


def _s5_unpack_grads(dbm8, dcm8, da, dd):
    db = dbm8.reshape(8, 8, 16, 4, 2, 2, 64).transpose(0, 1, 2, 3, 5, 4, 6).reshape(8, 8, 16, 8, 2, 64)
    db = jnp.einsum("agcgqp->agcqp", db).reshape(S5_GROUPS, 16, 2, 64)
    dc = dcm8.reshape(8, 4, 2, 2, 64, 8, 16).transpose(0, 1, 3, 2, 4, 5, 6).reshape(8, 8, 2, 64, 8, 16)
    dc = jnp.einsum("agqpgc->agqpc", dc).reshape(S5_GROUPS, 2, 64, 16)
    dar = da[:, 0, :].reshape(S5_GROUPS, 64)
    dai = da[:, 1, :].reshape(S5_GROUPS, 64)
    return (dar, dai, db[:, :, 0, :].transpose(0, 2, 1), db[:, :, 1, :].transpose(0, 2, 1),
            dc[:, 0].transpose(0, 2, 1), -dc[:, 1].transpose(0, 2, 1), dd.reshape(S5_GROUPS, 16))


def _shift_rows(x, s, row, down):
    t = x.shape[0]
    if s % 8 == 0:
        z = jnp.zeros((s, x.shape[1]), x.dtype)
        return jnp.concatenate([z, x[:t - s]], axis=0) if down else jnp.concatenate([x[s:], z], axis=0)
    if down:
        return jnp.where(row >= s, pltpu.roll(x, s, 0), 0.0)
    return jnp.where(row < t - s, pltpu.roll(x, t - s, 0), 0.0)


def _s5_scan(xr, xi, pr, pi, cr, ci, row, conj):
    t = xr[0].shape[0]
    sg = -1.0 if conj else 1.0
    edge = (t - 1) if conj else 0
    n = len(xr)
    for k in range(n):
        sl = slice(128 * k, 128 * (k + 1))
        p_r, p_i = pr[0:1, sl], sg * pi[0:1, sl]
        xr[k] = xr[k] + jnp.where(row == edge, p_r * cr[k] - p_i * ci[k], 0.0)
        xi[k] = xi[k] + jnp.where(row == edge, p_r * ci[k] + p_i * cr[k], 0.0)
    for lvl in range(S5_LEVELS):
        s = 1 << lvl
        for k in range(n):
            sl = slice(128 * k, 128 * (k + 1))
            p_r, p_i = pr[lvl:lvl + 1, sl], sg * pi[lvl:lvl + 1, sl]
            sr = _shift_rows(xr[k], s, row, not conj)
            si = _shift_rows(xi[k], s, row, not conj)
            xr[k] = xr[k] + p_r * sr - p_i * si
            xi[k] = xi[k] + p_r * si + p_i * sr
    return xr, xi


def _s5_fwd(name, proj, mats, side=None):
    bm8, cm8, p1, p2, d8 = mats
    n_rows = proj.shape[0]
    t = S5_CHUNK
    nch = n_rows // t

    def body(u_ref, bm_ref, cm_ref, pr_ref, pi_ref, d_ref, y_ref, z_ref, st_ref, carry):
        @pl.when(pl.program_id(1) == 0)
        def _():
            carry[...] = jnp.zeros_like(carry)

        cv = carry[...]
        st_ref[...] = cv
        u = u_ref[...]
        bu = jnp.dot(u.astype(BF16), bm_ref[...], preferred_element_type=F32)
        row = lax.broadcasted_iota(jnp.int32, (t, 128), 0)
        tile = lambda v, j: v[:, 128 * j:128 * (j + 1)]
        xr, xi = _s5_scan([tile(bu, 2 * k) for k in range(4)], [tile(bu, 2 * k + 1) for k in range(4)],
                          pr_ref[...], pi_ref[...], [tile(cv, 2 * k)[0:1] for k in range(4)],
                          [tile(cv, 2 * k + 1)[0:1] for k in range(4)], row, False)
        xall = jnp.concatenate([v for k in range(4) for v in (xr[k], xi[k])], axis=1)
        carry[...] = jnp.broadcast_to(xall[t - 1:t, :], (8, 1024))
        y = jnp.dot(xall.astype(BF16), cm_ref[...], preferred_element_type=F32) + d_ref[...] * u
        y_ref[...] = y
        z_ref[...] = _gelu(y).astype(BF16)

    return _call_hosting(
        side, body, name, (8, nch),
        [
            pl.BlockSpec((t, 128), lambda g, c: (c, g)),
            pl.BlockSpec((None, 128, 1024), lambda g, c: (g, 0, 0)),
            pl.BlockSpec((None, 1024, 128), lambda g, c: (g, 0, 0)),
            pl.BlockSpec((None, 8, 512), lambda g, c: (g, 0, 0)),
            pl.BlockSpec((None, 8, 512), lambda g, c: (g, 0, 0)),
            pl.BlockSpec((None, 1, 128), lambda g, c: (g, 0, 0)),
        ],
        [
            pl.BlockSpec((t, 128), lambda g, c: (c, g)),
            pl.BlockSpec((t, 128), lambda g, c: (c, g)),
            pl.BlockSpec((None, None, 8, 1024), lambda g, c: (g, c, 0, 0)),
        ],
        [_sds((n_rows, S5_WIDTH), F32), _sds((n_rows, S5_WIDTH), BF16), _sds((8, nch, 8, 1024), F32)],
        [pltpu.VMEM((8, 1024), F32)], ("parallel", "arbitrary"), (proj, bm8, cm8, p1, p2, d8))


def _s5_bwd(name, proj, dy, states, mats):
    bm8, cm8, p1, p2, d8 = mats
    n_rows = proj.shape[0]
    t = S5_CHUNK
    nch = n_rows // t
    nt_dims = (((1,), (1,)), ((), ()))
    tn_dims = (((0,), (0,)), ((), ()))

    def body(u_ref, dy_ref, st_ref, bm_ref, cm_ref, pr_ref, pi_ref, d_ref,
             du_ref, dbm_ref, dcm_ref, da_ref, dd_ref, gcarry):
        @pl.when(pl.program_id(1) == 0)
        def _():
            gcarry[...] = jnp.zeros_like(gcarry)
            dbm_ref[...] = jnp.zeros_like(dbm_ref)
            dcm_ref[...] = jnp.zeros_like(dcm_ref)
            da_ref[...] = jnp.zeros_like(da_ref)
            dd_ref[...] = jnp.zeros_like(dd_ref)

        u = u_ref[...]
        dyv = dy_ref[...]
        ub, dyb = u.astype(BF16), dyv.astype(BF16)
        bu = jnp.dot(ub, bm_ref[...], preferred_element_type=F32)
        dxd = lax.dot_general(dyb, cm_ref[...], nt_dims, preferred_element_type=F32)
        row = lax.broadcasted_iota(jnp.int32, (t, 128), 0)
        tile = lambda v, j: v[:, 128 * j:128 * (j + 1)]
        prv, piv, cv, gv = pr_ref[...], pi_ref[...], st_ref[...], gcarry[...]
        cr = [tile(cv, 2 * k)[0:1] for k in range(4)]
        ci = [tile(cv, 2 * k + 1)[0:1] for k in range(4)]
        xr, xi = _s5_scan([tile(bu, 2 * k) for k in range(4)], [tile(bu, 2 * k + 1) for k in range(4)],
                          prv, piv, cr, ci, row, False)
        gr, gi = _s5_scan([tile(dxd, 2 * k) for k in range(4)], [tile(dxd, 2 * k + 1) for k in range(4)],
                          prv, piv, [tile(gv, 2 * k)[0:1] for k in range(4)],
                          [tile(gv, 2 * k + 1)[0:1] for k in range(4)], row, True)
        dar, dai = [], []
        for k in range(4):
            xpr = jnp.where(row >= 1, pltpu.roll(xr[k], 1, 0), cr[k])
            xpi = jnp.where(row >= 1, pltpu.roll(xi[k], 1, 0), ci[k])
            dar.append(_colsum(gr[k] * xpr + gi[k] * xpi))
            dai.append(_colsum(gi[k] * xpr - gr[k] * xpi))
        xall = jnp.concatenate([v for k in range(4) for v in (xr[k], xi[k])], axis=1).astype(BF16)
        gf = jnp.concatenate([v for k in range(4) for v in (gr[k], gi[k])], axis=1)
        gcarry[...] = jnp.broadcast_to(gf[0:1, :], (8, 1024))
        gall = gf.astype(BF16)
        dcm_ref[...] += lax.dot_general(xall, dyb, tn_dims, preferred_element_type=F32)
        dbm_ref[...] += lax.dot_general(ub, gall, tn_dims, preferred_element_type=F32)
        du = lax.dot_general(gall, bm_ref[...], nt_dims, preferred_element_type=F32) + d_ref[...] * dyv
        du_ref[...] = du.astype(BF16)
        dd_ref[...] += _colsum(dyv * u)
        da_ref[0:1, :] += jnp.concatenate(dar, axis=1)
        da_ref[1:2, :] += jnp.concatenate(dai, axis=1)

    rev = lambda g, c: (nch - 1 - c, g)
    return pl.pallas_call(
        body, name=name,
        grid=(8, nch),
        in_specs=[
            pl.BlockSpec((t, 128), rev),
            pl.BlockSpec((t, 128), rev),
            pl.BlockSpec((None, None, 8, 1024), lambda g, c: (g, nch - 1 - c, 0, 0)),
            pl.BlockSpec((None, 128, 1024), lambda g, c: (g, 0, 0)),
            pl.BlockSpec((None, 1024, 128), lambda g, c: (g, 0, 0)),
            pl.BlockSpec((None, 8, 512), lambda g, c: (g, 0, 0)),
            pl.BlockSpec((None, 8, 512), lambda g, c: (g, 0, 0)),
            pl.BlockSpec((None, 1, 128), lambda g, c: (g, 0, 0)),
        ],
        out_specs=[
            pl.BlockSpec((t, 128), rev),
            pl.BlockSpec((None, 128, 1024), lambda g, c: (g, 0, 0)),
            pl.BlockSpec((None, 1024, 128), lambda g, c: (g, 0, 0)),
            pl.BlockSpec((None, 8, 512), lambda g, c: (g, 0, 0)),
            pl.BlockSpec((None, 1, 128), lambda g, c: (g, 0, 0)),
        ],
        out_shape=[_sds((n_rows, S5_WIDTH), BF16), _sds((8, 128, 1024), F32), _sds((8, 1024, 128), F32),
                   _sds((8, 8, 512), F32), _sds((8, 1, 128), F32)],
        scratch_shapes=[pltpu.VMEM((8, 1024), F32)],
        compiler_params=_params(("parallel", "arbitrary")),
    )(proj, dy, states, bm8, cm8, p1, p2, d8)


def _hgrn_lower_bounds(lb_param):
    p = jax.nn.softmax(lb_param, axis=0)
    return jnp.cumsum(p, axis=0) - p[0:1]


def _prefix16(x, r16):
    for s in (1, 2, 4, 8):
        x = x + jnp.where(r16 >= s, pltpu.roll(x, s, 0), 0.0)
    return x


def _suffix16(x, r16):
    n = x.shape[0]
    for s in (1, 2, 4, 8):
        x = x + jnp.where(r16 < HGRN_SUB - s, pltpu.roll(x, n - s, 0), 0.0)
    return x


_NT = (((1,), (1,)), ((), ()))
_TN = (((0,), (0,)), ((), ()))


def _dotf(a, b, dims=(((1,), (0,)), ((), ()))):
    return lax.dot_general(a.astype(BF16), b.astype(BF16), dims, preferred_element_type=F32)


def _hgrn_specs(n_blocks, rev):
    r = HGRN_BLOCK
    blk = (lambda b: n_blocks - 1 - b) if rev else (lambda b: b)
    proj_specs = [pl.BlockSpec((r, 128), (lambda h, b, c=c: (blk(b), 8 * c + h))) for c in (1, 2, 3, 4)]
    lb_spec = pl.BlockSpec((None, 1, 128), lambda h, b: (h, 0, 0))
    gain_spec = pl.BlockSpec((1, 128), lambda h, b: (0, 0))
    row_spec = pl.BlockSpec((r, 128), lambda h, b: (blk(b), h))
    st_spec = pl.BlockSpec((None, None, 128, 128), lambda h, b: (h, blk(b), 0, 0))
    return proj_specs, lb_spec, gain_spec, row_spec, st_spec, blk


def _hgrn_fwd(name, proj, lb, gain, side=None):
    n_rows = proj.shape[0]
    r = HGRN_BLOCK
    nb = n_rows // r
    nsub = r // HGRN_SUB
    proj_specs, lb_spec, gain_spec, row_spec, st_spec, _ = _hgrn_specs(nb, False)

    def body(q_ref, f_ref, i_ref, g_ref, lb_ref, gain_ref, o_ref, y_ref, st_ref, st_scr):
        @pl.when(pl.program_id(1) == 0)
        def _():
            st_scr[...] = jnp.zeros_like(st_scr)

        st_ref[...] = st_scr[...]
        q, f, v, g = q_ref[...], f_ref[...], i_ref[...], g_ref[...]
        lbv = lb_ref[...]
        qs = q * _sigmoid(q)
        fg = lbv + (1.0 - lbv) * _sigmoid(f)
        kk = 1.0 - fg
        r16 = lax.broadcasted_iota(jnp.int32, (r, 128), 0) & (HGRN_SUB - 1)
        b = _prefix16(jnp.log(fg), r16)
        qh = qs * jnp.exp(b)
        rs = lax.broadcasted_iota(jnp.int32, (HGRN_SUB, 128), 0)
        st = st_scr[...]
        outs = []
        for i in range(nsub):
            sl = slice(HGRN_SUB * i, HGRN_SUB * (i + 1))
            qsi, kki, vi, bi = qs[sl], kk[sl], v[sl], b[sl]
            o_i = _dotf(qh[sl], st, _NT)
            for s in range(HGRN_SUB):
                e = jnp.exp(jnp.where(rs >= s, bi - bi[s:s + 1], NEG))
                col = jnp.sum(qsi * e * kki[s:s + 1], axis=1, keepdims=True)
                o_i = o_i + col * vi[s:s + 1]
            bl = bi[HGRN_SUB - 1:HGRN_SUB]
            st = st * jnp.exp(bl) + _dotf(vi, kki * jnp.exp(bl - bi), _TN)
            outs.append(o_i)
        st_scr[...] = st
        o = jnp.concatenate(outs, axis=0)
        o_ref[...] = o
        rn = lax.rsqrt(jnp.mean(o * o, axis=1, keepdims=True) + EPS)
        y_ref[...] = (o * rn * gain_ref[...] * (g * _sigmoid(g))).astype(BF16)

    return _call_hosting(
        side, body, name, (HGRN_HEADS, nb), proj_specs + [lb_spec, gain_spec], [row_spec, row_spec, st_spec],
        [_sds((n_rows, HGRN_WIDTH), F32), _sds((n_rows, HGRN_WIDTH), BF16), _sds((HGRN_HEADS, nb, 128, 128), F32)],
        [pltpu.VMEM((128, 128), F32)], ("parallel", "arbitrary"), (proj, proj, proj, proj, lb, gain))


def _hgrn_bwd(name, proj, lb, gain, o_saved, states, dycat):
    n_rows = proj.shape[0]
    r = HGRN_BLOCK
    nb = n_rows // r
    nsub = r // HGRN_SUB
    proj_specs, lb_spec, gain_spec, row_spec, st_spec, blk = _hgrn_specs(nb, True)
    dy_spec = pl.BlockSpec((r, 128), lambda h, b: (blk(b), 8 + h))
    acc_spec = pl.BlockSpec((None, 1, 128), lambda h, b: (h, 0, 0))

    def body(q_ref, f_ref, i_ref, g_ref, lb_ref, gain_ref, o_ref, st_ref, dy_ref,
             dq_ref, df_ref, di_ref, dg_ref, dlb_ref, dgain_ref, dst_scr, sub_scr):
        @pl.when(pl.program_id(1) == 0)
        def _():
            dst_scr[...] = jnp.zeros_like(dst_scr)
            dlb_ref[...] = jnp.zeros_like(dlb_ref)
            dgain_ref[...] = jnp.zeros_like(dgain_ref)

        q, f, v, g = q_ref[...], f_ref[...], i_ref[...], g_ref[...]
        lbv, gain_v = lb_ref[...], gain_ref[...]
        sq = _sigmoid(q)
        qs = q * sq
        sf = _sigmoid(f)
        fg = lbv + (1.0 - lbv) * sf
        kk = 1.0 - fg
        r16 = lax.broadcasted_iota(jnp.int32, (r, 128), 0) & (HGRN_SUB - 1)
        b = _prefix16(jnp.log(fg), r16)
        eb = jnp.exp(b)
        qh = qs * eb

        o, dy = o_ref[...], dy_ref[...]
        rn = lax.rsqrt(jnp.mean(o * o, axis=1, keepdims=True) + EPS)
        on = o * rn
        sg = _sigmoid(g)
        sil = g * sg
        dgain_ref[...] += _colsum(dy * on * sil)
        dg_ref[...] = (dy * on * gain_v * (sg * (1.0 + g * (1.0 - sg)))).astype(BF16)
        don = dy * gain_v * sil
        do = rn * (don - on * jnp.mean(don * on, axis=1, keepdims=True))

        st = st_ref[...]
        for i in range(nsub):
            sl = slice(HGRN_SUB * i, HGRN_SUB * (i + 1))
            sub_scr[i] = st
            bi = b[sl]
            bl = bi[HGRN_SUB - 1:HGRN_SUB]
            st = st * jnp.exp(bl) + _dotf(v[sl], kk[sl] * jnp.exp(bl - bi), _TN)

        rs = lax.broadcasted_iota(jnp.int32, (HGRN_SUB, 128), 0)
        dst = dst_scr[...]
        parts = [None] * nsub
        for i in reversed(range(nsub)):
            sl = slice(HGRN_SUB * i, HGRN_SUB * (i + 1))
            sp = sub_scr[i]
            qsi, kki, vi, bi, doi, qhi = qs[sl], kk[sl], v[sl], b[sl], do[sl], qh[sl]
            bl = bi[HGRN_SUB - 1:HGRN_SUB]
            ebl = jnp.exp(bl)
            dec = jnp.exp(bl - bi)
            khat = kki * dec
            dqh = _dotf(doi, sp)
            dkhat = _dotf(vi, dst)
            dv = _dotf(khat, dst, _NT)
            zrow = _colsum(sp * dst) * ebl
            dq_in = jnp.zeros((HGRN_SUB, 128), F32)
            dk_in = jnp.zeros((HGRN_SUB, 128), F32)
            dv_in = jnp.zeros((HGRN_SUB, 128), F32)
            for s in range(HGRN_SUB):
                e = jnp.exp(jnp.where(rs >= s, bi - bi[s:s + 1], NEG))
                dpc = jnp.sum(doi * vi[s:s + 1], axis=1, keepdims=True)
                w = qsi * e
                pc = jnp.sum(w * kki[s:s + 1], axis=1, keepdims=True)
                dq_in = dq_in + dpc * e * kki[s:s + 1]
                dk_in = jnp.where(rs == s, _colsum(dpc * w), dk_in)
                dv_in = jnp.where(rs == s, _colsum(pc * doi), dv_in)
            kd = khat * dkhat
            parts[i] = (qsi * dq_in - kki * dk_in + qhi * dqh, kd, jnp.broadcast_to(zrow, (HGRN_SUB, 128)),
                        dq_in + dqh * eb[sl], dk_in + dkhat * dec, dv + dv_in)
            dst = dst * ebl + _dotf(doi, qhi, _TN)
        dst_scr[...] = dst

        cat = lambda j: jnp.concatenate([p[j] for p in parts], axis=0)
        d_b, kd, zr, dqs, dkk, dvv = (cat(j) for j in range(6))
        dlf = _suffix16(d_b, r16) + _prefix16(kd, r16) - kd + zr
        dfg = dlf / fg - dkk
        df_ref[...] = (dfg * (1.0 - lbv) * sf * (1.0 - sf)).astype(BF16)
        dlb_ref[...] += _colsum(dfg * (1.0 - sf))
        dq_ref[...] = (dqs * (sq * (1.0 + q * (1.0 - sq)))).astype(BF16)
        di_ref[...] = dvv.astype(BF16)

    return pl.pallas_call(
        body, name=name,
        grid=(HGRN_HEADS, nb),
        in_specs=proj_specs + [lb_spec, gain_spec, row_spec, st_spec, dy_spec],
        out_specs=[row_spec] * 4 + [acc_spec, acc_spec],
        out_shape=[_sds((n_rows, HGRN_WIDTH), BF16)] * 4 + [_sds((HGRN_HEADS, 1, 128), F32)] * 2,
        scratch_shapes=[pltpu.VMEM((128, 128), F32), pltpu.VMEM((nsub, 128, 128), F32)],
        compiler_params=_params(("parallel", "arbitrary")),
    )(proj, proj, proj, proj, lb, gain, o_saved, states, dycat)


def _alibi_slopes():
    return jnp.exp2(-8.0 * jnp.arange(1, ATT_HEADS + 1, dtype=F32) / ATT_HEADS)


def _swa_specs(n_blocks):
    blk = ATT_BLOCK
    prev = lambda i: jnp.maximum(i - 1, 0)
    smem = pl.BlockSpec(memory_space=pltpu.SMEM)
    return [
        smem, smem,
        pl.BlockSpec((blk, ATT_HEADS * ATT_DIM), lambda i: (i, 0)),
        pl.BlockSpec((blk, 256), lambda i: (i, 8)),
        pl.BlockSpec((blk, 256), lambda i: (prev(i), 8)),
        pl.BlockSpec((blk, 256), lambda i: (i, 9)),
        pl.BlockSpec((blk, 256), lambda i: (prev(i), 9)),
        pl.BlockSpec((1, ATT_DIM), lambda i: (0, 0)),
        pl.BlockSpec((1, ATT_DIM), lambda i: (0, 0)),
    ]


_ATT_GROUP = ATT_HEADS // ATT_KV
_ATT_ROWS = _ATT_GROUP * ATT_BLOCK


def _swa_mask(i):
    t_i = lax.broadcasted_iota(jnp.int32, (_ATT_ROWS, 2 * ATT_BLOCK), 0) & (ATT_BLOCK - 1)
    s_i = lax.broadcasted_iota(jnp.int32, (_ATT_ROWS, 2 * ATT_BLOCK), 1)
    dist = t_i + ATT_BLOCK - s_i
    valid = (dist >= 0) & (dist < ATT_BLOCK) & ((s_i >= ATT_BLOCK) | (i > 0))
    return valid, dist.astype(F32)


def _stack_heads(x):
    return jnp.concatenate([x[:, ATT_DIM * h:ATT_DIM * (h + 1)] for h in range(_ATT_GROUP)], axis=0)


def _unstack_heads(x):
    return jnp.concatenate([x[ATT_BLOCK * h:ATT_BLOCK * (h + 1)] for h in range(_ATT_GROUP)], axis=1)


def _head_column(ref, g):
    return jnp.concatenate([jnp.full((ATT_BLOCK, 1), ref[_ATT_GROUP * g + h], F32) for h in range(_ATT_GROUP)], axis=0)


def _swa_probs(qn, kn, slope, sink, valid, distf):
    s = lax.dot_general(qn, kn, _NT, preferred_element_type=F32) * (ATT_DIM ** -0.5) - slope * distf
    s = jnp.where(valid, s, NEG)
    m = jnp.maximum(jnp.max(s, axis=1, keepdims=True), sink)
    p = jnp.exp(s - m)
    es = jnp.exp(sink - m)
    inv = 1.0 / (jnp.sum(p, axis=1, keepdims=True) + es)
    return p * inv, es * inv


def _swa_fwd(name, qkv, q_gain, k_gain, sinks, slopes):
    n_rows = qkv.shape[0]
    nb = n_rows // ATT_BLOCK

    def body(sink_ref, slope_ref, q_ref, kc_ref, kp_ref, vc_ref, vp_ref, qg_ref, kg_ref, o_ref):
        i = pl.program_id(0)
        kb = jnp.concatenate([kp_ref[...], kc_ref[...]], axis=0)
        vb = jnp.concatenate([vp_ref[...], vc_ref[...]], axis=0)
        valid, distf = _swa_mask(i)
        qgv, kgv = qg_ref[...], kg_ref[...]
        gw = _ATT_GROUP * ATT_DIM
        for g in range(ATT_KV):
            kg = kb[:, 64 * g:64 * (g + 1)]
            rk = lax.rsqrt(jnp.mean(kg * kg, axis=1, keepdims=True) + EPS)
            kn = (kg * rk * kgv).astype(BF16)
            vv = vb[:, 64 * g:64 * (g + 1)].astype(BF16)
            qs = _stack_heads(q_ref[:, gw * g:gw * (g + 1)])
            rq = lax.rsqrt(jnp.mean(qs * qs, axis=1, keepdims=True) + EPS)
            pn, _ = _swa_probs((qs * rq * qgv).astype(BF16), kn, _head_column(slope_ref, g),
                               _head_column(sink_ref, g), valid, distf)
            out = jnp.dot(pn.astype(BF16), vv, preferred_element_type=F32)
            o_ref[:, gw * g:gw * (g + 1)] = _unstack_heads(out).astype(BF16)

    return pl.pallas_call(
        body, name=name,
        grid=(nb,),
        in_specs=_swa_specs(nb),
        out_specs=pl.BlockSpec((ATT_BLOCK, ATT_HEADS * ATT_DIM), lambda i: (i, 0)),
        out_shape=_sds((n_rows, ATT_HEADS * ATT_DIM), BF16),
        compiler_params=_params(("parallel",)),
    )(sinks, slopes, qkv, qkv, qkv, qkv, qkv, q_gain.reshape(1, -1), k_gain.reshape(1, -1))


def _swa_bwd(name, qkv, q_gain, k_gain, sinks, slopes, d_out):
    n_rows = qkv.shape[0]
    nb = n_rows // ATT_BLOCK
    blk = ATT_BLOCK

    def body(sink_ref, slope_ref, q_ref, kc_ref, kp_ref, vc_ref, vp_ref, qg_ref, kg_ref, do_ref,
             dq_ref, dkc_ref, dkp_ref, dvc_ref, dvp_ref, dsink_ref, dqg_ref, dkg_ref):
        i = pl.program_id(0)

        @pl.when(i == 0)
        def _():
            dsink_ref[...] = jnp.zeros_like(dsink_ref)
            dqg_ref[...] = jnp.zeros_like(dqg_ref)
            dkg_ref[...] = jnp.zeros_like(dkg_ref)

        kb = jnp.concatenate([kp_ref[...], kc_ref[...]], axis=0)
        vb = jnp.concatenate([vp_ref[...], vc_ref[...]], axis=0)
        kgv, qgv = kg_ref[...], qg_ref[...]
        valid, distf = _swa_mask(i)
        scale = ATT_DIM ** -0.5
        gw = _ATT_GROUP * ATT_DIM
        dks, dvs = [], []
        dqg, dkg = jnp.zeros((1, ATT_DIM), F32), jnp.zeros((1, ATT_DIM), F32)
        for g in range(ATT_KV):
            kg = kb[:, 64 * g:64 * (g + 1)]
            rk = lax.rsqrt(jnp.mean(kg * kg, axis=1, keepdims=True) + EPS)
            khat = kg * rk
            kn = (khat * kgv).astype(BF16)
            vv = vb[:, 64 * g:64 * (g + 1)].astype(BF16)
            qs = _stack_heads(q_ref[:, gw * g:gw * (g + 1)])
            rq = lax.rsqrt(jnp.mean(qs * qs, axis=1, keepdims=True) + EPS)
            qhat = qs * rq
            qn = (qhat * qgv).astype(BF16)
            pn, ps = _swa_probs(qn, kn, _head_column(slope_ref, g), _head_column(sink_ref, g), valid, distf)
            dos = _stack_heads(do_ref[:, gw * g:gw * (g + 1)]).astype(BF16)
            dp = lax.dot_general(dos, vv, _NT, preferred_element_type=F32)
            delta = jnp.sum(pn * dp, axis=1, keepdims=True)
            ds = (pn * (dp - delta)).astype(BF16)
            sd = ps * delta
            for h in range(_ATT_GROUP):
                hs = _ATT_GROUP * g + h
                dsink_ref[hs:hs + 1, :] += jnp.zeros((1, 128), F32) - jnp.sum(sd[blk * h:blk * (h + 1)])
            dvs.append(lax.dot_general(pn.astype(BF16), dos, _TN, preferred_element_type=F32))
            dkn = lax.dot_general(ds, qn, _TN, preferred_element_type=F32) * scale
            dqn = jnp.dot(ds, kn, preferred_element_type=F32) * scale
            dqg = dqg + _colsum(dqn * qhat)
            dqhat = dqn * qgv
            dqs = rq * (dqhat - qhat * jnp.mean(dqhat * qhat, axis=1, keepdims=True))
            dq_ref[:, gw * g:gw * (g + 1)] = _unstack_heads(dqs).astype(BF16)
            dkg = dkg + _colsum(dkn * khat)
            dkhat = dkn * kgv
            dks.append(rk * (dkhat - khat * jnp.mean(dkhat * khat, axis=1, keepdims=True)))
        dqg_ref[...] += dqg
        dkg_ref[...] += dkg
        dk = jnp.concatenate(dks, axis=1).astype(BF16)
        dv = jnp.concatenate(dvs, axis=1).astype(BF16)
        dkp_ref[...] = dk[:blk]
        dkc_ref[...] = dk[blk:]
        dvp_ref[...] = dv[:blk]
        dvc_ref[...] = dv[blk:]

    kv_spec = pl.BlockSpec((blk, 256), lambda i: (i, 0))
    full = pl.BlockSpec((blk, ATT_HEADS * ATT_DIM), lambda i: (i, 0))
    acc64 = pl.BlockSpec((1, ATT_DIM), lambda i: (0, 0))
    return pl.pallas_call(
        body, name=name,
        grid=(nb,),
        in_specs=_swa_specs(nb) + [full],
        out_specs=[full, kv_spec, kv_spec, kv_spec, kv_spec,
                   pl.BlockSpec((ATT_HEADS, 128), lambda i: (0, 0)), acc64, acc64],
        out_shape=[_sds((n_rows, ATT_HEADS * ATT_DIM), BF16)] + [_sds((n_rows, 256), BF16)] * 4
        + [_sds((ATT_HEADS, 128), F32), _sds((1, ATT_DIM), F32), _sds((1, ATT_DIM), F32)],
        compiler_params=_params(("arbitrary",)),
    )(sinks, slopes, qkv, qkv, qkv, qkv, qkv, q_gain.reshape(1, -1), k_gain.reshape(1, -1), d_out)


def _mesh_pos():
    return lax.axis_index("x"), lax.axis_index("y"), lax.axis_index("c")


_ANY = pl.BlockSpec(memory_space=pl.ANY)


def _allgather(name, shards):
    side = _gather_side(shards)
    n = len(shards)

    def body(*refs):
        side.start(refs[:n], refs[n:2 * n], refs[2 * n:])
        side.finish(refs[:n], refs[n:2 * n], refs[2 * n:])

    return pl.pallas_call(
        body, name=name,
        out_shape=side.out_shapes,
        in_specs=[_ANY] * n,
        out_specs=[_ANY] * n,
        scratch_shapes=side.scratch,
    )(*shards)


class _Side:
    def __init__(self, operands, out_shapes, scratch, start, finish):
        self.operands, self.out_shapes, self.scratch = list(operands), list(out_shapes), list(scratch)
        self.start, self.finish = start, finish


def _gather_side(shards):
    n = len(shards)

    def plan(x_refs, out_refs, sems):
        send_sems, recv_sems, local_sems = sems
        x, y, c = _mesh_pos()
        me, sibling = (x, y, c), (x, y, 1 - c)
        chips = [(1 - x, y), (x, 1 - y), (1 - x, 1 - y)]

        def slot(a, px, py, pc):
            return out_refs[a].at[4 * px + 2 * py + pc]

        def copy(a, k, block, to, src=None):
            return pltpu.make_async_remote_copy(
                src_ref=slot(a, *block) if src is None else src, dst_ref=slot(a, *block),
                send_sem=send_sems.at[7 * a + k], recv_sem=recv_sems.at[7 * a + k],
                device_id=to, device_id_type=MESH)

        local = [pltpu.make_async_copy(x_refs[a], slot(a, *me), local_sems.at[a]) for a in range(n)]
        first = [[copy(a, 0, me, sibling, src=x_refs[a])]
                 + [copy(a, 1 + j, me, (*chip, c), src=x_refs[a]) for j, chip in enumerate(chips)] for a in range(n)]
        from_chips = [[copy(a, 1 + j, (*chip, c), me) for j, chip in enumerate(chips)] for a in range(n)]
        forward = [[copy(a, 4 + j, (*chip, c), sibling) for j, chip in enumerate(chips)] for a in range(n)]
        from_sibling = [[copy(a, 0, sibling, me)] + [copy(a, 4 + j, (*chip, 1 - c), me) for j, chip in enumerate(chips)]
                        for a in range(n)]
        return local, first, from_chips, forward, from_sibling

    def start(x_refs, out_refs, sems):
        local, first, _, _, _ = plan(x_refs, out_refs, sems)
        for a in range(n):
            local[a].start()
            for cp in first[a]:
                cp.start()

    def finish(x_refs, out_refs, sems):
        local, first, from_chips, forward, from_sibling = plan(x_refs, out_refs, sems)
        for a in range(n):
            for j in range(3):
                from_chips[a][j].wait_recv()
                forward[a][j].start()
        for a in range(n):
            for cp in from_sibling[a]:
                cp.wait_recv()
        for a in range(n):
            for cp in first[a] + forward[a]:
                cp.wait_send()
            local[a].wait()

    return _Side(shards, [_sds((N_DEV,) + s.shape, s.dtype) for s in shards],
                 [pltpu.SemaphoreType.DMA((7 * n,)), pltpu.SemaphoreType.DMA((7 * n,)), pltpu.SemaphoreType.DMA((n,))],
                 start, finish)


def _swap_with_sibling(name, arrs):
    n = len(arrs)

    def body(*refs):
        x_refs, got_refs = refs[:n], refs[n:2 * n]
        send_sems, recv_sems = refs[2 * n:]
        x, y, c = _mesh_pos()
        copies = []
        for a in range(n):
            for j in range(4):
                k = 4 * a + j
                cp = pltpu.make_async_remote_copy(
                    src_ref=x_refs[a].at[j, 1 - c], dst_ref=got_refs[a].at[j],
                    send_sem=send_sems.at[k], recv_sem=recv_sems.at[k],
                    device_id=(x, y, 1 - c), device_id_type=MESH)
                cp.start()
                copies.append(cp)
        for cp in copies:
            cp.wait()

    return pl.pallas_call(
        body, name=name,
        out_shape=[_sds((4,) + t.shape[2:], t.dtype) for t in arrs],
        in_specs=[_ANY] * n,
        out_specs=[_ANY] * n,
        scratch_shapes=[pltpu.SemaphoreType.DMA((4 * n,))] * 2,
    )(*arrs)


def _exchange_chips(name, arrs):
    side = _exchange_side(arrs)
    n = len(arrs)

    def body(*refs):
        side.start(refs[:n], refs[n:2 * n], refs[2 * n:])
        side.finish(refs[:n], refs[n:2 * n], refs[2 * n:])

    return pl.pallas_call(
        body, name=name,
        out_shape=side.out_shapes,
        in_specs=[_ANY] * n,
        out_specs=[_ANY] * n,
        scratch_shapes=side.scratch,
    )(*arrs)


def _exchange_side(arrs):
    n = len(arrs)

    def plan(x_refs, out_refs, sems):
        send_sems, recv_sems, local_sems = sems
        x, y, c = _mesh_pos()
        me = 2 * x + y
        local = [pltpu.make_async_copy(x_refs[a].at[me], out_refs[a].at[me], local_sems.at[a]) for a in range(n)]
        sends, recvs = [], []
        for a in range(n):
            for k in range(1, 4):
                px, py = x ^ (k >> 1), y ^ (k & 1)
                peer = 2 * px + py
                sem = 3 * a + k - 1
                sends.append(pltpu.make_async_remote_copy(
                    src_ref=x_refs[a].at[peer], dst_ref=out_refs[a].at[me],
                    send_sem=send_sems.at[sem], recv_sem=recv_sems.at[sem],
                    device_id=(px, py, c), device_id_type=MESH))
                recvs.append(pltpu.make_async_remote_copy(
                    src_ref=x_refs[a].at[peer], dst_ref=out_refs[a].at[peer],
                    send_sem=send_sems.at[sem], recv_sem=recv_sems.at[sem],
                    device_id=(px, py, c), device_id_type=MESH))
        return local, sends, recvs

    def start(x_refs, out_refs, sems):
        local, sends, _ = plan(x_refs, out_refs, sems)
        for cp in local + sends:
            cp.start()

    def finish(x_refs, out_refs, sems):
        local, sends, recvs = plan(x_refs, out_refs, sems)
        for cp in recvs:
            cp.wait_recv()
        for cp in sends:
            cp.wait_send()
        for cp in local:
            cp.wait()

    return _Side(arrs, [_sds(t.shape, t.dtype) for t in arrs],
                 [pltpu.SemaphoreType.DMA((3 * n,)), pltpu.SemaphoreType.DMA((3 * n,)), pltpu.SemaphoreType.DMA((n,))],
                 start, finish)


def _sum_blocks(name, blocks, out_dtype=F32):
    n, n_rows, n_cols = blocks.shape
    tr = _row_tile(n_rows)

    def body(x_ref, o_ref):
        acc = x_ref[0].astype(F32)
        for s in range(1, n):
            acc = acc + x_ref[s].astype(F32)
        o_ref[...] = acc.astype(o_ref.dtype)

    return pl.pallas_call(
        body, name=name,
        grid=(n_rows // tr,),
        in_specs=[pl.BlockSpec((n, tr, n_cols), lambda i: (0, i, 0))],
        out_specs=pl.BlockSpec((tr, n_cols), lambda i: (i, 0)),
        out_shape=_sds((n_rows, n_cols), out_dtype),
        compiler_params=_params(("parallel",)),
    )(blocks)


def _add_pair(name, mine, got, core):
    n, n_rows, n_cols = got.shape
    tr = _row_tile(n_rows)

    def body(core_ref, a_ref, b_ref, o_ref):
        o_ref[...] = (a_ref[...].astype(F32) + b_ref[...].astype(F32)).astype(BF16)

    spec = pl.BlockSpec((None, tr, n_cols), lambda j, i, core_ref: (j, i, 0))
    return pl.pallas_call(
        body, name=name,
        grid_spec=pltpu.PrefetchScalarGridSpec(
            num_scalar_prefetch=1,
            grid=(n, n_rows // tr),
            in_specs=[pl.BlockSpec((None, None, tr, n_cols), lambda j, i, core_ref: (j, core_ref[0], i, 0)), spec],
            out_specs=spec,
        ),
        out_shape=_sds(got.shape, BF16),
        compiler_params=_params(("parallel", "parallel")),
    )(core, mine, got)


_MIX_PARTS = (
    (("w_in", "even_w_in", 1, D_MODEL, 5120), ("w_glu", "s5_w_glu", 0, S5_WIDTH, S5_WIDTH),
     ("w_out", "even_w_out", 0, D_MODEL, D_MODEL)),
    (("w_qkv", "odd_w_qkv", 1, D_MODEL, QKV_WIDTH), ("w_out", "odd_w_out", 0, D_MODEL, D_MODEL)),
)
_PACK_COLS = 1024
_FF_SHARD = D_FF // N_DEV
_BIG_NAMES = ("even_w_in", "s5_w_glu", "even_w_out", "odd_w_qkv", "odd_w_out")


def _part_rows(rows, cols):
    return rows * cols // N_DEV // _PACK_COLS


def _row_tile(n_rows):
    return next(t for t in (512, 480, 384, 256, 128) if n_rows % t == 0)


def _pack_mixer_shard(kind, j, args):
    return jnp.concatenate([args[name][j].astype(BF16).reshape(-1, _PACK_COLS) for _, name, _, _, _ in _MIX_PARTS[kind]],
                           axis=0)


def _unpack_mixer(kind, gathered):
    out, off = {}, 0
    for key, _, axis, rows, cols in _MIX_PARTS[kind]:
        n = _part_rows(rows, cols)
        part = gathered[:, off:off + n]
        off += n
        if axis == 1:
            part = part.reshape(N_DEV, rows, cols // N_DEV).transpose(1, 0, 2)
        out[key] = part.reshape(rows, cols)
    return out


def _pack_mixer_grads(kind, g):
    parts = []
    for key, _, axis, rows, cols in _MIX_PARTS[kind]:
        t = g[key]
        if axis == 1:
            t = t.reshape(rows, N_DEV, cols // N_DEV).transpose(1, 0, 2)
        parts.append(t.reshape(N_DEV, -1, _PACK_COLS))
    return jnp.concatenate(parts, axis=1)


def _unpack_mixer_grads(kind, flat):
    out, off = {}, 0
    for _, name, axis, rows, cols in _MIX_PARTS[kind]:
        n = _part_rows(rows, cols)
        shape = (rows, cols // N_DEV) if axis == 1 else (rows // N_DEV, cols)
        out[name] = flat[off:off + n].reshape(shape)
        off += n
    return out


def _pack_small(arrs, row_mult=512):
    parts = []
    for a in arrs:
        f = a.astype(F32).reshape(-1)
        parts.append(jnp.pad(f, (0, (-f.shape[0]) % 128)))
    f = jnp.concatenate(parts)
    f = jnp.pad(f, (0, (-f.shape[0]) % (128 * row_mult)))
    return f.reshape(-1, 128)


def _unpack_small(flat, shapes):
    f = flat.reshape(-1)
    out, off = [], 0
    for s in shapes:
        n = math.prod(s)
        out.append(f[off:off + n].reshape(s))
        off += n + (-n) % 128
    return out


_WEIGHTS = ("even_norm", "even_w_in", "s5_lambda_re", "s5_lambda_im", "s5_log_dt", "s5_b_re", "s5_b_im",
            "s5_c_re", "s5_c_im", "s5_d", "s5_w_glu", "s5_b_glu", "hgrn_lower_bound", "hgrn_o_norm",
            "even_w_out", "odd_norm", "odd_w_qkv", "q_norm", "k_norm", "att_sinks", "odd_w_out",
            "mlp_norm", "mlp_w_up", "mlp_w_down")
_MLP_NAMES = ("mlp_w_up", "mlp_w_down")
_SMALL_NAMES = tuple(n for n in _WEIGHTS if n not in _BIG_NAMES + _MLP_NAMES)


def _add_res(acc, res):
    return (acc + res,)


def _mm_hosting(side, *args, **kw):
    if side is None:
        return _mm(*args, **kw), None
    return _mm(*args, side=side, **kw)


def _mlp_fwd(h, xn, gain, next_gain, w_up, w_down, sides=(None, None)):
    n_rows, fs = h.shape[0], _FF_SHARD
    (up, act), got_up = _mm_hosting(
        sides[0], "mm_up", xn, w_up, "nn", out_dtypes=(F32, BF16), mkn=(n_rows, D_MODEL, D_FF), tn=fs,
        b_block=pl.BlockSpec((None, D_MODEL, fs), lambda i, j, kk: (j, kk, 0)),
        epi=lambda acc: (acc, jnp.square(jnp.maximum(acc, 0.0))))
    w_down4 = w_down.reshape(N_DEV // 2, 2 * fs, D_MODEL)
    if next_gain is None:
        out, got_down = _mm_hosting(
            sides[1], "mm_down_last", act, w_down4, "nn", mkn=(n_rows, D_FF, D_MODEL), tk=2 * fs,
            b_block=pl.BlockSpec((None, 2 * fs, 1024), lambda i, j, kk: (kk, 0, j)), epi=_add_res, extras=(h,))
        xn_next = None
    else:
        (out, xn_next), got_down = _mm_hosting(
            sides[1], "mm_down", act, w_down4, "nn", out_dtypes=(F32, BF16), mkn=(n_rows, D_FF, D_MODEL),
            tk=2 * fs, tn=D_MODEL, b_block=pl.BlockSpec((None, 2 * fs, D_MODEL), lambda i, j, kk: (kk, 0, 0)),
            epi=_res_norm, extras=(h,), row_vecs=(next_gain,))
    return out, xn_next, (h, gain, xn, up, act, w_up, w_down), got_up, got_down


def _mlp_bwd(cache, dh, dhb, sides=(None, None, None)):
    h, gain, xn, up, act, w_up, w_down = cache
    n_rows, fs = h.shape[0], _FF_SHARD
    dup, got0 = _mm_hosting(
        sides[0], "mm_dact", dhb, w_down, "nt", out_dtypes=(BF16,), mkn=(n_rows, D_MODEL, D_FF), tn=fs,
        b_block=pl.BlockSpec((None, fs, D_MODEL), lambda i, j, kk: (j, 0, kk)),
        epi=lambda acc, u: (acc * (2.0 * jnp.maximum(u, 0.0)),), extras=(up,))
    dw_down, got1 = _mm_hosting(
        sides[1], "mm_dw_down", act, dhb, "tn", out_dtypes=(BF16,),
        o_block=(pl.BlockSpec((None, 512, 1024), lambda i, j, kk: (i // 2, i % 2, j)), (N_DEV, fs, D_MODEL)))
    (dh_in, dhb_in, dgain), got2 = _mm_hosting(
        sides[2], "mm_dxn_up", dup, w_up, "nt", out_dtypes=(F32, BF16), mkn=(n_rows, D_FF, D_MODEL),
        tm=256, tn=D_MODEL, tk=2 * fs,
        b_block=pl.BlockSpec((None, D_MODEL, fs), lambda i, j, kk: (2 * kk, 0, 0)),
        b2_block=pl.BlockSpec((None, D_MODEL, fs), lambda i, j, kk: (2 * kk + 1, 0, 0)),
        epi=_rms_bwd_epi, extras=(h, dh), row_vecs=(gain,), col_acc=True)
    dw_up = _mm("mm_dw_up", xn, dup, "tn", out_dtypes=(BF16,),
                o_block=(pl.BlockSpec((None, 512, fs), lambda i, j, kk: (j, i, 0)), (N_DEV, D_MODEL, fs)))
    return dh_in, dhb_in, dgain, dw_up, dw_down, (got0, got1, got2)


def _even_fwd(h, xn, p, sides=(None, None, None)):
    proj, got_in = _mm_hosting(sides[0], "mm_w_in", xn, p["w_in"], "nn")
    (y_pre, z, s5_states), got_s5 = _s5_fwd("s5_fwd", proj, p["mats"], sides[1])
    gate, ya = _mm("mm_glu", z, p["w_glu"], "nn", out_dtypes=(F32, BF16), extras=(y_pre,), row_vecs=(p["b_glu"],),
                   epi=lambda acc, y, b: (acc, _gelu(y) * _sigmoid(acc + b)))
    (o, yb, h_states), got_h = _hgrn_fwd("hgrn_fwd", proj, p["lb"].reshape(8, 1, 128), p["o_gain"].reshape(1, 128),
                                         sides[2])
    ycat = jnp.concatenate([ya, yb], axis=1)
    out, xn_mlp = _mm("mm_w_out", ycat, p["w_out"], "nn", out_dtypes=(F32, BF16), tn=D_MODEL, epi=_res_norm,
                      extras=(h,), row_vecs=(p["mlp_gain"],))
    return out, xn_mlp, (h, xn, proj, y_pre, z, s5_states, gate, o, h_states, ycat), (got_in, got_s5, got_h)


def _even_bwd(cache, p, dh, dhb, sides=(None, None)):
    h, xn, proj, y_pre, z, s5_states, gate, o, h_states, ycat = cache
    g = {}
    dycat = _mm("mm_dy_out", dhb, p["w_out"], "nt", tn=D_MODEL)
    g["w_out"] = _mm("mm_dw_out", ycat, dhb, "tn", out_dtypes=(BF16,))
    dq, df, di, dg, dlb, dgain = _hgrn_bwd("hgrn_bwd", proj, p["lb"].reshape(8, 1, 128),
                                           p["o_gain"].reshape(1, 128), o, h_states, dycat)
    g["lb"] = dlb.reshape(-1)
    g["o_gain"] = jnp.sum(dgain, axis=0).reshape(-1)

    def glu_bwd1(dyc, y, gt, b):
        zf = _gelu(y)
        s = _sigmoid(gt + b)
        dya = dyc[:, :S5_WIDTH]
        d_gate = dya * zf * s * (1.0 - s)
        return (d_gate, dya * s), (_colsum(d_gate),)

    d_gate, dz_direct, db_glu = _rowwise("glu_bwd_gate", glu_bwd1, [dycat, y_pre, gate], [p["b_glu"].reshape(1, -1)],
                                         [(S5_WIDTH, BF16), (S5_WIDTH, F32)], [S5_WIDTH])
    g["b_glu"] = db_glu.reshape(-1)
    dy_pre = _mm("mm_dz_glu", d_gate, p["w_glu"], "nt", extras=(dz_direct, y_pre),
                 epi=lambda acc, dzd, y: ((acc + dzd) * _gelu_grad(y),))
    g["w_glu"] = _mm("mm_dw_glu", z, d_gate, "tn", out_dtypes=(BF16,))
    du, dbm, dcm, da, dd = _s5_bwd("s5_bwd", proj, dy_pre, s5_states, p["mats"])
    g["s5"] = (dbm, dcm, da, dd)
    dproj = jnp.concatenate([du, dq, df, di, dg], axis=1)
    (dh_in, dhb_in, dnorm), got0 = _mm_hosting(
        sides[0], "mm_dxn_in", dproj, p["w_in"], "nt", out_dtypes=(F32, BF16), tm=256, tn=D_MODEL, tk=2560,
        epi=_rms_bwd_epi, extras=(h, dh), row_vecs=(p["norm"],), col_acc=True)
    g["w_in"], got1 = _mm_hosting(sides[1], "mm_dw_in", xn, dproj, "tn", out_dtypes=(BF16,))
    g["norm"] = dnorm.reshape(-1)
    return dh_in, dhb_in, g, (got0, got1)


def _odd_fwd(h, xn, p, sides=(None, None, None)):
    qkv, got = _mm_hosting(sides[0], "mm_w_qkv", xn, p["w_qkv"], "nn", tn=1280)
    o = _swa_fwd("swa_fwd", qkv, p["q_gain"], p["k_gain"], p["sinks"], p["slopes"])
    out, xn_mlp = _mm("mm_w_out", o, p["w_out"], "nn", out_dtypes=(F32, BF16), tn=D_MODEL, epi=_res_norm,
                      extras=(h,), row_vecs=(p["mlp_gain"],))
    return out, xn_mlp, (h, xn, qkv, o), (got, None, None)


def _shift_up_block(x):
    return jnp.concatenate([x[ATT_BLOCK:], jnp.zeros((ATT_BLOCK, x.shape[1]), x.dtype)], axis=0)


def _odd_bwd(cache, p, dh, dhb, sides=(None, None)):
    h, xn, qkv, o = cache
    g = {}
    d_o = _mm("mm_dy_out", dhb, p["w_out"], "nt", tn=D_MODEL)
    g["w_out"] = _mm("mm_dw_out", o, dhb, "tn", out_dtypes=(BF16,))
    dq, dkc, dkp, dvc, dvp, dsink, dqg, dkg = _swa_bwd("swa_bwd", qkv, p["q_gain"], p["k_gain"], p["sinks"],
                                                       p["slopes"], d_o)
    dk = (dkc.astype(F32) + _shift_up_block(dkp).astype(F32)).astype(BF16)
    dv = (dvc.astype(F32) + _shift_up_block(dvp).astype(F32)).astype(BF16)
    g["sinks"], g["q_gain"], g["k_gain"] = dsink[:, 0], dqg.reshape(-1), dkg.reshape(-1)
    dqkv = jnp.concatenate([dq, dk, dv], axis=1)
    dh_in, dhb_in, dnorm = _mm("mm_dxn_qkv", dqkv, p["w_qkv"], "nt", out_dtypes=(F32, BF16), tm=256, tn=D_MODEL,
                               tk=1280, epi=_rms_bwd_epi, extras=(h, dh), row_vecs=(p["norm"],), col_acc=True)
    g["w_qkv"] = _mm("mm_dw_qkv", xn, dqkv, "tn", out_dtypes=(BF16,), tn=1280)
    g["norm"] = dnorm.reshape(-1)
    return dh_in, dhb_in, g, (None, None)


def kernel(x, even_norm, even_w_in, s5_lambda_re, s5_lambda_im, s5_log_dt, s5_b_re, s5_b_im, s5_c_re, s5_c_im, s5_d, s5_w_glu, s5_b_glu, hgrn_lower_bound, hgrn_o_norm, even_w_out, odd_norm, odd_w_qkv, q_norm, k_norm, att_sinks, odd_w_out, mlp_norm, mlp_w_up, mlp_w_down, loss_target, m_even_norm, m_even_w_in, m_s5_lambda_re, m_s5_lambda_im, m_s5_log_dt, m_s5_b_re, m_s5_b_im, m_s5_c_re, m_s5_c_im, m_s5_d, m_s5_w_glu, m_s5_b_glu, m_hgrn_lower_bound, m_hgrn_o_norm, m_even_w_out, m_odd_norm, m_odd_w_qkv, m_q_norm, m_k_norm, m_att_sinks, m_odd_w_out, m_mlp_norm, m_mlp_w_up, m_mlp_w_down, v_even_norm, v_even_w_in, v_s5_lambda_re, v_s5_lambda_im, v_s5_log_dt, v_s5_b_re, v_s5_b_im, v_s5_c_re, v_s5_c_im, v_s5_d, v_s5_w_glu, v_s5_b_glu, v_hgrn_lower_bound, v_hgrn_o_norm, v_even_w_out, v_odd_norm, v_odd_w_qkv, v_q_norm, v_k_norm, v_att_sinks, v_odd_w_out, v_mlp_norm, v_mlp_w_up, v_mlp_w_down):
    a = dict(locals())
    n_rows = x.shape[1]
    xi, yi, ci = _mesh_pos()
    me = 4 * xi + 2 * yi + ci

    chunks = [[_pack_mixer_shard(layer % 2, layer // 2, a), mlp_w_up[layer].astype(BF16),
               mlp_w_down[layer].astype(BF16)] for layer in range(DEPTH)]
    gathered = list(_allgather("gather_layer", chunks[0][:1])) + [None, None]
    (odd_gathered,) = _allgather("gather_odd_norm", [jnp.pad(odd_norm, ((0, 6), (0, 0)))])
    odd_norm_full = odd_gathered[:, :2].transpose(1, 0, 2).reshape(2, D_MODEL)

    lower_bounds, lb_vjp = jax.vjp(_hgrn_lower_bounds, hgrn_lower_bound)
    slopes = _alibi_slopes()
    s5_vjps = []

    h = x.reshape(n_rows, D_MODEL)
    mixer_gain = [even_norm[layer // 2] if layer % 2 == 0 else odd_norm_full[layer // 2] for layer in range(DEPTH)]
    xn = _rms_fwd("rms_fwd", h, mixer_gain[0])
    caches, layer_p = [], []
    for layer in range(DEPTH):
        kind, j = layer % 2, layer // 2
        wl = _unpack_mixer(kind, gathered[0])
        nxt = chunks[layer + 1] if layer + 1 < DEPTH else None
        sides = [_gather_side([t]) for t in nxt] if nxt is not None else [None] * 3
        late = [_gather_side([t]) if gathered[i] is None else None for i, t in ((1, chunks[layer][1]), (2, chunks[layer][2]))]
        if kind == 0:
            disc, vjp = jax.vjp(_s5_discretize, s5_lambda_re[j], s5_lambda_im[j], s5_log_dt[j], s5_b_re[j], s5_b_im[j])
            s5_vjps.append(vjp)
            p = dict(norm=mixer_gain[layer], w_in=wl["w_in"], w_glu=wl["w_glu"], b_glu=s5_b_glu[j],
                     mats=_s5_matrices(*disc, s5_c_re[j], s5_c_im[j], s5_d[j]),
                     lb=lower_bounds[j], o_gain=hgrn_o_norm[j], w_out=wl["w_out"], mlp_gain=mlp_norm[layer])
            h, xn, c_mix, got_mix = _even_fwd(h, xn, p, [sides[0]] + late)
        else:
            p = dict(norm=mixer_gain[layer], w_qkv=wl["w_qkv"], q_gain=q_norm[j], k_gain=k_norm[j],
                     sinks=att_sinks[j], slopes=slopes, w_out=wl["w_out"], mlp_gain=mlp_norm[layer])
            h, xn, c_mix, got_mix = _odd_fwd(h, xn, p, [sides[0], None, None])
        w_up_g = gathered[1] if gathered[1] is not None else got_mix[1][0]
        w_down_g = gathered[2] if gathered[2] is not None else got_mix[2][0]
        next_gain = mixer_gain[layer + 1] if layer + 1 < DEPTH else None
        h, xn, c_mlp, got_up, got_down = _mlp_fwd(h, xn, mlp_norm[layer], next_gain, w_up_g, w_down_g, sides[1:])
        caches.append((c_mix, c_mlp))
        layer_p.append(p)
        if nxt is not None:
            gathered = [got_mix[0][0], got_up[0], got_down[0]]
    dh, dhb, sq = _loss_head(h, loss_target.reshape(n_rows, D_MODEL))
    loss = lax.psum(0.5 * sq[0, 0] / D_MODEL, ("x", "y", "c"))

    core = ci.astype(jnp.int32).reshape(1)
    mix_g, mlp_norm_g, received = [None] * DEPTH, [None] * DEPTH, [None] * DEPTH
    pending = None
    for layer in reversed(range(DEPTH)):
        kind = layer % 2
        c_mix, c_mlp = caches[layer]
        sides = [_exchange_side([t]) for t in pending] if pending is not None else [None] * 3
        dh, dhb, d_mlp_norm, dw_up, dw_down, got = _mlp_bwd(c_mlp, dh, dhb, sides)
        if pending is not None:
            received[layer + 1] = [g[0] for g in got]
        mlp_norm_g[layer] = d_mlp_norm.reshape(-1)
        by_chip = [t.reshape((4, 2) + t.shape[1:]) for t in (dw_up, dw_down)]
        early = [None, None]
        if layer == 0:
            arrived = _swap_with_sibling("swap_grads", by_chip)
            early = [_exchange_side([_add_pair("add_sibling_grads", m, g, core)]) for m, g in zip(by_chip, arrived)]
        bwd = _even_bwd if kind == 0 else _odd_bwd
        dh, dhb, mix_g[layer], got = bwd(c_mix, layer_p[layer], dh, dhb, early)
        by_chip = [_pack_mixer_grads(kind, mix_g[layer]).reshape(4, 2, -1, _PACK_COLS)] + (by_chip if layer > 0 else [])
        arrived = _swap_with_sibling("swap_grads", by_chip)
        pending = [_add_pair("add_sibling_grads", m, g, core) for m, g in zip(by_chip, arrived)]
    received[0] = list(_exchange_chips("exchange_grads", pending)) + [got[0][0], got[1][0]]
    grad_x = dh.reshape(x.shape)

    ev, od = [mix_g[0], mix_g[2]], [mix_g[1], mix_g[3]]
    sums = [[_sum_blocks("sum_grads", r) for r in received[layer]] for layer in range(DEPTH)]
    grads = {"mlp_w_up": jnp.stack([s[1] for s in sums]), "mlp_w_down": jnp.stack([s[2] for s in sums])}
    per_layer = [_unpack_mixer_grads(layer % 2, sums[layer][0]) for layer in range(DEPTH)]
    for name in _BIG_NAMES:
        grads[name] = jnp.stack([g[name] for g in per_layer if name in g])

    s5_g = []
    for j in range(2):
        dar, dai, dbbr, dbbi, dcr, dci, dd = _s5_unpack_grads(*ev[j]["s5"])
        s5_g.append(tuple(s5_vjps[j]((dar, dai, dbbr, dbbi))) + (dcr, dci, dd))
    (d_lb_param,) = lb_vjp(jnp.stack([g["lb"] for g in ev]))
    small = {
        "even_norm": jnp.stack([g["norm"] for g in ev]),
        "s5_lambda_re": jnp.stack([g[0] for g in s5_g]), "s5_lambda_im": jnp.stack([g[1] for g in s5_g]),
        "s5_log_dt": jnp.stack([g[2] for g in s5_g]), "s5_b_re": jnp.stack([g[3] for g in s5_g]),
        "s5_b_im": jnp.stack([g[4] for g in s5_g]), "s5_c_re": jnp.stack([g[5] for g in s5_g]),
        "s5_c_im": jnp.stack([g[6] for g in s5_g]), "s5_d": jnp.stack([g[7] for g in s5_g]),
        "s5_b_glu": jnp.stack([g["b_glu"] for g in ev]), "hgrn_lower_bound": d_lb_param,
        "hgrn_o_norm": jnp.stack([g["o_gain"] for g in ev]), "odd_norm": jnp.stack([g["norm"] for g in od]),
        "q_norm": jnp.stack([g["q_gain"] for g in od]), "k_norm": jnp.stack([g["k_gain"] for g in od]),
        "att_sinks": jnp.stack([g["sinks"] for g in od]), "mlp_norm": jnp.stack(mlp_norm_g),
    }
    small_shapes = [small[n].shape for n in _SMALL_NAMES]
    (small_all,) = _allgather("gather_small_grads", [_pack_small([small[n] for n in _SMALL_NAMES])])
    small_sum = _sum_blocks("sum_small_grads", small_all)
    for n, g in zip(_SMALL_NAMES, _unpack_small(small_sum, small_shapes)):
        grads[n] = g
    grads["odd_norm"] = lax.dynamic_slice_in_dim(grads["odd_norm"], me * (D_MODEL // N_DEV), D_MODEL // N_DEV, axis=1)

    delta, new_m, new_v = {}, {}, {}
    for name in _BIG_NAMES + _MLP_NAMES:
        to2d = lambda t, c=a[name].shape[-1]: t.reshape(-1, c)
        d_, m_, v_ = _adamw("adamw_" + name, to2d(a[name]), to2d(grads[name]), to2d(a["m_" + name]), to2d(a["v_" + name]))
        delta[name], new_m[name], new_v[name] = (t.reshape(a[name].shape) for t in (d_, m_, v_))
    packed = [_pack_small([src[n] for n in _SMALL_NAMES])
              for src in (a, grads, {n: a["m_" + n] for n in _SMALL_NAMES}, {n: a["v_" + n] for n in _SMALL_NAMES})]
    shapes = [a[n].shape for n in _SMALL_NAMES]
    for dst, flat in zip((delta, new_m, new_v), _adamw("adamw_small", *packed)):
        for n, t in zip(_SMALL_NAMES, _unpack_small(flat, shapes)):
            dst[n] = t

    return (loss, grad_x, *[grads[n] for n in _WEIGHTS], *[delta[n] for n in _WEIGHTS],
            *[new_m[n] for n in _WEIGHTS], *[new_v[n] for n in _WEIGHTS])
```

```python
import math

import jax
import jax.numpy as jnp
from jax import lax
from jax.experimental import pallas as pl
from jax.experimental.pallas import tpu as pltpu

F32 = jnp.float32
BF16 = jnp.bfloat16
MESH = pl.DeviceIdType.MESH

D_MODEL = 2048
DEPTH = 4
EPS = 1e-6
S5_WIDTH = 1024
S5_GROUPS = 64
S5_STATE = 64
S5_GROUP_SIZE = 16
S5_MIN_DECAY = 1e-4
S5_CHUNK = 128
S5_LEVELS = 7
HGRN_WIDTH = 1024
HGRN_HEADS = 8
HGRN_DIM = 128
HGRN_SUB = 16
HGRN_BLOCK = 128
ATT_HEADS = 32
ATT_KV = 4
ATT_DIM = 64
ATT_BLOCK = 128
QKV_WIDTH = (ATT_HEADS + 2 * ATT_KV) * ATT_DIM
D_FF = 4 * D_MODEL
N_DEV = 8
NEG = -1e30
VMEM_LIMIT = 56 * 1024 * 1024

ADAM_LR, ADAM_B1, ADAM_B2, ADAM_EPS, ADAM_WD, ADAM_STEP = 0.001, 0.9, 0.999, 1e-08, 0.01, 10


def _params(sem=None):
    return pltpu.CompilerParams(dimension_semantics=sem, vmem_limit_bytes=VMEM_LIMIT)


def _sds(shape, dtype):
    return jax.ShapeDtypeStruct(shape, dtype)


def _call_hosting(side, body, name, grid, in_specs, out_specs, out_shape, scratch_shapes, sem, operands):
    if side is None:
        return pl.pallas_call(body, name=name, grid=grid, in_specs=in_specs, out_specs=out_specs, out_shape=out_shape,
                              scratch_shapes=scratch_shapes, compiler_params=_params(sem))(*operands), None
    n_in, n_out, n_scr = len(in_specs), len(out_specs), len(scratch_shapes)
    n_sin, n_sout = len(side.operands), len(side.out_shapes)
    any_spec = pl.BlockSpec(memory_space=pl.ANY)

    def hosting(*refs):
        ins, refs = refs[:n_in], refs[n_in:]
        sin, refs = refs[:n_sin], refs[n_sin:]
        outs, refs = refs[:n_out], refs[n_out:]
        sout, refs = refs[:n_sout], refs[n_sout:]
        scr, sems = refs[:n_scr], refs[n_scr:]
        ids = [pl.program_id(d) for d in range(len(grid))]
        first, last = ids[0] == 0, ids[0] == grid[0] - 1
        for d in range(1, len(grid)):
            first, last = first & (ids[d] == 0), last & (ids[d] == grid[d] - 1)

        @pl.when(first)
        def _():
            side.start(sin, sout, sems)

        body(*ins, *outs, *scr)

        @pl.when(last)
        def _():
            side.finish(sin, sout, sems)

    outs = pl.pallas_call(
        hosting, name=name, grid=grid,
        in_specs=list(in_specs) + [any_spec] * n_sin,
        out_specs=list(out_specs) + [any_spec] * n_sout,
        out_shape=list(out_shape) + side.out_shapes,
        scratch_shapes=list(scratch_shapes) + side.scratch,
        compiler_params=_params(("arbitrary",) * len(grid)),
    )(*operands, *side.operands)
    return outs[:n_out], outs[n_out:]


def _mm(name, a, b, mode, out_dtypes=(F32,), epi=None, extras=(), tm=512, tn=1024, tk=2048,
        mkn=None, b_block=None, b2_block=None, o_block=None, side=None, row_vecs=(), col_acc=False):
    if mkn is not None:
        m, k, n = mkn
    elif mode == "nn":
        (m, k), n = a.shape, b.shape[1]
    elif mode == "nt":
        (m, k), n = a.shape, b.shape[0]
    else:
        (k, m), n = a.shape, b.shape[1]
    tm, tn, tk = min(tm, m), min(tn, n), min(tk, k)
    assert m % tm == 0 and n % tn == 0 and k % tk == 0, (name, m, n, k)
    nk = k // tk
    if mode == "nn":
        a_spec = pl.BlockSpec((tm, tk), lambda i, j, kk: (i, kk))
        b_spec = pl.BlockSpec((tk, tn), lambda i, j, kk: (kk, j))
        dims = (((1,), (0,)), ((), ()))
    elif mode == "nt":
        a_spec = pl.BlockSpec((tm, tk), lambda i, j, kk: (i, kk))
        b_spec = pl.BlockSpec((tn, tk), lambda i, j, kk: (j, kk))
        dims = (((1,), (1,)), ((), ()))
    else:
        a_spec = pl.BlockSpec((tk, tm), lambda i, j, kk: (kk, i))
        b_spec = pl.BlockSpec((tk, tn), lambda i, j, kk: (kk, j))
        dims = (((0,), (0,)), ((), ()))
    o_spec = pl.BlockSpec((tm, tn), lambda i, j, kk: (i, j))
    if b_block is not None:
        b_spec = b_block
    out_specs = [o_spec] * len(out_dtypes)
    out_shape = [_sds((m, n), dt) for dt in out_dtypes]
    if o_block is not None:
        assert len(out_dtypes) == 1 and not extras
        out_specs, out_shape = [o_block[0]], [_sds(o_block[1], out_dtypes[0])]
    n_ex, n_out = len(extras) + len(row_vecs), len(out_dtypes)
    n_b = 1 if b2_block is None else 2
    grid = (m // tm, n // tn, nk)
    vec_spec = pl.BlockSpec((1, tn), lambda i, j, kk: (0, j))
    if col_acc:
        assert tn == n
        out_specs, out_shape = out_specs + [vec_spec], out_shape + [_sds((1, n), F32)]
    n_all = n_out + (1 if col_acc else 0)

    def body(*refs):
        a_ref, b_refs = refs[0], refs[1:1 + n_b]
        pos = 1 + n_b
        ex_refs = refs[pos:pos + n_ex]
        pos += n_ex
        out_refs = refs[pos:pos + n_out]
        col_ref = refs[pos + n_out] if col_acc else None
        pos += n_all
        acc_ref = refs[pos] if nk > 1 else None
        av = a_ref[...]
        if av.dtype != BF16:
            av = av.astype(BF16)
        part = None
        for q, b_ref in enumerate(b_refs):
            bv = b_ref[...]
            if bv.dtype != BF16:
                bv = bv.astype(BF16)
            aq = av if n_b == 1 else av[:, q * (tk // 2):(q + 1) * (tk // 2)]
            d = lax.dot_general(aq, bv, dims, preferred_element_type=F32)
            part = d if part is None else part + d

        def finish(acc):
            outs = epi(acc, *[r[...] for r in ex_refs]) if epi is not None else (acc,)
            for r, o in zip(out_refs, outs):
                r[...] = o.astype(r.dtype)
            if col_acc:
                row_tile = pl.program_id(0)

                @pl.when(row_tile == 0)
                def _():
                    col_ref[...] = outs[n_out]

                @pl.when(row_tile > 0)
                def _():
                    col_ref[...] += outs[n_out]

        if nk == 1:
            finish(part)
        else:
            kk = pl.program_id(2)

            @pl.when(kk == 0)
            def _():
                acc_ref[...] = part

            @pl.when(kk > 0)
            def _():
                acc_ref[...] += part

            @pl.when(kk == nk - 1)
            def _():
                finish(acc_ref[...])

    b_specs = [b_spec] if b2_block is None else [b_spec, b2_block]
    outs, side_outs = _call_hosting(
        side, body, name, grid, [a_spec] + b_specs + [o_spec] * len(extras) + [vec_spec] * len(row_vecs),
        out_specs, out_shape, [pltpu.VMEM((tm, tn), F32)] if nk > 1 else [],
        ("arbitrary",) * 3 if col_acc else ("parallel", "parallel", "arbitrary"),
        (a,) + (b,) * n_b + tuple(extras) + tuple(v.reshape(1, -1) for v in row_vecs))
    main = outs[0] if n_all == 1 else outs
    return main if side is None else (main, side_outs)


def _rowwise(name, fn, rows, vecs, outs, accs=(), tr=256):
    n_rows = rows[0].shape[0]
    tr = min(tr, n_rows)
    assert n_rows % tr == 0
    n_r, n_v, n_o, n_a = len(rows), len(vecs), len(outs), len(accs)

    def body(*refs):
        ins = [r[...] for r in refs[:n_r + n_v]]
        o_refs = refs[n_r + n_v:n_r + n_v + n_o]
        a_refs = refs[n_r + n_v + n_o:]
        ro, ao = fn(*ins)
        for r, o in zip(o_refs, ro):
            r[...] = o.astype(r.dtype)
        if n_a:
            step = pl.program_id(0)

            @pl.when(step == 0)
            def _():
                for r, o in zip(a_refs, ao):
                    r[...] = o

            @pl.when(step > 0)
            def _():
                for r, o in zip(a_refs, ao):
                    r[...] += o

    res = pl.pallas_call(
        body, name=name,
        grid=(n_rows // tr,),
        in_specs=[pl.BlockSpec((tr, r.shape[1]), lambda i: (i, 0)) for r in rows]
        + [pl.BlockSpec(v.shape, lambda i: (0, 0)) for v in vecs],
        out_specs=[pl.BlockSpec((tr, w), lambda i: (i, 0)) for w, _ in outs]
        + [pl.BlockSpec((1, w), lambda i: (0, 0)) for w in accs],
        out_shape=[_sds((n_rows, w), dt) for w, dt in outs] + [_sds((1, w), F32) for w in accs],
        compiler_params=_params(("arbitrary",)),
    )(*rows, *vecs)
    return res


def _colsum(x):
    return jnp.sum(x, axis=0, keepdims=True)


def _sigmoid(x):
    return 1.0 / (1.0 + jnp.exp(-x))


_GELU_C = math.sqrt(2.0 / math.pi)


def _gelu(y):
    return 0.5 * y * (1.0 + jnp.tanh(_GELU_C * (y + 0.044715 * y * y * y)))


def _gelu_grad(y):
    t = jnp.tanh(_GELU_C * (y + 0.044715 * y * y * y))
    return 0.5 * (1.0 + t) + 0.5 * y * (1.0 - t * t) * _GELU_C * (1.0 + 3.0 * 0.044715 * y * y)


def _rms_fwd(name, h, gain):
    def fn(x, g):
        r = lax.rsqrt(jnp.mean(x * x, axis=1, keepdims=True) + EPS)
        return (x * r * g,), ()
    return _rowwise(name, fn, [h], [gain.reshape(1, -1)], [(h.shape[1], BF16)])[0]


def _res_norm(acc, res, gain):
    hn = acc + res
    r = lax.rsqrt(jnp.mean(hn * hn, axis=1, keepdims=True) + EPS)
    return hn, hn * r * gain


def _rms_bwd_epi(dxn, h, dres, gain):
    r = lax.rsqrt(jnp.mean(h * h, axis=1, keepdims=True) + EPS)
    xh = h * r
    gdy = dxn * gain
    dx = r * (gdy - xh * jnp.mean(gdy * xh, axis=1, keepdims=True)) + dres
    return dx, dx, _colsum(dxn * xh)


def _loss_head(h, target):
    w = h.shape[1]

    def fn(x, t):
        e = x - t
        return (e * (1.0 / w), e * (1.0 / w)), (jnp.zeros((1, 128), F32) + jnp.sum(e * e),)
    return _rowwise("loss_head", fn, [h, target], [], [(w, F32), (w, BF16)], [128])


def _adamw(name, w, g, m, v):
    c1 = 1.0 - ADAM_B1 ** ADAM_STEP
    c2 = 1.0 - ADAM_B2 ** ADAM_STEP

    def fn(w_, g_, m_, v_):
        mn = ADAM_B1 * m_ + (1.0 - ADAM_B1) * g_
        vn = ADAM_B2 * v_ + (1.0 - ADAM_B2) * (g_ * g_)
        delta = -ADAM_LR * ((mn / c1) / (jnp.sqrt(vn / c2) + ADAM_EPS) + ADAM_WD * w_)
        return (delta, mn, vn), ()
    c = w.shape[1]
    return _rowwise(name, fn, [w, g, m, v], [], [(c, F32)] * 3)


def _s5_discretize(lam_re, lam_im, log_dt, b_re, b_im):
    lr = jnp.minimum(lam_re, -S5_MIN_DECAY)
    li = lam_im
    dt = jnp.exp(log_dt)[:, None]
    mag = jnp.exp(lr * dt)
    ar = mag * jnp.cos(li * dt)
    ai = mag * jnp.sin(li * dt)
    den = lr * lr + li * li
    zr = ((ar - 1.0) * lr + ai * li) / den
    zi = (ai * lr - (ar - 1.0) * li) / den
    bbr = zr[..., None] * b_re - zi[..., None] * b_im
    bbi = zr[..., None] * b_im + zi[..., None] * b_re
    return ar, ai, bbr, bbi


def _s5_matrices(ar, ai, bbr, bbi, c_re, c_im, d_skip):
    eye = jnp.eye(8, dtype=F32)
    bt = jnp.stack([bbr, bbi], axis=1).transpose(0, 3, 1, 2)
    bt = bt.reshape(8, 8, 16, 1, 2, 64) * eye[None, :, None, :, None, None]
    bm8 = bt.reshape(8, 8, 16, 4, 2, 2, 64).transpose(0, 1, 2, 3, 5, 4, 6).reshape(8, 128, 1024)
    ct = jnp.stack([c_re, -c_im], axis=1).transpose(0, 1, 3, 2)
    ct = ct.reshape(8, 8, 2, 64, 1, 16) * eye[None, :, None, None, :, None]
    cm8 = ct.reshape(8, 4, 2, 2, 64, 8, 16).transpose(0, 1, 3, 2, 4, 5, 6).reshape(8, 1024, 128)
    prs, pis = [], []
    pr, pi = ar, ai
    for _ in range(S5_LEVELS):
        prs.append(pr.reshape(8, 512))
        pis.append(pi.reshape(8, 512))
        pr, pi = pr * pr - pi * pi, 2.0 * pr * pi
    prs.append(jnp.zeros_like(prs[0]))
    pis.append(jnp.zeros_like(pis[0]))
    return (bm8.astype(BF16), cm8.astype(BF16), jnp.stack(prs, axis=1), jnp.stack(pis, axis=1),
            d_skip.reshape(8, 1, 128))


def _s5_unpack_grads(dbm8, dcm8, da, dd):
    db = dbm8.reshape(8, 8, 16, 4, 2, 2, 64).transpose(0, 1, 2, 3, 5, 4, 6).reshape(8, 8, 16, 8, 2, 64)
    db = jnp.einsum("agcgqp->agcqp", db).reshape(S5_GROUPS, 16, 2, 64)
    dc = dcm8.reshape(8, 4, 2, 2, 64, 8, 16).transpose(0, 1, 3, 2, 4, 5, 6).reshape(8, 8, 2, 64, 8, 16)
    dc = jnp.einsum("agqpgc->agqpc", dc).reshape(S5_GROUPS, 2, 64, 16)
    dar = da[:, 0, :].reshape(S5_GROUPS, 64)
    dai = da[:, 1, :].reshape(S5_GROUPS, 64)
    return (dar, dai, db[:, :, 0, :].transpose(0, 2, 1), db[:, :, 1, :].transpose(0, 2, 1),
            dc[:, 0].transpose(0, 2, 1), -dc[:, 1].transpose(0, 2, 1), dd.reshape(S5_GROUPS, 16))


def _shift_rows(x, s, row, down):
    t = x.shape[0]
    if s % 8 == 0:
        z = jnp.zeros((s, x.shape[1]), x.dtype)
        return jnp.concatenate([z, x[:t - s]], axis=0) if down else jnp.concatenate([x[s:], z], axis=0)
    if down:
        return jnp.where(row >= s, pltpu.roll(x, s, 0), 0.0)
    return jnp.where(row < t - s, pltpu.roll(x, t - s, 0), 0.0)


def _s5_scan(xr, xi, pr, pi, cr, ci, row, conj):
    t = xr[0].shape[0]
    sg = -1.0 if conj else 1.0
    edge = (t - 1) if conj else 0
    n = len(xr)
    for k in range(n):
        sl = slice(128 * k, 128 * (k + 1))
        p_r, p_i = pr[0:1, sl], sg * pi[0:1, sl]
        xr[k] = xr[k] + jnp.where(row == edge, p_r * cr[k] - p_i * ci[k], 0.0)
        xi[k] = xi[k] + jnp.where(row == edge, p_r * ci[k] + p_i * cr[k], 0.0)
    for lvl in range(S5_LEVELS):
        s = 1 << lvl
        for k in range(n):
            sl = slice(128 * k, 128 * (k + 1))
            p_r, p_i = pr[lvl:lvl + 1, sl], sg * pi[lvl:lvl + 1, sl]
            sr = _shift_rows(xr[k], s, row, not conj)
            si = _shift_rows(xi[k], s, row, not conj)
            xr[k] = xr[k] + p_r * sr - p_i * si
            xi[k] = xi[k] + p_r * si + p_i * sr
    return xr, xi


def _s5_fwd(name, proj, mats, side=None):
    bm8, cm8, p1, p2, d8 = mats
    n_rows = proj.shape[0]
    t = S5_CHUNK
    nch = n_rows // t

    def body(u_ref, bm_ref, cm_ref, pr_ref, pi_ref, d_ref, y_ref, z_ref, st_ref, carry):
        @pl.when(pl.program_id(1) == 0)
        def _():
            carry[...] = jnp.zeros_like(carry)

        cv = carry[...]
        st_ref[...] = cv
        u = u_ref[...]
        bu = jnp.dot(u.astype(BF16), bm_ref[...], preferred_element_type=F32)
        row = lax.broadcasted_iota(jnp.int32, (t, 128), 0)
        tile = lambda v, j: v[:, 128 * j:128 * (j + 1)]
        xr, xi = _s5_scan([tile(bu, 2 * k) for k in range(4)], [tile(bu, 2 * k + 1) for k in range(4)],
                          pr_ref[...], pi_ref[...], [tile(cv, 2 * k)[0:1] for k in range(4)],
                          [tile(cv, 2 * k + 1)[0:1] for k in range(4)], row, False)
        xall = jnp.concatenate([v for k in range(4) for v in (xr[k], xi[k])], axis=1)
        carry[...] = jnp.broadcast_to(xall[t - 1:t, :], (8, 1024))
        y = jnp.dot(xall.astype(BF16), cm_ref[...], preferred_element_type=F32) + d_ref[...] * u
        y_ref[...] = y
        z_ref[...] = _gelu(y).astype(BF16)

    return _call_hosting(
        side, body, name, (8, nch),
        [
            pl.BlockSpec((t, 128), lambda g, c: (c, g)),
            pl.BlockSpec((None, 128, 1024), lambda g, c: (g, 0, 0)),
            pl.BlockSpec((None, 1024, 128), lambda g, c: (g, 0, 0)),
            pl.BlockSpec((None, 8, 512), lambda g, c: (g, 0, 0)),
            pl.BlockSpec((None, 8, 512), lambda g, c: (g, 0, 0)),
            pl.BlockSpec((None, 1, 128), lambda g, c: (g, 0, 0)),
        ],
        [
            pl.BlockSpec((t, 128), lambda g, c: (c, g)),
            pl.BlockSpec((t, 128), lambda g, c: (c, g)),
            pl.BlockSpec((None, None, 8, 1024), lambda g, c: (g, c, 0, 0)),
        ],
        [_sds((n_rows, S5_WIDTH), F32), _sds((n_rows, S5_WIDTH), BF16), _sds((8, nch, 8, 1024), F32)],
        [pltpu.VMEM((8, 1024), F32)], ("parallel", "arbitrary"), (proj, bm8, cm8, p1, p2, d8))


def _s5_bwd(name, proj, dy, states, mats):
    bm8, cm8, p1, p2, d8 = mats
    n_rows = proj.shape[0]
    t = S5_CHUNK
    nch = n_rows // t
    nt_dims = (((1,), (1,)), ((), ()))
    tn_dims = (((0,), (0,)), ((), ()))

    def body(u_ref, dy_ref, st_ref, bm_ref, cm_ref, pr_ref, pi_ref, d_ref,
             du_ref, dbm_ref, dcm_ref, da_ref, dd_ref, gcarry):
        @pl.when(pl.program_id(1) == 0)
        def _():
            gcarry[...] = jnp.zeros_like(gcarry)
            dbm_ref[...] = jnp.zeros_like(dbm_ref)
            dcm_ref[...] = jnp.zeros_like(dcm_ref)
            da_ref[...] = jnp.zeros_like(da_ref)
            dd_ref[...] = jnp.zeros_like(dd_ref)

        u = u_ref[...]
        dyv = dy_ref[...]
        ub, dyb = u.astype(BF16), dyv.astype(BF16)
        bu = jnp.dot(ub, bm_ref[...], preferred_element_type=F32)
        dxd = lax.dot_general(dyb, cm_ref[...], nt_dims, preferred_element_type=F32)
        row = lax.broadcasted_iota(jnp.int32, (t, 128), 0)
        tile = lambda v, j: v[:, 128 * j:128 * (j + 1)]
        prv, piv, cv, gv = pr_ref[...], pi_ref[...], st_ref[...], gcarry[...]
        cr = [tile(cv, 2 * k)[0:1] for k in range(4)]
        ci = [tile(cv, 2 * k + 1)[0:1] for k in range(4)]
        xr, xi = _s5_scan([tile(bu, 2 * k) for k in range(4)], [tile(bu, 2 * k + 1) for k in range(4)],
                          prv, piv, cr, ci, row, False)
        gr, gi = _s5_scan([tile(dxd, 2 * k) for k in range(4)], [tile(dxd, 2 * k + 1) for k in range(4)],
                          prv, piv, [tile(gv, 2 * k)[0:1] for k in range(4)],
                          [tile(gv, 2 * k + 1)[0:1] for k in range(4)], row, True)
        dar, dai = [], []
        for k in range(4):
            xpr = jnp.where(row >= 1, pltpu.roll(xr[k], 1, 0), cr[k])
            xpi = jnp.where(row >= 1, pltpu.roll(xi[k], 1, 0), ci[k])
            dar.append(_colsum(gr[k] * xpr + gi[k] * xpi))
            dai.append(_colsum(gi[k] * xpr - gr[k] * xpi))
        xall = jnp.concatenate([v for k in range(4) for v in (xr[k], xi[k])], axis=1).astype(BF16)
        gf = jnp.concatenate([v for k in range(4) for v in (gr[k], gi[k])], axis=1)
        gcarry[...] = jnp.broadcast_to(gf[0:1, :], (8, 1024))
        gall = gf.astype(BF16)
        dcm_ref[...] += lax.dot_general(xall, dyb, tn_dims, preferred_element_type=F32)
        dbm_ref[...] += lax.dot_general(ub, gall, tn_dims, preferred_element_type=F32)
        du = lax.dot_general(gall, bm_ref[...], nt_dims, preferred_element_type=F32) + d_ref[...] * dyv
        du_ref[...] = du.astype(BF16)
        dd_ref[...] += _colsum(dyv * u)
        da_ref[0:1, :] += jnp.concatenate(dar, axis=1)
        da_ref[1:2, :] += jnp.concatenate(dai, axis=1)

    rev = lambda g, c: (nch - 1 - c, g)
    return pl.pallas_call(
        body, name=name,
        grid=(8, nch),
        in_specs=[
            pl.BlockSpec((t, 128), rev),
            pl.BlockSpec((t, 128), rev),
            pl.BlockSpec((None, None, 8, 1024), lambda g, c: (g, nch - 1 - c, 0, 0)),
            pl.BlockSpec((None, 128, 1024), lambda g, c: (g, 0, 0)),
            pl.BlockSpec((None, 1024, 128), lambda g, c: (g, 0, 0)),
            pl.BlockSpec((None, 8, 512), lambda g, c: (g, 0, 0)),
            pl.BlockSpec((None, 8, 512), lambda g, c: (g, 0, 0)),
            pl.BlockSpec((None, 1, 128), lambda g, c: (g, 0, 0)),
        ],
        out_specs=[
            pl.BlockSpec((t, 128), rev),
            pl.BlockSpec((None, 128, 1024), lambda g, c: (g, 0, 0)),
            pl.BlockSpec((None, 1024, 128), lambda g, c: (g, 0, 0)),
            pl.BlockSpec((None, 8, 512), lambda g, c: (g, 0, 0)),
            pl.BlockSpec((None, 1, 128), lambda g, c: (g, 0, 0)),
        ],
        out_shape=[_sds((n_rows, S5_WIDTH), BF16), _sds((8, 128, 1024), F32), _sds((8, 1024, 128), F32),
                   _sds((8, 8, 512), F32), _sds((8, 1, 128), F32)],
        scratch_shapes=[pltpu.VMEM((8, 1024), F32)],
        compiler_params=_params(("parallel", "arbitrary")),
    )(proj, dy, states, bm8, cm8, p1, p2, d8)


def _hgrn_lower_bounds(lb_param):
    p = jax.nn.softmax(lb_param, axis=0)
    return jnp.cumsum(p, axis=0) - p[0:1]


def _prefix16(x, r16):
    for s in (1, 2, 4, 8):
        x = x + jnp.where(r16 >= s, pltpu.roll(x, s, 0), 0.0)
    return x


def _suffix16(x, r16):
    n = x.shape[0]
    for s in (1, 2, 4, 8):
        x = x + jnp.where(r16 < HGRN_SUB - s, pltpu.roll(x, n - s, 0), 0.0)
    return x


_NT = (((1,), (1,)), ((), ()))
_TN = (((0,), (0,)), ((), ()))


def _dotf(a, b, dims=(((1,), (0,)), ((), ()))):
    return lax.dot_general(a.astype(BF16), b.astype(BF16), dims, preferred_element_type=F32)


def _hgrn_specs(n_blocks, rev):
    r = HGRN_BLOCK
    blk = (lambda b: n_blocks - 1 - b) if rev else (lambda b: b)
    proj_specs = [pl.BlockSpec((r, 128), (lambda h, b, c=c: (blk(b), 8 * c + h))) for c in (1, 2, 3, 4)]
    lb_spec = pl.BlockSpec((None, 1, 128), lambda h, b: (h, 0, 0))
    gain_spec = pl.BlockSpec((1, 128), lambda h, b: (0, 0))
    row_spec = pl.BlockSpec((r, 128), lambda h, b: (blk(b), h))
    st_spec = pl.BlockSpec((None, None, 128, 128), lambda h, b: (h, blk(b), 0, 0))
    return proj_specs, lb_spec, gain_spec, row_spec, st_spec, blk


def _hgrn_fwd(name, proj, lb, gain, side=None):
    n_rows = proj.shape[0]
    r = HGRN_BLOCK
    nb = n_rows // r
    nsub = r // HGRN_SUB
    proj_specs, lb_spec, gain_spec, row_spec, st_spec, _ = _hgrn_specs(nb, False)

    def body(q_ref, f_ref, i_ref, g_ref, lb_ref, gain_ref, o_ref, y_ref, st_ref, st_scr):
        @pl.when(pl.program_id(1) == 0)
        def _():
            st_scr[...] = jnp.zeros_like(st_scr)

        st_ref[...] = st_scr[...]
        q, f, v, g = q_ref[...], f_ref[...], i_ref[...], g_ref[...]
        lbv = lb_ref[...]
        qs = q * _sigmoid(q)
        fg = lbv + (1.0 - lbv) * _sigmoid(f)
        kk = 1.0 - fg
        r16 = lax.broadcasted_iota(jnp.int32, (r, 128), 0) & (HGRN_SUB - 1)
        b = _prefix16(jnp.log(fg), r16)
        qh = qs * jnp.exp(b)
        rs = lax.broadcasted_iota(jnp.int32, (HGRN_SUB, 128), 0)
        st = st_scr[...]
        outs = []
        for i in range(nsub):
            sl = slice(HGRN_SUB * i, HGRN_SUB * (i + 1))
            qsi, kki, vi, bi = qs[sl], kk[sl], v[sl], b[sl]
            o_i = _dotf(qh[sl], st, _NT)
            for s in range(HGRN_SUB):
                e = jnp.exp(jnp.where(rs >= s, bi - bi[s:s + 1], NEG))
                col = jnp.sum(qsi * e * kki[s:s + 1], axis=1, keepdims=True)
                o_i = o_i + col * vi[s:s + 1]
            bl = bi[HGRN_SUB - 1:HGRN_SUB]
            st = st * jnp.exp(bl) + _dotf(vi, kki * jnp.exp(bl - bi), _TN)
            outs.append(o_i)
        st_scr[...] = st
        o = jnp.concatenate(outs, axis=0)
        o_ref[...] = o
        rn = lax.rsqrt(jnp.mean(o * o, axis=1, keepdims=True) + EPS)
        y_ref[...] = (o * rn * gain_ref[...] * (g * _sigmoid(g))).astype(BF16)

    return _call_hosting(
        side, body, name, (HGRN_HEADS, nb), proj_specs + [lb_spec, gain_spec], [row_spec, row_spec, st_spec],
        [_sds((n_rows, HGRN_WIDTH), F32), _sds((n_rows, HGRN_WIDTH), BF16), _sds((HGRN_HEADS, nb, 128, 128), F32)],
        [pltpu.VMEM((128, 128), F32)], ("parallel", "arbitrary"), (proj, proj, proj, proj, lb, gain))


def _hgrn_bwd(name, proj, lb, gain, o_saved, states, dycat):
    n_rows = proj.shape[0]
    r = HGRN_BLOCK
    nb = n_rows // r
    nsub = r // HGRN_SUB
    proj_specs, lb_spec, gain_spec, row_spec, st_spec, blk = _hgrn_specs(nb, True)
    dy_spec = pl.BlockSpec((r, 128), lambda h, b: (blk(b), 8 + h))
    acc_spec = pl.BlockSpec((None, 1, 128), lambda h, b: (h, 0, 0))

    def body(q_ref, f_ref, i_ref, g_ref, lb_ref, gain_ref, o_ref, st_ref, dy_ref,
             dq_ref, df_ref, di_ref, dg_ref, dlb_ref, dgain_ref, dst_scr, sub_scr):
        @pl.when(pl.program_id(1) == 0)
        def _():
            dst_scr[...] = jnp.zeros_like(dst_scr)
            dlb_ref[...] = jnp.zeros_like(dlb_ref)
            dgain_ref[...] = jnp.zeros_like(dgain_ref)

        q, f, v, g = q_ref[...], f_ref[...], i_ref[...], g_ref[...]
        lbv, gain_v = lb_ref[...], gain_ref[...]
        sq = _sigmoid(q)
        qs = q * sq
        sf = _sigmoid(f)
        fg = lbv + (1.0 - lbv) * sf
        kk = 1.0 - fg
        r16 = lax.broadcasted_iota(jnp.int32, (r, 128), 0) & (HGRN_SUB - 1)
        b = _prefix16(jnp.log(fg), r16)
        eb = jnp.exp(b)
        qh = qs * eb

        o, dy = o_ref[...], dy_ref[...]
        rn = lax.rsqrt(jnp.mean(o * o, axis=1, keepdims=True) + EPS)
        on = o * rn
        sg = _sigmoid(g)
        sil = g * sg
        dgain_ref[...] += _colsum(dy * on * sil)
        dg_ref[...] = (dy * on * gain_v * (sg * (1.0 + g * (1.0 - sg)))).astype(BF16)
        don = dy * gain_v * sil
        do = rn * (don - on * jnp.mean(don * on, axis=1, keepdims=True))

        st = st_ref[...]
        for i in range(nsub):
            sl = slice(HGRN_SUB * i, HGRN_SUB * (i + 1))
            sub_scr[i] = st
            bi = b[sl]
            bl = bi[HGRN_SUB - 1:HGRN_SUB]
            st = st * jnp.exp(bl) + _dotf(v[sl], kk[sl] * jnp.exp(bl - bi), _TN)

        rs = lax.broadcasted_iota(jnp.int32, (HGRN_SUB, 128), 0)
        dst = dst_scr[...]
        parts = [None] * nsub
        for i in reversed(range(nsub)):
            sl = slice(HGRN_SUB * i, HGRN_SUB * (i + 1))
            sp = sub_scr[i]
            qsi, kki, vi, bi, doi, qhi = qs[sl], kk[sl], v[sl], b[sl], do[sl], qh[sl]
            bl = bi[HGRN_SUB - 1:HGRN_SUB]
            ebl = jnp.exp(bl)
            dec = jnp.exp(bl - bi)
            khat = kki * dec
            dqh = _dotf(doi, sp)
            dkhat = _dotf(vi, dst)
            dv = _dotf(khat, dst, _NT)
            zrow = _colsum(sp * dst) * ebl
            dq_in = jnp.zeros((HGRN_SUB, 128), F32)
            dk_in = jnp.zeros((HGRN_SUB, 128), F32)
            dv_in = jnp.zeros((HGRN_SUB, 128), F32)
            for s in range(HGRN_SUB):
                e = jnp.exp(jnp.where(rs >= s, bi - bi[s:s + 1], NEG))
                dpc = jnp.sum(doi * vi[s:s + 1], axis=1, keepdims=True)
                w = qsi * e
                pc = jnp.sum(w * kki[s:s + 1], axis=1, keepdims=True)
                dq_in = dq_in + dpc * e * kki[s:s + 1]
                dk_in = jnp.where(rs == s, _colsum(dpc * w), dk_in)
                dv_in = jnp.where(rs == s, _colsum(pc * doi), dv_in)
            kd = khat * dkhat
            parts[i] = (qsi * dq_in - kki * dk_in + qhi * dqh, kd, jnp.broadcast_to(zrow, (HGRN_SUB, 128)),
                        dq_in + dqh * eb[sl], dk_in + dkhat * dec, dv + dv_in)
            dst = dst * ebl + _dotf(doi, qhi, _TN)
        dst_scr[...] = dst

        cat = lambda j: jnp.concatenate([p[j] for p in parts], axis=0)
        d_b, kd, zr, dqs, dkk, dvv = (cat(j) for j in range(6))
        dlf = _suffix16(d_b, r16) + _prefix16(kd, r16) - kd + zr
        dfg = dlf / fg - dkk
        df_ref[...] = (dfg * (1.0 - lbv) * sf * (1.0 - sf)).astype(BF16)
        dlb_ref[...] += _colsum(dfg * (1.0 - sf))
        dq_ref[...] = (dqs * (sq * (1.0 + q * (1.0 - sq)))).astype(BF16)
        di_ref[...] = dvv.astype(BF16)

    return pl.pallas_call(
        body, name=name,
        grid=(HGRN_HEADS, nb),
        in_specs=proj_specs + [lb_spec, gain_spec, row_spec, st_spec, dy_spec],
        out_specs=[row_spec] * 4 + [acc_spec, acc_spec],
        out_shape=[_sds((n_rows, HGRN_WIDTH), BF16)] * 4 + [_sds((HGRN_HEADS, 1, 128), F32)] * 2,
        scratch_shapes=[pltpu.VMEM((128, 128), F32), pltpu.VMEM((nsub, 128, 128), F32)],
        compiler_params=_params(("parallel", "arbitrary")),
    )(proj, proj, proj, proj, lb, gain, o_saved, states, dycat)


def _alibi_slopes():
    return jnp.exp2(-8.0 * jnp.arange(1, ATT_HEADS + 1, dtype=F32) / ATT_HEADS)


def _swa_specs(n_blocks):
    blk = ATT_BLOCK
    prev = lambda i: jnp.maximum(i - 1, 0)
    smem = pl.BlockSpec(memory_space=pltpu.SMEM)
    return [
        smem, smem,
        pl.BlockSpec((blk, ATT_HEADS * ATT_DIM), lambda i: (i, 0)),
        pl.BlockSpec((blk, 256), lambda i: (i, 8)),
        pl.BlockSpec((blk, 256), lambda i: (prev(i), 8)),
        pl.BlockSpec((blk, 256), lambda i: (i, 9)),
        pl.BlockSpec((blk, 256), lambda i: (prev(i), 9)),
        pl.BlockSpec((1, ATT_DIM), lambda i: (0, 0)),
        pl.BlockSpec((1, ATT_DIM), lambda i: (0, 0)),
    ]


_ATT_GROUP = ATT_HEADS // ATT_KV
_ATT_ROWS = _ATT_GROUP * ATT_BLOCK


def _swa_mask(i):
    t_i = lax.broadcasted_iota(jnp.int32, (_ATT_ROWS, 2 * ATT_BLOCK), 0) & (ATT_BLOCK - 1)
    s_i = lax.broadcasted_iota(jnp.int32, (_ATT_ROWS, 2 * ATT_BLOCK), 1)
    dist = t_i + ATT_BLOCK - s_i
    valid = (dist >= 0) & (dist < ATT_BLOCK) & ((s_i >= ATT_BLOCK) | (i > 0))
    return valid, dist.astype(F32)


def _stack_heads(x):
    return jnp.concatenate([x[:, ATT_DIM * h:ATT_DIM * (h + 1)] for h in range(_ATT_GROUP)], axis=0)


def _unstack_heads(x):
    return jnp.concatenate([x[ATT_BLOCK * h:ATT_BLOCK * (h + 1)] for h in range(_ATT_GROUP)], axis=1)


def _head_column(ref, g):
    return jnp.concatenate([jnp.full((ATT_BLOCK, 1), ref[_ATT_GROUP * g + h], F32) for h in range(_ATT_GROUP)], axis=0)


def _swa_probs(qn, kn, slope, sink, valid, distf):
    s = lax.dot_general(qn, kn, _NT, preferred_element_type=F32) * (ATT_DIM ** -0.5) - slope * distf
    s = jnp.where(valid, s, NEG)
    m = jnp.maximum(jnp.max(s, axis=1, keepdims=True), sink)
    p = jnp.exp(s - m)
    es = jnp.exp(sink - m)
    inv = 1.0 / (jnp.sum(p, axis=1, keepdims=True) + es)
    return p * inv, es * inv


def _swa_fwd(name, qkv, q_gain, k_gain, sinks, slopes):
    n_rows = qkv.shape[0]
    nb = n_rows // ATT_BLOCK

    def body(sink_ref, slope_ref, q_ref, kc_ref, kp_ref, vc_ref, vp_ref, qg_ref, kg_ref, o_ref):
        i = pl.program_id(0)
        kb = jnp.concatenate([kp_ref[...], kc_ref[...]], axis=0)
        vb = jnp.concatenate([vp_ref[...], vc_ref[...]], axis=0)
        valid, distf = _swa_mask(i)
        qgv, kgv = qg_ref[...], kg_ref[...]
        gw = _ATT_GROUP * ATT_DIM
        for g in range(ATT_KV):
            kg = kb[:, 64 * g:64 * (g + 1)]
            rk = lax.rsqrt(jnp.mean(kg * kg, axis=1, keepdims=True) + EPS)
            kn = (kg * rk * kgv).astype(BF16)
            vv = vb[:, 64 * g:64 * (g + 1)].astype(BF16)
            qs = _stack_heads(q_ref[:, gw * g:gw * (g + 1)])
            rq = lax.rsqrt(jnp.mean(qs * qs, axis=1, keepdims=True) + EPS)
            pn, _ = _swa_probs((qs * rq * qgv).astype(BF16), kn, _head_column(slope_ref, g),
                               _head_column(sink_ref, g), valid, distf)
            out = jnp.dot(pn.astype(BF16), vv, preferred_element_type=F32)
            o_ref[:, gw * g:gw * (g + 1)] = _unstack_heads(out).astype(BF16)

    return pl.pallas_call(
        body, name=name,
        grid=(nb,),
        in_specs=_swa_specs(nb),
        out_specs=pl.BlockSpec((ATT_BLOCK, ATT_HEADS * ATT_DIM), lambda i: (i, 0)),
        out_shape=_sds((n_rows, ATT_HEADS * ATT_DIM), BF16),
        compiler_params=_params(("parallel",)),
    )(sinks, slopes, qkv, qkv, qkv, qkv, qkv, q_gain.reshape(1, -1), k_gain.reshape(1, -1))


def _swa_bwd(name, qkv, q_gain, k_gain, sinks, slopes, d_out):
    n_rows = qkv.shape[0]
    nb = n_rows // ATT_BLOCK
    blk = ATT_BLOCK

    def body(sink_ref, slope_ref, q_ref, kc_ref, kp_ref, vc_ref, vp_ref, qg_ref, kg_ref, do_ref,
             dq_ref, dkc_ref, dkp_ref, dvc_ref, dvp_ref, dsink_ref, dqg_ref, dkg_ref):
        i = pl.program_id(0)

        @pl.when(i == 0)
        def _():
            dsink_ref[...] = jnp.zeros_like(dsink_ref)
            dqg_ref[...] = jnp.zeros_like(dqg_ref)
            dkg_ref[...] = jnp.zeros_like(dkg_ref)

        kb = jnp.concatenate([kp_ref[...], kc_ref[...]], axis=0)
        vb = jnp.concatenate([vp_ref[...], vc_ref[...]], axis=0)
        kgv, qgv = kg_ref[...], qg_ref[...]
        valid, distf = _swa_mask(i)
        scale = ATT_DIM ** -0.5
        gw = _ATT_GROUP * ATT_DIM
        dks, dvs = [], []
        dqg, dkg = jnp.zeros((1, ATT_DIM), F32), jnp.zeros((1, ATT_DIM), F32)
        for g in range(ATT_KV):
            kg = kb[:, 64 * g:64 * (g + 1)]
            rk = lax.rsqrt(jnp.mean(kg * kg, axis=1, keepdims=True) + EPS)
            khat = kg * rk
            kn = (khat * kgv).astype(BF16)
            vv = vb[:, 64 * g:64 * (g + 1)].astype(BF16)
            qs = _stack_heads(q_ref[:, gw * g:gw * (g + 1)])
            rq = lax.rsqrt(jnp.mean(qs * qs, axis=1, keepdims=True) + EPS)
            qhat = qs * rq
            qn = (qhat * qgv).astype(BF16)
            pn, ps = _swa_probs(qn, kn, _head_column(slope_ref, g), _head_column(sink_ref, g), valid, distf)
            dos = _stack_heads(do_ref[:, gw * g:gw * (g + 1)]).astype(BF16)
            dp = lax.dot_general(dos, vv, _NT, preferred_element_type=F32)
            delta = jnp.sum(pn * dp, axis=1, keepdims=True)
            ds = (pn * (dp - delta)).astype(BF16)
            sd = ps * delta
            for h in range(_ATT_GROUP):
                hs = _ATT_GROUP * g + h
                dsink_ref[hs:hs + 1, :] += jnp.zeros((1, 128), F32) - jnp.sum(sd[blk * h:blk * (h + 1)])
            dvs.append(lax.dot_general(pn.astype(BF16), dos, _TN, preferred_element_type=F32))
            dkn = lax.dot_general(ds, qn, _TN, preferred_element_type=F32) * scale
            dqn = jnp.dot(ds, kn, preferred_element_type=F32) * scale
            dqg = dqg + _colsum(dqn * qhat)
            dqhat = dqn * qgv
            dqs = rq * (dqhat - qhat * jnp.mean(dqhat * qhat, axis=1, keepdims=True))
            dq_ref[:, gw * g:gw * (g + 1)] = _unstack_heads(dqs).astype(BF16)
            dkg = dkg + _colsum(dkn * khat)
            dkhat = dkn * kgv
            dks.append(rk * (dkhat - khat * jnp.mean(dkhat * khat, axis=1, keepdims=True)))
        dqg_ref[...] += dqg
        dkg_ref[...] += dkg
        dk = jnp.concatenate(dks, axis=1).astype(BF16)
        dv = jnp.concatenate(dvs, axis=1).astype(BF16)
        dkp_ref[...] = dk[:blk]
        dkc_ref[...] = dk[blk:]
        dvp_ref[...] = dv[:blk]
        dvc_ref[...] = dv[blk:]

    kv_spec = pl.BlockSpec((blk, 256), lambda i: (i, 0))
    full = pl.BlockSpec((blk, ATT_HEADS * ATT_DIM), lambda i: (i, 0))
    acc64 = pl.BlockSpec((1, ATT_DIM), lambda i: (0, 0))
    return pl.pallas_call(
        body, name=name,
        grid=(nb,),
        in_specs=_swa_specs(nb) + [full],
        out_specs=[full, kv_spec, kv_spec, kv_spec, kv_spec,
                   pl.BlockSpec((ATT_HEADS, 128), lambda i: (0, 0)), acc64, acc64],
        out_shape=[_sds((n_rows, ATT_HEADS * ATT_DIM), BF16)] + [_sds((n_rows, 256), BF16)] * 4
        + [_sds((ATT_HEADS, 128), F32), _sds((1, ATT_DIM), F32), _sds((1, ATT_DIM), F32)],
        compiler_params=_params(("arbitrary",)),
    )(sinks, slopes, qkv, qkv, qkv, qkv, qkv, q_gain.reshape(1, -1), k_gain.reshape(1, -1), d_out)


def _mesh_pos():
    return lax.axis_index("x"), lax.axis_index("y"), lax.axis_index("c")


_ANY = pl.BlockSpec(memory_space=pl.ANY)


def _allgather(name, shards):
    side = _gather_side(shards)
    n = len(shards)

    def body(*refs):
        side.start(refs[:n], refs[n:2 * n], refs[2 * n:])
        side.finish(refs[:n], refs[n:2 * n], refs[2 * n:])

    return pl.pallas_call(
        body, name=name,
        out_shape=side.out_shapes,
        in_specs=[_ANY] * n,
        out_specs=[_ANY] * n,
        scratch_shapes=side.scratch,
    )(*shards)


class _Side:
    def __init__(self, operands, out_shapes, scratch, start, finish):
        self.operands, self.out_shapes, self.scratch = list(operands), list(out_shapes), list(scratch)
        self.start, self.finish = start, finish


def _gather_side(shards):
    n = len(shards)

    def plan(x_refs, out_refs, sems):
        send_sems, recv_sems, local_sems = sems
        x, y, c = _mesh_pos()
        me, sibling = (x, y, c), (x, y, 1 - c)
        chips = [(1 - x, y), (x, 1 - y), (1 - x, 1 - y)]

        def slot(a, px, py, pc):
            return out_refs[a].at[4 * px + 2 * py + pc]

        def copy(a, k, block, to, src=None):
            return pltpu.make_async_remote_copy(
                src_ref=slot(a, *block) if src is None else src, dst_ref=slot(a, *block),
                send_sem=send_sems.at[7 * a + k], recv_sem=recv_sems.at[7 * a + k],
                device_id=to, device_id_type=MESH)

        local = [pltpu.make_async_copy(x_refs[a], slot(a, *me), local_sems.at[a]) for a in range(n)]
        first = [[copy(a, 0, me, sibling, src=x_refs[a])]
                 + [copy(a, 1 + j, me, (*chip, c), src=x_refs[a]) for j, chip in enumerate(chips)] for a in range(n)]
        from_chips = [[copy(a, 1 + j, (*chip, c), me) for j, chip in enumerate(chips)] for a in range(n)]
        forward = [[copy(a, 4 + j, (*chip, c), sibling) for j, chip in enumerate(chips)] for a in range(n)]
        from_sibling = [[copy(a, 0, sibling, me)] + [copy(a, 4 + j, (*chip, 1 - c), me) for j, chip in enumerate(chips)]
                        for a in range(n)]
        return local, first, from_chips, forward, from_sibling

    def start(x_refs, out_refs, sems):
        local, first, _, _, _ = plan(x_refs, out_refs, sems)
        for a in range(n):
            local[a].start()
            for cp in first[a]:
                cp.start()

    def finish(x_refs, out_refs, sems):
        local, first, from_chips, forward, from_sibling = plan(x_refs, out_refs, sems)
        for a in range(n):
            for j in range(3):
                from_chips[a][j].wait_recv()
                forward[a][j].start()
        for a in range(n):
            for cp in from_sibling[a]:
                cp.wait_recv()
        for a in range(n):
            for cp in first[a] + forward[a]:
                cp.wait_send()
            local[a].wait()

    return _Side(shards, [_sds((N_DEV,) + s.shape, s.dtype) for s in shards],
                 [pltpu.SemaphoreType.DMA((7 * n,)), pltpu.SemaphoreType.DMA((7 * n,)), pltpu.SemaphoreType.DMA((n,))],
                 start, finish)


def _swap_with_sibling(name, arrs):
    n = len(arrs)

    def body(*refs):
        x_refs, got_refs = refs[:n], refs[n:2 * n]
        send_sems, recv_sems = refs[2 * n:]
        x, y, c = _mesh_pos()
        copies = []
        for a in range(n):
            for j in range(4):
                k = 4 * a + j
                cp = pltpu.make_async_remote_copy(
                    src_ref=x_refs[a].at[j, 1 - c], dst_ref=got_refs[a].at[j],
                    send_sem=send_sems.at[k], recv_sem=recv_sems.at[k],
                    device_id=(x, y, 1 - c), device_id_type=MESH)
                cp.start()
                copies.append(cp)
        for cp in copies:
            cp.wait()

    return pl.pallas_call(
        body, name=name,
        out_shape=[_sds((4,) + t.shape[2:], t.dtype) for t in arrs],
        in_specs=[_ANY] * n,
        out_specs=[_ANY] * n,
        scratch_shapes=[pltpu.SemaphoreType.DMA((4 * n,))] * 2,
    )(*arrs)


def _exchange_chips(name, arrs):
    side = _exchange_side(arrs)
    n = len(arrs)

    def body(*refs):
        side.start(refs[:n], refs[n:2 * n], refs[2 * n:])
        side.finish(refs[:n], refs[n:2 * n], refs[2 * n:])

    return pl.pallas_call(
        body, name=name,
        out_shape=side.out_shapes,
        in_specs=[_ANY] * n,
        out_specs=[_ANY] * n,
        scratch_shapes=side.scratch,
    )(*arrs)


def _exchange_side(arrs):
    n = len(arrs)

    def plan(x_refs, out_refs, sems):
        send_sems, recv_sems, local_sems = sems
        x, y, c = _mesh_pos()
        me = 2 * x + y
        local = [pltpu.make_async_copy(x_refs[a].at[me], out_refs[a].at[me], local_sems.at[a]) for a in range(n)]
        sends, recvs = [], []
        for a in range(n):
            for k in range(1, 4):
                px, py = x ^ (k >> 1), y ^ (k & 1)
                peer = 2 * px + py
                sem = 3 * a + k - 1
                sends.append(pltpu.make_async_remote_copy(
                    src_ref=x_refs[a].at[peer], dst_ref=out_refs[a].at[me],
                    send_sem=send_sems.at[sem], recv_sem=recv_sems.at[sem],
                    device_id=(px, py, c), device_id_type=MESH))
                recvs.append(pltpu.make_async_remote_copy(
                    src_ref=x_refs[a].at[peer], dst_ref=out_refs[a].at[peer],
                    send_sem=send_sems.at[sem], recv_sem=recv_sems.at[sem],
                    device_id=(px, py, c), device_id_type=MESH))
        return local, sends, recvs

    def start(x_refs, out_refs, sems):
        local, sends, _ = plan(x_refs, out_refs, sems)
        for cp in local + sends:
            cp.start()

    def finish(x_refs, out_refs, sems):
        local, sends, recvs = plan(x_refs, out_refs, sems)
        for cp in recvs:
            cp.wait_recv()
        for cp in sends:
            cp.wait_send()
        for cp in local:
            cp.wait()

    return _Side(arrs, [_sds(t.shape, t.dtype) for t in arrs],
                 [pltpu.SemaphoreType.DMA((3 * n,)), pltpu.SemaphoreType.DMA((3 * n,)), pltpu.SemaphoreType.DMA((n,))],
                 start, finish)


def _sum_blocks(name, blocks, out_dtype=F32):
    n, n_rows, n_cols = blocks.shape
    tr = _row_tile(n_rows)

    def body(x_ref, o_ref):
        acc = x_ref[0].astype(F32)
        for s in range(1, n):
            acc = acc + x_ref[s].astype(F32)
        o_ref[...] = acc.astype(o_ref.dtype)

    return pl.pallas_call(
        body, name=name,
        grid=(n_rows // tr,),
        in_specs=[pl.BlockSpec((n, tr, n_cols), lambda i: (0, i, 0))],
        out_specs=pl.BlockSpec((tr, n_cols), lambda i: (i, 0)),
        out_shape=_sds((n_rows, n_cols), out_dtype),
        compiler_params=_params(("parallel",)),
    )(blocks)


def _add_pair(name, mine, got, core):
    n, n_rows, n_cols = got.shape
    tr = _row_tile(n_rows)

    def body(core_ref, a_ref, b_ref, o_ref):
        o_ref[...] = (a_ref[...].astype(F32) + b_ref[...].astype(F32)).astype(BF16)

    spec = pl.BlockSpec((None, tr, n_cols), lambda j, i, core_ref: (j, i, 0))
    return pl.pallas_call(
        body, name=name,
        grid_spec=pltpu.PrefetchScalarGridSpec(
            num_scalar_prefetch=1,
            grid=(n, n_rows // tr),
            in_specs=[pl.BlockSpec((None, None, tr, n_cols), lambda j, i, core_ref: (j, core_ref[0], i, 0)), spec],
            out_specs=spec,
        ),
        out_shape=_sds(got.shape, BF16),
        compiler_params=_params(("parallel", "parallel")),
    )(core, mine, got)


_MIX_PARTS = (
    (("w_in", "even_w_in", 1, D_MODEL, 5120), ("w_glu", "s5_w_glu", 0, S5_WIDTH, S5_WIDTH),
     ("w_out", "even_w_out", 0, D_MODEL, D_MODEL)),
    (("w_qkv", "odd_w_qkv", 1, D_MODEL, QKV_WIDTH), ("w_out", "odd_w_out", 0, D_MODEL, D_MODEL)),
)
_PACK_COLS = 1024
_FF_SHARD = D_FF // N_DEV
_BIG_NAMES = ("even_w_in", "s5_w_glu", "even_w_out", "odd_w_qkv", "odd_w_out")


def _part_rows(rows, cols):
    return rows * cols // N_DEV // _PACK_COLS


def _row_tile(n_rows):
    return next(t for t in (512, 480, 384, 256, 128) if n_rows % t == 0)


def _pack_mixer_shard(kind, j, args):
    return jnp.concatenate([args[name][j].astype(BF16).reshape(-1, _PACK_COLS) for _, name, _, _, _ in _MIX_PARTS[kind]],
                           axis=0)


def _unpack_mixer(kind, gathered):
    out, off = {}, 0
    for key, _, axis, rows, cols in _MIX_PARTS[kind]:
        n = _part_rows(rows, cols)
        part = gathered[:, off:off + n]
        off += n
        if axis == 1:
            part = part.reshape(N_DEV, rows, cols // N_DEV).transpose(1, 0, 2)
        out[key] = part.reshape(rows, cols)
    return out


def _pack_mixer_grads(kind, g):
    parts = []
    for key, _, axis, rows, cols in _MIX_PARTS[kind]:
        t = g[key]
        if axis == 1:
            t = t.reshape(rows, N_DEV, cols // N_DEV).transpose(1, 0, 2)
        parts.append(t.reshape(N_DEV, -1, _PACK_COLS))
    return jnp.concatenate(parts, axis=1)


def _unpack_mixer_grads(kind, flat):
    out, off = {}, 0
    for _, name, axis, rows, cols in _MIX_PARTS[kind]:
        n = _part_rows(rows, cols)
        shape = (rows, cols // N_DEV) if axis == 1 else (rows // N_DEV, cols)
        out[name] = flat[off:off + n].reshape(shape)
        off += n
    return out


def _pack_small(arrs, row_mult=512):
    parts = []
    for a in arrs:
        f = a.astype(F32).reshape(-1)
        parts.append(jnp.pad(f, (0, (-f.shape[0]) % 128)))
    f = jnp.concatenate(parts)
    f = jnp.pad(f, (0, (-f.shape[0]) % (128 * row_mult)))
    return f.reshape(-1, 128)


def _unpack_small(flat, shapes):
    f = flat.reshape(-1)
    out, off = [], 0
    for s in shapes:
        n = math.prod(s)
        out.append(f[off:off + n].reshape(s))
        off += n + (-n) % 128
    return out


_WEIGHTS = ("even_norm", "even_w_in", "s5_lambda_re", "s5_lambda_im", "s5_log_dt", "s5_b_re", "s5_b_im",
            "s5_c_re", "s5_c_im", "s5_d", "s5_w_glu", "s5_b_glu", "hgrn_lower_bound", "hgrn_o_norm",
            "even_w_out", "odd_norm", "odd_w_qkv", "q_norm", "k_norm", "att_sinks", "odd_w_out",
            "mlp_norm", "mlp_w_up", "mlp_w_down")
_MLP_NAMES = ("mlp_w_up", "mlp_w_down")
_SMALL_NAMES = tuple(n for n in _WEIGHTS if n not in _BIG_NAMES + _MLP_NAMES)


def _add_res(acc, res):
    return (acc + res,)


def _mm_hosting(side, *args, **kw):
    if side is None:
        return _mm(*args, **kw), None
    return _mm(*args, side=side, **kw)


def _mlp_fwd(h, xn, gain, next_gain, w_up, w_down, sides=(None, None)):
    n_rows, fs = h.shape[0], _FF_SHARD
    (up, act), got_up = _mm_hosting(
        sides[0], "mm_up", xn, w_up, "nn", out_dtypes=(F32, BF16), mkn=(n_rows, D_MODEL, D_FF), tm=1024, tn=fs,
        b_block=pl.BlockSpec((None, D_MODEL, fs), lambda i, j, kk: (j, kk, 0)),
        epi=lambda acc: (acc, jnp.square(jnp.maximum(acc, 0.0))))
    w_down4 = w_down.reshape(N_DEV // 2, 2 * fs, D_MODEL)
    if next_gain is None:
        out, got_down = _mm_hosting(
            sides[1], "mm_down_last", act, w_down4, "nn", mkn=(n_rows, D_FF, D_MODEL), tk=2 * fs,
            b_block=pl.BlockSpec((None, 2 * fs, 1024), lambda i, j, kk: (kk, 0, j)), epi=_add_res, extras=(h,))
        xn_next = None
    else:
        (out, xn_next), got_down = _mm_hosting(
            sides[1], "mm_down", act, w_down4, "nn", out_dtypes=(F32, BF16), mkn=(n_rows, D_FF, D_MODEL),
            tk=2 * fs, tn=D_MODEL, b_block=pl.BlockSpec((None, 2 * fs, D_MODEL), lambda i, j, kk: (kk, 0, 0)),
            epi=_res_norm, extras=(h,), row_vecs=(next_gain,))
    return out, xn_next, (h, gain, xn, up, act, w_up, w_down), got_up, got_down


def _mlp_bwd(cache, dh, dhb, sides=(None, None, None)):
    h, gain, xn, up, act, w_up, w_down = cache
    n_rows, fs = h.shape[0], _FF_SHARD
    dup, got0 = _mm_hosting(
        sides[0], "mm_dact", dhb, w_down, "nt", out_dtypes=(BF16,), mkn=(n_rows, D_MODEL, D_FF), tm=1024, tn=fs,
        b_block=pl.BlockSpec((None, fs, D_MODEL), lambda i, j, kk: (j, 0, kk)),
        epi=lambda acc, u: (acc * (2.0 * jnp.maximum(u, 0.0)),), extras=(up,))
    dw_down, got1 = _mm_hosting(
        sides[1], "mm_dw_down", act, dhb, "tn", out_dtypes=(BF16,), tk=n_rows,
        o_block=(pl.BlockSpec((None, 512, 1024), lambda i, j, kk: (i // 2, i % 2, j)), (N_DEV, fs, D_MODEL)))
    (dh_in, dhb_in, dgain), got2 = _mm_hosting(
        sides[2], "mm_dxn_up", dup, w_up, "nt", out_dtypes=(F32, BF16), mkn=(n_rows, D_FF, D_MODEL),
        tm=256, tn=D_MODEL, tk=2 * fs,
        b_block=pl.BlockSpec((None, D_MODEL, fs), lambda i, j, kk: (2 * kk, 0, 0)),
        b2_block=pl.BlockSpec((None, D_MODEL, fs), lambda i, j, kk: (2 * kk + 1, 0, 0)),
        epi=_rms_bwd_epi, extras=(h, dh), row_vecs=(gain,), col_acc=True)
    dw_up = _mm("mm_dw_up", xn, dup, "tn", out_dtypes=(BF16,), tk=n_rows,
                o_block=(pl.BlockSpec((None, 512, fs), lambda i, j, kk: (j, i, 0)), (N_DEV, D_MODEL, fs)))
    return dh_in, dhb_in, dgain, dw_up, dw_down, (got0, got1, got2)


def _even_fwd(h, xn, p, sides=(None, None, None)):
    proj, got_in = _mm_hosting(sides[0], "mm_w_in", xn, p["w_in"], "nn")
    (y_pre, z, s5_states), got_s5 = _s5_fwd("s5_fwd", proj, p["mats"], sides[1])
    gate, ya = _mm("mm_glu", z, p["w_glu"], "nn", out_dtypes=(F32, BF16), extras=(y_pre,), row_vecs=(p["b_glu"],),
                   epi=lambda acc, y, b: (acc, _gelu(y) * _sigmoid(acc + b)))
    (o, yb, h_states), got_h = _hgrn_fwd("hgrn_fwd", proj, p["lb"].reshape(8, 1, 128), p["o_gain"].reshape(1, 128),
                                         sides[2])
    ycat = jnp.concatenate([ya, yb], axis=1)
    out, xn_mlp = _mm("mm_w_out", ycat, p["w_out"], "nn", out_dtypes=(F32, BF16), tn=D_MODEL, epi=_res_norm,
                      extras=(h,), row_vecs=(p["mlp_gain"],))
    return out, xn_mlp, (h, xn, proj, y_pre, z, s5_states, gate, o, h_states, ycat), (got_in, got_s5, got_h)


def _even_bwd(cache, p, dh, dhb, sides=(None, None)):
    h, xn, proj, y_pre, z, s5_states, gate, o, h_states, ycat = cache
    g = {}
    dycat = _mm("mm_dy_out", dhb, p["w_out"], "nt", tn=D_MODEL)
    g["w_out"] = _mm("mm_dw_out", ycat, dhb, "tn", out_dtypes=(BF16,))
    dq, df, di, dg, dlb, dgain = _hgrn_bwd("hgrn_bwd", proj, p["lb"].reshape(8, 1, 128),
                                           p["o_gain"].reshape(1, 128), o, h_states, dycat)
    g["lb"] = dlb.reshape(-1)
    g["o_gain"] = jnp.sum(dgain, axis=0).reshape(-1)

    def glu_bwd1(dyc, y, gt, b):
        zf = _gelu(y)
        s = _sigmoid(gt + b)
        dya = dyc[:, :S5_WIDTH]
        d_gate = dya * zf * s * (1.0 - s)
        return (d_gate, dya * s), (_colsum(d_gate),)

    d_gate, dz_direct, db_glu = _rowwise("glu_bwd_gate", glu_bwd1, [dycat, y_pre, gate], [p["b_glu"].reshape(1, -1)],
                                         [(S5_WIDTH, BF16), (S5_WIDTH, F32)], [S5_WIDTH])
    g["b_glu"] = db_glu.reshape(-1)
    dy_pre = _mm("mm_dz_glu", d_gate, p["w_glu"], "nt", extras=(dz_direct, y_pre),
                 epi=lambda acc, dzd, y: ((acc + dzd) * _gelu_grad(y),))
    g["w_glu"] = _mm("mm_dw_glu", z, d_gate, "tn", out_dtypes=(BF16,))
    du, dbm, dcm, da, dd = _s5_bwd("s5_bwd", proj, dy_pre, s5_states, p["mats"])
    g["s5"] = (dbm, dcm, da, dd)
    dproj = jnp.concatenate([du, dq, df, di, dg], axis=1)
    (dh_in, dhb_in, dnorm), got0 = _mm_hosting(
        sides[0], "mm_dxn_in", dproj, p["w_in"], "nt", out_dtypes=(F32, BF16), tm=256, tn=D_MODEL, tk=2560,
        epi=_rms_bwd_epi, extras=(h, dh), row_vecs=(p["norm"],), col_acc=True)
    g["w_in"], got1 = _mm_hosting(sides[1], "mm_dw_in", xn, dproj, "tn", out_dtypes=(BF16,))
    g["norm"] = dnorm.reshape(-1)
    return dh_in, dhb_in, g, (got0, got1)


def _odd_fwd(h, xn, p, sides=(None, None, None)):
    qkv, got = _mm_hosting(sides[0], "mm_w_qkv", xn, p["w_qkv"], "nn", tn=1280)
    o = _swa_fwd("swa_fwd", qkv, p["q_gain"], p["k_gain"], p["sinks"], p["slopes"])
    out, xn_mlp = _mm("mm_w_out", o, p["w_out"], "nn", out_dtypes=(F32, BF16), tn=D_MODEL, epi=_res_norm,
                      extras=(h,), row_vecs=(p["mlp_gain"],))
    return out, xn_mlp, (h, xn, qkv, o), (got, None, None)


def _shift_up_block(x):
    return jnp.concatenate([x[ATT_BLOCK:], jnp.zeros((ATT_BLOCK, x.shape[1]), x.dtype)], axis=0)


def _odd_bwd(cache, p, dh, dhb, sides=(None, None)):
    h, xn, qkv, o = cache
    g = {}
    d_o = _mm("mm_dy_out", dhb, p["w_out"], "nt", tn=D_MODEL)
    g["w_out"] = _mm("mm_dw_out", o, dhb, "tn", out_dtypes=(BF16,))
    dq, dkc, dkp, dvc, dvp, dsink, dqg, dkg = _swa_bwd("swa_bwd", qkv, p["q_gain"], p["k_gain"], p["sinks"],
                                                       p["slopes"], d_o)
    dk = (dkc.astype(F32) + _shift_up_block(dkp).astype(F32)).astype(BF16)
    dv = (dvc.astype(F32) + _shift_up_block(dvp).astype(F32)).astype(BF16)
    g["sinks"], g["q_gain"], g["k_gain"] = dsink[:, 0], dqg.reshape(-1), dkg.reshape(-1)
    dqkv = jnp.concatenate([dq, dk, dv], axis=1)
    dh_in, dhb_in, dnorm = _mm("mm_dxn_qkv", dqkv, p["w_qkv"], "nt", out_dtypes=(F32, BF16), tm=256, tn=D_MODEL,
                               tk=1280, epi=_rms_bwd_epi, extras=(h, dh), row_vecs=(p["norm"],), col_acc=True)
    g["w_qkv"] = _mm("mm_dw_qkv", xn, dqkv, "tn", out_dtypes=(BF16,), tn=1280)
    g["norm"] = dnorm.reshape(-1)
    return dh_in, dhb_in, g, (None, None)


def kernel(x, even_norm, even_w_in, s5_lambda_re, s5_lambda_im, s5_log_dt, s5_b_re, s5_b_im, s5_c_re, s5_c_im, s5_d, s5_w_glu, s5_b_glu, hgrn_lower_bound, hgrn_o_norm, even_w_out, odd_norm, odd_w_qkv, q_norm, k_norm, att_sinks, odd_w_out, mlp_norm, mlp_w_up, mlp_w_down, loss_target, m_even_norm, m_even_w_in, m_s5_lambda_re, m_s5_lambda_im, m_s5_log_dt, m_s5_b_re, m_s5_b_im, m_s5_c_re, m_s5_c_im, m_s5_d, m_s5_w_glu, m_s5_b_glu, m_hgrn_lower_bound, m_hgrn_o_norm, m_even_w_out, m_odd_norm, m_odd_w_qkv, m_q_norm, m_k_norm, m_att_sinks, m_odd_w_out, m_mlp_norm, m_mlp_w_up, m_mlp_w_down, v_even_norm, v_even_w_in, v_s5_lambda_re, v_s5_lambda_im, v_s5_log_dt, v_s5_b_re, v_s5_b_im, v_s5_c_re, v_s5_c_im, v_s5_d, v_s5_w_glu, v_s5_b_glu, v_hgrn_lower_bound, v_hgrn_o_norm, v_even_w_out, v_odd_norm, v_odd_w_qkv, v_q_norm, v_k_norm, v_att_sinks, v_odd_w_out, v_mlp_norm, v_mlp_w_up, v_mlp_w_down):
    a = dict(locals())
    n_rows = x.shape[1]
    xi, yi, ci = _mesh_pos()
    me = 4 * xi + 2 * yi + ci

    chunks = [[_pack_mixer_shard(layer % 2, layer // 2, a), mlp_w_up[layer].astype(BF16),
               mlp_w_down[layer].astype(BF16)] for layer in range(DEPTH)]
    gathered = list(_allgather("gather_layer", chunks[0][:1])) + [None, None]
    (odd_gathered,) = _allgather("gather_odd_norm", [jnp.pad(odd_norm, ((0, 6), (0, 0)))])
    odd_norm_full = odd_gathered[:, :2].transpose(1, 0, 2).reshape(2, D_MODEL)

    lower_bounds, lb_vjp = jax.vjp(_hgrn_lower_bounds, hgrn_lower_bound)
    slopes = _alibi_slopes()
    s5_vjps = []

    h = x.reshape(n_rows, D_MODEL)
    mixer_gain = [even_norm[layer // 2] if layer % 2 == 0 else odd_norm_full[layer // 2] for layer in range(DEPTH)]
    xn = _rms_fwd("rms_fwd", h, mixer_gain[0])
    caches, layer_p = [], []
    for layer in range(DEPTH):
        kind, j = layer % 2, layer // 2
        wl = _unpack_mixer(kind, gathered[0])
        nxt = chunks[layer + 1] if layer + 1 < DEPTH else None
        sides = [_gather_side([t]) for t in nxt] if nxt is not None else [None] * 3
        late = [_gather_side([t]) if gathered[i] is None else None for i, t in ((1, chunks[layer][1]), (2, chunks[layer][2]))]
        if kind == 0:
            disc, vjp = jax.vjp(_s5_discretize, s5_lambda_re[j], s5_lambda_im[j], s5_log_dt[j], s5_b_re[j], s5_b_im[j])
            s5_vjps.append(vjp)
            p = dict(norm=mixer_gain[layer], w_in=wl["w_in"], w_glu=wl["w_glu"], b_glu=s5_b_glu[j],
                     mats=_s5_matrices(*disc, s5_c_re[j], s5_c_im[j], s5_d[j]),
                     lb=lower_bounds[j], o_gain=hgrn_o_norm[j], w_out=wl["w_out"], mlp_gain=mlp_norm[layer])
            h, xn, c_mix, got_mix = _even_fwd(h, xn, p, [sides[0]] + late)
        else:
            p = dict(norm=mixer_gain[layer], w_qkv=wl["w_qkv"], q_gain=q_norm[j], k_gain=k_norm[j],
                     sinks=att_sinks[j], slopes=slopes, w_out=wl["w_out"], mlp_gain=mlp_norm[layer])
            h, xn, c_mix, got_mix = _odd_fwd(h, xn, p, [sides[0], None, None])
        w_up_g = gathered[1] if gathered[1] is not None else got_mix[1][0]
        w_down_g = gathered[2] if gathered[2] is not None else got_mix[2][0]
        next_gain = mixer_gain[layer + 1] if layer + 1 < DEPTH else None
        h, xn, c_mlp, got_up, got_down = _mlp_fwd(h, xn, mlp_norm[layer], next_gain, w_up_g, w_down_g, sides[1:])
        caches.append((c_mix, c_mlp))
        layer_p.append(p)
        if nxt is not None:
            gathered = [got_mix[0][0], got_up[0], got_down[0]]
    dh, dhb, sq = _loss_head(h, loss_target.reshape(n_rows, D_MODEL))
    loss = lax.psum(0.5 * sq[0, 0] / D_MODEL, ("x", "y", "c"))

    core = ci.astype(jnp.int32).reshape(1)
    mix_g, mlp_norm_g, received = [None] * DEPTH, [None] * DEPTH, [None] * DEPTH
    pending = None
    for layer in reversed(range(DEPTH)):
        kind = layer % 2
        c_mix, c_mlp = caches[layer]
        sides = [_exchange_side([t]) for t in pending] if pending is not None else [None] * 3
        dh, dhb, d_mlp_norm, dw_up, dw_down, got = _mlp_bwd(c_mlp, dh, dhb, sides)
        if pending is not None:
            received[layer + 1] = [g[0] for g in got]
        mlp_norm_g[layer] = d_mlp_norm.reshape(-1)
        by_chip = [t.reshape((4, 2) + t.shape[1:]) for t in (dw_up, dw_down)]
        early = [None, None]
        if layer == 0:
            arrived = _swap_with_sibling("swap_grads", by_chip)
            early = [_exchange_side([_add_pair("add_sibling_grads", m, g, core)]) for m, g in zip(by_chip, arrived)]
        bwd = _even_bwd if kind == 0 else _odd_bwd
        dh, dhb, mix_g[layer], got = bwd(c_mix, layer_p[layer], dh, dhb, early)
        by_chip = [_pack_mixer_grads(kind, mix_g[layer]).reshape(4, 2, -1, _PACK_COLS)] + (by_chip if layer > 0 else [])
        arrived = _swap_with_sibling("swap_grads", by_chip)
        pending = [_add_pair("add_sibling_grads", m, g, core) for m, g in zip(by_chip, arrived)]
    received[0] = list(_exchange_chips("exchange_grads", pending)) + [got[0][0], got[1][0]]
    grad_x = dh.reshape(x.shape)

    ev, od = [mix_g[0], mix_g[2]], [mix_g[1], mix_g[3]]
    sums = [[_sum_blocks("sum_grads", r) for r in received[layer]] for layer in range(DEPTH)]
    grads = {"mlp_w_up": jnp.stack([s[1] for s in sums]), "mlp_w_down": jnp.stack([s[2] for s in sums])}
    per_layer = [_unpack_mixer_grads(layer % 2, sums[layer][0]) for layer in range(DEPTH)]
    for name in _BIG_NAMES:
        grads[name] = jnp.stack([g[name] for g in per_layer if name in g])

    s5_g = []
    for j in range(2):
        dar, dai, dbbr, dbbi, dcr, dci, dd = _s5_unpack_grads(*ev[j]["s5"])
        s5_g.append(tuple(s5_vjps[j]((dar, dai, dbbr, dbbi))) + (dcr, dci, dd))
    (d_lb_param,) = lb_vjp(jnp.stack([g["lb"] for g in ev]))
    small = {
        "even_norm": jnp.stack([g["norm"] for g in ev]),
        "s5_lambda_re": jnp.stack([g[0] for g in s5_g]), "s5_lambda_im": jnp.stack([g[1] for g in s5_g]),
        "s5_log_dt": jnp.stack([g[2] for g in s5_g]), "s5_b_re": jnp.stack([g[3] for g in s5_g]),
        "s5_b_im": jnp.stack([g[4] for g in s5_g]), "s5_c_re": jnp.stack([g[5] for g in s5_g]),
        "s5_c_im": jnp.stack([g[6] for g in s5_g]), "s5_d": jnp.stack([g[7] for g in s5_g]),
        "s5_b_glu": jnp.stack([g["b_glu"] for g in ev]), "hgrn_lower_bound": d_lb_param,
        "hgrn_o_norm": jnp.stack([g["o_gain"] for g in ev]), "odd_norm": jnp.stack([g["norm"] for g in od]),
        "q_norm": jnp.stack([g["q_gain"] for g in od]), "k_norm": jnp.stack([g["k_gain"] for g in od]),
        "att_sinks": jnp.stack([g["sinks"] for g in od]), "mlp_norm": jnp.stack(mlp_norm_g),
    }
    small_shapes = [small[n].shape for n in _SMALL_NAMES]
    (small_all,) = _allgather("gather_small_grads", [_pack_small([small[n] for n in _SMALL_NAMES])])
    small_sum = _sum_blocks("sum_small_grads", small_all)
    for n, g in zip(_SMALL_NAMES, _unpack_small(small_sum, small_shapes)):
        grads[n] = g
    grads["odd_norm"] = lax.dynamic_slice_in_dim(grads["odd_norm"], me * (D_MODEL // N_DEV), D_MODEL // N_DEV, axis=1)

    delta, new_m, new_v = {}, {}, {}
    for name in _BIG_NAMES + _MLP_NAMES:
        to2d = lambda t, c=a[name].shape[-1]: t.reshape(-1, c)
        d_, m_, v_ = _adamw("adamw_" + name, to2d(a[name]), to2d(grads[name]), to2d(a["m_" + name]), to2d(a["v_" + name]))
        delta[name], new_m[name], new_v[name] = (t.reshape(a[name].shape) for t in (d_, m_, v_))
    packed = [_pack_small([src[n] for n in _SMALL_NAMES])
              for src in (a, grads, {n: a["m_" + n] for n in _SMALL_NAMES}, {n: a["v_" + n] for n in _SMALL_NAMES})]
    shapes = [a[n].shape for n in _SMALL_NAMES]
    for dst, flat in zip((delta, new_m, new_v), _adamw("adamw_small", *packed)):
        for n, t in zip(_SMALL_NAMES, _unpack_small(flat, shapes)):
            dst[n] = t

    return (loss, grad_x, *[grads[n] for n in _WEIGHTS], *[delta[n] for n in _WEIGHTS],
            *[new_m[n] for n in _WEIGHTS], *[new_v[n] for n in _WEIGHTS])
```

```python
import math

import jax
import jax.numpy as jnp
from jax import lax
from jax.experimental import pallas as pl
from jax.experimental.pallas import tpu as pltpu

F32 = jnp.float32
BF16 = jnp.bfloat16
MESH = pl.DeviceIdType.MESH

D_MODEL = 2048
DEPTH = 4
EPS = 1e-6
S5_WIDTH = 1024
S5_GROUPS = 64
S5_STATE = 64
S5_GROUP_SIZE = 16
S5_MIN_DECAY = 1e-4
S5_CHUNK = 128
S5_LEVELS = 7
HGRN_WIDTH = 1024
HGRN_HEADS = 8
HGRN_DIM = 128
HGRN_SUB = 16
HGRN_BLOCK = 128
ATT_HEADS = 32
ATT_KV = 4
ATT_DIM = 64
ATT_BLOCK = 128
QKV_WIDTH = (ATT_HEADS + 2 * ATT_KV) * ATT_DIM
D_FF = 4 * D_MODEL
N_DEV = 8
NEG = -1e30
VMEM_LIMIT = 56 * 1024 * 1024

ADAM_LR, ADAM_B1, ADAM_B2, ADAM_EPS, ADAM_WD, ADAM_STEP = 0.001, 0.9, 0.999, 1e-08, 0.01, 10


def _params(sem=None):
    return pltpu.CompilerParams(dimension_semantics=sem, vmem_limit_bytes=VMEM_LIMIT)


def _sds(shape, dtype):
    return jax.ShapeDtypeStruct(shape, dtype)


def _call_hosting(side, body, name, grid, in_specs, out_specs, out_shape, scratch_shapes, sem, operands):
    if side is None:
        return pl.pallas_call(body, name=name, grid=grid, in_specs=in_specs, out_specs=out_specs, out_shape=out_shape,
                              scratch_shapes=scratch_shapes, compiler_params=_params(sem))(*operands), None
    n_in, n_out, n_scr = len(in_specs), len(out_specs), len(scratch_shapes)
    n_sin, n_sout = len(side.operands), len(side.out_shapes)
    any_spec = pl.BlockSpec(memory_space=pl.ANY)

    def hosting(*refs):
        ins, refs = refs[:n_in], refs[n_in:]
        sin, refs = refs[:n_sin], refs[n_sin:]
        outs, refs = refs[:n_out], refs[n_out:]
        sout, refs = refs[:n_sout], refs[n_sout:]
        scr, sems = refs[:n_scr], refs[n_scr:]
        ids = [pl.program_id(d) for d in range(len(grid))]
        first, last = ids[0] == 0, ids[0] == grid[0] - 1
        for d in range(1, len(grid)):
            first, last = first & (ids[d] == 0), last & (ids[d] == grid[d] - 1)

        @pl.when(first)
        def _():
            side.start(sin, sout, sems)

        body(*ins, *outs, *scr)

        @pl.when(last)
        def _():
            side.finish(sin, sout, sems)

    outs = pl.pallas_call(
        hosting, name=name, grid=grid,
        in_specs=list(in_specs) + [any_spec] * n_sin,
        out_specs=list(out_specs) + [any_spec] * n_sout,
        out_shape=list(out_shape) + side.out_shapes,
        scratch_shapes=list(scratch_shapes) + side.scratch,
        compiler_params=_params(("arbitrary",) * len(grid)),
    )(*operands, *side.operands)
    return outs[:n_out], outs[n_out:]


def _mm(name, a, b, mode, out_dtypes=(F32,), epi=None, extras=(), tm=512, tn=1024, tk=2048,
        mkn=None, b_block=None, b2_block=None, o_block=None, side=None, row_vecs=(), col_acc=False):
    if mkn is not None:
        m, k, n = mkn
    elif mode == "nn":
        (m, k), n = a.shape, b.shape[1]
    elif mode == "nt":
        (m, k), n = a.shape, b.shape[0]
    else:
        (k, m), n = a.shape, b.shape[1]
    tm, tn, tk = min(tm, m), min(tn, n), min(tk, k)
    assert m % tm == 0 and n % tn == 0 and k % tk == 0, (name, m, n, k)
    nk = k // tk
    if mode == "nn":
        a_spec = pl.BlockSpec((tm, tk), lambda i, j, kk: (i, kk))
        b_spec = pl.BlockSpec((tk, tn), lambda i, j, kk: (kk, j))
        dims = (((1,), (0,)), ((), ()))
    elif mode == "nt":
        a_spec = pl.BlockSpec((tm, tk), lambda i, j, kk: (i, kk))
        b_spec = pl.BlockSpec((tn, tk), lambda i, j, kk: (j, kk))
        dims = (((1,), (1,)), ((), ()))
    else:
        a_spec = pl.BlockSpec((tk, tm), lambda i, j, kk: (kk, i))
        b_spec = pl.BlockSpec((tk, tn), lambda i, j, kk: (kk, j))
        dims = (((0,), (0,)), ((), ()))
    o_spec = pl.BlockSpec((tm, tn), lambda i, j, kk: (i, j))
    if b_block is not None:
        b_spec = b_block
    out_specs = [o_spec] * len(out_dtypes)
    out_shape = [_sds((m, n), dt) for dt in out_dtypes]
    if o_block is not None:
        assert len(out_dtypes) == 1 and not extras
        out_specs, out_shape = [o_block[0]], [_sds(o_block[1], out_dtypes[0])]
    n_ex, n_out = len(extras) + len(row_vecs), len(out_dtypes)
    n_b = 1 if b2_block is None else 2
    grid = (m // tm, n // tn, nk)
    vec_spec = pl.BlockSpec((1, tn), lambda i, j, kk: (0, j))
    if col_acc:
        assert tn == n
        out_specs, out_shape = out_specs + [vec_spec], out_shape + [_sds((1, n), F32)]
    n_all = n_out + (1 if col_acc else 0)

    def body(*refs):
        a_ref, b_refs = refs[0], refs[1:1 + n_b]
        pos = 1 + n_b
        ex_refs = refs[pos:pos + n_ex]
        pos += n_ex
        out_refs = refs[pos:pos + n_out]
        col_ref = refs[pos + n_out] if col_acc else None
        pos += n_all
        acc_ref = refs[pos] if nk > 1 else None
        av = a_ref[...]
        if av.dtype != BF16:
            av = av.astype(BF16)
        part = None
        for q, b_ref in enumerate(b_refs):
            bv = b_ref[...]
            if bv.dtype != BF16:
                bv = bv.astype(BF16)
            aq = av if n_b == 1 else av[:, q * (tk // 2):(q + 1) * (tk // 2)]
            d = lax.dot_general(aq, bv, dims, preferred_element_type=F32)
            part = d if part is None else part + d

        def finish(acc):
            outs = epi(acc, *[r[...] for r in ex_refs]) if epi is not None else (acc,)
            for r, o in zip(out_refs, outs):
                r[...] = o.astype(r.dtype)
            if col_acc:
                row_tile = pl.program_id(0)

                @pl.when(row_tile == 0)
                def _():
                    col_ref[...] = outs[n_out]

                @pl.when(row_tile > 0)
                def _():
                    col_ref[...] += outs[n_out]

        if nk == 1:
            finish(part)
        else:
            kk = pl.program_id(2)

            @pl.when(kk == 0)
            def _():
                acc_ref[...] = part

            @pl.when(kk > 0)
            def _():
                acc_ref[...] += part

            @pl.when(kk == nk - 1)
            def _():
                finish(acc_ref[...])

    b_specs = [b_spec] if b2_block is None else [b_spec, b2_block]
    outs, side_outs = _call_hosting(
        side, body, name, grid, [a_spec] + b_specs + [o_spec] * len(extras) + [vec_spec] * len(row_vecs),
        out_specs, out_shape, [pltpu.VMEM((tm, tn), F32)] if nk > 1 else [],
        ("arbitrary",) * 3 if col_acc else ("parallel", "parallel", "arbitrary"),
        (a,) + (b,) * n_b + tuple(extras) + tuple(v.reshape(1, -1) for v in row_vecs))
    main = outs[0] if n_all == 1 else outs
    return main if side is None else (main, side_outs)


def _rowwise(name, fn, rows, vecs, outs, accs=(), tr=256):
    n_rows = rows[0].shape[0]
    tr = min(tr, n_rows)
    assert n_rows % tr == 0
    n_r, n_v, n_o, n_a = len(rows), len(vecs), len(outs), len(accs)

    def body(*refs):
        ins = [r[...] for r in refs[:n_r + n_v]]
        o_refs = refs[n_r + n_v:n_r + n_v + n_o]
        a_refs = refs[n_r + n_v + n_o:]
        ro, ao = fn(*ins)
        for r, o in zip(o_refs, ro):
            r[...] = o.astype(r.dtype)
        if n_a:
            step = pl.program_id(0)

            @pl.when(step == 0)
            def _():
                for r, o in zip(a_refs, ao):
                    r[...] = o

            @pl.when(step > 0)
            def _():
                for r, o in zip(a_refs, ao):
                    r[...] += o

    res = pl.pallas_call(
        body, name=name,
        grid=(n_rows // tr,),
        in_specs=[pl.BlockSpec((tr, r.shape[1]), lambda i: (i, 0)) for r in rows]
        + [pl.BlockSpec(v.shape, lambda i: (0, 0)) for v in vecs],
        out_specs=[pl.BlockSpec((tr, w), lambda i: (i, 0)) for w, _ in outs]
        + [pl.BlockSpec((1, w), lambda i: (0, 0)) for w in accs],
        out_shape=[_sds((n_rows, w), dt) for w, dt in outs] + [_sds((1, w), F32) for w in accs],
        compiler_params=_params(("arbitrary",)),
    )(*rows, *vecs)
    return res


def _colsum(x):
    return jnp.sum(x, axis=0, keepdims=True)


def _sigmoid(x):
    return 1.0 / (1.0 + jnp.exp(-x))


_GELU_C = math.sqrt(2.0 / math.pi)


def _gelu(y):
    return 0.5 * y * (1.0 + jnp.tanh(_GELU_C * (y + 0.044715 * y * y * y)))


def _gelu_grad(y):
    t = jnp.tanh(_GELU_C * (y + 0.044715 * y * y * y))
    return 0.5 * (1.0 + t) + 0.5 * y * (1.0 - t * t) * _GELU_C * (1.0 + 3.0 * 0.044715 * y * y)


def _rms_fwd(name, h, gain):
    def fn(x, g):
        r = lax.rsqrt(jnp.mean(x * x, axis=1, keepdims=True) + EPS)
        return (x * r * g,), ()
    return _rowwise(name, fn, [h], [gain.reshape(1, -1)], [(h.shape[1], BF16)])[0]


def _res_norm(acc, res, gain):
    hn = acc + res
    r = lax.rsqrt(jnp.mean(hn * hn, axis=1, keepdims=True) + EPS)
    return hn, hn * r * gain


def _rms_bwd_epi(dxn, h, dres, gain):
    r = lax.rsqrt(jnp.mean(h * h, axis=1, keepdims=True) + EPS)
    xh = h * r
    gdy = dxn * gain
    dx = r * (gdy - xh * jnp.mean(gdy * xh, axis=1, keepdims=True)) + dres
    return dx, dx, _colsum(dxn * xh)


def _loss_head(h, target):
    w = h.shape[1]

    def fn(x, t):
        e = x - t
        return (e * (1.0 / w), e * (1.0 / w)), (jnp.zeros((1, 128), F32) + jnp.sum(e * e),)
    return _rowwise("loss_head", fn, [h, target], [], [(w, F32), (w, BF16)], [128])


def _adamw(name, w, g, m, v):
    c1 = 1.0 - ADAM_B1 ** ADAM_STEP
    c2 = 1.0 - ADAM_B2 ** ADAM_STEP

    def fn(w_, g_, m_, v_):
        mn = ADAM_B1 * m_ + (1.0 - ADAM_B1) * g_
        vn = ADAM_B2 * v_ + (1.0 - ADAM_B2) * (g_ * g_)
        delta = -ADAM_LR * ((mn / c1) / (jnp.sqrt(vn / c2) + ADAM_EPS) + ADAM_WD * w_)
        return (delta, mn, vn), ()
    c = w.shape[1]
    return _rowwise(name, fn, [w, g, m, v], [], [(c, F32)] * 3)


def _s5_discretize(lam_re, lam_im, log_dt, b_re, b_im):
    lr = jnp.minimum(lam_re, -S5_MIN_DECAY)
    li = lam_im
    dt = jnp.exp(log_dt)[:, None]
    mag = jnp.exp(lr * dt)
    ar = mag * jnp.cos(li * dt)
    ai = mag * jnp.sin(li * dt)
    den = lr * lr + li * li
    zr = ((ar - 1.0) * lr + ai * li) / den
    zi = (ai * lr - (ar - 1.0) * li) / den
    bbr = zr[..., None] * b_re - zi[..., None] * b_im
    bbi = zr[..., None] * b_im + zi[..., None] * b_re
    return ar, ai, bbr, bbi


def _s5_matrices(ar, ai, bbr, bbi, c_re, c_im, d_skip):
    eye = jnp.eye(8, dtype=F32)
    bt = jnp.stack([bbr, bbi], axis=1).transpose(0, 3, 1, 2)
    bt = bt.reshape(8, 8, 16, 1, 2, 64) * eye[None, :, None, :, None, None]
    bm8 = bt.reshape(8, 8, 16, 4, 2, 2, 64).transpose(0, 1, 2, 3, 5, 4, 6).reshape(8, 128, 1024)
    ct = jnp.stack([c_re, -c_im], axis=1).transpose(0, 1, 3, 2)
    ct = ct.reshape(8, 8, 2, 64, 1, 16) * eye[None, :, None, None, :, None]
    cm8 = ct.reshape(8, 4, 2, 2, 64, 8, 16).transpose(0, 1, 3, 2, 4, 5, 6).reshape(8, 1024, 128)
    prs, pis = [], []
    pr, pi = ar, ai
    for _ in range(S5_LEVELS):
        prs.append(pr.reshape(8, 512))
        pis.append(pi.reshape(8, 512))
        pr, pi = pr * pr - pi * pi, 2.0 * pr * pi
    prs.append(jnp.zeros_like(prs[0]))
    pis.append(jnp.zeros_like(pis[0]))
    return (bm8.astype(BF16), cm8.astype(BF16), jnp.stack(prs, axis=1), jnp.stack(pis, axis=1),
            d_skip.reshape(8, 1, 128))


def _s5_unpack_grads(dbm8, dcm8, da, dd):
    db = dbm8.reshape(8, 8, 16, 4, 2, 2, 64).transpose(0, 1, 2, 3, 5, 4, 6).reshape(8, 8, 16, 8, 2, 64)
    db = jnp.einsum("agcgqp->agcqp", db).reshape(S5_GROUPS, 16, 2, 64)
    dc = dcm8.reshape(8, 4, 2, 2, 64, 8, 16).transpose(0, 1, 3, 2, 4, 5, 6).reshape(8, 8, 2, 64, 8, 16)
    dc = jnp.einsum("agqpgc->agqpc", dc).reshape(S5_GROUPS, 2, 64, 16)
    dar = da[:, 0, :].reshape(S5_GROUPS, 64)
    dai = da[:, 1, :].reshape(S5_GROUPS, 64)
    return (dar, dai, db[:, :, 0, :].transpose(0, 2, 1), db[:, :, 1, :].transpose(0, 2, 1),
            dc[:, 0].transpose(0, 2, 1), -dc[:, 1].transpose(0, 2, 1), dd.reshape(S5_GROUPS, 16))


def _shift_rows(x, s, row, down):
    t = x.shape[0]
    if s % 8 == 0:
        z = jnp.zeros((s, x.shape[1]), x.dtype)
        return jnp.concatenate([z, x[:t - s]], axis=0) if down else jnp.concatenate([x[s:], z], axis=0)
    if down:
        return jnp.where(row >= s, pltpu.roll(x, s, 0), 0.0)
    return jnp.where(row < t - s, pltpu.roll(x, t - s, 0), 0.0)


def _s5_scan(xr, xi, pr, pi, cr, ci, row, conj):
    t = xr[0].shape[0]
    sg = -1.0 if conj else 1.0
    edge = (t - 1) if conj else 0
    n = len(xr)
    for k in range(n):
        sl = slice(128 * k, 128 * (k + 1))
        p_r, p_i = pr[0:1, sl], sg * pi[0:1, sl]
        xr[k] = xr[k] + jnp.where(row == edge, p_r * cr[k] - p_i * ci[k], 0.0)
        xi[k] = xi[k] + jnp.where(row == edge, p_r * ci[k] + p_i * cr[k], 0.0)
    for lvl in range(S5_LEVELS):
        s = 1 << lvl
        for k in range(n):
            sl = slice(128 * k, 128 * (k + 1))
            p_r, p_i = pr[lvl:lvl + 1, sl], sg * pi[lvl:lvl + 1, sl]
            sr = _shift_rows(xr[k], s, row, not conj)
            si = _shift_rows(xi[k], s, row, not conj)
            xr[k] = xr[k] + p_r * sr - p_i * si
            xi[k] = xi[k] + p_r * si + p_i * sr
    return xr, xi


def _s5_fwd(name, proj, mats, side=None):
    bm8, cm8, p1, p2, d8 = mats
    n_rows = proj.shape[0]
    t = S5_CHUNK
    nch = n_rows // t

    def body(u_ref, bm_ref, cm_ref, pr_ref, pi_ref, d_ref, y_ref, z_ref, st_ref, carry):
        @pl.when(pl.program_id(1) == 0)
        def _():
            carry[...] = jnp.zeros_like(carry)

        cv = carry[...]
        st_ref[...] = cv
        u = u_ref[...]
        bu = jnp.dot(u.astype(BF16), bm_ref[...], preferred_element_type=F32)
        row = lax.broadcasted_iota(jnp.int32, (t, 128), 0)
        tile = lambda v, j: v[:, 128 * j:128 * (j + 1)]
        xr, xi = _s5_scan([tile(bu, 2 * k) for k in range(4)], [tile(bu, 2 * k + 1) for k in range(4)],
                          pr_ref[...], pi_ref[...], [tile(cv, 2 * k)[0:1] for k in range(4)],
                          [tile(cv, 2 * k + 1)[0:1] for k in range(4)], row, False)
        xall = jnp.concatenate([v for k in range(4) for v in (xr[k], xi[k])], axis=1)
        carry[...] = jnp.broadcast_to(xall[t - 1:t, :], (8, 1024))
        y = jnp.dot(xall.astype(BF16), cm_ref[...], preferred_element_type=F32) + d_ref[...] * u
        y_ref[...] = y
        z_ref[...] = _gelu(y).astype(BF16)

    return _call_hosting(
        side, body, name, (8, nch),
        [
            pl.BlockSpec((t, 128), lambda g, c: (c, g)),
            pl.BlockSpec((None, 128, 1024), lambda g, c: (g, 0, 0)),
            pl.BlockSpec((None, 1024, 128), lambda g, c: (g, 0, 0)),
            pl.BlockSpec((None, 8, 512), lambda g, c: (g, 0, 0)),
            pl.BlockSpec((None, 8, 512), lambda g, c: (g, 0, 0)),
            pl.BlockSpec((None, 1, 128), lambda g, c: (g, 0, 0)),
        ],
        [
            pl.BlockSpec((t, 128), lambda g, c: (c, g)),
            pl.BlockSpec((t, 128), lambda g, c: (c, g)),
            pl.BlockSpec((None, None, 8, 1024), lambda g, c: (g, c, 0, 0)),
        ],
        [_sds((n_rows, S5_WIDTH), F32), _sds((n_rows, S5_WIDTH), BF16), _sds((8, nch, 8, 1024), F32)],
        [pltpu.VMEM((8, 1024), F32)], ("parallel", "arbitrary"), (proj, bm8, cm8, p1, p2, d8))


def _s5_bwd(name, proj, dy, states, mats):
    bm8, cm8, p1, p2, d8 = mats
    n_rows = proj.shape[0]
    t = S5_CHUNK
    nch = n_rows // t
    nt_dims = (((1,), (1,)), ((), ()))
    tn_dims = (((0,), (0,)), ((), ()))

    def body(u_ref, dy_ref, st_ref, bm_ref, cm_ref, pr_ref, pi_ref, d_ref,
             du_ref, dbm_ref, dcm_ref, da_ref, dd_ref, gcarry):
        @pl.when(pl.program_id(1) == 0)
        def _():
            gcarry[...] = jnp.zeros_like(gcarry)
            dbm_ref[...] = jnp.zeros_like(dbm_ref)
            dcm_ref[...] = jnp.zeros_like(dcm_ref)
            da_ref[...] = jnp.zeros_like(da_ref)
            dd_ref[...] = jnp.zeros_like(dd_ref)

        u = u_ref[...]
        dyv = dy_ref[...]
        ub, dyb = u.astype(BF16), dyv.astype(BF16)
        bu = jnp.dot(ub, bm_ref[...], preferred_element_type=F32)
        dxd = lax.dot_general(dyb, cm_ref[...], nt_dims, preferred_element_type=F32)
        row = lax.broadcasted_iota(jnp.int32, (t, 128), 0)
        tile = lambda v, j: v[:, 128 * j:128 * (j + 1)]
        prv, piv, cv, gv = pr_ref[...], pi_ref[...], st_ref[...], gcarry[...]
        cr = [tile(cv, 2 * k)[0:1] for k in range(4)]
        ci = [tile(cv, 2 * k + 1)[0:1] for k in range(4)]
        xr, xi = _s5_scan([tile(bu, 2 * k) for k in range(4)], [tile(bu, 2 * k + 1) for k in range(4)],
                          prv, piv, cr, ci, row, False)
        gr, gi = _s5_scan([tile(dxd, 2 * k) for k in range(4)], [tile(dxd, 2 * k + 1) for k in range(4)],
                          prv, piv, [tile(gv, 2 * k)[0:1] for k in range(4)],
                          [tile(gv, 2 * k + 1)[0:1] for k in range(4)], row, True)
        dar, dai = [], []
        for k in range(4):
            xpr = jnp.where(row >= 1, pltpu.roll(xr[k], 1, 0), cr[k])
            xpi = jnp.where(row >= 1, pltpu.roll(xi[k], 1, 0), ci[k])
            dar.append(_colsum(gr[k] * xpr + gi[k] * xpi))
            dai.append(_colsum(gi[k] * xpr - gr[k] * xpi))
        xall = jnp.concatenate([v for k in range(4) for v in (xr[k], xi[k])], axis=1).astype(BF16)
        gf = jnp.concatenate([v for k in range(4) for v in (gr[k], gi[k])], axis=1)
        gcarry[...] = jnp.broadcast_to(gf[0:1, :], (8, 1024))
        gall = gf.astype(BF16)
        dcm_ref[...] += lax.dot_general(xall, dyb, tn_dims, preferred_element_type=F32)
        dbm_ref[...] += lax.dot_general(ub, gall, tn_dims, preferred_element_type=F32)
        du = lax.dot_general(gall, bm_ref[...], nt_dims, preferred_element_type=F32) + d_ref[...] * dyv
        du_ref[...] = du.astype(BF16)
        dd_ref[...] += _colsum(dyv * u)
        da_ref[0:1, :] += jnp.concatenate(dar, axis=1)
        da_ref[1:2, :] += jnp.concatenate(dai, axis=1)

    rev = lambda g, c: (nch - 1 - c, g)
    return pl.pallas_call(
        body, name=name,
        grid=(8, nch),
        in_specs=[
            pl.BlockSpec((t, 128), rev),
            pl.BlockSpec((t, 128), rev),
            pl.BlockSpec((None, None, 8, 1024), lambda g, c: (g, nch - 1 - c, 0, 0)),
            pl.BlockSpec((None, 128, 1024), lambda g, c: (g, 0, 0)),
            pl.BlockSpec((None, 1024, 128), lambda g, c: (g, 0, 0)),
            pl.BlockSpec((None, 8, 512), lambda g, c: (g, 0, 0)),
            pl.BlockSpec((None, 8, 512), lambda g, c: (g, 0, 0)),
            pl.BlockSpec((None, 1, 128), lambda g, c: (g, 0, 0)),
        ],
        out_specs=[
            pl.BlockSpec((t, 128), rev),
            pl.BlockSpec((None, 128, 1024), lambda g, c: (g, 0, 0)),
            pl.BlockSpec((None, 1024, 128), lambda g, c: (g, 0, 0)),
            pl.BlockSpec((None, 8, 512), lambda g, c: (g, 0, 0)),
            pl.BlockSpec((None, 1, 128), lambda g, c: (g, 0, 0)),
        ],
        out_shape=[_sds((n_rows, S5_WIDTH), BF16), _sds((8, 128, 1024), F32), _sds((8, 1024, 128), F32),
                   _sds((8, 8, 512), F32), _sds((8, 1, 128), F32)],
        scratch_shapes=[pltpu.VMEM((8, 1024), F32)],
        compiler_params=_params(("parallel", "arbitrary")),
    )(proj, dy, states, bm8, cm8, p1, p2, d8)


def _hgrn_lower_bounds(lb_param):
    p = jax.nn.softmax(lb_param, axis=0)
    return jnp.cumsum(p, axis=0) - p[0:1]


def _prefix16(x, r16):
    for s in (1, 2, 4, 8):
        x = x + jnp.where(r16 >= s, pltpu.roll(x, s, 0), 0.0)
    return x


def _suffix16(x, r16):
    n = x.shape[0]
    for s in (1, 2, 4, 8):
        x = x + jnp.where(r16 < HGRN_SUB - s, pltpu.roll(x, n - s, 0), 0.0)
    return x


_NT = (((1,), (1,)), ((), ()))
_TN = (((0,), (0,)), ((), ()))


def _dotf(a, b, dims=(((1,), (0,)), ((), ()))):
    return lax.dot_general(a.astype(BF16), b.astype(BF16), dims, preferred_element_type=F32)


def _hgrn_specs(n_blocks, rev):
    r = HGRN_BLOCK
    blk = (lambda b: n_blocks - 1 - b) if rev else (lambda b: b)
    proj_specs = [pl.BlockSpec((r, 128), (lambda h, b, c=c: (blk(b), 8 * c + h))) for c in (1, 2, 3, 4)]
    lb_spec = pl.BlockSpec((None, 1, 128), lambda h, b: (h, 0, 0))
    gain_spec = pl.BlockSpec((1, 128), lambda h, b: (0, 0))
    row_spec = pl.BlockSpec((r, 128), lambda h, b: (blk(b), h))
    st_spec = pl.BlockSpec((None, None, 128, 128), lambda h, b: (h, blk(b), 0, 0))
    return proj_specs, lb_spec, gain_spec, row_spec, st_spec, blk


def _hgrn_fwd(name, proj, lb, gain, side=None):
    n_rows = proj.shape[0]
    r = HGRN_BLOCK
    nb = n_rows // r
    nsub = r // HGRN_SUB
    proj_specs, lb_spec, gain_spec, row_spec, st_spec, _ = _hgrn_specs(nb, False)

    def body(q_ref, f_ref, i_ref, g_ref, lb_ref, gain_ref, o_ref, y_ref, st_ref, st_scr):
        @pl.when(pl.program_id(1) == 0)
        def _():
            st_scr[...] = jnp.zeros_like(st_scr)

        st_ref[...] = st_scr[...]
        q, f, v, g = q_ref[...], f_ref[...], i_ref[...], g_ref[...]
        lbv = lb_ref[...]
        qs = q * _sigmoid(q)
        fg = lbv + (1.0 - lbv) * _sigmoid(f)
        kk = 1.0 - fg
        r16 = lax.broadcasted_iota(jnp.int32, (r, 128), 0) & (HGRN_SUB - 1)
        b = _prefix16(jnp.log(fg), r16)
        qh = qs * jnp.exp(b)
        rs = lax.broadcasted_iota(jnp.int32, (HGRN_SUB, 128), 0)
        st = st_scr[...]
        outs = []
        for i in range(nsub):
            sl = slice(HGRN_SUB * i, HGRN_SUB * (i + 1))
            qsi, kki, vi, bi = qs[sl], kk[sl], v[sl], b[sl]
            o_i = _dotf(qh[sl], st, _NT)
            for s in range(HGRN_SUB):
                e = jnp.exp(jnp.where(rs >= s, bi - bi[s:s + 1], NEG))
                col = jnp.sum(qsi * e * kki[s:s + 1], axis=1, keepdims=True)
                o_i = o_i + col * vi[s:s + 1]
            bl = bi[HGRN_SUB - 1:HGRN_SUB]
            st = st * jnp.exp(bl) + _dotf(vi, kki * jnp.exp(bl - bi), _TN)
            outs.append(o_i)
        st_scr[...] = st
        o = jnp.concatenate(outs, axis=0)
        o_ref[...] = o
        rn = lax.rsqrt(jnp.mean(o * o, axis=1, keepdims=True) + EPS)
        y_ref[...] = (o * rn * gain_ref[...] * (g * _sigmoid(g))).astype(BF16)

    return _call_hosting(
        side, body, name, (HGRN_HEADS, nb), proj_specs + [lb_spec, gain_spec], [row_spec, row_spec, st_spec],
        [_sds((n_rows, HGRN_WIDTH), F32), _sds((n_rows, HGRN_WIDTH), BF16), _sds((HGRN_HEADS, nb, 128, 128), F32)],
        [pltpu.VMEM((128, 128), F32)], ("parallel", "arbitrary"), (proj, proj, proj, proj, lb, gain))


def _hgrn_bwd(name, proj, lb, gain, o_saved, states, dycat):
    n_rows = proj.shape[0]
    r = HGRN_BLOCK
    nb = n_rows // r
    nsub = r // HGRN_SUB
    proj_specs, lb_spec, gain_spec, row_spec, st_spec, blk = _hgrn_specs(nb, True)
    dy_spec = pl.BlockSpec((r, 128), lambda h, b: (blk(b), 8 + h))
    acc_spec = pl.BlockSpec((None, 1, 128), lambda h, b: (h, 0, 0))

    def body(q_ref, f_ref, i_ref, g_ref, lb_ref, gain_ref, o_ref, st_ref, dy_ref,
             dq_ref, df_ref, di_ref, dg_ref, dlb_ref, dgain_ref, dst_scr, sub_scr):
        @pl.when(pl.program_id(1) == 0)
        def _():
            dst_scr[...] = jnp.zeros_like(dst_scr)
            dlb_ref[...] = jnp.zeros_like(dlb_ref)
            dgain_ref[...] = jnp.zeros_like(dgain_ref)

        q, f, v, g = q_ref[...], f_ref[...], i_ref[...], g_ref[...]
        lbv, gain_v = lb_ref[...], gain_ref[...]
        sq = _sigmoid(q)
        qs = q * sq
        sf = _sigmoid(f)
        fg = lbv + (1.0 - lbv) * sf
        kk = 1.0 - fg
        r16 = lax.broadcasted_iota(jnp.int32, (r, 128), 0) & (HGRN_SUB - 1)
        b = _prefix16(jnp.log(fg), r16)
        eb = jnp.exp(b)
        qh = qs * eb

        o, dy = o_ref[...], dy_ref[...]
        rn = lax.rsqrt(jnp.mean(o * o, axis=1, keepdims=True) + EPS)
        on = o * rn
        sg = _sigmoid(g)
        sil = g * sg
        dgain_ref[...] += _colsum(dy * on * sil)
        dg_ref[...] = (dy * on * gain_v * (sg * (1.0 + g * (1.0 - sg)))).astype(BF16)
        don = dy * gain_v * sil
        do = rn * (don - on * jnp.mean(don * on, axis=1, keepdims=True))

        st = st_ref[...]
        for i in range(nsub):
            sl = slice(HGRN_SUB * i, HGRN_SUB * (i + 1))
            sub_scr[i] = st
            bi = b[sl]
            bl = bi[HGRN_SUB - 1:HGRN_SUB]
            st = st * jnp.exp(bl) + _dotf(v[sl], kk[sl] * jnp.exp(bl - bi), _TN)

        rs = lax.broadcasted_iota(jnp.int32, (HGRN_SUB, 128), 0)
        dst = dst_scr[...]
        parts = [None] * nsub
        for i in reversed(range(nsub)):
            sl = slice(HGRN_SUB * i, HGRN_SUB * (i + 1))
            sp = sub_scr[i]
            qsi, kki, vi, bi, doi, qhi = qs[sl], kk[sl], v[sl], b[sl], do[sl], qh[sl]
            bl = bi[HGRN_SUB - 1:HGRN_SUB]
            ebl = jnp.exp(bl)
            dec = jnp.exp(bl - bi)
            khat = kki * dec
            dqh = _dotf(doi, sp)
            dkhat = _dotf(vi, dst)
            dv = _dotf(khat, dst, _NT)
            zrow = _colsum(sp * dst) * ebl
            dq_in = jnp.zeros((HGRN_SUB, 128), F32)
            dk_in = jnp.zeros((HGRN_SUB, 128), F32)
            dv_in = jnp.zeros((HGRN_SUB, 128), F32)
            for s in range(HGRN_SUB):
                e = jnp.exp(jnp.where(rs >= s, bi - bi[s:s + 1], NEG))
                dpc = jnp.sum(doi * vi[s:s + 1], axis=1, keepdims=True)
                w = qsi * e
                pc = jnp.sum(w * kki[s:s + 1], axis=1, keepdims=True)
                dq_in = dq_in + dpc * e * kki[s:s + 1]
                dk_in = jnp.where(rs == s, _colsum(dpc * w), dk_in)
                dv_in = jnp.where(rs == s, _colsum(pc * doi), dv_in)
            kd = khat * dkhat
            parts[i] = (qsi * dq_in - kki * dk_in + qhi * dqh, kd, jnp.broadcast_to(zrow, (HGRN_SUB, 128)),
                        dq_in + dqh * eb[sl], dk_in + dkhat * dec, dv + dv_in)
            dst = dst * ebl + _dotf(doi, qhi, _TN)
        dst_scr[...] = dst

        cat = lambda j: jnp.concatenate([p[j] for p in parts], axis=0)
        d_b, kd, zr, dqs, dkk, dvv = (cat(j) for j in range(6))
        dlf = _suffix16(d_b, r16) + _prefix16(kd, r16) - kd + zr
        dfg = dlf / fg - dkk
        df_ref[...] = (dfg * (1.0 - lbv) * sf * (1.0 - sf)).astype(BF16)
        dlb_ref[...] += _colsum(dfg * (1.0 - sf))
        dq_ref[...] = (dqs * (sq * (1.0 + q * (1.0 - sq)))).astype(BF16)
        di_ref[...] = dvv.astype(BF16)

    return pl.pallas_call(
        body, name=name,
        grid=(HGRN_HEADS, nb),
        in_specs=proj_specs + [lb_spec, gain_spec, row_spec, st_spec, dy_spec],
        out_specs=[row_spec] * 4 + [acc_spec, acc_spec],
        out_shape=[_sds((n_rows, HGRN_WIDTH), BF16)] * 4 + [_sds((HGRN_HEADS, 1, 128), F32)] * 2,
        scratch_shapes=[pltpu.VMEM((128, 128), F32), pltpu.VMEM((nsub, 128, 128), F32)],
        compiler_params=_params(("parallel", "arbitrary")),
    )(proj, proj, proj, proj, lb, gain, o_saved, states, dycat)


def _alibi_slopes():
    return jnp.exp2(-8.0 * jnp.arange(1, ATT_HEADS + 1, dtype=F32) / ATT_HEADS)


def _swa_specs(n_blocks):
    blk = ATT_BLOCK
    prev = lambda i: jnp.maximum(i - 1, 0)
    smem = pl.BlockSpec(memory_space=pltpu.SMEM)
    return [
        smem, smem,
        pl.BlockSpec((blk, ATT_HEADS * ATT_DIM), lambda i: (i, 0)),
        pl.BlockSpec((blk, 256), lambda i: (i, 8)),
        pl.BlockSpec((blk, 256), lambda i: (prev(i), 8)),
        pl.BlockSpec((blk, 256), lambda i: (i, 9)),
        pl.BlockSpec((blk, 256), lambda i: (prev(i), 9)),
        pl.BlockSpec((1, ATT_DIM), lambda i: (0, 0)),
        pl.BlockSpec((1, ATT_DIM), lambda i: (0, 0)),
    ]


_ATT_GROUP = ATT_HEADS // ATT_KV
_ATT_ROWS = _ATT_GROUP * ATT_BLOCK


def _swa_mask(i):
    t_i = lax.broadcasted_iota(jnp.int32, (_ATT_ROWS, 2 * ATT_BLOCK), 0) & (ATT_BLOCK - 1)
    s_i = lax.broadcasted_iota(jnp.int32, (_ATT_ROWS, 2 * ATT_BLOCK), 1)
    dist = t_i + ATT_BLOCK - s_i
    valid = (dist >= 0) & (dist < ATT_BLOCK) & ((s_i >= ATT_BLOCK) | (i > 0))
    return valid, dist.astype(F32)


def _stack_heads(x):
    return jnp.concatenate([x[:, ATT_DIM * h:ATT_DIM * (h + 1)] for h in range(_ATT_GROUP)], axis=0)


def _unstack_heads(x):
    return jnp.concatenate([x[ATT_BLOCK * h:ATT_BLOCK * (h + 1)] for h in range(_ATT_GROUP)], axis=1)


def _head_column(ref, g):
    return jnp.concatenate([jnp.full((ATT_BLOCK, 1), ref[_ATT_GROUP * g + h], F32) for h in range(_ATT_GROUP)], axis=0)


def _swa_probs(qn, kn, slope, sink, valid, distf):
    s = lax.dot_general(qn, kn, _NT, preferred_element_type=F32) * (ATT_DIM ** -0.5) - slope * distf
    s = jnp.where(valid, s, NEG)
    m = jnp.maximum(jnp.max(s, axis=1, keepdims=True), sink)
    p = jnp.exp(s - m)
    es = jnp.exp(sink - m)
    inv = 1.0 / (jnp.sum(p, axis=1, keepdims=True) + es)
    return p * inv, es * inv


def _swa_fwd(name, qkv, q_gain, k_gain, sinks, slopes):
    n_rows = qkv.shape[0]
    nb = n_rows // ATT_BLOCK

    def body(sink_ref, slope_ref, q_ref, kc_ref, kp_ref, vc_ref, vp_ref, qg_ref, kg_ref, o_ref):
        i = pl.program_id(0)
        kb = jnp.concatenate([kp_ref[...], kc_ref[...]], axis=0)
        vb = jnp.concatenate([vp_ref[...], vc_ref[...]], axis=0)
        valid, distf = _swa_mask(i)
        qgv, kgv = qg_ref[...], kg_ref[...]
        gw = _ATT_GROUP * ATT_DIM
        for g in range(ATT_KV):
            kg = kb[:, 64 * g:64 * (g + 1)]
            rk = lax.rsqrt(jnp.mean(kg * kg, axis=1, keepdims=True) + EPS)
            kn = (kg * rk * kgv).astype(BF16)
            vv = vb[:, 64 * g:64 * (g + 1)].astype(BF16)
            qs = _stack_heads(q_ref[:, gw * g:gw * (g + 1)])
            rq = lax.rsqrt(jnp.mean(qs * qs, axis=1, keepdims=True) + EPS)
            pn, _ = _swa_probs((qs * rq * qgv).astype(BF16), kn, _head_column(slope_ref, g),
                               _head_column(sink_ref, g), valid, distf)
            out = jnp.dot(pn.astype(BF16), vv, preferred_element_type=F32)
            o_ref[:, gw * g:gw * (g + 1)] = _unstack_heads(out).astype(BF16)

    return pl.pallas_call(
        body, name=name,
        grid=(nb,),
        in_specs=_swa_specs(nb),
        out_specs=pl.BlockSpec((ATT_BLOCK, ATT_HEADS * ATT_DIM), lambda i: (i, 0)),
        out_shape=_sds((n_rows, ATT_HEADS * ATT_DIM), BF16),
        compiler_params=_params(("parallel",)),
    )(sinks, slopes, qkv, qkv, qkv, qkv, qkv, q_gain.reshape(1, -1), k_gain.reshape(1, -1))


def _swa_bwd(name, qkv, q_gain, k_gain, sinks, slopes, d_out):
    n_rows = qkv.shape[0]
    nb = n_rows // ATT_BLOCK
    blk = ATT_BLOCK

    def body(sink_ref, slope_ref, q_ref, kc_ref, kp_ref, vc_ref, vp_ref, qg_ref, kg_ref, do_ref,
             dq_ref, dkc_ref, dkp_ref, dvc_ref, dvp_ref, dsink_ref, dqg_ref, dkg_ref):
        i = pl.program_id(0)

        @pl.when(i == 0)
        def _():
            dsink_ref[...] = jnp.zeros_like(dsink_ref)
            dqg_ref[...] = jnp.zeros_like(dqg_ref)
            dkg_ref[...] = jnp.zeros_like(dkg_ref)

        kb = jnp.concatenate([kp_ref[...], kc_ref[...]], axis=0)
        vb = jnp.concatenate([vp_ref[...], vc_ref[...]], axis=0)
        kgv, qgv = kg_ref[...], qg_ref[...]
        valid, distf = _swa_mask(i)
        scale = ATT_DIM ** -0.5
        gw = _ATT_GROUP * ATT_DIM
        dks, dvs = [], []
        dqg, dkg = jnp.zeros((1, ATT_DIM), F32), jnp.zeros((1, ATT_DIM), F32)
        for g in range(ATT_KV):
            kg = kb[:, 64 * g:64 * (g + 1)]
            rk = lax.rsqrt(jnp.mean(kg * kg, axis=1, keepdims=True) + EPS)
            khat = kg * rk
            kn = (khat * kgv).astype(BF16)
            vv = vb[:, 64 * g:64 * (g + 1)].astype(BF16)
            qs = _stack_heads(q_ref[:, gw * g:gw * (g + 1)])
            rq = lax.rsqrt(jnp.mean(qs * qs, axis=1, keepdims=True) + EPS)
            qhat = qs * rq
            qn = (qhat * qgv).astype(BF16)
            pn, ps = _swa_probs(qn, kn, _head_column(slope_ref, g), _head_column(sink_ref, g), valid, distf)
            dos = _stack_heads(do_ref[:, gw * g:gw * (g + 1)]).astype(BF16)
            dp = lax.dot_general(dos, vv, _NT, preferred_element_type=F32)
            delta = jnp.sum(pn * dp, axis=1, keepdims=True)
            ds = (pn * (dp - delta)).astype(BF16)
            sd = ps * delta
            for h in range(_ATT_GROUP):
                hs = _ATT_GROUP * g + h
                dsink_ref[hs:hs + 1, :] += jnp.zeros((1, 128), F32) - jnp.sum(sd[blk * h:blk * (h + 1)])
            dvs.append(lax.dot_general(pn.astype(BF16), dos, _TN, preferred_element_type=F32))
            dkn = lax.dot_general(ds, qn, _TN, preferred_element_type=F32) * scale
            dqn = jnp.dot(ds, kn, preferred_element_type=F32) * scale
            dqg = dqg + _colsum(dqn * qhat)
            dqhat = dqn * qgv
            dqs = rq * (dqhat - qhat * jnp.mean(dqhat * qhat, axis=1, keepdims=True))
            dq_ref[:, gw * g:gw * (g + 1)] = _unstack_heads(dqs).astype(BF16)
            dkg = dkg + _colsum(dkn * khat)
            dkhat = dkn * kgv
            dks.append(rk * (dkhat - khat * jnp.mean(dkhat * khat, axis=1, keepdims=True)))
        dqg_ref[...] += dqg
        dkg_ref[...] += dkg
        dk = jnp.concatenate(dks, axis=1).astype(BF16)
        dv = jnp.concatenate(dvs, axis=1).astype(BF16)
        dkp_ref[...] = dk[:blk]
        dkc_ref[...] = dk[blk:]
        dvp_ref[...] = dv[:blk]
        dvc_ref[...] = dv[blk:]

    kv_spec = pl.BlockSpec((blk, 256), lambda i: (i, 0))
    full = pl.BlockSpec((blk, ATT_HEADS * ATT_DIM), lambda i: (i, 0))
    acc64 = pl.BlockSpec((1, ATT_DIM), lambda i: (0, 0))
    return pl.pallas_call(
        body, name=name,
        grid=(nb,),
        in_specs=_swa_specs(nb) + [full],
        out_specs=[full, kv_spec, kv_spec, kv_spec, kv_spec,
                   pl.BlockSpec((ATT_HEADS, 128), lambda i: (0, 0)), acc64, acc64],
        out_shape=[_sds((n_rows, ATT_HEADS * ATT_DIM), BF16)] + [_sds((n_rows, 256), BF16)] * 4
        + [_sds((ATT_HEADS, 128), F32), _sds((1, ATT_DIM), F32), _sds((1, ATT_DIM), F32)],
        compiler_params=_params(("arbitrary",)),
    )(sinks, slopes, qkv, qkv, qkv, qkv, qkv, q_gain.reshape(1, -1), k_gain.reshape(1, -1), d_out)


def _mesh_pos():
    return lax.axis_index("x"), lax.axis_index("y"), lax.axis_index("c")


_ANY = pl.BlockSpec(memory_space=pl.ANY)


def _allgather(name, shards):
    side = _gather_side(shards)
    n = len(shards)

    def body(*refs):
        side.start(refs[:n], refs[n:2 * n], refs[2 * n:])
        side.finish(refs[:n], refs[n:2 * n], refs[2 * n:])

    return pl.pallas_call(
        body, name=name,
        out_shape=side.out_shapes,
        in_specs=[_ANY] * n,
        out_specs=[_ANY] * n,
        scratch_shapes=side.scratch,
    )(*shards)


class _Side:
    def __init__(self, operands, out_shapes, scratch, start, finish):
        self.operands, self.out_shapes, self.scratch = list(operands), list(out_shapes), list(scratch)
        self.start, self.finish = start, finish


def _gather_side(shards):
    n = len(shards)

    def plan(x_refs, out_refs, sems):
        send_sems, recv_sems, local_sems = sems
        x, y, c = _mesh_pos()
        me, sibling = (x, y, c), (x, y, 1 - c)
        chips = [(1 - x, y), (x, 1 - y), (1 - x, 1 - y)]

        def slot(a, px, py, pc):
            return out_refs[a].at[4 * px + 2 * py + pc]

        def copy(a, k, block, to, src=None):
            return pltpu.make_async_remote_copy(
                src_ref=slot(a, *block) if src is None else src, dst_ref=slot(a, *block),
                send_sem=send_sems.at[7 * a + k], recv_sem=recv_sems.at[7 * a + k],
                device_id=to, device_id_type=MESH)

        local = [pltpu.make_async_copy(x_refs[a], slot(a, *me), local_sems.at[a]) for a in range(n)]
        first = [[copy(a, 0, me, sibling, src=x_refs[a])]
                 + [copy(a, 1 + j, me, (*chip, c), src=x_refs[a]) for j, chip in enumerate(chips)] for a in range(n)]
        from_chips = [[copy(a, 1 + j, (*chip, c), me) for j, chip in enumerate(chips)] for a in range(n)]
        forward = [[copy(a, 4 + j, (*chip, c), sibling) for j, chip in enumerate(chips)] for a in range(n)]
        from_sibling = [[copy(a, 0, sibling, me)] + [copy(a, 4 + j, (*chip, 1 - c), me) for j, chip in enumerate(chips)]
                        for a in range(n)]
        return local, first, from_chips, forward, from_sibling

    def start(x_refs, out_refs, sems):
        local, first, _, _, _ = plan(x_refs, out_refs, sems)
        for a in range(n):
            local[a].start()
            for cp in first[a]:
                cp.start()

    def finish(x_refs, out_refs, sems):
        local, first, from_chips, forward, from_sibling = plan(x_refs, out_refs, sems)
        for a in range(n):
            for j in range(3):
                from_chips[a][j].wait_recv()
                forward[a][j].start()
        for a in range(n):
            for cp in from_sibling[a]:
                cp.wait_recv()
        for a in range(n):
            for cp in first[a] + forward[a]:
                cp.wait_send()
            local[a].wait()

    return _Side(shards, [_sds((N_DEV,) + s.shape, s.dtype) for s in shards],
                 [pltpu.SemaphoreType.DMA((7 * n,)), pltpu.SemaphoreType.DMA((7 * n,)), pltpu.SemaphoreType.DMA((n,))],
                 start, finish)


def _swap_with_sibling(name, arrs):
    n = len(arrs)

    def body(*refs):
        x_refs, got_refs = refs[:n], refs[n:2 * n]
        send_sems, recv_sems = refs[2 * n:]
        x, y, c = _mesh_pos()
        copies = []
        for a in range(n):
            for j in range(4):
                k = 4 * a + j
                cp = pltpu.make_async_remote_copy(
                    src_ref=x_refs[a].at[j, 1 - c], dst_ref=got_refs[a].at[j],
                    send_sem=send_sems.at[k], recv_sem=recv_sems.at[k],
                    device_id=(x, y, 1 - c), device_id_type=MESH)
                cp.start()
                copies.append(cp)
        for cp in copies:
            cp.wait()

    return pl.pallas_call(
        body, name=name,
        out_shape=[_sds((4,) + t.shape[2:], t.dtype) for t in arrs],
        in_specs=[_ANY] * n,
        out_specs=[_ANY] * n,
        scratch_shapes=[pltpu.SemaphoreType.DMA((4 * n,))] * 2,
    )(*arrs)


def _exchange_chips(name, arrs):
    side = _exchange_side(arrs)
    n = len(arrs)

    def body(*refs):
        side.start(refs[:n], refs[n:2 * n], refs[2 * n:])
        side.finish(refs[:n], refs[n:2 * n], refs[2 * n:])

    return pl.pallas_call(
        body, name=name,
        out_shape=side.out_shapes,
        in_specs=[_ANY] * n,
        out_specs=[_ANY] * n,
        scratch_shapes=side.scratch,
    )(*arrs)


def _exchange_side(arrs):
    n = len(arrs)

    def plan(x_refs, out_refs, sems):
        send_sems, recv_sems, local_sems = sems
        x, y, c = _mesh_pos()
        me = 2 * x + y
        local = [pltpu.make_async_copy(x_refs[a].at[me], out_refs[a].at[me], local_sems.at[a]) for a in range(n)]
        sends, recvs = [], []
        for a in range(n):
            for k in range(1, 4):
                px, py = x ^ (k >> 1), y ^ (k & 1)
                peer = 2 * px + py
                sem = 3 * a + k - 1
                sends.append(pltpu.make_async_remote_copy(
                    src_ref=x_refs[a].at[peer], dst_ref=out_refs[a].at[me],
                    send_sem=send_sems.at[sem], recv_sem=recv_sems.at[sem],
                    device_id=(px, py, c), device_id_type=MESH))
                recvs.append(pltpu.make_async_remote_copy(
                    src_ref=x_refs[a].at[peer], dst_ref=out_refs[a].at[peer],
                    send_sem=send_sems.at[sem], recv_sem=recv_sems.at[sem],
                    device_id=(px, py, c), device_id_type=MESH))
        return local, sends, recvs

    def start(x_refs, out_refs, sems):
        local, sends, _ = plan(x_refs, out_refs, sems)
        for cp in local + sends:
            cp.start()

    def finish(x_refs, out_refs, sems):
        local, sends, recvs = plan(x_refs, out_refs, sems)
        for cp in recvs:
            cp.wait_recv()
        for cp in sends:
            cp.wait_send()
        for cp in local:
            cp.wait()

    return _Side(arrs, [_sds(t.shape, t.dtype) for t in arrs],
                 [pltpu.SemaphoreType.DMA((3 * n,)), pltpu.SemaphoreType.DMA((3 * n,)), pltpu.SemaphoreType.DMA((n,))],
                 start, finish)


def _sum_blocks(name, blocks, out_dtype=F32):
    n, n_rows, n_cols = blocks.shape
    tr = _row_tile(n_rows)

    def body(x_ref, o_ref):
        acc = x_ref[0].astype(F32)
        for s in range(1, n):
            acc = acc + x_ref[s].astype(F32)
        o_ref[...] = acc.astype(o_ref.dtype)

    return pl.pallas_call(
        body, name=name,
        grid=(n_rows // tr,),
        in_specs=[pl.BlockSpec((n, tr, n_cols), lambda i: (0, i, 0))],
        out_specs=pl.BlockSpec((tr, n_cols), lambda i: (i, 0)),
        out_shape=_sds((n_rows, n_cols), out_dtype),
        compiler_params=_params(("parallel",)),
    )(blocks)


def _add_pair(name, mine, got, core):
    n, n_rows, n_cols = got.shape
    tr = _row_tile(n_rows)

    def body(core_ref, a_ref, b_ref, o_ref):
        o_ref[...] = (a_ref[...].astype(F32) + b_ref[...].astype(F32)).astype(BF16)

    spec = pl.BlockSpec((None, tr, n_cols), lambda j, i, core_ref: (j, i, 0))
    return pl.pallas_call(
        body, name=name,
        grid_spec=pltpu.PrefetchScalarGridSpec(
            num_scalar_prefetch=1,
            grid=(n, n_rows // tr),
            in_specs=[pl.BlockSpec((None, None, tr, n_cols), lambda j, i, core_ref: (j, core_ref[0], i, 0)), spec],
            out_specs=spec,
        ),
        out_shape=_sds(got.shape, BF16),
        compiler_params=_params(("parallel", "parallel")),
    )(core, mine, got)


_MIX_PARTS = (
    (("w_in", "even_w_in", 1, D_MODEL, 5120), ("w_glu", "s5_w_glu", 0, S5_WIDTH, S5_WIDTH),
     ("w_out", "even_w_out", 0, D_MODEL, D_MODEL)),
    (("w_qkv", "odd_w_qkv", 1, D_MODEL, QKV_WIDTH), ("w_out", "odd_w_out", 0, D_MODEL, D_MODEL)),
)
_PACK_COLS = 1024
_FF_SHARD = D_FF // N_DEV
_BIG_NAMES = ("even_w_in", "s5_w_glu", "even_w_out", "odd_w_qkv", "odd_w_out")


def _part_rows(rows, cols):
    return rows * cols // N_DEV // _PACK_COLS


def _row_tile(n_rows):
    return next(t for t in (512, 480, 384, 256, 128) if n_rows % t == 0)


def _pack_mixer_shard(kind, j, args):
    return jnp.concatenate([args[name][j].astype(BF16).reshape(-1, _PACK_COLS) for _, name, _, _, _ in _MIX_PARTS[kind]],
                           axis=0)


def _unpack_mixer(kind, gathered):
    out, off = {}, 0
    for key, _, axis, rows, cols in _MIX_PARTS[kind]:
        n = _part_rows(rows, cols)
        part = gathered[:, off:off + n]
        off += n
        if axis == 1:
            part = part.reshape(N_DEV, rows, cols // N_DEV).transpose(1, 0, 2)
        out[key] = part.reshape(rows, cols)
    return out


def _pack_mixer_grads(kind, g):
    parts = []
    for key, _, axis, rows, cols in _MIX_PARTS[kind]:
        t = g[key]
        if axis == 1:
            t = t.reshape(rows, N_DEV, cols // N_DEV).transpose(1, 0, 2)
        parts.append(t.reshape(N_DEV, -1, _PACK_COLS))
    return jnp.concatenate(parts, axis=1)


def _unpack_mixer_grads(kind, flat):
    out, off = {}, 0
    for _, name, axis, rows, cols in _MIX_PARTS[kind]:
        n = _part_rows(rows, cols)
        shape = (rows, cols // N_DEV) if axis == 1 else (rows // N_DEV, cols)
        out[name] = flat[off:off + n].reshape(shape)
        off += n
    return out


def _pack_small(arrs, row_mult=512):
    parts = []
    for a in arrs:
        f = a.astype(F32).reshape(-1)
        parts.append(jnp.pad(f, (0, (-f.shape[0]) % 128)))
    f = jnp.concatenate(parts)
    f = jnp.pad(f, (0, (-f.shape[0]) % (128 * row_mult)))
    return f.reshape(-1, 128)


def _unpack_small(flat, shapes):
    f = flat.reshape(-1)
    out, off = [], 0
    for s in shapes:
        n = math.prod(s)
        out.append(f[off:off + n].reshape(s))
        off += n + (-n) % 128
    return out


_WEIGHTS = ("even_norm", "even_w_in", "s5_lambda_re", "s5_lambda_im", "s5_log_dt", "s5_b_re", "s5_b_im",
            "s5_c_re", "s5_c_im", "s5_d", "s5_w_glu", "s5_b_glu", "hgrn_lower_bound", "hgrn_o_norm",
            "even_w_out", "odd_norm", "odd_w_qkv", "q_norm", "k_norm", "att_sinks", "odd_w_out",
            "mlp_norm", "mlp_w_up", "mlp_w_down")
_MLP_NAMES = ("mlp_w_up", "mlp_w_down")
_SMALL_NAMES = tuple(n for n in _WEIGHTS if n not in _BIG_NAMES + _MLP_NAMES)


def _add_res(acc, res):
    return (acc + res,)


def _mm_hosting(side, *args, **kw):
    if side is None:
        return _mm(*args, **kw), None
    return _mm(*args, side=side, **kw)


def _mlp_fwd(h, xn, gain, next_gain, w_up, w_down, sides=(None, None)):
    n_rows, fs = h.shape[0], _FF_SHARD
    (up, act), got_up = _mm_hosting(
        sides[0], "mm_up", xn, w_up, "nn", out_dtypes=(F32, BF16), mkn=(n_rows, D_MODEL, D_FF), tm=1024, tn=fs,
        b_block=pl.BlockSpec((None, D_MODEL, fs), lambda i, j, kk: (j, kk, 0)),
        epi=lambda acc: (acc, jnp.square(jnp.maximum(acc, 0.0))))
    w_down4 = w_down.reshape(N_DEV // 2, 2 * fs, D_MODEL)
    if next_gain is None:
        out, got_down = _mm_hosting(
            sides[1], "mm_down_last", act, w_down4, "nn", mkn=(n_rows, D_FF, D_MODEL), tk=2 * fs,
            b_block=pl.BlockSpec((None, 2 * fs, 1024), lambda i, j, kk: (kk, 0, j)), epi=_add_res, extras=(h,))
        xn_next = None
    else:
        (out, xn_next), got_down = _mm_hosting(
            sides[1], "mm_down", act, w_down4, "nn", out_dtypes=(F32, BF16), mkn=(n_rows, D_FF, D_MODEL),
            tk=2 * fs, tn=D_MODEL, b_block=pl.BlockSpec((None, 2 * fs, D_MODEL), lambda i, j, kk: (kk, 0, 0)),
            epi=_res_norm, extras=(h,), row_vecs=(next_gain,))
    return out, xn_next, (h, gain, xn, up, act, w_up, w_down), got_up, got_down


def _mlp_bwd(cache, dh, dhb, sides=(None, None, None)):
    h, gain, xn, up, act, w_up, w_down = cache
    n_rows, fs = h.shape[0], _FF_SHARD
    dup, got0 = _mm_hosting(
        sides[0], "mm_dact", dhb, w_down, "nt", out_dtypes=(BF16,), mkn=(n_rows, D_MODEL, D_FF), tm=1024, tn=fs,
        b_block=pl.BlockSpec((None, fs, D_MODEL), lambda i, j, kk: (j, 0, kk)),
        epi=lambda acc, u: (acc * (2.0 * jnp.maximum(u, 0.0)),), extras=(up,))
    dw_down, got1 = _mm_hosting(
        sides[1], "mm_dw_down", act, dhb, "tn", out_dtypes=(BF16,), tk=n_rows,
        o_block=(pl.BlockSpec((None, 512, 1024), lambda i, j, kk: (i // 2, i % 2, j)), (N_DEV, fs, D_MODEL)))
    (dh_in, dhb_in, dgain), got2 = _mm_hosting(
        sides[2], "mm_dxn_up", dup, w_up, "nt", out_dtypes=(F32, BF16), mkn=(n_rows, D_FF, D_MODEL),
        tm=256, tn=D_MODEL, tk=2 * fs,
        b_block=pl.BlockSpec((None, D_MODEL, fs), lambda i, j, kk: (2 * kk, 0, 0)),
        b2_block=pl.BlockSpec((None, D_MODEL, fs), lambda i, j, kk: (2 * kk + 1, 0, 0)),
        epi=_rms_bwd_epi, extras=(h, dh), row_vecs=(gain,), col_acc=True)
    dw_up = _mm("mm_dw_up", xn, dup, "tn", out_dtypes=(BF16,), tk=n_rows,
                o_block=(pl.BlockSpec((None, 512, fs), lambda i, j, kk: (j, i, 0)), (N_DEV, D_MODEL, fs)))
    return dh_in, dhb_in, dgain, dw_up, dw_down, (got0, got1, got2)


def _even_fwd(h, xn, p, sides=(None, None, None)):
    proj, got_in = _mm_hosting(sides[0], "mm_w_in", xn, p["w_in"], "nn", tm=1024)
    (y_pre, z, s5_states), got_s5 = _s5_fwd("s5_fwd", proj, p["mats"], sides[1])
    gate, ya = _mm("mm_glu", z, p["w_glu"], "nn", out_dtypes=(F32, BF16), extras=(y_pre,), row_vecs=(p["b_glu"],),
                   epi=lambda acc, y, b: (acc, _gelu(y) * _sigmoid(acc + b)))
    (o, yb, h_states), got_h = _hgrn_fwd("hgrn_fwd", proj, p["lb"].reshape(8, 1, 128), p["o_gain"].reshape(1, 128),
                                         sides[2])
    ycat = jnp.concatenate([ya, yb], axis=1)
    out, xn_mlp = _mm("mm_w_out", ycat, p["w_out"], "nn", out_dtypes=(F32, BF16), tn=D_MODEL, epi=_res_norm,
                      extras=(h,), row_vecs=(p["mlp_gain"],))
    return out, xn_mlp, (h, xn, proj, y_pre, z, s5_states, gate, o, h_states, ycat), (got_in, got_s5, got_h)


def _even_bwd(cache, p, dh, dhb, sides=(None, None)):
    h, xn, proj, y_pre, z, s5_states, gate, o, h_states, ycat = cache
    g = {}
    dycat = _mm("mm_dy_out", dhb, p["w_out"], "nt", tn=D_MODEL)
    g["w_out"] = _mm("mm_dw_out", ycat, dhb, "tn", out_dtypes=(BF16,), tk=4096)
    dq, df, di, dg, dlb, dgain = _hgrn_bwd("hgrn_bwd", proj, p["lb"].reshape(8, 1, 128),
                                           p["o_gain"].reshape(1, 128), o, h_states, dycat)
    g["lb"] = dlb.reshape(-1)
    g["o_gain"] = jnp.sum(dgain, axis=0).reshape(-1)

    def glu_bwd1(dyc, y, gt, b):
        zf = _gelu(y)
        s = _sigmoid(gt + b)
        dya = dyc[:, :S5_WIDTH]
        d_gate = dya * zf * s * (1.0 - s)
        return (d_gate, dya * s), (_colsum(d_gate),)

    d_gate, dz_direct, db_glu = _rowwise("glu_bwd_gate", glu_bwd1, [dycat, y_pre, gate], [p["b_glu"].reshape(1, -1)],
                                         [(S5_WIDTH, BF16), (S5_WIDTH, F32)], [S5_WIDTH])
    g["b_glu"] = db_glu.reshape(-1)
    dy_pre = _mm("mm_dz_glu", d_gate, p["w_glu"], "nt", extras=(dz_direct, y_pre),
                 epi=lambda acc, dzd, y: ((acc + dzd) * _gelu_grad(y),))
    g["w_glu"] = _mm("mm_dw_glu", z, d_gate, "tn", out_dtypes=(BF16,), tk=4096)
    du, dbm, dcm, da, dd = _s5_bwd("s5_bwd", proj, dy_pre, s5_states, p["mats"])
    g["s5"] = (dbm, dcm, da, dd)
    dproj = jnp.concatenate([du, dq, df, di, dg], axis=1)
    (dh_in, dhb_in, dnorm), got0 = _mm_hosting(
        sides[0], "mm_dxn_in", dproj, p["w_in"], "nt", out_dtypes=(F32, BF16), tm=256, tn=D_MODEL, tk=2560,
        epi=_rms_bwd_epi, extras=(h, dh), row_vecs=(p["norm"],), col_acc=True)
    g["w_in"], got1 = _mm_hosting(sides[1], "mm_dw_in", xn, dproj, "tn", out_dtypes=(BF16,), tk=4096)
    g["norm"] = dnorm.reshape(-1)
    return dh_in, dhb_in, g, (got0, got1)


def _odd_fwd(h, xn, p, sides=(None, None, None)):
    qkv, got = _mm_hosting(sides[0], "mm_w_qkv", xn, p["w_qkv"], "nn", tm=1024, tn=1280)
    o = _swa_fwd("swa_fwd", qkv, p["q_gain"], p["k_gain"], p["sinks"], p["slopes"])
    out, xn_mlp = _mm("mm_w_out", o, p["w_out"], "nn", out_dtypes=(F32, BF16), tn=D_MODEL, epi=_res_norm,
                      extras=(h,), row_vecs=(p["mlp_gain"],))
    return out, xn_mlp, (h, xn, qkv, o), (got, None, None)


def _shift_up_block(x):
    return jnp.concatenate([x[ATT_BLOCK:], jnp.zeros((ATT_BLOCK, x.shape[1]), x.dtype)], axis=0)


def _odd_bwd(cache, p, dh, dhb, sides=(None, None)):
    h, xn, qkv, o = cache
    g = {}
    d_o = _mm("mm_dy_out", dhb, p["w_out"], "nt", tn=D_MODEL)
    g["w_out"] = _mm("mm_dw_out", o, dhb, "tn", out_dtypes=(BF16,), tk=4096)
    dq, dkc, dkp, dvc, dvp, dsink, dqg, dkg = _swa_bwd("swa_bwd", qkv, p["q_gain"], p["k_gain"], p["sinks"],
                                                       p["slopes"], d_o)
    dk = (dkc.astype(F32) + _shift_up_block(dkp).astype(F32)).astype(BF16)
    dv = (dvc.astype(F32) + _shift_up_block(dvp).astype(F32)).astype(BF16)
    g["sinks"], g["q_gain"], g["k_gain"] = dsink[:, 0], dqg.reshape(-1), dkg.reshape(-1)
    dqkv = jnp.concatenate([dq, dk, dv], axis=1)
    dh_in, dhb_in, dnorm = _mm("mm_dxn_qkv", dqkv, p["w_qkv"], "nt", out_dtypes=(F32, BF16), tm=256, tn=D_MODEL,
                               tk=1280, epi=_rms_bwd_epi, extras=(h, dh), row_vecs=(p["norm"],), col_acc=True)
    g["w_qkv"] = _mm("mm_dw_qkv", xn, dqkv, "tn", out_dtypes=(BF16,), tn=1280, tk=4096)
    g["norm"] = dnorm.reshape(-1)
    return dh_in, dhb_in, g, (None, None)


def kernel(x, even_norm, even_w_in, s5_lambda_re, s5_lambda_im, s5_log_dt, s5_b_re, s5_b_im, s5_c_re, s5_c_im, s5_d, s5_w_glu, s5_b_glu, hgrn_lower_bound, hgrn_o_norm, even_w_out, odd_norm, odd_w_qkv, q_norm, k_norm, att_sinks, odd_w_out, mlp_norm, mlp_w_up, mlp_w_down, loss_target, m_even_norm, m_even_w_in, m_s5_lambda_re, m_s5_lambda_im, m_s5_log_dt, m_s5_b_re, m_s5_b_im, m_s5_c_re, m_s5_c_im, m_s5_d, m_s5_w_glu, m_s5_b_glu, m_hgrn_lower_bound, m_hgrn_o_norm, m_even_w_out, m_odd_norm, m_odd_w_qkv, m_q_norm, m_k_norm, m_att_sinks, m_odd_w_out, m_mlp_norm, m_mlp_w_up, m_mlp_w_down, v_even_norm, v_even_w_in, v_s5_lambda_re, v_s5_lambda_im, v_s5_log_dt, v_s5_b_re, v_s5_b_im, v_s5_c_re, v_s5_c_im, v_s5_d, v_s5_w_glu, v_s5_b_glu, v_hgrn_lower_bound, v_hgrn_o_norm, v_even_w_out, v_odd_norm, v_odd_w_qkv, v_q_norm, v_k_norm, v_att_sinks, v_odd_w_out, v_mlp_norm, v_mlp_w_up, v_mlp_w_down):
    a = dict(locals())
    n_rows = x.shape[1]
    xi, yi, ci = _mesh_pos()
    me = 4 * xi + 2 * yi + ci

    chunks = [[_pack_mixer_shard(layer % 2, layer // 2, a), mlp_w_up[layer].astype(BF16),
               mlp_w_down[layer].astype(BF16)] for layer in range(DEPTH)]
    gathered = list(_allgather("gather_layer", chunks[0][:1])) + [None, None]
    (odd_gathered,) = _allgather("gather_odd_norm", [jnp.pad(odd_norm, ((0, 6), (0, 0)))])
    odd_norm_full = odd_gathered[:, :2].transpose(1, 0, 2).reshape(2, D_MODEL)

    lower_bounds, lb_vjp = jax.vjp(_hgrn_lower_bounds, hgrn_lower_bound)
    slopes = _alibi_slopes()
    s5_vjps = []

    h = x.reshape(n_rows, D_MODEL)
    mixer_gain = [even_norm[layer // 2] if layer % 2 == 0 else odd_norm_full[layer // 2] for layer in range(DEPTH)]
    xn = _rms_fwd("rms_fwd", h, mixer_gain[0])
    caches, layer_p = [], []
    for layer in range(DEPTH):
        kind, j = layer % 2, layer // 2
        wl = _unpack_mixer(kind, gathered[0])
        nxt = chunks[layer + 1] if layer + 1 < DEPTH else None
        sides = [_gather_side([t]) for t in nxt] if nxt is not None else [None] * 3
        late = [_gather_side([t]) if gathered[i] is None else None for i, t in ((1, chunks[layer][1]), (2, chunks[layer][2]))]
        if kind == 0:
            disc, vjp = jax.vjp(_s5_discretize, s5_lambda_re[j], s5_lambda_im[j], s5_log_dt[j], s5_b_re[j], s5_b_im[j])
            s5_vjps.append(vjp)
            p = dict(norm=mixer_gain[layer], w_in=wl["w_in"], w_glu=wl["w_glu"], b_glu=s5_b_glu[j],
                     mats=_s5_matrices(*disc, s5_c_re[j], s5_c_im[j], s5_d[j]),
                     lb=lower_bounds[j], o_gain=hgrn_o_norm[j], w_out=wl["w_out"], mlp_gain=mlp_norm[layer])
            h, xn, c_mix, got_mix = _even_fwd(h, xn, p, [sides[0]] + late)
        else:
            p = dict(norm=mixer_gain[layer], w_qkv=wl["w_qkv"], q_gain=q_norm[j], k_gain=k_norm[j],
                     sinks=att_sinks[j], slopes=slopes, w_out=wl["w_out"], mlp_gain=mlp_norm[layer])
            h, xn, c_mix, got_mix = _odd_fwd(h, xn, p, [sides[0], None, None])
        w_up_g = gathered[1] if gathered[1] is not None else got_mix[1][0]
        w_down_g = gathered[2] if gathered[2] is not None else got_mix[2][0]
        next_gain = mixer_gain[layer + 1] if layer + 1 < DEPTH else None
        h, xn, c_mlp, got_up, got_down = _mlp_fwd(h, xn, mlp_norm[layer], next_gain, w_up_g, w_down_g, sides[1:])
        caches.append((c_mix, c_mlp))
        layer_p.append(p)
        if nxt is not None:
            gathered = [got_mix[0][0], got_up[0], got_down[0]]
    dh, dhb, sq = _loss_head(h, loss_target.reshape(n_rows, D_MODEL))
    loss = lax.psum(0.5 * sq[0, 0] / D_MODEL, ("x", "y", "c"))

    core = ci.astype(jnp.int32).reshape(1)
    mix_g, mlp_norm_g, received = [None] * DEPTH, [None] * DEPTH, [None] * DEPTH
    pending = None
    for layer in reversed(range(DEPTH)):
        kind = layer % 2
        c_mix, c_mlp = caches[layer]
        sides = [_exchange_side([t]) for t in pending] if pending is not None else [None] * 3
        dh, dhb, d_mlp_norm, dw_up, dw_down, got = _mlp_bwd(c_mlp, dh, dhb, sides)
        if pending is not None:
            received[layer + 1] = [g[0] for g in got]
        mlp_norm_g[layer] = d_mlp_norm.reshape(-1)
        by_chip = [t.reshape((4, 2) + t.shape[1:]) for t in (dw_up, dw_down)]
        early = [None, None]
        if layer == 0:
            arrived = _swap_with_sibling("swap_grads", by_chip)
            early = [_exchange_side([_add_pair("add_sibling_grads", m, g, core)]) for m, g in zip(by_chip, arrived)]
        bwd = _even_bwd if kind == 0 else _odd_bwd
        dh, dhb, mix_g[layer], got = bwd(c_mix, layer_p[layer], dh, dhb, early)
        by_chip = [_pack_mixer_grads(kind, mix_g[layer]).reshape(4, 2, -1, _PACK_COLS)] + (by_chip if layer > 0 else [])
        arrived = _swap_with_sibling("swap_grads", by_chip)
        pending = [_add_pair("add_sibling_grads", m, g, core) for m, g in zip(by_chip, arrived)]
    received[0] = list(_exchange_chips("exchange_grads", pending)) + [got[0][0], got[1][0]]
    grad_x = dh.reshape(x.shape)

    ev, od = [mix_g[0], mix_g[2]], [mix_g[1], mix_g[3]]
    sums = [[_sum_blocks("sum_grads", r) for r in received[layer]] for layer in range(DEPTH)]
    grads = {"mlp_w_up": jnp.stack([s[1] for s in sums]), "mlp_w_down": jnp.stack([s[2] for s in sums])}
    per_layer = [_unpack_mixer_grads(layer % 2, sums[layer][0]) for layer in range(DEPTH)]
    for name in _BIG_NAMES:
        grads[name] = jnp.stack([g[name] for g in per_layer if name in g])

    s5_g = []
    for j in range(2):
        dar, dai, dbbr, dbbi, dcr, dci, dd = _s5_unpack_grads(*ev[j]["s5"])
        s5_g.append(tuple(s5_vjps[j]((dar, dai, dbbr, dbbi))) + (dcr, dci, dd))
    (d_lb_param,) = lb_vjp(jnp.stack([g["lb"] for g in ev]))
    small = {
        "even_norm": jnp.stack([g["norm"] for g in ev]),
        "s5_lambda_re": jnp.stack([g[0] for g in s5_g]), "s5_lambda_im": jnp.stack([g[1] for g in s5_g]),
        "s5_log_dt": jnp.stack([g[2] for g in s5_g]), "s5_b_re": jnp.stack([g[3] for g in s5_g]),
        "s5_b_im": jnp.stack([g[4] for g in s5_g]), "s5_c_re": jnp.stack([g[5] for g in s5_g]),
        "s5_c_im": jnp.stack([g[6] for g in s5_g]), "s5_d": jnp.stack([g[7] for g in s5_g]),
        "s5_b_glu": jnp.stack([g["b_glu"] for g in ev]), "hgrn_lower_bound": d_lb_param,
        "hgrn_o_norm": jnp.stack([g["o_gain"] for g in ev]), "odd_norm": jnp.stack([g["norm"] for g in od]),
        "q_norm": jnp.stack([g["q_gain"] for g in od]), "k_norm": jnp.stack([g["k_gain"] for g in od]),
        "att_sinks": jnp.stack([g["sinks"] for g in od]), "mlp_norm": jnp.stack(mlp_norm_g),
    }
    small_shapes = [small[n].shape for n in _SMALL_NAMES]
    (small_all,) = _allgather("gather_small_grads", [_pack_small([small[n] for n in _SMALL_NAMES])])
    small_sum = _sum_blocks("sum_small_grads", small_all)
    for n, g in zip(_SMALL_NAMES, _unpack_small(small_sum, small_shapes)):
        grads[n] = g
    grads["odd_norm"] = lax.dynamic_slice_in_dim(grads["odd_norm"], me * (D_MODEL // N_DEV), D_MODEL // N_DEV, axis=1)

    delta, new_m, new_v = {}, {}, {}
    for name in _BIG_NAMES + _MLP_NAMES:
        to2d = lambda t, c=a[name].shape[-1]: t.reshape(-1, c)
        d_, m_, v_ = _adamw("adamw_" + name, to2d(a[name]), to2d(grads[name]), to2d(a["m_" + name]), to2d(a["v_" + name]))
        delta[name], new_m[name], new_v[name] = (t.reshape(a[name].shape) for t in (d_, m_, v_))
    packed = [_pack_small([src[n] for n in _SMALL_NAMES])
              for src in (a, grads, {n: a["m_" + n] for n in _SMALL_NAMES}, {n: a["v_" + n] for n in _SMALL_NAMES})]
    shapes = [a[n].shape for n in _SMALL_NAMES]
    for dst, flat in zip((delta, new_m, new_v), _adamw("adamw_small", *packed)):
        for n, t in zip(_SMALL_NAMES, _unpack_small(flat, shapes)):
            dst[n] = t

    return (loss, grad_x, *[grads[n] for n in _WEIGHTS], *[delta[n] for n in _WEIGHTS],
            *[new_m[n] for n in _WEIGHTS], *[new_v[n] for n in _WEIGHTS])
```

```python
import math

import jax
import jax.numpy as jnp
from jax import lax
from jax.experimental import pallas as pl
from jax.experimental.pallas import tpu as pltpu

F32 = jnp.float32
BF16 = jnp.bfloat16
MESH = pl.DeviceIdType.MESH

D_MODEL = 2048
DEPTH = 4
EPS = 1e-6
S5_WIDTH = 1024
S5_GROUPS = 64
S5_STATE = 64
S5_GROUP_SIZE = 16
S5_MIN_DECAY = 1e-4
S5_CHUNK = 128
S5_LEVELS = 7
HGRN_WIDTH = 1024
HGRN_HEADS = 8
HGRN_DIM = 128
HGRN_SUB = 16
HGRN_BLOCK = 128
ATT_HEADS = 32
ATT_KV = 4
ATT_DIM = 64
ATT_BLOCK = 128
QKV_WIDTH = (ATT_HEADS + 2 * ATT_KV) * ATT_DIM
D_FF = 4 * D_MODEL
N_DEV = 8
NEG = -1e30
VMEM_LIMIT = 56 * 1024 * 1024

ADAM_LR, ADAM_B1, ADAM_B2, ADAM_EPS, ADAM_WD, ADAM_STEP = 0.001, 0.9, 0.999, 1e-08, 0.01, 10


def _params(sem=None):
    return pltpu.CompilerParams(dimension_semantics=sem, vmem_limit_bytes=VMEM_LIMIT)


def _sds(shape, dtype):
    return jax.ShapeDtypeStruct(shape, dtype)


def _call_hosting(side, body, name, grid, in_specs, out_specs, out_shape, scratch_shapes, sem, operands):
    if side is None:
        return pl.pallas_call(body, name=name, grid=grid, in_specs=in_specs, out_specs=out_specs, out_shape=out_shape,
                              scratch_shapes=scratch_shapes, compiler_params=_params(sem))(*operands), None
    n_in, n_out, n_scr = len(in_specs), len(out_specs), len(scratch_shapes)
    n_sin, n_sout = len(side.operands), len(side.out_shapes)
    any_spec = pl.BlockSpec(memory_space=pl.ANY)

    def hosting(*refs):
        ins, refs = refs[:n_in], refs[n_in:]
        sin, refs = refs[:n_sin], refs[n_sin:]
        outs, refs = refs[:n_out], refs[n_out:]
        sout, refs = refs[:n_sout], refs[n_sout:]
        scr, sems = refs[:n_scr], refs[n_scr:]
        ids = [pl.program_id(d) for d in range(len(grid))]
        first, last = ids[0] == 0, ids[0] == grid[0] - 1
        for d in range(1, len(grid)):
            first, last = first & (ids[d] == 0), last & (ids[d] == grid[d] - 1)

        @pl.when(first)
        def _():
            side.start(sin, sout, sems)

        body(*ins, *outs, *scr)

        @pl.when(last)
        def _():
            side.finish(sin, sout, sems)

    outs = pl.pallas_call(
        hosting, name=name, grid=grid,
        in_specs=list(in_specs) + [any_spec] * n_sin,
        out_specs=list(out_specs) + [any_spec] * n_sout,
        out_shape=list(out_shape) + side.out_shapes,
        scratch_shapes=list(scratch_shapes) + side.scratch,
        compiler_params=_params(("arbitrary",) * len(grid)),
    )(*operands, *side.operands)
    return outs[:n_out], outs[n_out:]


def _mm(name, a, b, mode, out_dtypes=(F32,), epi=None, extras=(), tm=512, tn=1024, tk=2048,
        mkn=None, b_block=None, b2_block=None, o_block=None, side=None, row_vecs=(), col_acc=False):
    if mkn is not None:
        m, k, n = mkn
    elif mode == "nn":
        (m, k), n = a.shape, b.shape[1]
    elif mode == "nt":
        (m, k), n = a.shape, b.shape[0]
    else:
        (k, m), n = a.shape, b.shape[1]
    tm, tn, tk = min(tm, m), min(tn, n), min(tk, k)
    assert m % tm == 0 and n % tn == 0 and k % tk == 0, (name, m, n, k)
    nk = k // tk
    if mode == "nn":
        a_spec = pl.BlockSpec((tm, tk), lambda i, j, kk: (i, kk))
        b_spec = pl.BlockSpec((tk, tn), lambda i, j, kk: (kk, j))
        dims = (((1,), (0,)), ((), ()))
    elif mode == "nt":
        a_spec = pl.BlockSpec((tm, tk), lambda i, j, kk: (i, kk))
        b_spec = pl.BlockSpec((tn, tk), lambda i, j, kk: (j, kk))
        dims = (((1,), (1,)), ((), ()))
    else:
        a_spec = pl.BlockSpec((tk, tm), lambda i, j, kk: (kk, i))
        b_spec = pl.BlockSpec((tk, tn), lambda i, j, kk: (kk, j))
        dims = (((0,), (0,)), ((), ()))
    o_spec = pl.BlockSpec((tm, tn), lambda i, j, kk: (i, j))
    if b_block is not None:
        b_spec = b_block
    out_specs = [o_spec] * len(out_dtypes)
    out_shape = [_sds((m, n), dt) for dt in out_dtypes]
    if o_block is not None:
        assert len(out_dtypes) == 1 and not extras
        out_specs, out_shape = [o_block[0]], [_sds(o_block[1], out_dtypes[0])]
    n_ex, n_out = len(extras) + len(row_vecs), len(out_dtypes)
    n_b = 1 if b2_block is None else 2
    grid = (m // tm, n // tn, nk)
    vec_spec = pl.BlockSpec((1, tn), lambda i, j, kk: (0, j))
    if col_acc:
        assert tn == n
        out_specs, out_shape = out_specs + [vec_spec], out_shape + [_sds((1, n), F32)]
    n_all = n_out + (1 if col_acc else 0)

    def body(*refs):
        a_ref, b_refs = refs[0], refs[1:1 + n_b]
        pos = 1 + n_b
        ex_refs = refs[pos:pos + n_ex]
        pos += n_ex
        out_refs = refs[pos:pos + n_out]
        col_ref = refs[pos + n_out] if col_acc else None
        pos += n_all
        acc_ref = refs[pos] if nk > 1 else None
        av = a_ref[...]
        if av.dtype != BF16:
            av = av.astype(BF16)
        part = None
        for q, b_ref in enumerate(b_refs):
            bv = b_ref[...]
            if bv.dtype != BF16:
                bv = bv.astype(BF16)
            aq = av if n_b == 1 else av[:, q * (tk // 2):(q + 1) * (tk // 2)]
            d = lax.dot_general(aq, bv, dims, preferred_element_type=F32)
            part = d if part is None else part + d

        def finish(acc):
            outs = epi(acc, *[r[...] for r in ex_refs]) if epi is not None else (acc,)
            for r, o in zip(out_refs, outs):
                r[...] = o.astype(r.dtype)
            if col_acc:
                row_tile = pl.program_id(0)

                @pl.when(row_tile == 0)
                def _():
                    col_ref[...] = outs[n_out]

                @pl.when(row_tile > 0)
                def _():
                    col_ref[...] += outs[n_out]

        if nk == 1:
            finish(part)
        else:
            kk = pl.program_id(2)

            @pl.when(kk == 0)
            def _():
                acc_ref[...] = part

            @pl.when(kk > 0)
            def _():
                acc_ref[...] += part

            @pl.when(kk == nk - 1)
            def _():
                finish(acc_ref[...])

    b_specs = [b_spec] if b2_block is None else [b_spec, b2_block]
    outs, side_outs = _call_hosting(
        side, body, name, grid, [a_spec] + b_specs + [o_spec] * len(extras) + [vec_spec] * len(row_vecs),
        out_specs, out_shape, [pltpu.VMEM((tm, tn), F32)] if nk > 1 else [],
        ("arbitrary",) * 3 if col_acc else ("parallel", "parallel", "arbitrary"),
        (a,) + (b,) * n_b + tuple(extras) + tuple(v.reshape(1, -1) for v in row_vecs))
    main = outs[0] if n_all == 1 else outs
    return main if side is None else (main, side_outs)


def _rowwise(name, fn, rows, vecs, outs, accs=(), tr=256):
    n_rows = rows[0].shape[0]
    tr = min(tr, n_rows)
    assert n_rows % tr == 0
    n_r, n_v, n_o, n_a = len(rows), len(vecs), len(outs), len(accs)

    def body(*refs):
        ins = [r[...] for r in refs[:n_r + n_v]]
        o_refs = refs[n_r + n_v:n_r + n_v + n_o]
        a_refs = refs[n_r + n_v + n_o:]
        ro, ao = fn(*ins)
        for r, o in zip(o_refs, ro):
            r[...] = o.astype(r.dtype)
        if n_a:
            step = pl.program_id(0)

            @pl.when(step == 0)
            def _():
                for r, o in zip(a_refs, ao):
                    r[...] = o

            @pl.when(step > 0)
            def _():
                for r, o in zip(a_refs, ao):
                    r[...] += o

    res = pl.pallas_call(
        body, name=name,
        grid=(n_rows // tr,),
        in_specs=[pl.BlockSpec((tr, r.shape[1]), lambda i: (i, 0)) for r in rows]
        + [pl.BlockSpec(v.shape, lambda i: (0, 0)) for v in vecs],
        out_specs=[pl.BlockSpec((tr, w), lambda i: (i, 0)) for w, _ in outs]
        + [pl.BlockSpec((1, w), lambda i: (0, 0)) for w in accs],
        out_shape=[_sds((n_rows, w), dt) for w, dt in outs] + [_sds((1, w), F32) for w in accs],
        compiler_params=_params(("arbitrary",)),
    )(*rows, *vecs)
    return res


def _colsum(x):
    return jnp.sum(x, axis=0, keepdims=True)


def _sigmoid(x):
    return 1.0 / (1.0 + jnp.exp(-x))


_GELU_C = math.sqrt(2.0 / math.pi)


def _gelu(y):
    return 0.5 * y * (1.0 + jnp.tanh(_GELU_C * (y + 0.044715 * y * y * y)))


def _gelu_grad(y):
    t = jnp.tanh(_GELU_C * (y + 0.044715 * y * y * y))
    return 0.5 * (1.0 + t) + 0.5 * y * (1.0 - t * t) * _GELU_C * (1.0 + 3.0 * 0.044715 * y * y)


def _rms_fwd(name, h, gain):
    def fn(x, g):
        r = lax.rsqrt(jnp.mean(x * x, axis=1, keepdims=True) + EPS)
        return (x * r * g,), ()
    return _rowwise(name, fn, [h], [gain.reshape(1, -1)], [(h.shape[1], BF16)])[0]


def _res_norm(acc, res, gain):
    hn = acc + res
    r = lax.rsqrt(jnp.mean(hn * hn, axis=1, keepdims=True) + EPS)
    return hn, hn * r * gain


def _rms_bwd_epi(dxn, h, dres, gain):
    r = lax.rsqrt(jnp.mean(h * h, axis=1, keepdims=True) + EPS)
    xh = h * r
    gdy = dxn * gain
    dx = r * (gdy - xh * jnp.mean(gdy * xh, axis=1, keepdims=True)) + dres
    return dx, dx, _colsum(dxn * xh)


def _loss_head(h, target):
    w = h.shape[1]

    def fn(x, t):
        e = x - t
        return (e * (1.0 / w), e * (1.0 / w)), (jnp.zeros((1, 128), F32) + jnp.sum(e * e),)
    return _rowwise("loss_head", fn, [h, target], [], [(w, F32), (w, BF16)], [128])


def _adamw(name, w, g, m, v):
    c1 = 1.0 - ADAM_B1 ** ADAM_STEP
    c2 = 1.0 - ADAM_B2 ** ADAM_STEP

    def fn(w_, g_, m_, v_):
        mn = ADAM_B1 * m_ + (1.0 - ADAM_B1) * g_
        vn = ADAM_B2 * v_ + (1.0 - ADAM_B2) * (g_ * g_)
        delta = -ADAM_LR * ((mn / c1) / (jnp.sqrt(vn / c2) + ADAM_EPS) + ADAM_WD * w_)
        return (delta, mn, vn), ()
    c = w.shape[1]
    return _rowwise(name, fn, [w, g, m, v], [], [(c, F32)] * 3)


def _s5_discretize(lam_re, lam_im, log_dt, b_re, b_im):
    lr = jnp.minimum(lam_re, -S5_MIN_DECAY)
    li = lam_im
    dt = jnp.exp(log_dt)[:, None]
    mag = jnp.exp(lr * dt)
    ar = mag * jnp.cos(li * dt)
    ai = mag * jnp.sin(li * dt)
    den = lr * lr + li * li
    zr = ((ar - 1.0) * lr + ai * li) / den
    zi = (ai * lr - (ar - 1.0) * li) / den
    bbr = zr[..., None] * b_re - zi[..., None] * b_im
    bbi = zr[..., None] * b_im + zi[..., None] * b_re
    return ar, ai, bbr, bbi


def _s5_matrices(ar, ai, bbr, bbi, c_re, c_im, d_skip):
    eye = jnp.eye(8, dtype=F32)
    bt = jnp.stack([bbr, bbi], axis=1).transpose(0, 3, 1, 2)
    bt = bt.reshape(8, 8, 16, 1, 2, 64) * eye[None, :, None, :, None, None]
    bm8 = bt.reshape(8, 8, 16, 4, 2, 2, 64).transpose(0, 1, 2, 3, 5, 4, 6).reshape(8, 128, 1024)
    ct = jnp.stack([c_re, -c_im], axis=1).transpose(0, 1, 3, 2)
    ct = ct.reshape(8, 8, 2, 64, 1, 16) * eye[None, :, None, None, :, None]
    cm8 = ct.reshape(8, 4, 2, 2, 64, 8, 16).transpose(0, 1, 3, 2, 4, 5, 6).reshape(8, 1024, 128)
    prs, pis = [], []
    pr, pi = ar, ai
    for _ in range(S5_LEVELS):
        prs.append(pr.reshape(8, 512))
        pis.append(pi.reshape(8, 512))
        pr, pi = pr * pr - pi * pi, 2.0 * pr * pi
    prs.append(jnp.zeros_like(prs[0]))
    pis.append(jnp.zeros_like(pis[0]))
    return (bm8.astype(BF16), cm8.astype(BF16), jnp.stack(prs, axis=1), jnp.stack(pis, axis=1),
            d_skip.reshape(8, 1, 128))


def _s5_unpack_grads(dbm8, dcm8, da, dd):
    db = dbm8.reshape(8, 8, 16, 4, 2, 2, 64).transpose(0, 1, 2, 3, 5, 4, 6).reshape(8, 8, 16, 8, 2, 64)
    db = jnp.einsum("agcgqp->agcqp", db).reshape(S5_GROUPS, 16, 2, 64)
    dc = dcm8.reshape(8, 4, 2, 2, 64, 8, 16).transpose(0, 1, 3, 2, 4, 5, 6).reshape(8, 8, 2, 64, 8, 16)
    dc = jnp.einsum("agqpgc->agqpc", dc).reshape(S5_GROUPS, 2, 64, 16)
    dar = da[:, 0, :].reshape(S5_GROUPS, 64)
    dai = da[:, 1, :].reshape(S5_GROUPS, 64)
    return (dar, dai, db[:, :, 0, :].transpose(0, 2, 1), db[:, :, 1, :].transpose(0, 2, 1),
            dc[:, 0].transpose(0, 2, 1), -dc[:, 1].transpose(0, 2, 1), dd.reshape(S5_GROUPS, 16))


def _shift_rows(x, s, row, down):
    t = x.shape[0]
    if s % 8 == 0:
        z = jnp.zeros((s, x.shape[1]), x.dtype)
        return jnp.concatenate([z, x[:t - s]], axis=0) if down else jnp.concatenate([x[s:], z], axis=0)
    if down:
        return jnp.where(row >= s, pltpu.roll(x, s, 0), 0.0)
    return jnp.where(row < t - s, pltpu.roll(x, t - s, 0), 0.0)


def _s5_scan(xr, xi, pr, pi, cr, ci, row, conj):
    t = xr[0].shape[0]
    sg = -1.0 if conj else 1.0
    edge = (t - 1) if conj else 0
    n = len(xr)
    for k in range(n):
        sl = slice(128 * k, 128 * (k + 1))
        p_r, p_i = pr[0:1, sl], sg * pi[0:1, sl]
        xr[k] = xr[k] + jnp.where(row == edge, p_r * cr[k] - p_i * ci[k], 0.0)
        xi[k] = xi[k] + jnp.where(row == edge, p_r * ci[k] + p_i * cr[k], 0.0)
    for lvl in range(S5_LEVELS):
        s = 1 << lvl
        for k in range(n):
            sl = slice(128 * k, 128 * (k + 1))
            p_r, p_i = pr[lvl:lvl + 1, sl], sg * pi[lvl:lvl + 1, sl]
            sr = _shift_rows(xr[k], s, row, not conj)
            si = _shift_rows(xi[k], s, row, not conj)
            xr[k] = xr[k] + p_r * sr - p_i * si
            xi[k] = xi[k] + p_r * si + p_i * sr
    return xr, xi


def _s5_fwd(name, proj, mats, side=None):
    bm8, cm8, p1, p2, d8 = mats
    n_rows = proj.shape[0]
    t = S5_CHUNK
    nch = n_rows // t

    def body(u_ref, bm_ref, cm_ref, pr_ref, pi_ref, d_ref, y_ref, z_ref, st_ref, carry):
        @pl.when(pl.program_id(1) == 0)
        def _():
            carry[...] = jnp.zeros_like(carry)

        cv = carry[...]
        st_ref[...] = cv
        u = u_ref[...]
        bu = jnp.dot(u.astype(BF16), bm_ref[...], preferred_element_type=F32)
        row = lax.broadcasted_iota(jnp.int32, (t, 128), 0)
        tile = lambda v, j: v[:, 128 * j:128 * (j + 1)]
        xr, xi = _s5_scan([tile(bu, 2 * k) for k in range(4)], [tile(bu, 2 * k + 1) for k in range(4)],
                          pr_ref[...], pi_ref[...], [tile(cv, 2 * k)[0:1] for k in range(4)],
                          [tile(cv, 2 * k + 1)[0:1] for k in range(4)], row, False)
        xall = jnp.concatenate([v for k in range(4) for v in (xr[k], xi[k])], axis=1)
        carry[...] = jnp.broadcast_to(xall[t - 1:t, :], (8, 1024))
        y = jnp.dot(xall.astype(BF16), cm_ref[...], preferred_element_type=F32) + d_ref[...] * u
        y_ref[...] = y
        z_ref[...] = _gelu(y).astype(BF16)

    return _call_hosting(
        side, body, name, (8, nch),
        [
            pl.BlockSpec((t, 128), lambda g, c: (c, g)),
            pl.BlockSpec((None, 128, 1024), lambda g, c: (g, 0, 0)),
            pl.BlockSpec((None, 1024, 128), lambda g, c: (g, 0, 0)),
            pl.BlockSpec((None, 8, 512), lambda g, c: (g, 0, 0)),
            pl.BlockSpec((None, 8, 512), lambda g, c: (g, 0, 0)),
            pl.BlockSpec((None, 1, 128), lambda g, c: (g, 0, 0)),
        ],
        [
            pl.BlockSpec((t, 128), lambda g, c: (c, g)),
            pl.BlockSpec((t, 128), lambda g, c: (c, g)),
            pl.BlockSpec((None, None, 8, 1024), lambda g, c: (g, c, 0, 0)),
        ],
        [_sds((n_rows, S5_WIDTH), F32), _sds((n_rows, S5_WIDTH), BF16), _sds((8, nch, 8, 1024), F32)],
        [pltpu.VMEM((8, 1024), F32)], ("parallel", "arbitrary"), (proj, bm8, cm8, p1, p2, d8))


def _s5_bwd(name, proj, dy, states, mats):
    bm8, cm8, p1, p2, d8 = mats
    n_rows = proj.shape[0]
    t = S5_CHUNK
    nch = n_rows // t
    nt_dims = (((1,), (1,)), ((), ()))
    tn_dims = (((0,), (0,)), ((), ()))

    def body(u_ref, dy_ref, st_ref, bm_ref, cm_ref, pr_ref, pi_ref, d_ref,
             du_ref, dbm_ref, dcm_ref, da_ref, dd_ref, gcarry):
        @pl.when(pl.program_id(1) == 0)
        def _():
            gcarry[...] = jnp.zeros_like(gcarry)
            dbm_ref[...] = jnp.zeros_like(dbm_ref)
            dcm_ref[...] = jnp.zeros_like(dcm_ref)
            da_ref[...] = jnp.zeros_like(da_ref)
            dd_ref[...] = jnp.zeros_like(dd_ref)

        u = u_ref[...]
        dyv = dy_ref[...]
        ub, dyb = u.astype(BF16), dyv.astype(BF16)
        bu = jnp.dot(ub, bm_ref[...], preferred_element_type=F32)
        dxd = lax.dot_general(dyb, cm_ref[...], nt_dims, preferred_element_type=F32)
        row = lax.broadcasted_iota(jnp.int32, (t, 128), 0)
        tile = lambda v, j: v[:, 128 * j:128 * (j + 1)]
        prv, piv, cv, gv = pr_ref[...], pi_ref[...], st_ref[...], gcarry[...]
        cr = [tile(cv, 2 * k)[0:1] for k in range(4)]
        ci = [tile(cv, 2 * k + 1)[0:1] for k in range(4)]
        xr, xi = _s5_scan([tile(bu, 2 * k) for k in range(4)], [tile(bu, 2 * k + 1) for k in range(4)],
                          prv, piv, cr, ci, row, False)
        gr, gi = _s5_scan([tile(dxd, 2 * k) for k in range(4)], [tile(dxd, 2 * k + 1) for k in range(4)],
                          prv, piv, [tile(gv, 2 * k)[0:1] for k in range(4)],
                          [tile(gv, 2 * k + 1)[0:1] for k in range(4)], row, True)
        dar, dai = [], []
        for k in range(4):
            xpr = jnp.where(row >= 1, pltpu.roll(xr[k], 1, 0), cr[k])
            xpi = jnp.where(row >= 1, pltpu.roll(xi[k], 1, 0), ci[k])
            dar.append(_colsum(gr[k] * xpr + gi[k] * xpi))
            dai.append(_colsum(gi[k] * xpr - gr[k] * xpi))
        xall = jnp.concatenate([v for k in range(4) for v in (xr[k], xi[k])], axis=1).astype(BF16)
        gf = jnp.concatenate([v for k in range(4) for v in (gr[k], gi[k])], axis=1)
        gcarry[...] = jnp.broadcast_to(gf[0:1, :], (8, 1024))
        gall = gf.astype(BF16)
        dcm_ref[...] += lax.dot_general(xall, dyb, tn_dims, preferred_element_type=F32)
        dbm_ref[...] += lax.dot_general(ub, gall, tn_dims, preferred_element_type=F32)
        du = lax.dot_general(gall, bm_ref[...], nt_dims, preferred_element_type=F32) + d_ref[...] * dyv
        du_ref[...] = du.astype(BF16)
        dd_ref[...] += _colsum(dyv * u)
        da_ref[0:1, :] += jnp.concatenate(dar, axis=1)
        da_ref[1:2, :] += jnp.concatenate(dai, axis=1)

    rev = lambda g, c: (nch - 1 - c, g)
    return pl.pallas_call(
        body, name=name,
        grid=(8, nch),
        in_specs=[
            pl.BlockSpec((t, 128), rev),
            pl.BlockSpec((t, 128), rev),
            pl.BlockSpec((None, None, 8, 1024), lambda g, c: (g, nch - 1 - c, 0, 0)),
            pl.BlockSpec((None, 128, 1024), lambda g, c: (g, 0, 0)),
            pl.BlockSpec((None, 1024, 128), lambda g, c: (g, 0, 0)),
            pl.BlockSpec((None, 8, 512), lambda g, c: (g, 0, 0)),
            pl.BlockSpec((None, 8, 512), lambda g, c: (g, 0, 0)),
            pl.BlockSpec((None, 1, 128), lambda g, c: (g, 0, 0)),
        ],
        out_specs=[
            pl.BlockSpec((t, 128), rev),
            pl.BlockSpec((None, 128, 1024), lambda g, c: (g, 0, 0)),
            pl.BlockSpec((None, 1024, 128), lambda g, c: (g, 0, 0)),
            pl.BlockSpec((None, 8, 512), lambda g, c: (g, 0, 0)),
            pl.BlockSpec((None, 1, 128), lambda g, c: (g, 0, 0)),
        ],
        out_shape=[_sds((n_rows, S5_WIDTH), BF16), _sds((8, 128, 1024), F32), _sds((8, 1024, 128), F32),
                   _sds((8, 8, 512), F32), _sds((8, 1, 128), F32)],
        scratch_shapes=[pltpu.VMEM((8, 1024), F32)],
        compiler_params=_params(("parallel", "arbitrary")),
    )(proj, dy, states, bm8, cm8, p1, p2, d8)


def _hgrn_lower_bounds(lb_param):
    p = jax.nn.softmax(lb_param, axis=0)
    return jnp.cumsum(p, axis=0) - p[0:1]


def _prefix16(x, r16):
    for s in (1, 2, 4, 8):
        x = x + jnp.where(r16 >= s, pltpu.roll(x, s, 0), 0.0)
    return x


def _suffix16(x, r16):
    n = x.shape[0]
    for s in (1, 2, 4, 8):
        x = x + jnp.where(r16 < HGRN_SUB - s, pltpu.roll(x, n - s, 0), 0.0)
    return x


_NT = (((1,), (1,)), ((), ()))
_TN = (((0,), (0,)), ((), ()))


def _dotf(a, b, dims=(((1,), (0,)), ((), ()))):
    return lax.dot_general(a.astype(BF16), b.astype(BF16), dims, preferred_element_type=F32)


def _hgrn_specs(n_blocks, rev):
    r = HGRN_BLOCK
    blk = (lambda b: n_blocks - 1 - b) if rev else (lambda b: b)
    proj_specs = [pl.BlockSpec((r, 128), (lambda h, b, c=c: (blk(b), 8 * c + h))) for c in (1, 2, 3, 4)]
    lb_spec = pl.BlockSpec((None, 1, 128), lambda h, b: (h, 0, 0))
    gain_spec = pl.BlockSpec((1, 128), lambda h, b: (0, 0))
    row_spec = pl.BlockSpec((r, 128), lambda h, b: (blk(b), h))
    st_spec = pl.BlockSpec((None, None, 128, 128), lambda h, b: (h, blk(b), 0, 0))
    return proj_specs, lb_spec, gain_spec, row_spec, st_spec, blk


def _hgrn_fwd(name, proj, lb, gain, side=None):
    n_rows = proj.shape[0]
    r = HGRN_BLOCK
    nb = n_rows // r
    nsub = r // HGRN_SUB
    proj_specs, lb_spec, gain_spec, row_spec, st_spec, _ = _hgrn_specs(nb, False)

    def body(q_ref, f_ref, i_ref, g_ref, lb_ref, gain_ref, o_ref, y_ref, st_ref, st_scr):
        @pl.when(pl.program_id(1) == 0)
        def _():
            st_scr[...] = jnp.zeros_like(st_scr)

        st_ref[...] = st_scr[...]
        q, f, v, g = q_ref[...], f_ref[...], i_ref[...], g_ref[...]
        lbv = lb_ref[...]
        qs = q * _sigmoid(q)
        fg = lbv + (1.0 - lbv) * _sigmoid(f)
        kk = 1.0 - fg
        r16 = lax.broadcasted_iota(jnp.int32, (r, 128), 0) & (HGRN_SUB - 1)
        b = _prefix16(jnp.log(fg), r16)
        qh = qs * jnp.exp(b)
        rs = lax.broadcasted_iota(jnp.int32, (HGRN_SUB, 128), 0)
        st = st_scr[...]
        outs = []
        for i in range(nsub):
            sl = slice(HGRN_SUB * i, HGRN_SUB * (i + 1))
            qsi, kki, vi, bi = qs[sl], kk[sl], v[sl], b[sl]
            o_i = _dotf(qh[sl], st, _NT)
            for s in range(HGRN_SUB):
                e = jnp.exp(jnp.where(rs >= s, bi - bi[s:s + 1], NEG))
                col = jnp.sum(qsi * e * kki[s:s + 1], axis=1, keepdims=True)
                o_i = o_i + col * vi[s:s + 1]
            bl = bi[HGRN_SUB - 1:HGRN_SUB]
            st = st * jnp.exp(bl) + _dotf(vi, kki * jnp.exp(bl - bi), _TN)
            outs.append(o_i)
        st_scr[...] = st
        o = jnp.concatenate(outs, axis=0)
        o_ref[...] = o
        rn = lax.rsqrt(jnp.mean(o * o, axis=1, keepdims=True) + EPS)
        y_ref[...] = (o * rn * gain_ref[...] * (g * _sigmoid(g))).astype(BF16)

    return _call_hosting(
        side, body, name, (HGRN_HEADS, nb), proj_specs + [lb_spec, gain_spec], [row_spec, row_spec, st_spec],
        [_sds((n_rows, HGRN_WIDTH), F32), _sds((n_rows, HGRN_WIDTH), BF16), _sds((HGRN_HEADS, nb, 128, 128), F32)],
        [pltpu.VMEM((128, 128), F32)], ("parallel", "arbitrary"), (proj, proj, proj, proj, lb, gain))


def _hgrn_bwd(name, proj, lb, gain, o_saved, states, dycat):
    n_rows = proj.shape[0]
    r = HGRN_BLOCK
    nb = n_rows // r
    nsub = r // HGRN_SUB
    proj_specs, lb_spec, gain_spec, row_spec, st_spec, blk = _hgrn_specs(nb, True)
    dy_spec = pl.BlockSpec((r, 128), lambda h, b: (blk(b), 8 + h))
    acc_spec = pl.BlockSpec((None, 1, 128), lambda h, b: (h, 0, 0))

    def body(q_ref, f_ref, i_ref, g_ref, lb_ref, gain_ref, o_ref, st_ref, dy_ref,
             dq_ref, df_ref, di_ref, dg_ref, dlb_ref, dgain_ref, dst_scr, sub_scr):
        @pl.when(pl.program_id(1) == 0)
        def _():
            dst_scr[...] = jnp.zeros_like(dst_scr)
            dlb_ref[...] = jnp.zeros_like(dlb_ref)
            dgain_ref[...] = jnp.zeros_like(dgain_ref)

        q, f, v, g = q_ref[...], f_ref[...], i_ref[...], g_ref[...]
        lbv, gain_v = lb_ref[...], gain_ref[...]
        sq = _sigmoid(q)
        qs = q * sq
        sf = _sigmoid(f)
        fg = lbv + (1.0 - lbv) * sf
        kk = 1.0 - fg
        r16 = lax.broadcasted_iota(jnp.int32, (r, 128), 0) & (HGRN_SUB - 1)
        b = _prefix16(jnp.log(fg), r16)
        eb = jnp.exp(b)
        qh = qs * eb

        o, dy = o_ref[...], dy_ref[...]
        rn = lax.rsqrt(jnp.mean(o * o, axis=1, keepdims=True) + EPS)
        on = o * rn
        sg = _sigmoid(g)
        sil = g * sg
        dgain_ref[...] += _colsum(dy * on * sil)
        dg_ref[...] = (dy * on * gain_v * (sg * (1.0 + g * (1.0 - sg)))).astype(BF16)
        don = dy * gain_v * sil
        do = rn * (don - on * jnp.mean(don * on, axis=1, keepdims=True))

        st = st_ref[...]
        for i in range(nsub):
            sl = slice(HGRN_SUB * i, HGRN_SUB * (i + 1))
            sub_scr[i] = st
            bi = b[sl]
            bl = bi[HGRN_SUB - 1:HGRN_SUB]
            st = st * jnp.exp(bl) + _dotf(v[sl], kk[sl] * jnp.exp(bl - bi), _TN)

        rs = lax.broadcasted_iota(jnp.int32, (HGRN_SUB, 128), 0)
        dst = dst_scr[...]
        parts = [None] * nsub
        for i in reversed(range(nsub)):
            sl = slice(HGRN_SUB * i, HGRN_SUB * (i + 1))
            sp = sub_scr[i]
            qsi, kki, vi, bi, doi, qhi = qs[sl], kk[sl], v[sl], b[sl], do[sl], qh[sl]
            bl = bi[HGRN_SUB - 1:HGRN_SUB]
            ebl = jnp.exp(bl)
            dec = jnp.exp(bl - bi)
            khat = kki * dec
            dqh = _dotf(doi, sp)
            dkhat = _dotf(vi, dst)
            dv = _dotf(khat, dst, _NT)
            zrow = _colsum(sp * dst) * ebl
            dq_in = jnp.zeros((HGRN_SUB, 128), F32)
            dk_in = jnp.zeros((HGRN_SUB, 128), F32)
            dv_in = jnp.zeros((HGRN_SUB, 128), F32)
            for s in range(HGRN_SUB):
                e = jnp.exp(jnp.where(rs >= s, bi - bi[s:s + 1], NEG))
                dpc = jnp.sum(doi * vi[s:s + 1], axis=1, keepdims=True)
                w = qsi * e
                pc = jnp.sum(w * kki[s:s + 1], axis=1, keepdims=True)
                dq_in = dq_in + dpc * e * kki[s:s + 1]
                dk_in = jnp.where(rs == s, _colsum(dpc * w), dk_in)
                dv_in = jnp.where(rs == s, _colsum(pc * doi), dv_in)
            kd = khat * dkhat
            parts[i] = (qsi * dq_in - kki * dk_in + qhi * dqh, kd, jnp.broadcast_to(zrow, (HGRN_SUB, 128)),
                        dq_in + dqh * eb[sl], dk_in + dkhat * dec, dv + dv_in)
            dst = dst * ebl + _dotf(doi, qhi, _TN)
        dst_scr[...] = dst

        cat = lambda j: jnp.concatenate([p[j] for p in parts], axis=0)
        d_b, kd, zr, dqs, dkk, dvv = (cat(j) for j in range(6))
        dlf = _suffix16(d_b, r16) + _prefix16(kd, r16) - kd + zr
        dfg = dlf / fg - dkk
        df_ref[...] = (dfg * (1.0 - lbv) * sf * (1.0 - sf)).astype(BF16)
        dlb_ref[...] += _colsum(dfg * (1.0 - sf))
        dq_ref[...] = (dqs * (sq * (1.0 + q * (1.0 - sq)))).astype(BF16)
        di_ref[...] = dvv.astype(BF16)

    return pl.pallas_call(
        body, name=name,
        grid=(HGRN_HEADS, nb),
        in_specs=proj_specs + [lb_spec, gain_spec, row_spec, st_spec, dy_spec],
        out_specs=[row_spec] * 4 + [acc_spec, acc_spec],
        out_shape=[_sds((n_rows, HGRN_WIDTH), BF16)] * 4 + [_sds((HGRN_HEADS, 1, 128), F32)] * 2,
        scratch_shapes=[pltpu.VMEM((128, 128), F32), pltpu.VMEM((nsub, 128, 128), F32)],
        compiler_params=_params(("parallel", "arbitrary")),
    )(proj, proj, proj, proj, lb, gain, o_saved, states, dycat)


def _alibi_slopes():
    return jnp.exp2(-8.0 * jnp.arange(1, ATT_HEADS + 1, dtype=F32) / ATT_HEADS)


def _swa_specs(n_blocks):
    blk = ATT_BLOCK
    prev = lambda i: jnp.maximum(i - 1, 0)
    smem = pl.BlockSpec(memory_space=pltpu.SMEM)
    return [
        smem, smem,
        pl.BlockSpec((blk, ATT_HEADS * ATT_DIM), lambda i: (i, 0)),
        pl.BlockSpec((blk, 256), lambda i: (i, 8)),
        pl.BlockSpec((blk, 256), lambda i: (prev(i), 8)),
        pl.BlockSpec((blk, 256), lambda i: (i, 9)),
        pl.BlockSpec((blk, 256), lambda i: (prev(i), 9)),
        pl.BlockSpec((1, ATT_DIM), lambda i: (0, 0)),
        pl.BlockSpec((1, ATT_DIM), lambda i: (0, 0)),
    ]


_ATT_GROUP = ATT_HEADS // ATT_KV
_ATT_ROWS = _ATT_GROUP * ATT_BLOCK


def _swa_mask(i):
    t_i = lax.broadcasted_iota(jnp.int32, (_ATT_ROWS, 2 * ATT_BLOCK), 0) & (ATT_BLOCK - 1)
    s_i = lax.broadcasted_iota(jnp.int32, (_ATT_ROWS, 2 * ATT_BLOCK), 1)
    dist = t_i + ATT_BLOCK - s_i
    valid = (dist >= 0) & (dist < ATT_BLOCK) & ((s_i >= ATT_BLOCK) | (i > 0))
    return valid, dist.astype(F32)


def _stack_heads(x):
    return jnp.concatenate([x[:, ATT_DIM * h:ATT_DIM * (h + 1)] for h in range(_ATT_GROUP)], axis=0)


def _unstack_heads(x):
    return jnp.concatenate([x[ATT_BLOCK * h:ATT_BLOCK * (h + 1)] for h in range(_ATT_GROUP)], axis=1)


def _head_column(ref, g):
    return jnp.concatenate([jnp.full((ATT_BLOCK, 1), ref[_ATT_GROUP * g + h], F32) for h in range(_ATT_GROUP)], axis=0)


def _swa_probs(qn, kn, slope, sink, valid, distf):
    s = lax.dot_general(qn, kn, _NT, preferred_element_type=F32) * (ATT_DIM ** -0.5) - slope * distf
    s = jnp.where(valid, s, NEG)
    m = jnp.maximum(jnp.max(s, axis=1, keepdims=True), sink)
    p = jnp.exp(s - m)
    es = jnp.exp(sink - m)
    inv = 1.0 / (jnp.sum(p, axis=1, keepdims=True) + es)
    return p * inv, es * inv


def _swa_fwd(name, qkv, q_gain, k_gain, sinks, slopes):
    n_rows = qkv.shape[0]
    nb = n_rows // ATT_BLOCK

    def body(sink_ref, slope_ref, q_ref, kc_ref, kp_ref, vc_ref, vp_ref, qg_ref, kg_ref, o_ref):
        i = pl.program_id(0)
        kb = jnp.concatenate([kp_ref[...], kc_ref[...]], axis=0)
        vb = jnp.concatenate([vp_ref[...], vc_ref[...]], axis=0)
        valid, distf = _swa_mask(i)
        qgv, kgv = qg_ref[...], kg_ref[...]
        gw = _ATT_GROUP * ATT_DIM
        for g in range(ATT_KV):
            kg = kb[:, 64 * g:64 * (g + 1)]
            rk = lax.rsqrt(jnp.mean(kg * kg, axis=1, keepdims=True) + EPS)
            kn = (kg * rk * kgv).astype(BF16)
            vv = vb[:, 64 * g:64 * (g + 1)].astype(BF16)
            qs = _stack_heads(q_ref[:, gw * g:gw * (g + 1)])
            rq = lax.rsqrt(jnp.mean(qs * qs, axis=1, keepdims=True) + EPS)
            pn, _ = _swa_probs((qs * rq * qgv).astype(BF16), kn, _head_column(slope_ref, g),
                               _head_column(sink_ref, g), valid, distf)
            out = jnp.dot(pn.astype(BF16), vv, preferred_element_type=F32)
            o_ref[:, gw * g:gw * (g + 1)] = _unstack_heads(out).astype(BF16)

    return pl.pallas_call(
        body, name=name,
        grid=(nb,),
        in_specs=_swa_specs(nb),
        out_specs=pl.BlockSpec((ATT_BLOCK, ATT_HEADS * ATT_DIM), lambda i: (i, 0)),
        out_shape=_sds((n_rows, ATT_HEADS * ATT_DIM), BF16),
        compiler_params=_params(("parallel",)),
    )(sinks, slopes, qkv, qkv, qkv, qkv, qkv, q_gain.reshape(1, -1), k_gain.reshape(1, -1))


def _swa_bwd(name, qkv, q_gain, k_gain, sinks, slopes, d_out):
    n_rows = qkv.shape[0]
    nb = n_rows // ATT_BLOCK
    blk = ATT_BLOCK

    def body(sink_ref, slope_ref, q_ref, kc_ref, kp_ref, vc_ref, vp_ref, qg_ref, kg_ref, do_ref,
             dq_ref, dkc_ref, dkp_ref, dvc_ref, dvp_ref, dsink_ref, dqg_ref, dkg_ref):
        i = pl.program_id(0)

        @pl.when(i == 0)
        def _():
            dsink_ref[...] = jnp.zeros_like(dsink_ref)
            dqg_ref[...] = jnp.zeros_like(dqg_ref)
            dkg_ref[...] = jnp.zeros_like(dkg_ref)

        kb = jnp.concatenate([kp_ref[...], kc_ref[...]], axis=0)
        vb = jnp.concatenate([vp_ref[...], vc_ref[...]], axis=0)
        kgv, qgv = kg_ref[...], qg_ref[...]
        valid, distf = _swa_mask(i)
        scale = ATT_DIM ** -0.5
        gw = _ATT_GROUP * ATT_DIM
        dks, dvs = [], []
        dqg, dkg = jnp.zeros((1, ATT_DIM), F32), jnp.zeros((1, ATT_DIM), F32)
        for g in range(ATT_KV):
            kg = kb[:, 64 * g:64 * (g + 1)]
            rk = lax.rsqrt(jnp.mean(kg * kg, axis=1, keepdims=True) + EPS)
            khat = kg * rk
            kn = (khat * kgv).astype(BF16)
            vv = vb[:, 64 * g:64 * (g + 1)].astype(BF16)
            qs = _stack_heads(q_ref[:, gw * g:gw * (g + 1)])
            rq = lax.rsqrt(jnp.mean(qs * qs, axis=1, keepdims=True) + EPS)
            qhat = qs * rq
            qn = (qhat * qgv).astype(BF16)
            pn, ps = _swa_probs(qn, kn, _head_column(slope_ref, g), _head_column(sink_ref, g), valid, distf)
            dos = _stack_heads(do_ref[:, gw * g:gw * (g + 1)]).astype(BF16)
            dp = lax.dot_general(dos, vv, _NT, preferred_element_type=F32)
            delta = jnp.sum(pn * dp, axis=1, keepdims=True)
            ds = (pn * (dp - delta)).astype(BF16)
            sd = ps * delta
            for h in range(_ATT_GROUP):
                hs = _ATT_GROUP * g + h
                dsink_ref[hs:hs + 1, :] += jnp.zeros((1, 128), F32) - jnp.sum(sd[blk * h:blk * (h + 1)])
            dvs.append(lax.dot_general(pn.astype(BF16), dos, _TN, preferred_element_type=F32))
            dkn = lax.dot_general(ds, qn, _TN, preferred_element_type=F32) * scale
            dqn = jnp.dot(ds, kn, preferred_element_type=F32) * scale
            dqg = dqg + _colsum(dqn * qhat)
            dqhat = dqn * qgv
            dqs = rq * (dqhat - qhat * jnp.mean(dqhat * qhat, axis=1, keepdims=True))
            dq_ref[:, gw * g:gw * (g + 1)] = _unstack_heads(dqs).astype(BF16)
            dkg = dkg + _colsum(dkn * khat)
            dkhat = dkn * kgv
            dks.append(rk * (dkhat - khat * jnp.mean(dkhat * khat, axis=1, keepdims=True)))
        dqg_ref[...] += dqg
        dkg_ref[...] += dkg
        dk = jnp.concatenate(dks, axis=1).astype(BF16)
        dv = jnp.concatenate(dvs, axis=1).astype(BF16)
        dkp_ref[...] = dk[:blk]
        dkc_ref[...] = dk[blk:]
        dvp_ref[...] = dv[:blk]
        dvc_ref[...] = dv[blk:]

    kv_spec = pl.BlockSpec((blk, 256), lambda i: (i, 0))
    full = pl.BlockSpec((blk, ATT_HEADS * ATT_DIM), lambda i: (i, 0))
    acc64 = pl.BlockSpec((1, ATT_DIM), lambda i: (0, 0))
    return pl.pallas_call(
        body, name=name,
        grid=(nb,),
        in_specs=_swa_specs(nb) + [full],
        out_specs=[full, kv_spec, kv_spec, kv_spec, kv_spec,
                   pl.BlockSpec((ATT_HEADS, 128), lambda i: (0, 0)), acc64, acc64],
        out_shape=[_sds((n_rows, ATT_HEADS * ATT_DIM), BF16)] + [_sds((n_rows, 256), BF16)] * 4
        + [_sds((ATT_HEADS, 128), F32), _sds((1, ATT_DIM), F32), _sds((1, ATT_DIM), F32)],
        compiler_params=_params(("arbitrary",)),
    )(sinks, slopes, qkv, qkv, qkv, qkv, qkv, q_gain.reshape(1, -1), k_gain.reshape(1, -1), d_out)


def _mesh_pos():
    return lax.axis_index("x"), lax.axis_index("y"), lax.axis_index("c")


_ANY = pl.BlockSpec(memory_space=pl.ANY)


def _allgather(name, shards):
    side = _gather_side(shards)
    n = len(shards)

    def body(*refs):
        side.start(refs[:n], refs[n:2 * n], refs[2 * n:])
        side.finish(refs[:n], refs[n:2 * n], refs[2 * n:])

    return pl.pallas_call(
        body, name=name,
        out_shape=side.out_shapes,
        in_specs=[_ANY] * n,
        out_specs=[_ANY] * n,
        scratch_shapes=side.scratch,
    )(*shards)


class _Side:
    def __init__(self, operands, out_shapes, scratch, start, finish):
        self.operands, self.out_shapes, self.scratch = list(operands), list(out_shapes), list(scratch)
        self.start, self.finish = start, finish


def _gather_side(shards):
    n = len(shards)

    def plan(x_refs, out_refs, sems):
        send_sems, recv_sems, local_sems = sems
        x, y, c = _mesh_pos()
        me, sibling = (x, y, c), (x, y, 1 - c)
        chips = [(1 - x, y), (x, 1 - y), (1 - x, 1 - y)]

        def slot(a, px, py, pc):
            return out_refs[a].at[4 * px + 2 * py + pc]

        def copy(a, k, block, to, src=None):
            return pltpu.make_async_remote_copy(
                src_ref=slot(a, *block) if src is None else src, dst_ref=slot(a, *block),
                send_sem=send_sems.at[7 * a + k], recv_sem=recv_sems.at[7 * a + k],
                device_id=to, device_id_type=MESH)

        local = [pltpu.make_async_copy(x_refs[a], slot(a, *me), local_sems.at[a]) for a in range(n)]
        first = [[copy(a, 0, me, sibling, src=x_refs[a])]
                 + [copy(a, 1 + j, me, (*chip, c), src=x_refs[a]) for j, chip in enumerate(chips)] for a in range(n)]
        from_chips = [[copy(a, 1 + j, (*chip, c), me) for j, chip in enumerate(chips)] for a in range(n)]
        forward = [[copy(a, 4 + j, (*chip, c), sibling) for j, chip in enumerate(chips)] for a in range(n)]
        from_sibling = [[copy(a, 0, sibling, me)] + [copy(a, 4 + j, (*chip, 1 - c), me) for j, chip in enumerate(chips)]
                        for a in range(n)]
        return local, first, from_chips, forward, from_sibling

    def start(x_refs, out_refs, sems):
        local, first, _, _, _ = plan(x_refs, out_refs, sems)
        for a in range(n):
            local[a].start()
            for cp in first[a]:
                cp.start()

    def finish(x_refs, out_refs, sems):
        local, first, from_chips, forward, from_sibling = plan(x_refs, out_refs, sems)
        for a in range(n):
            for j in range(3):
                from_chips[a][j].wait_recv()
                forward[a][j].start()
        for a in range(n):
            for cp in from_sibling[a]:
                cp.wait_recv()
        for a in range(n):
            for cp in first[a] + forward[a]:
                cp.wait_send()
            local[a].wait()

    return _Side(shards, [_sds((N_DEV,) + s.shape, s.dtype) for s in shards],
                 [pltpu.SemaphoreType.DMA((7 * n,)), pltpu.SemaphoreType.DMA((7 * n,)), pltpu.SemaphoreType.DMA((n,))],
                 start, finish)


def _swap_with_sibling(name, arrs):
    n = len(arrs)

    def body(*refs):
        x_refs, got_refs = refs[:n], refs[n:2 * n]
        send_sems, recv_sems = refs[2 * n:]
        x, y, c = _mesh_pos()
        copies = []
        for a in range(n):
            for j in range(4):
                k = 4 * a + j
                cp = pltpu.make_async_remote_copy(
                    src_ref=x_refs[a].at[j, 1 - c], dst_ref=got_refs[a].at[j],
                    send_sem=send_sems.at[k], recv_sem=recv_sems.at[k],
                    device_id=(x, y, 1 - c), device_id_type=MESH)
                cp.start()
                copies.append(cp)
        for cp in copies:
            cp.wait()

    return pl.pallas_call(
        body, name=name,
        out_shape=[_sds((4,) + t.shape[2:], t.dtype) for t in arrs],
        in_specs=[_ANY] * n,
        out_specs=[_ANY] * n,
        scratch_shapes=[pltpu.SemaphoreType.DMA((4 * n,))] * 2,
    )(*arrs)


def _exchange_chips(name, arrs):
    side = _exchange_side(arrs)
    n = len(arrs)

    def body(*refs):
        side.start(refs[:n], refs[n:2 * n], refs[2 * n:])
        side.finish(refs[:n], refs[n:2 * n], refs[2 * n:])

    return pl.pallas_call(
        body, name=name,
        out_shape=side.out_shapes,
        in_specs=[_ANY] * n,
        out_specs=[_ANY] * n,
        scratch_shapes=side.scratch,
    )(*arrs)


def _exchange_side(arrs):
    n = len(arrs)

    def plan(x_refs, out_refs, sems):
        send_sems, recv_sems, local_sems = sems
        x, y, c = _mesh_pos()
        me = 2 * x + y
        local = [pltpu.make_async_copy(x_refs[a].at[me], out_refs[a].at[me], local_sems.at[a]) for a in range(n)]
        sends, recvs = [], []
        for a in range(n):
            for k in range(1, 4):
                px, py = x ^ (k >> 1), y ^ (k & 1)
                peer = 2 * px + py
                sem = 3 * a + k - 1
                sends.append(pltpu.make_async_remote_copy(
                    src_ref=x_refs[a].at[peer], dst_ref=out_refs[a].at[me],
                    send_sem=send_sems.at[sem], recv_sem=recv_sems.at[sem],
                    device_id=(px, py, c), device_id_type=MESH))
                recvs.append(pltpu.make_async_remote_copy(
                    src_ref=x_refs[a].at[peer], dst_ref=out_refs[a].at[peer],
                    send_sem=send_sems.at[sem], recv_sem=recv_sems.at[sem],
                    device_id=(px, py, c), device_id_type=MESH))
        return local, sends, recvs

    def start(x_refs, out_refs, sems):
        local, sends, _ = plan(x_refs, out_refs, sems)
        for cp in local + sends:
            cp.start()

    def finish(x_refs, out_refs, sems):
        local, sends, recvs = plan(x_refs, out_refs, sems)
        for cp in recvs:
            cp.wait_recv()
        for cp in sends:
            cp.wait_send()
        for cp in local:
            cp.wait()

    return _Side(arrs, [_sds(t.shape, t.dtype) for t in arrs],
                 [pltpu.SemaphoreType.DMA((3 * n,)), pltpu.SemaphoreType.DMA((3 * n,)), pltpu.SemaphoreType.DMA((n,))],
                 start, finish)


def _sum_blocks(name, blocks, out_dtype=F32):
    n, n_rows, n_cols = blocks.shape
    tr = _row_tile(n_rows)

    def body(x_ref, o_ref):
        acc = x_ref[0].astype(F32)
        for s in range(1, n):
            acc = acc + x_ref[s].astype(F32)
        o_ref[...] = acc.astype(o_ref.dtype)

    return pl.pallas_call(
        body, name=name,
        grid=(n_rows // tr,),
        in_specs=[pl.BlockSpec((n, tr, n_cols), lambda i: (0, i, 0))],
        out_specs=pl.BlockSpec((tr, n_cols), lambda i: (i, 0)),
        out_shape=_sds((n_rows, n_cols), out_dtype),
        compiler_params=_params(("parallel",)),
    )(blocks)


def _add_pair(name, mine, got, core):
    n, n_rows, n_cols = got.shape
    tr = _row_tile(n_rows)

    def body(core_ref, a_ref, b_ref, o_ref):
        o_ref[...] = (a_ref[...].astype(F32) + b_ref[...].astype(F32)).astype(BF16)

    spec = pl.BlockSpec((None, tr, n_cols), lambda j, i, core_ref: (j, i, 0))
    return pl.pallas_call(
        body, name=name,
        grid_spec=pltpu.PrefetchScalarGridSpec(
            num_scalar_prefetch=1,
            grid=(n, n_rows // tr),
            in_specs=[pl.BlockSpec((None, None, tr, n_cols), lambda j, i, core_ref: (j, core_ref[0], i, 0)), spec],
            out_specs=spec,
        ),
        out_shape=_sds(got.shape, BF16),
        compiler_params=_params(("parallel", "parallel")),
    )(core, mine, got)


_MIX_PARTS = (
    (("w_in", "even_w_in", 1, D_MODEL, 5120), ("w_glu", "s5_w_glu", 0, S5_WIDTH, S5_WIDTH),
     ("w_out", "even_w_out", 0, D_MODEL, D_MODEL)),
    (("w_qkv", "odd_w_qkv", 1, D_MODEL, QKV_WIDTH), ("w_out", "odd_w_out", 0, D_MODEL, D_MODEL)),
)
_PACK_COLS = 1024
_FF_SHARD = D_FF // N_DEV
_BIG_NAMES = ("even_w_in", "s5_w_glu", "even_w_out", "odd_w_qkv", "odd_w_out")


def _part_rows(rows, cols):
    return rows * cols // N_DEV // _PACK_COLS


def _row_tile(n_rows):
    return next(t for t in (512, 480, 384, 256, 128) if n_rows % t == 0)


def _pack_mixer_shard(kind, j, args):
    return jnp.concatenate([args[name][j].astype(BF16).reshape(-1, _PACK_COLS) for _, name, _, _, _ in _MIX_PARTS[kind]],
                           axis=0)


def _unpack_mixer(kind, gathered):
    out, off = {}, 0
    for key, _, axis, rows, cols in _MIX_PARTS[kind]:
        n = _part_rows(rows, cols)
        part = gathered[:, off:off + n]
        off += n
        if axis == 1:
            part = part.reshape(N_DEV, rows, cols // N_DEV).transpose(1, 0, 2)
        out[key] = part.reshape(rows, cols)
    return out


def _pack_mixer_grads(kind, g):
    parts = []
    for key, _, axis, rows, cols in _MIX_PARTS[kind]:
        t = g[key]
        if axis == 1:
            t = t.reshape(rows, N_DEV, cols // N_DEV).transpose(1, 0, 2)
        parts.append(t.reshape(N_DEV, -1, _PACK_COLS))
    return jnp.concatenate(parts, axis=1)


def _unpack_mixer_grads(kind, flat):
    out, off = {}, 0
    for _, name, axis, rows, cols in _MIX_PARTS[kind]:
        n = _part_rows(rows, cols)
        shape = (rows, cols // N_DEV) if axis == 1 else (rows // N_DEV, cols)
        out[name] = flat[off:off + n].reshape(shape)
        off += n
    return out


def _pack_small(arrs, row_mult=512):
    parts = []
    for a in arrs:
        f = a.astype(F32).reshape(-1)
        parts.append(jnp.pad(f, (0, (-f.shape[0]) % 128)))
    f = jnp.concatenate(parts)
    f = jnp.pad(f, (0, (-f.shape[0]) % (128 * row_mult)))
    return f.reshape(-1, 128)


def _unpack_small(flat, shapes):
    f = flat.reshape(-1)
    out, off = [], 0
    for s in shapes:
        n = math.prod(s)
        out.append(f[off:off + n].reshape(s))
        off += n + (-n) % 128
    return out


_WEIGHTS = ("even_norm", "even_w_in", "s5_lambda_re", "s5_lambda_im", "s5_log_dt", "s5_b_re", "s5_b_im",
            "s5_c_re", "s5_c_im", "s5_d", "s5_w_glu", "s5_b_glu", "hgrn_lower_bound", "hgrn_o_norm",
            "even_w_out", "odd_norm", "odd_w_qkv", "q_norm", "k_norm", "att_sinks", "odd_w_out",
            "mlp_norm", "mlp_w_up", "mlp_w_down")
_MLP_NAMES = ("mlp_w_up", "mlp_w_down")
_SMALL_NAMES = tuple(n for n in _WEIGHTS if n not in _BIG_NAMES + _MLP_NAMES)


def _add_res(acc, res):
    return (acc + res,)


def _mm_hosting(side, *args, **kw):
    if side is None:
        return _mm(*args, **kw), None
    return _mm(*args, side=side, **kw)


def _mlp_fwd(h, xn, gain, next_gain, w_up, w_down, sides=(None, None)):
    n_rows, fs = h.shape[0], _FF_SHARD
    (up, act), got_up = _mm_hosting(
        sides[0], "mm_up", xn, w_up, "nn", out_dtypes=(BF16, BF16), mkn=(n_rows, D_MODEL, D_FF), tm=1024, tn=fs,
        b_block=pl.BlockSpec((None, D_MODEL, fs), lambda i, j, kk: (j, kk, 0)),
        epi=lambda acc: (acc, jnp.square(jnp.maximum(acc, 0.0))))
    w_down4 = w_down.reshape(N_DEV // 2, 2 * fs, D_MODEL)
    if next_gain is None:
        out, got_down = _mm_hosting(
            sides[1], "mm_down_last", act, w_down4, "nn", mkn=(n_rows, D_FF, D_MODEL), tk=2 * fs,
            b_block=pl.BlockSpec((None, 2 * fs, 1024), lambda i, j, kk: (kk, 0, j)), epi=_add_res, extras=(h,))
        xn_next = None
    else:
        (out, xn_next), got_down = _mm_hosting(
            sides[1], "mm_down", act, w_down4, "nn", out_dtypes=(F32, BF16), mkn=(n_rows, D_FF, D_MODEL),
            tk=2 * fs, tn=D_MODEL, b_block=pl.BlockSpec((None, 2 * fs, D_MODEL), lambda i, j, kk: (kk, 0, 0)),
            epi=_res_norm, extras=(h,), row_vecs=(next_gain,))
    return out, xn_next, (h, gain, xn, up, act, w_up, w_down), got_up, got_down


def _mlp_bwd(cache, dh, dhb, sides=(None, None, None)):
    h, gain, xn, up, act, w_up, w_down = cache
    n_rows, fs = h.shape[0], _FF_SHARD
    dup, got0 = _mm_hosting(
        sides[0], "mm_dact", dhb, w_down, "nt", out_dtypes=(BF16,), mkn=(n_rows, D_MODEL, D_FF), tm=1024, tn=fs,
        b_block=pl.BlockSpec((None, fs, D_MODEL), lambda i, j, kk: (j, 0, kk)),
        epi=lambda acc, u: (acc * (2.0 * jnp.maximum(u.astype(F32), 0.0)),), extras=(up,))
    dw_down, got1 = _mm_hosting(
        sides[1], "mm_dw_down", act, dhb, "tn", out_dtypes=(BF16,), tk=n_rows,
        o_block=(pl.BlockSpec((None, 512, 1024), lambda i, j, kk: (i // 2, i % 2, j)), (N_DEV, fs, D_MODEL)))
    (dh_in, dhb_in, dgain), got2 = _mm_hosting(
        sides[2], "mm_dxn_up", dup, w_up, "nt", out_dtypes=(F32, BF16), mkn=(n_rows, D_FF, D_MODEL),
        tm=256, tn=D_MODEL, tk=2 * fs,
        b_block=pl.BlockSpec((None, D_MODEL, fs), lambda i, j, kk: (2 * kk, 0, 0)),
        b2_block=pl.BlockSpec((None, D_MODEL, fs), lambda i, j, kk: (2 * kk + 1, 0, 0)),
        epi=_rms_bwd_epi, extras=(h, dh), row_vecs=(gain,), col_acc=True)
    dw_up = _mm("mm_dw_up", xn, dup, "tn", out_dtypes=(BF16,), tk=n_rows,
                o_block=(pl.BlockSpec((None, 512, fs), lambda i, j, kk: (j, i, 0)), (N_DEV, D_MODEL, fs)))
    return dh_in, dhb_in, dgain, dw_up, dw_down, (got0, got1, got2)


def _even_fwd(h, xn, p, sides=(None, None, None)):
    proj, got_in = _mm_hosting(sides[0], "mm_w_in", xn, p["w_in"], "nn", tm=1024)
    (y_pre, z, s5_states), got_s5 = _s5_fwd("s5_fwd", proj, p["mats"], sides[1])
    gate, ya = _mm("mm_glu", z, p["w_glu"], "nn", out_dtypes=(F32, BF16), extras=(y_pre,), row_vecs=(p["b_glu"],),
                   epi=lambda acc, y, b: (acc, _gelu(y) * _sigmoid(acc + b)))
    (o, yb, h_states), got_h = _hgrn_fwd("hgrn_fwd", proj, p["lb"].reshape(8, 1, 128), p["o_gain"].reshape(1, 128),
                                         sides[2])
    ycat = jnp.concatenate([ya, yb], axis=1)
    out, xn_mlp = _mm("mm_w_out", ycat, p["w_out"], "nn", out_dtypes=(F32, BF16), tn=D_MODEL, epi=_res_norm,
                      extras=(h,), row_vecs=(p["mlp_gain"],))
    return out, xn_mlp, (h, xn, proj, y_pre, z, s5_states, gate, o, h_states, ycat), (got_in, got_s5, got_h)


def _even_bwd(cache, p, dh, dhb, sides=(None, None)):
    h, xn, proj, y_pre, z, s5_states, gate, o, h_states, ycat = cache
    g = {}
    dycat = _mm("mm_dy_out", dhb, p["w_out"], "nt", tn=D_MODEL)
    g["w_out"] = _mm("mm_dw_out", ycat, dhb, "tn", out_dtypes=(BF16,), tk=4096)
    dq, df, di, dg, dlb, dgain = _hgrn_bwd("hgrn_bwd", proj, p["lb"].reshape(8, 1, 128),
                                           p["o_gain"].reshape(1, 128), o, h_states, dycat)
    g["lb"] = dlb.reshape(-1)
    g["o_gain"] = jnp.sum(dgain, axis=0).reshape(-1)

    def glu_bwd1(dyc, y, gt, b):
        zf = _gelu(y)
        s = _sigmoid(gt + b)
        dya = dyc[:, :S5_WIDTH]
        d_gate = dya * zf * s * (1.0 - s)
        return (d_gate, dya * s), (_colsum(d_gate),)

    d_gate, dz_direct, db_glu = _rowwise("glu_bwd_gate", glu_bwd1, [dycat, y_pre, gate], [p["b_glu"].reshape(1, -1)],
                                         [(S5_WIDTH, BF16), (S5_WIDTH, F32)], [S5_WIDTH])
    g["b_glu"] = db_glu.reshape(-1)
    dy_pre = _mm("mm_dz_glu", d_gate, p["w_glu"], "nt", extras=(dz_direct, y_pre),
                 epi=lambda acc, dzd, y: ((acc + dzd) * _gelu_grad(y),))
    g["w_glu"] = _mm("mm_dw_glu", z, d_gate, "tn", out_dtypes=(BF16,), tk=4096)
    du, dbm, dcm, da, dd = _s5_bwd("s5_bwd", proj, dy_pre, s5_states, p["mats"])
    g["s5"] = (dbm, dcm, da, dd)
    dproj = jnp.concatenate([du, dq, df, di, dg], axis=1)
    (dh_in, dhb_in, dnorm), got0 = _mm_hosting(
        sides[0], "mm_dxn_in", dproj, p["w_in"], "nt", out_dtypes=(F32, BF16), tm=256, tn=D_MODEL, tk=2560,
        epi=_rms_bwd_epi, extras=(h, dh), row_vecs=(p["norm"],), col_acc=True)
    g["w_in"], got1 = _mm_hosting(sides[1], "mm_dw_in", xn, dproj, "tn", out_dtypes=(BF16,), tk=4096)
    g["norm"] = dnorm.reshape(-1)
    return dh_in, dhb_in, g, (got0, got1)


def _odd_fwd(h, xn, p, sides=(None, None, None)):
    qkv, got = _mm_hosting(sides[0], "mm_w_qkv", xn, p["w_qkv"], "nn", tm=1024, tn=1280)
    o = _swa_fwd("swa_fwd", qkv, p["q_gain"], p["k_gain"], p["sinks"], p["slopes"])
    out, xn_mlp = _mm("mm_w_out", o, p["w_out"], "nn", out_dtypes=(F32, BF16), tn=D_MODEL, epi=_res_norm,
                      extras=(h,), row_vecs=(p["mlp_gain"],))
    return out, xn_mlp, (h, xn, qkv, o), (got, None, None)


def _shift_up_block(x):
    return jnp.concatenate([x[ATT_BLOCK:], jnp.zeros((ATT_BLOCK, x.shape[1]), x.dtype)], axis=0)


def _odd_bwd(cache, p, dh, dhb, sides=(None, None)):
    h, xn, qkv, o = cache
    g = {}
    d_o = _mm("mm_dy_out", dhb, p["w_out"], "nt", tn=D_MODEL)
    g["w_out"] = _mm("mm_dw_out", o, dhb, "tn", out_dtypes=(BF16,), tk=4096)
    dq, dkc, dkp, dvc, dvp, dsink, dqg, dkg = _swa_bwd("swa_bwd", qkv, p["q_gain"], p["k_gain"], p["sinks"],
                                                       p["slopes"], d_o)
    dk = (dkc.astype(F32) + _shift_up_block(dkp).astype(F32)).astype(BF16)
    dv = (dvc.astype(F32) + _shift_up_block(dvp).astype(F32)).astype(BF16)
    g["sinks"], g["q_gain"], g["k_gain"] = dsink[:, 0], dqg.reshape(-1), dkg.reshape(-1)
    dqkv = jnp.concatenate([dq, dk, dv], axis=1)
    dh_in, dhb_in, dnorm = _mm("mm_dxn_qkv", dqkv, p["w_qkv"], "nt", out_dtypes=(F32, BF16), tm=256, tn=D_MODEL,
                               tk=1280, epi=_rms_bwd_epi, extras=(h, dh), row_vecs=(p["norm"],), col_acc=True)
    g["w_qkv"] = _mm("mm_dw_qkv", xn, dqkv, "tn", out_dtypes=(BF16,), tn=1280, tk=4096)
    g["norm"] = dnorm.reshape(-1)
    return dh_in, dhb_in, g, (None, None)


def kernel(x, even_norm, even_w_in, s5_lambda_re, s5_lambda_im, s5_log_dt, s5_b_re, s5_b_im, s5_c_re, s5_c_im, s5_d, s5_w_glu, s5_b_glu, hgrn_lower_bound, hgrn_o_norm, even_w_out, odd_norm, odd_w_qkv, q_norm, k_norm, att_sinks, odd_w_out, mlp_norm, mlp_w_up, mlp_w_down, loss_target, m_even_norm, m_even_w_in, m_s5_lambda_re, m_s5_lambda_im, m_s5_log_dt, m_s5_b_re, m_s5_b_im, m_s5_c_re, m_s5_c_im, m_s5_d, m_s5_w_glu, m_s5_b_glu, m_hgrn_lower_bound, m_hgrn_o_norm, m_even_w_out, m_odd_norm, m_odd_w_qkv, m_q_norm, m_k_norm, m_att_sinks, m_odd_w_out, m_mlp_norm, m_mlp_w_up, m_mlp_w_down, v_even_norm, v_even_w_in, v_s5_lambda_re, v_s5_lambda_im, v_s5_log_dt, v_s5_b_re, v_s5_b_im, v_s5_c_re, v_s5_c_im, v_s5_d, v_s5_w_glu, v_s5_b_glu, v_hgrn_lower_bound, v_hgrn_o_norm, v_even_w_out, v_odd_norm, v_odd_w_qkv, v_q_norm, v_k_norm, v_att_sinks, v_odd_w_out, v_mlp_norm, v_mlp_w_up, v_mlp_w_down):
    a = dict(locals())
    n_rows = x.shape[1]
    xi, yi, ci = _mesh_pos()
    me = 4 * xi + 2 * yi + ci

    chunks = [[_pack_mixer_shard(layer % 2, layer // 2, a), mlp_w_up[layer].astype(BF16),
               mlp_w_down[layer].astype(BF16)] for layer in range(DEPTH)]
    gathered = list(_allgather("gather_layer", chunks[0][:1])) + [None, None]
    (odd_gathered,) = _allgather("gather_odd_norm", [jnp.pad(odd_norm, ((0, 6), (0, 0)))])
    odd_norm_full = odd_gathered[:, :2].transpose(1, 0, 2).reshape(2, D_MODEL)

    lower_bounds, lb_vjp = jax.vjp(_hgrn_lower_bounds, hgrn_lower_bound)
    slopes = _alibi_slopes()
    s5_vjps = []

    h = x.reshape(n_rows, D_MODEL)
    mixer_gain = [even_norm[layer // 2] if layer % 2 == 0 else odd_norm_full[layer // 2] for layer in range(DEPTH)]
    xn = _rms_fwd("rms_fwd", h, mixer_gain[0])
    caches, layer_p = [], []
    for layer in range(DEPTH):
        kind, j = layer % 2, layer // 2
        wl = _unpack_mixer(kind, gathered[0])
        nxt = chunks[layer + 1] if layer + 1 < DEPTH else None
        sides = [_gather_side([t]) for t in nxt] if nxt is not None else [None] * 3
        late = [_gather_side([t]) if gathered[i] is None else None for i, t in ((1, chunks[layer][1]), (2, chunks[layer][2]))]
        if kind == 0:
            disc, vjp = jax.vjp(_s5_discretize, s5_lambda_re[j], s5_lambda_im[j], s5_log_dt[j], s5_b_re[j], s5_b_im[j])
            s5_vjps.append(vjp)
            p = dict(norm=mixer_gain[layer], w_in=wl["w_in"], w_glu=wl["w_glu"], b_glu=s5_b_glu[j],
                     mats=_s5_matrices(*disc, s5_c_re[j], s5_c_im[j], s5_d[j]),
                     lb=lower_bounds[j], o_gain=hgrn_o_norm[j], w_out=wl["w_out"], mlp_gain=mlp_norm[layer])
            h, xn, c_mix, got_mix = _even_fwd(h, xn, p, [sides[0]] + late)
        else:
            p = dict(norm=mixer_gain[layer], w_qkv=wl["w_qkv"], q_gain=q_norm[j], k_gain=k_norm[j],
                     sinks=att_sinks[j], slopes=slopes, w_out=wl["w_out"], mlp_gain=mlp_norm[layer])
            h, xn, c_mix, got_mix = _odd_fwd(h, xn, p, [sides[0], None, None])
        w_up_g = gathered[1] if gathered[1] is not None else got_mix[1][0]
        w_down_g = gathered[2] if gathered[2] is not None else got_mix[2][0]
        next_gain = mixer_gain[layer + 1] if layer + 1 < DEPTH else None
        h, xn, c_mlp, got_up, got_down = _mlp_fwd(h, xn, mlp_norm[layer], next_gain, w_up_g, w_down_g, sides[1:])
        caches.append((c_mix, c_mlp))
        layer_p.append(p)
        if nxt is not None:
            gathered = [got_mix[0][0], got_up[0], got_down[0]]
    dh, dhb, sq = _loss_head(h, loss_target.reshape(n_rows, D_MODEL))
    loss = lax.psum(0.5 * sq[0, 0] / D_MODEL, ("x", "y", "c"))

    core = ci.astype(jnp.int32).reshape(1)
    mix_g, mlp_norm_g, received = [None] * DEPTH, [None] * DEPTH, [None] * DEPTH
    pending = None
    for layer in reversed(range(DEPTH)):
        kind = layer % 2
        c_mix, c_mlp = caches[layer]
        sides = [_exchange_side([t]) for t in pending] if pending is not None else [None] * 3
        dh, dhb, d_mlp_norm, dw_up, dw_down, got = _mlp_bwd(c_mlp, dh, dhb, sides)
        if pending is not None:
            received[layer + 1] = [g[0] for g in got]
        mlp_norm_g[layer] = d_mlp_norm.reshape(-1)
        by_chip = [t.reshape((4, 2) + t.shape[1:]) for t in (dw_up, dw_down)]
        early = [None, None]
        if layer == 0:
            arrived = _swap_with_sibling("swap_grads", by_chip)
            early = [_exchange_side([_add_pair("add_sibling_grads", m, g, core)]) for m, g in zip(by_chip, arrived)]
        bwd = _even_bwd if kind == 0 else _odd_bwd
        dh, dhb, mix_g[layer], got = bwd(c_mix, layer_p[layer], dh, dhb, early)
        by_chip = [_pack_mixer_grads(kind, mix_g[layer]).reshape(4, 2, -1, _PACK_COLS)] + (by_chip if layer > 0 else [])
        arrived = _swap_with_sibling("swap_grads", by_chip)
        pending = [_add_pair("add_sibling_grads", m, g, core) for m, g in zip(by_chip, arrived)]
    received[0] = list(_exchange_chips("exchange_grads", pending)) + [got[0][0], got[1][0]]
    grad_x = dh.reshape(x.shape)

    ev, od = [mix_g[0], mix_g[2]], [mix_g[1], mix_g[3]]
    sums = [[_sum_blocks("sum_grads", r) for r in received[layer]] for layer in range(DEPTH)]
    grads = {"mlp_w_up": jnp.stack([s[1] for s in sums]), "mlp_w_down": jnp.stack([s[2] for s in sums])}
    per_layer = [_unpack_mixer_grads(layer % 2, sums[layer][0]) for layer in range(DEPTH)]
    for name in _BIG_NAMES:
        grads[name] = jnp.stack([g[name] for g in per_layer if name in g])

    s5_g = []
    for j in range(2):
        dar, dai, dbbr, dbbi, dcr, dci, dd = _s5_unpack_grads(*ev[j]["s5"])
        s5_g.append(tuple(s5_vjps[j]((dar, dai, dbbr, dbbi))) + (dcr, dci, dd))
    (d_lb_param,) = lb_vjp(jnp.stack([g["lb"] for g in ev]))
    small = {
        "even_norm": jnp.stack([g["norm"] for g in ev]),
        "s5_lambda_re": jnp.stack([g[0] for g in s5_g]), "s5_lambda_im": jnp.stack([g[1] for g in s5_g]),
        "s5_log_dt": jnp.stack([g[2] for g in s5_g]), "s5_b_re": jnp.stack([g[3] for g in s5_g]),
        "s5_b_im": jnp.stack([g[4] for g in s5_g]), "s5_c_re": jnp.stack([g[5] for g in s5_g]),
        "s5_c_im": jnp.stack([g[6] for g in s5_g]), "s5_d": jnp.stack([g[7] for g in s5_g]),
        "s5_b_glu": jnp.stack([g["b_glu"] for g in ev]), "hgrn_lower_bound": d_lb_param,
        "hgrn_o_norm": jnp.stack([g["o_gain"] for g in ev]), "odd_norm": jnp.stack([g["norm"] for g in od]),
        "q_norm": jnp.stack([g["q_gain"] for g in od]), "k_norm": jnp.stack([g["k_gain"] for g in od]),
        "att_sinks": jnp.stack([g["sinks"] for g in od]), "mlp_norm": jnp.stack(mlp_norm_g),
    }
    small_shapes = [small[n].shape for n in _SMALL_NAMES]
    (small_all,) = _allgather("gather_small_grads", [_pack_small([small[n] for n in _SMALL_NAMES])])
    small_sum = _sum_blocks("sum_small_grads", small_all)
    for n, g in zip(_SMALL_NAMES, _unpack_small(small_sum, small_shapes)):
        grads[n] = g
    grads["odd_norm"] = lax.dynamic_slice_in_dim(grads["odd_norm"], me * (D_MODEL // N_DEV), D_MODEL // N_DEV, axis=1)

    delta, new_m, new_v = {}, {}, {}
    for name in _BIG_NAMES + _MLP_NAMES:
        to2d = lambda t, c=a[name].shape[-1]: t.reshape(-1, c)
        d_, m_, v_ = _adamw("adamw_" + name, to2d(a[name]), to2d(grads[name]), to2d(a["m_" + name]), to2d(a["v_" + name]))
        delta[name], new_m[name], new_v[name] = (t.reshape(a[name].shape) for t in (d_, m_, v_))
    packed = [_pack_small([src[n] for n in _SMALL_NAMES])
              for src in (a, grads, {n: a["m_" + n] for n in _SMALL_NAMES}, {n: a["v_" + n] for n in _SMALL_NAMES})]
    shapes = [a[n].shape for n in _SMALL_NAMES]
    for dst, flat in zip((delta, new_m, new_v), _adamw("adamw_small", *packed)):
        for n, t in zip(_SMALL_NAMES, _unpack_small(flat, shapes)):
            dst[n] = t

    return (loss, grad_x, *[grads[n] for n in _WEIGHTS], *[delta[n] for n in _WEIGHTS],
            *[new_m[n] for n in _WEIGHTS], *[new_v[n] for n in _WEIGHTS])
```
